```python
import math
import jax
import jax.numpy as jnp
from jax import lax
import numpy as np

D_MODEL = 2048
BATCH = 16
SEQ = 2048
DEPTH = 2

CTX_LEN = 256
GRID_W = 64
ROPE_BASE = 10000.0
EPS = 1e-6
Q_BLOCK = 128

DIFF_HEADS = 4
DIFF_DK = 128
DIFF_DV = 256
DN_HEADS = 8
DN_DK = 128
DN_DV = 128
DN_CONV = 5
DN_CHUNK = 64

SWA_HEADS = 32
SWA_KV_HEADS = 4
SWA_GROUP = SWA_HEADS // SWA_KV_HEADS
SWA_DH = 64
WINDOW = 128

N_EXPERTS = 64
TOP_K = 8
N_GROUPS = 8
TOPK_GROUPS = 4
EXPERT_FF = 512
SHARED_FF = 512
ROUTED_SCALE = 2.5
MOE_BLOCK = 256

DN_QKV = DN_HEADS * (2 * DN_DK + DN_DV)
AB_COLS = (2 * DIFF_HEADS * DIFF_DK, 2 * DIFF_HEADS * DIFF_DK, DIFF_HEADS * DIFF_DV,
           DN_QKV, DN_HEADS * DN_DV, 2 * DN_HEADS, 2 * DN_HEADS)
AB_IN = sum(AB_COLS)
AB_OUT = DIFF_HEADS * DIFF_DV + DN_HEADS * DN_DV
SWA_COLS = (SWA_HEADS * SWA_DH, SWA_KV_HEADS * SWA_DH, SWA_KV_HEADS * SWA_DH)
SWA_IN = sum(SWA_COLS)
SWA_OUT = SWA_HEADS * SWA_DH
N_EVEN = (DEPTH + 1) // 2
N_ODD = DEPTH // 2

kernel_name = 'hybrid_diffdelta_swa_moe_dit'


def rmsnorm(x, w):
    xf = x.astype(jnp.float32)
    y = xf * lax.rsqrt(jnp.mean(xf * xf, axis=-1, keepdims=True) + EPS)
    return y.astype(x.dtype) * w


def l2norm(x):
    xf = x.astype(jnp.float32)
    return xf * lax.rsqrt(jnp.sum(xf * xf, axis=-1, keepdims=True) + EPS)


def modulate(h, shift, scale):
    return h * (1.0 + scale) + shift


def split_cols(z, sizes):
    return jnp.split(z, [int(i) for i in np.cumsum(sizes)[:-1]], axis=-1)


def axial_rope_tables(n_tok, head_dim):
    rows = n_tok // GRID_W
    row = jnp.repeat(jnp.arange(rows, dtype=jnp.float32), GRID_W)
    col = jnp.tile(jnp.arange(GRID_W, dtype=jnp.float32), rows)
    axis_dim = head_dim // 2
    inv_freq = ROPE_BASE ** (-jnp.arange(0, axis_dim, 2, dtype=jnp.float32) / axis_dim)
    ang_r = (row[:, None] * inv_freq)[:, None, :]
    ang_c = (col[:, None] * inv_freq)[:, None, :]
    return jnp.cos(ang_r), jnp.sin(ang_r), jnp.cos(ang_c), jnp.sin(ang_c)


def _rotate_half(xa, cos, sin):
    x1, x2 = jnp.split(xa, 2, axis=-1)
    return jnp.concatenate([x1 * cos - x2 * sin, x2 * cos + x1 * sin], axis=-1)


def apply_axial_rope(x, tabs):
    cos_r, sin_r, cos_c, sin_c = tabs
    xr, xc = jnp.split(x.astype(jnp.float32), 2, axis=-1)
    out = jnp.concatenate([_rotate_half(xr, cos_r, sin_r), _rotate_half(xc, cos_c, sin_c)], axis=-1)
    return out.astype(x.dtype)


def centred_dwconv_silu(x, w):
    ch = x.shape[-1]
    y = lax.conv_general_dilated(x, w[:, None, :].astype(x.dtype), (1,), ((DN_CONV // 2, DN_CONV // 2),),
                                 dimension_numbers=('NWC', 'WIO', 'NWC'), feature_group_count=ch)
    return jax.nn.silu(y)


def diff_attention(qc, kc, vc, qx, kx, vx, lam, with_ctx):
    scale = DIFF_DK ** -0.5
    k_all = jnp.concatenate([kx, kc], axis=1)
    v_all = jnp.concatenate([vx, vc], axis=1)

    def attend(q, k, v):
        s = jnp.einsum('bqhcd,bkhcd->bhcqk', q, k).astype(jnp.float32) * scale
        p = jax.nn.softmax(s, axis=-1)
        a = p[:, :, 0] - lam * p[:, :, 1]
        return jnp.einsum('bhqk,bkhd->bqhd', a.astype(v.dtype), v)

    b, t = qx.shape[:2]
    nb = t // Q_BLOCK
    qb = jnp.moveaxis(qx.reshape((b, nb, Q_BLOCK) + qx.shape[2:]), 1, 0)
    ox = lax.map(lambda q: attend(q, k_all, v_all), qb)
    ox = jnp.moveaxis(ox, 0, 1).reshape(b, t, DIFF_HEADS, DIFF_DV)
    oc = attend(qc, kc, vc) if with_ctx else None
    return oc, ox


def gated_delta_chunked(q, k, v, g, beta, s0):
    b, t, h, dk = q.shape
    dv = v.shape[-1]
    n = t // DN_CHUNK

    def to_chunks(a):
        a = a.reshape((b, n, DN_CHUNK, h) + a.shape[3:])
        return jnp.moveaxis(a, (1, 3), (0, 2))

    qc, kc, vc, gc, bc = (to_chunks(a) for a in (q, k, v, g, beta))
    gc = jnp.cumsum(gc, axis=-1)
    idx = jnp.arange(DN_CHUNK)
    lower_incl = idx[:, None] >= idx[None, :]
    decay = jnp.exp(jnp.where(lower_incl, gc[..., :, None] - gc[..., None, :], -jnp.inf))
    kb = kc * bc[..., None]
    lmat = jnp.where(idx[:, None] > idx[None, :], jnp.einsum('nbhid,nbhjd->nbhij', kb, kc) * decay, 0.0)
    eye = jnp.eye(DN_CHUNK, dtype=jnp.float32)
    tmat = lax.linalg.triangular_solve(lmat + eye, jnp.broadcast_to(eye, lmat.shape),
                                       left_side=True, lower=True, unit_diagonal=True)
    u = jnp.einsum('nbhij,nbhjv->nbhiv', tmat, vc * bc[..., None])
    w = jnp.einsum('nbhij,nbhjk->nbhik', tmat, kb * jnp.exp(gc)[..., None])
    qk = jnp.einsum('nbhid,nbhjd->nbhij', qc, kc) * decay

    def step(s, xs):
        q_i, k_i, u_i, w_i, g_i, a_i = xs
        v_new = u_i - jnp.einsum('bhck,bhkv->bhcv', w_i, s)
        o = (jnp.einsum('bhck,bhkv->bhcv', q_i * jnp.exp(g_i)[..., None], s)
             + jnp.einsum('bhij,bhjv->bhiv', a_i, v_new))
        g_last = g_i[..., -1:]
        s = (s * jnp.exp(g_last)[..., None]
             + jnp.einsum('bhck,bhcv->bhkv', k_i * jnp.exp(g_last - g_i)[..., None], v_new))
        return s, o

    s, o = lax.scan(step, s0, (qc, kc, u, w, gc, qk))
    o = jnp.moveaxis(o, (0, 2), (1, 3)).reshape(b, t, h, dv)
    return o, s


def bidirectional_deltanet(dn_c, dn_x):
    qc, kc, vc, gc, bc = dn_c
    qx, kx, vx, gx, bx = dn_x
    s0 = jnp.zeros((qc.shape[0], DN_HEADS, DN_DK, DN_DV), jnp.float32)
    rev = lambda a: jnp.flip(a, axis=1)
    oc_f, sc_f = gated_delta_chunked(qc, kc, vc, gc[:, :, 0], bc[:, :, 0], s0)
    ox_f, _ = gated_delta_chunked(qx, kx, vx, gx[:, :, 0], bx[:, :, 0], sc_f)
    oc_b, sc_b = gated_delta_chunked(rev(qc), rev(kc), rev(vc), rev(gc[:, :, 1]), rev(bc[:, :, 1]), s0)
    ox_b, _ = gated_delta_chunked(rev(qx), rev(kx), rev(vx), rev(gx[:, :, 1]), rev(bx[:, :, 1]), sc_b)
    return oc_f + rev(oc_b), ox_f + rev(ox_b)


def mixer_ab(hc, hx, layer, w_in, w_out, q_norm, k_norm, lam_vec, subln, conv_w, a_log, dt_bias,
             out_norm, with_ctx):
    b, n_c, _ = hc.shape
    t = hx.shape[1]
    z = jnp.concatenate([hc, hx], axis=1) @ w_in
    tabs = axial_rope_tables(t, DIFF_DK)
    lam_init = 0.8 - 0.6 * math.exp(-0.3 * layer)
    lv = lam_vec.astype(jnp.float32)
    lam = jnp.exp(jnp.sum(lv[0] * lv[1])) - jnp.exp(jnp.sum(lv[2] * lv[3])) + lam_init

    def project(zs, rope):
        bs, ts, _ = zs.shape
        q_d, k_d, v_d, qkv_n, gate, beta_raw, alpha_raw = split_cols(zs, AB_COLS)
        q_d = rmsnorm(q_d.reshape(bs, ts, 2 * DIFF_HEADS, DIFF_DK), q_norm)
        k_d = rmsnorm(k_d.reshape(bs, ts, 2 * DIFF_HEADS, DIFF_DK), k_norm)
        if rope:
            q_d = apply_axial_rope(q_d, tabs)
            k_d = apply_axial_rope(k_d, tabs)
        diff = (q_d.reshape(bs, ts, DIFF_HEADS, 2, DIFF_DK), k_d.reshape(bs, ts, DIFF_HEADS, 2, DIFF_DK),
                v_d.reshape(bs, ts, DIFF_HEADS, DIFF_DV))
        q_n, k_n, v_n = split_cols(centred_dwconv_silu(qkv_n, conv_w),
                                   (DN_HEADS * DN_DK, DN_HEADS * DN_DK, DN_HEADS * DN_DV))
        q_n = l2norm(q_n.reshape(bs, ts, DN_HEADS, DN_DK)) * DN_DK ** -0.5
        k_n = l2norm(k_n.reshape(bs, ts, DN_HEADS, DN_DK))
        v_n = v_n.reshape(bs, ts, DN_HEADS, DN_DV).astype(jnp.float32)
        beta = jax.nn.sigmoid(beta_raw.astype(jnp.float32)).reshape(bs, ts, 2, DN_HEADS)
        g = -jnp.exp(a_log.astype(jnp.float32)) * jax.nn.softplus(
            alpha_raw.astype(jnp.float32).reshape(bs, ts, 2, DN_HEADS) + dt_bias.astype(jnp.float32))
        return diff, (q_n, k_n, v_n, g, beta), gate.reshape(bs, ts, DN_HEADS, DN_DV)

    diff_c, dn_c, gate_c = project(z[:, :n_c], False)
    diff_x, dn_x, gate_x = project(z[:, n_c:], True)
    od_c, od_x = diff_attention(*diff_c, *diff_x, lam, with_ctx)
    on_c, on_x = bidirectional_deltanet(dn_c, dn_x)

    def merge(od, on, gate):
        bs, ts = od.shape[:2]
        od = rmsnorm(od, subln) * (1.0 - lam_init)
        on = (rmsnorm(on, out_norm) * jax.nn.silu(gate.astype(jnp.float32))).astype(od.dtype)
        return jnp.concatenate([od.reshape(bs, ts, -1), on.reshape(bs, ts, -1)], axis=-1)

    if with_ctx:
        y = jnp.concatenate([merge(od_c, on_c, gate_c), merge(od_x, on_x, gate_x)], axis=1) @ w_out
        return y[:, :n_c], y[:, n_c:]
    return None, merge(od_x, on_x, gate_x) @ w_out


def window_attention(qc, kc, vc, qx, kx, vx, sink, with_ctx):
    scale = SWA_DH ** -0.5
    b, t = qx.shape[:2]
    nb = t // Q_BLOCK
    sink_l = sink.astype(jnp.float32)[None, :, :, None, None]

    def with_sink(s):
        return jnp.concatenate([s, jnp.broadcast_to(sink_l, s.shape[:-1] + (1,))], axis=-1)

    pad = ((0, 0), (Q_BLOCK, Q_BLOCK), (0, 0), (0, 0))
    kp, vp = jnp.pad(kx, pad), jnp.pad(vx, pad)
    band = 3 * Q_BLOCK
    qb = jnp.moveaxis(qx.reshape((b, nb, Q_BLOCK) + qx.shape[2:]), 1, 0)

    def block(args):
        i, q = args
        start = i * Q_BLOCK
        kb = lax.dynamic_slice_in_dim(kp, start, band, axis=1)
        vb = lax.dynamic_slice_in_dim(vp, start, band, axis=1)
        qpos = start + jnp.arange(Q_BLOCK)
        kpos = start - Q_BLOCK + jnp.arange(band)
        valid = ((jnp.abs(qpos[:, None] - kpos[None, :]) <= WINDOW)
                 & (kpos[None, :] >= 0) & (kpos[None, :] < t))
        s_lat = jnp.where(valid, jnp.einsum('bqhgd,bkhd->bhgqk', q, kb).astype(jnp.float32) * scale, -jnp.inf)
        s_ctx = jnp.einsum('bqhgd,bkhd->bhgqk', q, kc).astype(jnp.float32) * scale
        p = jax.nn.softmax(with_sink(jnp.concatenate([s_lat, s_ctx], axis=-1)), axis=-1).astype(vx.dtype)
        return (jnp.einsum('bhgqk,bkhd->bqhgd', p[..., :band], vb)
                + jnp.einsum('bhgqk,bkhd->bqhgd', p[..., band:-1], vc))

    ox = lax.map(block, (jnp.arange(nb), qb))
    ox = jnp.moveaxis(ox, 0, 1).reshape(b, t, SWA_OUT)
    oc = None
    if with_ctx:
        s = jnp.einsum('bqhgd,bkhd->bhgqk', qc, kc).astype(jnp.float32) * scale
        p = jax.nn.softmax(with_sink(s), axis=-1).astype(vc.dtype)
        oc = jnp.einsum('bhgqk,bkhd->bqhgd', p[..., :-1], vc).reshape(b, qc.shape[1], SWA_OUT)
    return oc, ox


def mixer_swa(hc, hx, w_in, w_out, q_norm, k_norm, sink, with_ctx):
    n_c = hc.shape[1]
    t = hx.shape[1]
    z = jnp.concatenate([hc, hx], axis=1) @ w_in
    tabs = axial_rope_tables(t, SWA_DH)

    def project(zs, rope):
        bs, ts, _ = zs.shape
        q, k, v = split_cols(zs, SWA_COLS)
        q = rmsnorm(q.reshape(bs, ts, SWA_HEADS, SWA_DH), q_norm)
        k = rmsnorm(k.reshape(bs, ts, SWA_KV_HEADS, SWA_DH), k_norm)
        if rope:
            q = apply_axial_rope(q, tabs)
            k = apply_axial_rope(k, tabs)
        return (q.reshape(bs, ts, SWA_KV_HEADS, SWA_GROUP, SWA_DH), k,
                v.reshape(bs, ts, SWA_KV_HEADS, SWA_DH))

    oc, ox = window_attention(*project(z[:, :n_c], False), *project(z[:, n_c:], True),
                              sink.reshape(SWA_KV_HEADS, SWA_GROUP), with_ctx)
    if with_ctx:
        y = jnp.concatenate([oc, ox], axis=1) @ w_out
        return y[:, :n_c], y[:, n_c:]
    return None, ox @ w_out


def swiglu(h, w1, w3, w2):
    return (jax.nn.silu(h @ w1) * (h @ w3)) @ w2


def routed_experts(h, top_idx, top_w, w1, w3, w2):
    n, k = top_idx.shape
    a = n * k
    n_blocks = (a + N_EXPERTS * (MOE_BLOCK - 1) + MOE_BLOCK - 1) // MOE_BLOCK
    flat_e = top_idx.reshape(a)
    order = jnp.argsort(flat_e)
    sorted_e = flat_e[order]
    counts = jnp.zeros((N_EXPERTS,), jnp.int32).at[flat_e].add(1)
    padded = (counts + MOE_BLOCK - 1) // MOE_BLOCK * MOE_BLOCK
    pad_end = jnp.cumsum(padded)
    pad_start = pad_end - padded
    first = jnp.cumsum(counts) - counts
    dest = pad_start[sorted_e] + jnp.arange(a, dtype=jnp.int32) - first[sorted_e]
    rows = n_blocks * MOE_BLOCK
    row_tok = jnp.zeros((rows,), jnp.int32).at[dest].set((order // k).astype(jnp.int32))
    row_w = jnp.zeros((rows,), h.dtype).at[dest].set(top_w.reshape(a)[order])
    block_e = jnp.minimum(jnp.searchsorted(pad_end, jnp.arange(n_blocks, dtype=jnp.int32) * MOE_BLOCK,
                                           side='right'), N_EXPERTS - 1)

    def step(acc, blk):
        tok, wt, e = blk
        xb = h[tok]
        y = swiglu(xb, w1[e], w3[e], w2[e]) * wt[:, None]
        return acc.at[tok].add(y), None

    acc, _ = lax.scan(step, jnp.zeros_like(h),
                      (row_tok.reshape(n_blocks, MOE_BLOCK), row_w.reshape(n_blocks, MOE_BLOCK), block_e))
    return acc


def moe(h, router_w, router_bias, w1, w3, w2, ws1, ws3, ws2):
    n = h.shape[0]
    scores = jax.nn.sigmoid((h @ router_w).astype(jnp.float32))
    sel = scores + router_bias.astype(jnp.float32)
    group_score = lax.top_k(sel.reshape(n, N_GROUPS, N_EXPERTS // N_GROUPS), 2)[0].sum(-1)
    _, top_groups = lax.top_k(group_score, TOPK_GROUPS)
    group_mask = jnp.any(top_groups[:, :, None] == jnp.arange(N_GROUPS)[None, None, :], axis=1)
    expert_mask = jnp.repeat(group_mask, N_EXPERTS // N_GROUPS, axis=1)
    _, top_idx = lax.top_k(jnp.where(expert_mask, sel, -jnp.inf), TOP_K)
    top_w = jnp.take_along_axis(scores, top_idx, axis=1)
    top_w = top_w / jnp.sum(top_w, axis=1, keepdims=True) * ROUTED_SCALE
    return routed_experts(h, top_idx, top_w.astype(h.dtype), w1, w3, w2) + swiglu(h, ws1, ws3, ws2)


def setup_inputs(seed: int = 0) -> dict:
    key = jax.random.key(seed)
    ks = jax.random.split(key, 32)
    f32 = jnp.float32

    def nrm(k, shape, scale):
        return jax.random.normal(k, shape, f32) * scale

    def gain(k, shape):
        return 1.0 + 0.02 * jax.random.normal(k, shape, f32)

    dt = jnp.exp(jax.random.uniform(ks[16], (N_EVEN, 2, DN_HEADS), f32, math.log(1e-3), math.log(1e-1)))
    return {
        'x': nrm(ks[0], (BATCH, SEQ, D_MODEL), 1.0),
        'c': nrm(ks[1], (BATCH, D_MODEL), 1.0),
        'ctx': nrm(ks[2], (BATCH, CTX_LEN, D_MODEL), 1.0),
        'c_ctx': nrm(ks[3], (D_MODEL,), 1.0),
        'mod_w': nrm(ks[4], (DEPTH, D_MODEL, 6 * D_MODEL), 0.5 * D_MODEL ** -0.5),
        'mod_b': nrm(ks[5], (DEPTH, 6 * D_MODEL), 0.02),
        'norm_mix': gain(ks[6], (DEPTH, D_MODEL)),
        'norm_ffn': gain(ks[7], (DEPTH, D_MODEL)),
        'ab_w_in': nrm(ks[8], (N_EVEN, D_MODEL, AB_IN), D_MODEL ** -0.5),
        'ab_w_out': nrm(ks[9], (N_EVEN, AB_OUT, D_MODEL), AB_OUT ** -0.5),
        'diff_q_norm': gain(ks[10], (N_EVEN, DIFF_DK)),
        'diff_k_norm': gain(ks[11], (N_EVEN, DIFF_DK)),
        'diff_lambda': nrm(ks[12], (N_EVEN, 4, DIFF_DK), 0.1),
        'diff_subln': gain(ks[13], (N_EVEN, DIFF_DV)),
        'dn_conv': nrm(ks[14], (N_EVEN, DN_CONV, DN_QKV), DN_CONV ** -0.5),
        'dn_a_log': jnp.log(jax.random.uniform(ks[15], (N_EVEN, 2, DN_HEADS), f32, 1.0, 16.0)),
        'dn_dt_bias': dt + jnp.log(-jnp.expm1(-dt)),
        'dn_out_norm': gain(ks[17], (N_EVEN, DN_DV)),
        'swa_w_in': nrm(ks[18], (N_ODD, D_MODEL, SWA_IN), D_MODEL ** -0.5),
        'swa_w_out': nrm(ks[19], (N_ODD, SWA_OUT, D_MODEL), SWA_OUT ** -0.5),
        'swa_q_norm': gain(ks[20], (N_ODD, SWA_DH)),
        'swa_k_norm': gain(ks[21], (N_ODD, SWA_DH)),
        'swa_sink': nrm(ks[22], (N_ODD, SWA_HEADS), 0.5),
        'router_w': nrm(ks[23], (DEPTH, D_MODEL, N_EXPERTS), D_MODEL ** -0.5),
        'router_bias': nrm(ks[24], (DEPTH, N_EXPERTS), 0.01),
        'exp_w1': nrm(ks[25], (DEPTH, N_EXPERTS, D_MODEL, EXPERT_FF), D_MODEL ** -0.5),
        'exp_w3': nrm(ks[26], (DEPTH, N_EXPERTS, D_MODEL, EXPERT_FF), D_MODEL ** -0.5),
        'exp_w2': nrm(ks[27], (DEPTH, N_EXPERTS, EXPERT_FF, D_MODEL), EXPERT_FF ** -0.5),
        'shared_w1': nrm(ks[28], (DEPTH, D_MODEL, SHARED_FF), D_MODEL ** -0.5),
        'shared_w3': nrm(ks[29], (DEPTH, D_MODEL, SHARED_FF), D_MODEL ** -0.5),
        'shared_w2': nrm(ks[30], (DEPTH, SHARED_FF, D_MODEL), SHARED_FF ** -0.5),
    }


def reference(x, c, ctx, c_ctx, mod_w, mod_b, norm_mix, norm_ffn, ab_w_in, ab_w_out, diff_q_norm,
              diff_k_norm, diff_lambda, diff_subln, dn_conv, dn_a_log, dn_dt_bias, dn_out_norm, swa_w_in,
              swa_w_out, swa_q_norm, swa_k_norm, swa_sink, router_w, router_bias, exp_w1, exp_w3, exp_w2,
              shared_w1, shared_w3, shared_w2):
    b, t, d = x.shape
    n_c = ctx.shape[1]
    silu_c = jax.nn.silu(c)
    silu_cc = jax.nn.silu(c_ctx)
    for layer in range(DEPTH):
        with_ctx = layer < DEPTH - 1
        p = layer // 2
        mod_x = (silu_c @ mod_w[layer] + mod_b[layer])[:, None, :]
        mod_c = (silu_cc @ mod_w[layer] + mod_b[layer])[None, None, :]
        sh1x, sc1x, g1x, sh2x, sc2x, g2x = jnp.split(mod_x, 6, axis=-1)
        sh1c, sc1c, g1c, sh2c, sc2c, g2c = jnp.split(mod_c, 6, axis=-1)
        hx = modulate(rmsnorm(x, norm_mix[layer]), sh1x, sc1x)
        hc = modulate(rmsnorm(ctx, norm_mix[layer]), sh1c, sc1c)
        if layer % 2 == 0:
            yc, yx = mixer_ab(hc, hx, layer, ab_w_in[p], ab_w_out[p], diff_q_norm[p], diff_k_norm[p],
                              diff_lambda[p], diff_subln[p], dn_conv[p], dn_a_log[p], dn_dt_bias[p],
                              dn_out_norm[p], with_ctx)
        else:
            yc, yx = mixer_swa(hc, hx, swa_w_in[p], swa_w_out[p], swa_q_norm[p], swa_k_norm[p],
                               swa_sink[p], with_ctx)
        x = x + g1x * yx
        hx = modulate(rmsnorm(x, norm_ffn[layer]), sh2x, sc2x)
        moe_args = (router_w[layer], router_bias[layer], exp_w1[layer], exp_w3[layer], exp_w2[layer],
                    shared_w1[layer], shared_w3[layer], shared_w2[layer])
        if with_ctx:
            ctx = ctx + g1c * yc
            hc = modulate(rmsnorm(ctx, norm_ffn[layer]), sh2c, sc2c)
            f = moe(jnp.concatenate([hc.reshape(b * n_c, d), hx.reshape(b * t, d)], axis=0), *moe_args)
            ctx = ctx + g2c * f[:b * n_c].reshape(b, n_c, d)
            x = x + g2x * f[b * n_c:].reshape(b, t, d)
        else:
            x = x + g2x * moe(hx.reshape(b * t, d), *moe_args).reshape(b, t, d)
    return x
```

```python
import functools
import math

import jax
import jax.numpy as jnp
from jax import lax
from jax.experimental import pallas as pl
from jax.experimental.pallas import tpu as pltpu

F32 = jnp.float32
BF16 = jnp.bfloat16
I32 = jnp.int32
U32 = jnp.uint32

EPS = 1e-6
GRID_W = 64
ROPE_BASE = 10000.0
DIFF_HEADS = 4
DIFF_DK = 128
DIFF_DV = 256
DN_HEADS = 8
DN_DK = 128
DN_DV = 128
DN_CONV = 5
DN_CHUNK = 128
SWA_HEADS = 32
SWA_KV_HEADS = 4
SWA_GROUP = SWA_HEADS // SWA_KV_HEADS
SWA_DH = 64
WINDOW = 128
Q_BLOCK = 128
N_EXPERTS = 64
TOP_K = 8
N_GROUPS = 8
GROUP_SIZE = N_EXPERTS // N_GROUPS
TOPK_GROUPS = 4
ROUTED_SCALE = 2.5
MOE_BLOCK = 256

LANES = 128
SUBLANES = 8
VMEM_LIMIT_BYTES = 56 * 1024 * 1024

NT_DIMS = (((1,), (1,)), ((), ()))


def _params(*semantics):
    return pltpu.CompilerParams(dimension_semantics=semantics, vmem_limit_bytes=VMEM_LIMIT_BYTES)


def _tile(n, pref, mult):
    if n <= pref:
        return n
    t = pref - pref % mult
    while t > mult and n % t:
        t -= mult
    assert n % t == 0, (n, pref, mult)
    return t


def _mm(a, b):
    return jnp.dot(a.astype(BF16), b.astype(BF16), preferred_element_type=F32)


def _split2(x):
    hi = x.astype(BF16)
    lo = (x - hi.astype(F32)).astype(BF16)
    return hi, lo


def _dot3(a, b, dims=None):
    if dims is None:
        dims = (((a.ndim - 1,), (0,)), ((), ()))
    ah, al = _split2(a)
    bh, bl = _split2(b)
    d = functools.partial(lax.dot_general, dimension_numbers=dims, preferred_element_type=F32)
    return d(ah, bh) + d(ah, bl) + d(al, bh)


def _silu(x):
    return x * jax.nn.sigmoid(x)


def _softplus(x):
    return jnp.maximum(x, 0.0) + jnp.log(1.0 + jnp.exp(-jnp.abs(x)))


def _pack_pairs(lo, hi):
    ulo = lax.bitcast_convert_type(lo.astype(BF16).astype(F32), U32) >> 16
    uhi = lax.bitcast_convert_type(hi.astype(BF16).astype(F32), U32) & jnp.uint32(0xFFFF0000)
    return ulo | uhi


def _unpack_pairs(u):
    lo = lax.bitcast_convert_type(u << 16, F32)
    hi = lax.bitcast_convert_type(u & jnp.uint32(0xFFFF0000), F32)
    return lo, hi


def _mod_kernel(a_ref, w_ref, b_ref, o_ref):
    o_ref[...] = _dot3(_silu(a_ref[...]), w_ref[...]) + b_ref[...]


def _modulation(a, w, b):
    r, d = a.shape
    n = w.shape[1]
    tn = _tile(n, 768, LANES)
    return pl.pallas_call(
        _mod_kernel,
        grid=(n // tn,),
        in_specs=[pl.BlockSpec((r, d), lambda j: (0, 0)),
                  pl.BlockSpec((d, tn), lambda j: (0, j)),
                  pl.BlockSpec((1, tn), lambda j: (0, j))],
        out_specs=pl.BlockSpec((r, tn), lambda j: (0, j)),
        out_shape=jax.ShapeDtypeStruct((r, n), F32),
        compiler_params=_params("parallel"),
        name="modulation",
    )(a, w, b.reshape(1, n))


def _norm_mod(x, nw, sh, sc):
    y = x * lax.rsqrt(jnp.mean(x * x, axis=-1, keepdims=True) + EPS) * nw
    return y * (1.0 + sc) + sh


def _nmm_kernel(x_ref, nw_ref, sh_ref, sc_ref, w_ref, *rest, has_small):
    if has_small:
        ws_ref, o_ref, os_ref, h_ref = rest
    else:
        o_ref, h_ref = rest

    @pl.when(pl.program_id(1) == 0)
    def _():
        h = _norm_mod(x_ref[...], nw_ref[...], sh_ref[0], sc_ref[0])
        h_ref[...] = h.astype(BF16)
        if has_small:
            os_ref[...] = _dot3(h, ws_ref[...])

    o_ref[...] = jnp.dot(h_ref[...], w_ref[...], preferred_element_type=F32).astype(o_ref.dtype)


def _norm_mod_matmul(x, nw, sh, sc, w, rows_per_group, w_small=None):
    m, d = x.shape
    n = w.shape[1]
    tm = math.gcd(_tile(m, 512, SUBLANES), rows_per_group)
    tn = _tile(n, 1024, LANES)
    has_small = w_small is not None
    grp = lambda i, j: ((i * tm) // rows_per_group, 0, 0)
    in_specs = [pl.BlockSpec((tm, d), lambda i, j: (i, 0)),
                pl.BlockSpec((1, d), lambda i, j: (0, 0)),
                pl.BlockSpec((1, 1, d), grp),
                pl.BlockSpec((1, 1, d), grp),
                pl.BlockSpec((d, tn), lambda i, j: (0, j))]
    args = [x, nw.reshape(1, d), sh, sc, w]
    out_specs = [pl.BlockSpec((tm, tn), lambda i, j: (i, j))]
    out_shape = [jax.ShapeDtypeStruct((m, n), BF16)]
    if has_small:
        in_specs.append(pl.BlockSpec((d, LANES), lambda i, j: (0, 0)))
        args.append(w_small)
        out_specs.append(pl.BlockSpec((tm, LANES), lambda i, j: (i, 0)))
        out_shape.append(jax.ShapeDtypeStruct((m, LANES), F32))
    outs = pl.pallas_call(
        functools.partial(_nmm_kernel, has_small=has_small),
        grid=(m // tm, n // tn),
        in_specs=in_specs,
        out_specs=out_specs,
        out_shape=out_shape,
        scratch_shapes=[pltpu.VMEM((tm, d), BF16)],
        compiler_params=_params("parallel", "arbitrary"),
        name="norm_mod_matmul",
    )(*args)
    return outs if has_small else outs[0]


def _rope_tables(t, head_dim):
    q = head_dim // 4
    pos = jnp.arange(t, dtype=I32)
    row = (pos // GRID_W).astype(F32)
    col = (pos % GRID_W).astype(F32)
    axis_dim = head_dim // 2
    inv_freq = ROPE_BASE ** (-jnp.arange(0, axis_dim, 2, dtype=F32) / axis_dim)
    lane = jnp.arange(LANES) % head_dim
    freq = inv_freq[lane % q]
    p = jnp.where((lane < head_dim // 2)[None, :], row[:, None], col[:, None])
    ang = p * freq[None, :]
    sign = jnp.where((lane % (2 * q)) < q, -1.0, 1.0)
    return jnp.cos(ang), jnp.sin(ang) * sign[None, :]


def _hnr_kernel(z_ref, w_ref, *rest, hd, scale, rope):
    if rope:
        cos_ref, sin_ref, o_ref = rest
    else:
        (o_ref,) = rest
    tm, cb = z_ref.shape
    lane = lax.broadcasted_iota(I32, (tm, LANES), 1)
    w = w_ref[...]
    q = hd // 4
    for g in range(cb // LANES):
        sl = slice(g * LANES, (g + 1) * LANES)
        x = z_ref[:, sl].astype(F32)
        x2 = x * x
        if hd == LANES:
            ms = jnp.mean(x2, axis=-1, keepdims=True)
        else:
            lo = jnp.sum(jnp.where(lane < hd, x2, 0.0), axis=-1, keepdims=True)
            hi = jnp.sum(jnp.where(lane >= hd, x2, 0.0), axis=-1, keepdims=True)
            ms = jnp.where(lane < hd, lo, hi) * (1.0 / hd)
        y = x * lax.rsqrt(ms + EPS) * w
        if rope:
            nxt = pltpu.roll(y, LANES - q, 1)
            prv = pltpu.roll(y, q, 1)
            partner = jnp.where((lane & (2 * q - 1)) < q, nxt, prv)
            y = y * cos_ref[...] + partner * sin_ref[...]
        o_ref[:, sl] = (y * scale).astype(o_ref.dtype)


def _headnorm_rope(z, col0, ncols, w, hd, scale, tables, t):
    m = z.shape[0]
    cb = min(512, ncols)
    assert ncols % cb == 0 and col0 % cb == 0 and LANES % hd == 0
    tm = _tile(t, 512, SUBLANES) if tables is not None else _tile(m, 512, SUBLANES)
    cblk0 = col0 // cb
    rope = tables is not None
    in_specs = [pl.BlockSpec((tm, cb), lambda i, j: (i, cblk0 + j)),
                pl.BlockSpec((1, LANES), lambda i, j: (0, 0))]
    args = [z, jnp.tile(w, LANES // hd).reshape(1, LANES)]
    if rope:
        nt = t // tm
        in_specs += [pl.BlockSpec((tm, LANES), lambda i, j: (i % nt, 0))] * 2
        args += list(tables)
    return pl.pallas_call(
        functools.partial(_hnr_kernel, hd=hd, scale=scale, rope=rope),
        grid=(m // tm, ncols // cb),
        in_specs=in_specs,
        out_specs=pl.BlockSpec((tm, cb), lambda i, j: (i, j)),
        out_shape=jax.ShapeDtypeStruct((m, ncols), BF16),
        compiler_params=_params("parallel", "parallel"),
        name="headnorm_rope",
    )(*args)


def _diff_attn_kernel(lv_ref, q_ref, *rest, lam_init, has_lat):
    if has_lat:
        kx_ref, vx_ref, kc_ref, vc_ref, w_ref, o_ref = rest
    else:
        kc_ref, vc_ref, w_ref, o_ref = rest
    lv = lv_ref[...]
    lam = (jnp.exp(jnp.sum(lv[0:1] * lv[1:2], keepdims=True))
           - jnp.exp(jnp.sum(lv[2:3] * lv[3:4], keepdims=True)) + lam_init)
    q = q_ref[...]

    def probs(c):
        sl = slice(c * DIFF_DK, (c + 1) * DIFF_DK)
        qc = q[:, sl]
        s_c = lax.dot_general(qc, kc_ref[:, sl], NT_DIMS, preferred_element_type=F32)
        m = jnp.max(s_c, axis=-1, keepdims=True)
        p_x = None
        if has_lat:
            s_x = lax.dot_general(qc, kx_ref[:, sl], NT_DIMS, preferred_element_type=F32)
            m = jnp.maximum(m, jnp.max(s_x, axis=-1, keepdims=True))
            p_x = jnp.exp(s_x - m)
        p_c = jnp.exp(s_c - m)
        l = jnp.sum(p_c, axis=-1, keepdims=True)
        if has_lat:
            l = l + jnp.sum(p_x, axis=-1, keepdims=True)
        return p_x, p_c, 1.0 / l

    p1x, p1c, r1 = probs(0)
    p2x, p2c, r2 = probs(1)
    r2 = r2 * lam
    o = jnp.dot((p1c * r1 - p2c * r2).astype(BF16), vc_ref[...], preferred_element_type=F32)
    if has_lat:
        o = o + jnp.dot((p1x * r1 - p2x * r2).astype(BF16), vx_ref[...], preferred_element_type=F32)
    o = o * lax.rsqrt(jnp.mean(o * o, axis=-1, keepdims=True) + EPS)
    o_ref[...] = (o * w_ref[...] * (1.0 - lam_init)).astype(o_ref.dtype)


def _diff_attention(lam_vec, q, kc, zc, subln, lam_init, b, n_c, kx=None, zx=None, t=None):
    has_lat = kx is not None
    hw = 2 * DIFF_DK
    vblk0 = (2 * DIFF_HEADS * hw) // DIFF_DV
    tq_all = t if has_lat else n_c
    tq = _tile(tq_all, 256, SUBLANES)
    nq = tq_all // tq
    in_specs = [pl.BlockSpec((4, DIFF_DK), lambda bi, h, qi: (0, 0)),
                pl.BlockSpec((tq, hw), lambda bi, h, qi: (bi * nq + qi, h))]
    args = [lam_vec, q]
    if has_lat:
        in_specs += [pl.BlockSpec((t, hw), lambda bi, h, qi: (bi, h)),
                     pl.BlockSpec((t, DIFF_DV), lambda bi, h, qi: (bi, vblk0 + h))]
        args += [kx, zx]
    in_specs += [pl.BlockSpec((n_c, hw), lambda bi, h, qi: (bi, h)),
                 pl.BlockSpec((n_c, DIFF_DV), lambda bi, h, qi: (bi, vblk0 + h)),
                 pl.BlockSpec((1, DIFF_DV), lambda bi, h, qi: (0, 0))]
    args += [kc, zc, subln.reshape(1, DIFF_DV)]
    return pl.pallas_call(
        functools.partial(_diff_attn_kernel, lam_init=lam_init, has_lat=has_lat),
        grid=(b, DIFF_HEADS, nq),
        in_specs=in_specs,
        out_specs=pl.BlockSpec((tq, DIFF_DV), lambda bi, h, qi: (bi * nq + qi, h)),
        out_shape=jax.ShapeDtypeStruct((b * tq_all, DIFF_HEADS * DIFF_DV), BF16),
        compiler_params=_params("parallel", "parallel", "arbitrary"),
        name="diff_attention",
    )(*args)


def _dn_prep_kernel(z_ref, cw_ref, o_ref, pad_ref, *, seg):
    halo = SUBLANES
    pad_ref[0:halo, :] = jnp.zeros((halo, LANES), F32)
    pad_ref[halo + seg:2 * halo + seg, :] = jnp.zeros((halo, LANES), F32)
    pad_ref[halo:halo + seg, :] = z_ref[...].astype(F32)
    kind = pl.program_id(1) // DN_HEADS
    qk_scale = jnp.where(kind == 0, DN_DK ** -0.5, 1.0).astype(F32)
    rows = _tile(seg, 256, SUBLANES)
    for r0 in range(0, seg, rows):
        acc = jnp.zeros((rows, LANES), F32)
        for j in range(DN_CONV):
            acc = acc + cw_ref[j:j + 1, :] * pad_ref[pl.ds(halo + r0 + j - DN_CONV // 2, rows), :]
        y = _silu(acc)
        nrm = y * lax.rsqrt(jnp.sum(y * y, axis=-1, keepdims=True) + EPS) * qk_scale
        o_ref[r0:r0 + rows, :] = jnp.where(kind < 2, nrm, y).astype(o_ref.dtype)


def _dn_prep(z, col0, conv_w, seg):
    m = z.shape[0]
    ncols = conv_w.shape[1]
    cblk0 = col0 // LANES
    cw = jnp.zeros((SUBLANES, ncols), F32).at[:DN_CONV].set(conv_w)
    return pl.pallas_call(
        functools.partial(_dn_prep_kernel, seg=seg),
        grid=(m // seg, ncols // LANES),
        in_specs=[pl.BlockSpec((seg, LANES), lambda s, g: (s, cblk0 + g)),
                  pl.BlockSpec((SUBLANES, LANES), lambda s, g: (0, g))],
        out_specs=pl.BlockSpec((seg, LANES), lambda s, g: (s, g)),
        out_shape=jax.ShapeDtypeStruct((m, ncols), BF16),
        scratch_shapes=[pltpu.VMEM((seg + 2 * SUBLANES, LANES), F32)],
        compiler_params=_params("parallel", "parallel"),
        name="dn_prep",
    )(z, cw)


def _deltanet_kernel(q_ref, k_ref, v_ref, zs_ref, zst_ref, pr_ref, pca_ref, pcd_ref, s0_ref,
                     o_ref, sout_ref, s_ref, *, reverse, n_chunks):
    step = pl.program_id(1)
    c = DN_CHUNK

    @pl.when(step == 0)
    def _():
        s_ref[...] = s0_ref[0]

    ri = lax.broadcasted_iota(I32, (c, c), 0)
    ci = lax.broadcasted_iota(I32, (c, c), 1)
    if reverse:
        later, strict, later_t = ri <= ci, ri < ci, ri >= ci
    else:
        later, strict, later_t = ri >= ci, ri > ci, ri <= ci
    eye = (ri == ci).astype(F32)
    tri = later.astype(BF16)
    tri_t = later_t.astype(BF16)

    zs = zs_ref[...]
    beta_cols = jax.nn.sigmoid(zs)
    g_cols = -jnp.exp(pr_ref[0:1, :]) * _softplus(zs + pr_ref[1:2, :])
    g_hi = g_cols.astype(BF16)
    g_r1 = g_cols - g_hi.astype(F32)
    g_mid = g_r1.astype(BF16)
    g_lo = (g_r1 - g_mid.astype(F32)).astype(BF16)
    d = functools.partial(jnp.dot, preferred_element_type=F32)
    gc_cols = d(tri, g_hi) + d(tri, g_mid) + d(tri, g_lo)
    g_rows = -jnp.exp(pca_ref[...]) * _softplus(zst_ref[...] + pcd_ref[...])
    h_hi = g_rows.astype(BF16)
    h_r1 = g_rows - h_hi.astype(F32)
    h_mid = h_r1.astype(BF16)
    h_lo = (h_r1 - h_mid.astype(F32)).astype(BF16)
    gc_rows = d(h_hi, tri_t) + d(h_mid, tri_t) + d(h_lo, tri_t)

    dir_off = DN_HEADS if reverse else 0
    last = 0 if reverse else c - 1
    neg_inf = jnp.float32(-jnp.inf)
    for h in range(DN_HEADS):
        sl = slice(h * DN_DK, (h + 1) * DN_DK)
        cb = dir_off + h
        cg = 2 * DN_HEADS + dir_off + h
        beta = beta_cols[:, cb:cb + 1]
        gcol = gc_cols[:, cg:cg + 1]
        grow = gc_rows[cg:cg + 1, :]
        glast = grow[:, last:last + 1]
        q = q_ref[:, sl]
        k = k_ref[:, sl]
        kf = k.astype(F32)
        vf = v_ref[:, sl].astype(F32)
        decay = jnp.exp(jnp.where(later, gcol - grow, neg_inf))
        kb = kf * beta
        both = lax.dot_general(jnp.concatenate([kb.astype(BF16), q], axis=0), k, NT_DIMS,
                               preferred_element_type=F32)
        lmat = jnp.where(strict, both[:c] * decay, 0.0)
        amat = both[c:] * decay
        inv = eye - jnp.where((ri >> 1) == (ci >> 1), lmat, 0.0)
        lev = 1
        while (1 << lev) < c:
            off = jnp.where(((ri >> (lev + 1)) == (ci >> (lev + 1))) & ((ri >> lev) != (ci >> lev)),
                            lmat, 0.0)
            inv = inv - _mm(_mm(inv, off), inv)
            lev += 1
        eg = jnp.exp(gcol)
        uw = _mm(inv, jnp.concatenate([vf * beta, kb * eg], axis=1))
        u = uw[:, :DN_DV]
        w = uw[:, DN_DV:]
        s = s_ref[h]
        ws_qs = _mm(jnp.concatenate([w, q.astype(F32) * eg], axis=0), s)
        v_new = u - ws_qs[:c]
        o_ref[:, sl] = ws_qs[c:] + _mm(amat, v_new)
        kdec = kf * jnp.exp(glast - gcol)
        s_ref[h] = s * jnp.exp(glast) + _mm(kdec.T, v_new)

    @pl.when(step == n_chunks - 1)
    def _():
        sout_ref[0] = s_ref[...]


def _deltanet(dn, zs, zst, prm, s0, b, seg, reverse):
    pr, pca, pcd = prm
    n = seg // DN_CHUNK
    hw = DN_HEADS * DN_DK
    rb = (lambda bi, s: bi * n + (n - 1 - s)) if reverse else (lambda bi, s: bi * n + s)
    state = pl.BlockSpec((1, DN_HEADS, DN_DK, DN_DV), lambda bi, s: (bi, 0, 0, 0))
    return pl.pallas_call(
        functools.partial(_deltanet_kernel, reverse=reverse, n_chunks=n),
        grid=(b, n),
        in_specs=[pl.BlockSpec((DN_CHUNK, hw), lambda bi, s: (rb(bi, s), 0)),
                  pl.BlockSpec((DN_CHUNK, hw), lambda bi, s: (rb(bi, s), 1)),
                  pl.BlockSpec((DN_CHUNK, hw), lambda bi, s: (rb(bi, s), 2)),
                  pl.BlockSpec((DN_CHUNK, LANES), lambda bi, s: (rb(bi, s), 0)),
                  pl.BlockSpec((4 * DN_HEADS, DN_CHUNK), lambda bi, s: (0, rb(bi, s))),
                  pl.BlockSpec((SUBLANES, LANES), lambda bi, s: (0, 0)),
                  pl.BlockSpec((4 * DN_HEADS, LANES), lambda bi, s: (0, 0)),
                  pl.BlockSpec((4 * DN_HEADS, LANES), lambda bi, s: (0, 0)),
                  state],
        out_specs=[pl.BlockSpec((DN_CHUNK, hw), lambda bi, s: (rb(bi, s), 0)), state],
        out_shape=[jax.ShapeDtypeStruct((b * seg, hw), F32),
                   jax.ShapeDtypeStruct((b, DN_HEADS, DN_DK, DN_DV), F32)],
        scratch_shapes=[pltpu.VMEM((DN_HEADS, DN_DK, DN_DV), F32)],
        compiler_params=_params("parallel", "arbitrary"),
        name="deltanet_bwd" if reverse else "deltanet_fwd",
    )(dn, dn, dn, zs, zst, pr, pca, pcd, s0)


def _deltanet_params(a_log, dt_bias):
    nh = 2 * DN_HEADS
    a = a_log.reshape(nh).astype(F32)
    dtb = dt_bias.reshape(nh).astype(F32)
    pr = jnp.zeros((SUBLANES, LANES), F32).at[0, nh:2 * nh].set(a).at[1, nh:2 * nh].set(dtb)
    pca = jnp.zeros((2 * nh, LANES), F32).at[nh:].set(jnp.broadcast_to(a[:, None], (nh, LANES)))
    pcd = jnp.zeros((2 * nh, LANES), F32).at[nh:].set(jnp.broadcast_to(dtb[:, None], (nh, LANES)))
    return pr, pca, pcd


def _outproj_ab_kernel(od_ref, of_ref, ob_ref, gate_ref, nw_ref, w_ref, x_ref, g_ref, o_ref, h_ref):
    @pl.when(pl.program_id(1) == 0)
    def _():
        nd = od_ref.shape[1]
        h_ref[:, :nd] = od_ref[...]
        for h in range(DN_HEADS):
            sl = slice(h * DN_DV, (h + 1) * DN_DV)
            y = of_ref[:, sl] + ob_ref[:, sl]
            y = y * lax.rsqrt(jnp.mean(y * y, axis=-1, keepdims=True) + EPS) * nw_ref[...]
            h_ref[:, nd + h * DN_DV:nd + (h + 1) * DN_DV] = (y * _silu(gate_ref[:, sl].astype(F32))).astype(BF16)

    y = jnp.dot(h_ref[...], w_ref[...], preferred_element_type=F32)
    o_ref[...] = x_ref[...] + g_ref[0] * y


def _outproj_ab(od, o_f, o_b, z, gate_col0, out_norm, w, x, gate, rows_per_group):
    m, d = x.shape
    nd, nn = od.shape[1], o_f.shape[1]
    tm = math.gcd(_tile(m, 512, SUBLANES), rows_per_group)
    tn = _tile(d, 1024, LANES)
    grp = lambda i, j: ((i * tm) // rows_per_group, 0, j)
    gblk = gate_col0 // nn
    return pl.pallas_call(
        _outproj_ab_kernel,
        grid=(m // tm, d // tn),
        in_specs=[pl.BlockSpec((tm, nd), lambda i, j: (i, 0)),
                  pl.BlockSpec((tm, nn), lambda i, j: (i, 0)),
                  pl.BlockSpec((tm, nn), lambda i, j: (i, 0)),
                  pl.BlockSpec((tm, nn), lambda i, j: (i, gblk)),
                  pl.BlockSpec((1, DN_DV), lambda i, j: (0, 0)),
                  pl.BlockSpec((nd + nn, tn), lambda i, j: (0, j)),
                  pl.BlockSpec((tm, tn), lambda i, j: (i, j)),
                  pl.BlockSpec((1, 1, tn), grp)],
        out_specs=pl.BlockSpec((tm, tn), lambda i, j: (i, j)),
        out_shape=jax.ShapeDtypeStruct((m, d), F32),
        scratch_shapes=[pltpu.VMEM((tm, nd + nn), BF16)],
        compiler_params=_params("parallel", "arbitrary"),
        name="outproj_ab",
    )(od, o_f, o_b, z, out_norm.reshape(1, DN_DV), w, x, gate)


def _outproj_kernel(a_ref, w_ref, x_ref, g_ref, o_ref):
    o_ref[...] = x_ref[...] + g_ref[0] * jnp.dot(a_ref[...], w_ref[...], preferred_element_type=F32)


def _outproj(a, w, x, gate, rows_per_group):
    m, d = x.shape
    kdim = a.shape[1]
    tm = math.gcd(_tile(m, 512, SUBLANES), rows_per_group)
    tn = _tile(d, 1024, LANES)
    grp = lambda i, j: ((i * tm) // rows_per_group, 0, j)
    return pl.pallas_call(
        _outproj_kernel,
        grid=(m // tm, d // tn),
        in_specs=[pl.BlockSpec((tm, kdim), lambda i, j: (i, 0)),
                  pl.BlockSpec((kdim, tn), lambda i, j: (0, j)),
                  pl.BlockSpec((tm, tn), lambda i, j: (i, j)),
                  pl.BlockSpec((1, 1, tn), grp)],
        out_specs=pl.BlockSpec((tm, tn), lambda i, j: (i, j)),
        out_shape=jax.ShapeDtypeStruct((m, d), F32),
        compiler_params=_params("parallel", "arbitrary"),
        name="outproj",
    )(a, w, x, gate)


def _swa_kernel(sink_ref, q_ref, k0_ref, k1_ref, k2_ref, v0_ref, v1_ref, v2_ref, kc_ref, vc_ref, o_ref,
                *, t):
    kvh = pl.program_id(1)
    i = pl.program_id(2)
    qb = Q_BLOCK
    npair = SWA_GROUP // 2
    lane = lax.broadcasted_iota(I32, (qb, LANES), 1)
    q = q_ref[...]
    parts = []
    for p in range(npair):
        qp = q[:, p * LANES:(p + 1) * LANES]
        parts.append(jnp.where(lane < SWA_DH, qp, jnp.zeros_like(qp)))
        parts.append(jnp.where(lane >= SWA_DH, qp, jnp.zeros_like(qp)))
    qq = jnp.concatenate(parts, axis=0)
    k_lat = jnp.concatenate([k0_ref[...], k1_ref[...], k2_ref[...]], axis=0)
    v_lat = jnp.concatenate([v0_ref[...], v1_ref[...], v2_ref[...]], axis=0)
    rows = SWA_GROUP * qb
    band = 3 * qb
    s_lat = lax.dot_general(qq, k_lat, NT_DIMS, preferred_element_type=F32)
    s_ctx = lax.dot_general(qq, kc_ref[...], NT_DIMS, preferred_element_type=F32)
    r_io = lax.broadcasted_iota(I32, (rows, band), 0)
    c_io = lax.broadcasted_iota(I32, (rows, band), 1)
    qpos = i * qb + (r_io & (qb - 1))
    kpos = (i - 1) * qb + c_io
    valid = (jnp.abs(qpos - kpos) <= WINDOW) & (kpos >= 0) & (kpos < t)
    s_lat = jnp.where(valid, s_lat, -jnp.inf)
    head = lax.broadcasted_iota(I32, (rows, 1), 0) >> (qb.bit_length() - 1)
    sink = jnp.zeros((rows, 1), F32)
    for g in range(SWA_GROUP):
        sink = jnp.where(head == g, sink_ref[kvh, g], sink)
    m = jnp.maximum(jnp.maximum(jnp.max(s_lat, axis=-1, keepdims=True),
                                jnp.max(s_ctx, axis=-1, keepdims=True)), sink)
    p_lat = jnp.exp(s_lat - m)
    p_ctx = jnp.exp(s_ctx - m)
    l = (jnp.sum(p_lat, axis=-1, keepdims=True) + jnp.sum(p_ctx, axis=-1, keepdims=True)
         + jnp.exp(sink - m))
    r = 1.0 / l
    o = (jnp.dot((p_lat * r).astype(BF16), v_lat, preferred_element_type=F32)
         + jnp.dot((p_ctx * r).astype(BF16), vc_ref[...], preferred_element_type=F32))
    for p in range(npair):
        lo = o[(2 * p) * qb:(2 * p + 1) * qb]
        hi = o[(2 * p + 1) * qb:(2 * p + 2) * qb]
        o_ref[:, p * LANES:(p + 1) * LANES] = jnp.where(lane < SWA_DH, lo, hi).astype(o_ref.dtype)


def _swa_attention(sink, q, kx, vx, kc, vc, b, t, n_c):
    qb = Q_BLOCK
    nb = t // qb
    gw = SWA_GROUP * SWA_DH
    lat = lambda off: pl.BlockSpec(
        (qb, LANES), lambda bi, h, i: (bi * nb + jnp.clip(i + off, 0, nb - 1), h))
    ctx = pl.BlockSpec((n_c, LANES), lambda bi, h, i: (bi, h))
    return pl.pallas_call(
        functools.partial(_swa_kernel, t=t),
        grid=(b, SWA_KV_HEADS, nb),
        in_specs=[pl.BlockSpec(memory_space=pltpu.SMEM),
                  pl.BlockSpec((qb, gw), lambda bi, h, i: (bi * nb + i, h)),
                  lat(-1), lat(0), lat(1), lat(-1), lat(0), lat(1), ctx, ctx],
        out_specs=pl.BlockSpec((qb, gw), lambda bi, h, i: (bi * nb + i, h)),
        out_shape=jax.ShapeDtypeStruct((b * t, SWA_HEADS * SWA_DH), BF16),
        compiler_params=_params("parallel", "parallel", "arbitrary"),
        name="swa_attention",
    )(sink, q, kx, kx, kx, vx, vx, vx, kc, vc)


def _dup_heads(a, col0):
    m = a.shape[0]
    h = a[:, col0:col0 + SWA_KV_HEADS * SWA_DH].reshape(m, SWA_KV_HEADS, 1, SWA_DH)
    return jnp.broadcast_to(h, (m, SWA_KV_HEADS, LANES // SWA_DH, SWA_DH)).reshape(m, SWA_KV_HEADS * LANES)


def _first_max(vals, iota, size, axis):
    m = jnp.max(vals, axis=axis, keepdims=True)
    first = jnp.min(jnp.where(vals == m, iota, size), axis=axis, keepdims=True)
    return m, first


def _router_kernel(x_ref, nw_ref, sh_ref, sc_ref, rwt_ref, rb_ref, c0_ref,
                   hp_ref, idx_ref, wt_ref, rank_ref, cnt_ref, carry_ref):
    @pl.when(pl.program_id(0) == 0)
    def _():
        carry_ref[...] = c0_ref[...]

    h = _norm_mod(x_ref[...], nw_ref[...], sh_ref[0], sc_ref[0])
    tm, d = h.shape
    hp_ref[...] = _pack_pairs(h[:, :d // 2], h[:, d // 2:])
    scores = jax.nn.sigmoid(_dot3(rwt_ref[...], h, NT_DIMS))
    sel = scores + rb_ref[...]
    neg = jnp.float32(-jnp.inf)

    g_io = lax.broadcasted_iota(I32, (GROUP_SIZE, tm), 0)
    gs_rows = []
    for g in range(N_GROUPS):
        sg = sel[g * GROUP_SIZE:(g + 1) * GROUP_SIZE]
        m1, f1 = _first_max(sg, g_io, GROUP_SIZE, 0)
        m2 = jnp.max(jnp.where(g_io == f1, neg, sg), axis=0, keepdims=True)
        gs_rows.append(m1 + m2)
    cur = jnp.concatenate(gs_rows, axis=0)
    n_io = lax.broadcasted_iota(I32, (N_GROUPS, tm), 0)
    gmask = jnp.zeros((N_GROUPS, tm), I32)
    for _ in range(TOPK_GROUPS):
        _, f = _first_max(cur, n_io, N_GROUPS, 0)
        hit = n_io == f
        gmask = jnp.where(hit, 1, gmask)
        cur = jnp.where(hit, neg, cur)
    cur = jnp.concatenate(
        [jnp.where(gmask[g:g + 1] > 0, sel[g * GROUP_SIZE:(g + 1) * GROUP_SIZE], neg) for g in range(N_GROUPS)],
        axis=0)

    e_io = lax.broadcasted_iota(I32, (N_EXPERTS, tm), 0)
    chosen = jnp.zeros((N_EXPERTS, tm), F32)
    idx_rows, w_rows = [], []
    for _ in range(TOP_K):
        _, f = _first_max(cur, e_io, N_EXPERTS, 0)
        hit = e_io == f
        idx_rows.append(f)
        w_rows.append(jnp.sum(jnp.where(hit, scores, 0.0), axis=0, keepdims=True))
        chosen = jnp.where(hit, 1.0, chosen)
        cur = jnp.where(hit, neg, cur)
    idx = jnp.concatenate(idx_rows, axis=0)
    w = jnp.concatenate(w_rows, axis=0)
    idx_ref[...] = idx
    wt_ref[...] = w * (1.0 / jnp.sum(w, axis=0, keepdims=True)) * ROUTED_SCALE

    onehot = chosen.astype(BF16)
    before = (lax.broadcasted_iota(I32, (tm, tm), 0) < lax.broadcasted_iota(I32, (tm, tm), 1)).astype(BF16)
    base = carry_ref[:, 0:1] + jnp.dot(onehot, before, preferred_element_type=F32)
    rank_ref[...] = jnp.concatenate(
        [jnp.sum(jnp.where(e_io == idx_rows[k], base, 0.0), axis=0, keepdims=True) for k in range(TOP_K)],
        axis=0).astype(I32)
    carry_ref[...] = carry_ref[...] + jnp.sum(chosen, axis=1, keepdims=True)
    cnt_ref[...] = carry_ref[...]


def _router(x, nw, sh, sc, rwt, rb, counts0, rows_per_group):
    m, d = x.shape
    tm = math.gcd(_tile(m, 256, LANES), rows_per_group)
    grp = lambda i: ((i * tm) // rows_per_group, 0, 0)
    tok = lambda rows: pl.BlockSpec((rows, tm), lambda i: (0, i))
    return pl.pallas_call(
        _router_kernel,
        grid=(m // tm,),
        in_specs=[pl.BlockSpec((tm, d), lambda i: (i, 0)),
                  pl.BlockSpec((1, d), lambda i: (0, 0)),
                  pl.BlockSpec((1, 1, d), grp),
                  pl.BlockSpec((1, 1, d), grp),
                  pl.BlockSpec((N_EXPERTS, d), lambda i: (0, 0)),
                  pl.BlockSpec((N_EXPERTS, 1), lambda i: (0, 0)),
                  pl.BlockSpec((N_EXPERTS, LANES), lambda i: (0, 0))],
        out_specs=[pl.BlockSpec((tm, d // 2), lambda i: (i, 0)),
                   tok(TOP_K), tok(TOP_K), tok(TOP_K),
                   pl.BlockSpec((N_EXPERTS, LANES), lambda i: (0, 0))],
        out_shape=[jax.ShapeDtypeStruct((m, d // 2), U32),
                   jax.ShapeDtypeStruct((TOP_K, m), I32),
                   jax.ShapeDtypeStruct((TOP_K, m), F32),
                   jax.ShapeDtypeStruct((TOP_K, m), I32),
                   jax.ShapeDtypeStruct((N_EXPERTS, LANES), F32)],
        scratch_shapes=[pltpu.VMEM((N_EXPERTS, LANES), F32)],
        compiler_params=_params("arbitrary"),
        name="moe_router",
    )(x, nw.reshape(1, d), sh, sc, rwt, rb.reshape(N_EXPERTS, 1), counts0)


def _row_copy(src, src_row, dst, dst_row, sem):
    return pltpu.make_async_copy(src.at[pl.ds(src_row, 1)], dst.at[pl.ds(dst_row, 1)], sem)


def _dispatch_kernel(nv_ref, dest_ref, hp_ref, xs_ref, zero_ref, sem):
    tm = hp_ref.shape[0]

    @pl.when(pl.program_id(0) == 0)
    def _():
        zero_ref[...] = jnp.zeros(zero_ref.shape, U32)

        def fill(blk, carry):
            @pl.when(nv_ref[blk] < MOE_BLOCK)
            def _():
                cp = pltpu.make_async_copy(
                    zero_ref, xs_ref.at[pl.ds(blk * MOE_BLOCK, MOE_BLOCK)], sem)
                cp.start()
                cp.wait()
            return carry

        lax.fori_loop(0, nv_ref.shape[0], fill, 0)

    def start(t, carry):
        for k in range(TOP_K):
            _row_copy(hp_ref, t, xs_ref, dest_ref[k, t], sem).start()
        return carry

    def wait(t, carry):
        for k in range(TOP_K):
            _row_copy(hp_ref, t, xs_ref, dest_ref[k, t], sem).wait()
        return carry

    lax.fori_loop(0, tm, start, 0)
    lax.fori_loop(0, tm, wait, 0)


def _dispatch(block_nv, dest, hp, n_rows):
    m, dh = hp.shape
    tm = _tile(m, 256, LANES)
    return pl.pallas_call(
        _dispatch_kernel,
        grid_spec=pltpu.PrefetchScalarGridSpec(
            num_scalar_prefetch=1,
            grid=(m // tm,),
            in_specs=[pl.BlockSpec((TOP_K, tm), lambda i, nv: (0, i), memory_space=pltpu.SMEM),
                      pl.BlockSpec((tm, dh), lambda i, nv: (i, 0))],
            out_specs=pl.BlockSpec(memory_space=pl.ANY),
            scratch_shapes=[pltpu.VMEM((MOE_BLOCK, dh), U32), pltpu.SemaphoreType.DMA(())]),
        out_shape=jax.ShapeDtypeStruct((n_rows, dh), U32),
        compiler_params=_params("arbitrary"),
        name="moe_dispatch",
    )(block_nv, dest, hp)


def _gffn_kernel(be_ref, nv_ref, xs_ref, w1_ref, w3_ref, w2_ref, ys_ref, w1b_ref, w3b_ref, w2b_ref):
    i = pl.program_id(0)
    nv = nv_ref[i]
    prev = be_ref[jnp.maximum(i - 1, 0)]

    @pl.when((i == 0) | (be_ref[i] != prev))
    def _():
        w1b_ref[...] = w1_ref[0].astype(BF16)
        w3b_ref[...] = w3_ref[0].astype(BF16)
        w2b_ref[...] = w2_ref[0].astype(BF16)

    @pl.when(nv > 0)
    def _():
        lo, hi = _unpack_pairs(xs_ref[...])
        dh = lo.shape[1]
        lo = lo.astype(BF16)
        hi = hi.astype(BF16)
        d = functools.partial(jnp.dot, preferred_element_type=F32)
        a = d(lo, w1b_ref[:dh, :]) + d(hi, w1b_ref[dh:, :])
        g = d(lo, w3b_ref[:dh, :]) + d(hi, w3b_ref[dh:, :])
        y = d((_silu(a) * g).astype(BF16), w2b_ref[...])
        ys_ref[...] = _pack_pairs(y[:, :dh], y[:, dh:])

    @pl.when(nv == 0)
    def _():
        ys_ref[...] = jnp.zeros(ys_ref.shape, U32)


def _grouped_ffn(block_e, block_nv, xs, w1, w3, w2):
    n_rows, dh = xs.shape
    _, d, f = w1.shape
    nb = n_rows // MOE_BLOCK
    return pl.pallas_call(
        _gffn_kernel,
        grid_spec=pltpu.PrefetchScalarGridSpec(
            num_scalar_prefetch=2,
            grid=(nb,),
            in_specs=[pl.BlockSpec((MOE_BLOCK, dh), lambda i, be, nv: (i, 0)),
                      pl.BlockSpec((1, d, f), lambda i, be, nv: (be[i], 0, 0)),
                      pl.BlockSpec((1, d, f), lambda i, be, nv: (be[i], 0, 0)),
                      pl.BlockSpec((1, f, d), lambda i, be, nv: (be[i], 0, 0))],
            out_specs=pl.BlockSpec((MOE_BLOCK, dh), lambda i, be, nv: (i, 0)),
            scratch_shapes=[pltpu.VMEM((d, f), BF16), pltpu.VMEM((d, f), BF16), pltpu.VMEM((f, d), BF16)]),
        out_shape=jax.ShapeDtypeStruct((n_rows, dh), U32),
        compiler_params=_params("arbitrary"),
        name="moe_grouped_ffn",
    )(block_e, block_nv, xs, w1, w3, w2)


def _combine_kernel(dest_ref, x_ref, hp_ref, wt_ref, g_ref, ws1_ref, ws3_ref, ws2_ref, ys_ref,
                    o_ref, buf_ref, sem):
    tm, dh = hp_ref.shape

    def start(t, carry):
        for k in range(TOP_K):
            _row_copy(ys_ref, dest_ref[k, t], buf_ref.at[k], t, sem).start()
        return carry

    def wait(t, carry):
        for k in range(TOP_K):
            _row_copy(ys_ref, dest_ref[k, t], buf_ref.at[k], t, sem).wait()
        return carry

    lax.fori_loop(0, tm, start, 0)
    lo, hi = _unpack_pairs(hp_ref[...])
    lo = lo.astype(BF16)
    hi = hi.astype(BF16)
    d = functools.partial(jnp.dot, preferred_element_type=F32)
    a = d(lo, ws1_ref[:dh, :]) + d(hi, ws1_ref[dh:, :])
    g = d(lo, ws3_ref[:dh, :]) + d(hi, ws3_ref[dh:, :])
    shared = d((_silu(a) * g).astype(BF16), ws2_ref[...])
    lax.fori_loop(0, tm, wait, 0)
    acc_lo = jnp.zeros((tm, dh), F32)
    acc_hi = jnp.zeros((tm, dh), F32)
    for k in range(TOP_K):
        ylo, yhi = _unpack_pairs(buf_ref[k])
        wk = wt_ref[:, k:k + 1]
        acc_lo = acc_lo + wk * ylo
        acc_hi = acc_hi + wk * yhi
    o_ref[:, :dh] = x_ref[:, :dh] + g_ref[0][:, :dh] * (acc_lo + shared[:, :dh])
    o_ref[:, dh:] = x_ref[:, dh:] + g_ref[0][:, dh:] * (acc_hi + shared[:, dh:])


def _combine(dest, x, hp, wt, gate, ws1, ws3, ws2, ys, rows_per_group):
    m, d = x.shape
    dh = d // 2
    f = ws1.shape[1]
    tm = math.gcd(_tile(m, 128, LANES), rows_per_group)
    grp = lambda i: ((i * tm) // rows_per_group, 0, 0)
    return pl.pallas_call(
        _combine_kernel,
        grid=(m // tm,),
        in_specs=[pl.BlockSpec((TOP_K, tm), lambda i: (0, i), memory_space=pltpu.SMEM),
                  pl.BlockSpec((tm, d), lambda i: (i, 0)),
                  pl.BlockSpec((tm, dh), lambda i: (i, 0)),
                  pl.BlockSpec((tm, TOP_K), lambda i: (i, 0)),
                  pl.BlockSpec((1, 1, d), grp),
                  pl.BlockSpec((d, f), lambda i: (0, 0)),
                  pl.BlockSpec((d, f), lambda i: (0, 0)),
                  pl.BlockSpec((f, d), lambda i: (0, 0)),
                  pl.BlockSpec(memory_space=pl.ANY)],
        out_specs=pl.BlockSpec((tm, d), lambda i: (i, 0)),
        out_shape=jax.ShapeDtypeStruct((m, d), F32),
        scratch_shapes=[pltpu.VMEM((TOP_K, tm, dh), U32), pltpu.SemaphoreType.DMA(())],
        compiler_params=_params("arbitrary"),
        name="moe_combine",
    )(dest, x, hp, wt, gate, ws1, ws3, ws2, ys)


def _moe(streams, nw, rw, rb, w1, w3, w2, ws1, ws3, ws2):
    d = streams[0][0].shape[1]
    rwt = rw.T
    counts = jnp.zeros((N_EXPERTS, LANES), F32)
    routed = []
    for x, sh, sc, _, rpg in streams:
        hp, idx, wt, rank, counts = _router(x, nw, sh, sc, rwt, rb, counts, rpg)
        routed.append((hp, idx, wt, rank))
    n_assign = sum(s[0].shape[0] for s in streams) * TOP_K
    n_blocks = (n_assign + N_EXPERTS * (MOE_BLOCK - 1) + MOE_BLOCK - 1) // MOE_BLOCK
    cnt = counts[:, 0].astype(I32)
    padded = (cnt + MOE_BLOCK - 1) // MOE_BLOCK * MOE_BLOCK
    pad_end = jnp.cumsum(padded)
    pad_start = pad_end - padded
    bstart = jnp.arange(n_blocks, dtype=I32) * MOE_BLOCK
    block_e = jnp.minimum(jnp.searchsorted(pad_end, bstart, side='right'), N_EXPERTS - 1).astype(I32)
    block_nv = jnp.clip(cnt[block_e] - (bstart - pad_start[block_e]), 0, MOE_BLOCK).astype(I32)
    e_ar = jnp.arange(N_EXPERTS, dtype=I32)
    dests = [jnp.sum(jnp.where(idx[:, :, None] == e_ar, pad_start, 0), axis=-1) + rank
             for _, idx, _, rank in routed]
    if len(streams) > 1:
        hp_all = jnp.concatenate([r[0] for r in routed], axis=0)
        dest_all = jnp.concatenate(dests, axis=1)
    else:
        hp_all, dest_all = routed[0][0], dests[0]
    xs = _dispatch(block_nv, dest_all, hp_all, n_blocks * MOE_BLOCK)
    ys = _grouped_ffn(block_e, block_nv, xs, w1, w3, w2)
    ws1b, ws3b, ws2b = ws1.astype(BF16), ws3.astype(BF16), ws2.astype(BF16)
    return [_combine(dest, x, hp, wt.T, gate, ws1b, ws3b, ws2b, ys, rpg)
            for (x, _, _, gate, rpg), (hp, _, wt, _), dest in zip(streams, routed, dests)]


def _mixer_ab(xs, cs, mx, mc, b, t, n_c, layer, nw, w_in, w_out, q_norm, k_norm, lam_vec, subln, conv_w,
              a_log, dt_bias, out_norm):
    n_main = w_in.shape[1] - 4 * DN_HEADS
    assert n_main % LANES == 0
    w_main = w_in[:, :n_main].astype(BF16)
    w_small = jnp.zeros((w_in.shape[0], LANES), F32).at[:, :4 * DN_HEADS].set(w_in[:, n_main:])
    z_x, zs_x = _norm_mod_matmul(xs, nw, mx[0], mx[1], w_main, t, w_small)
    z_c, zs_c = _norm_mod_matmul(cs, nw, mc[0], mc[1], w_main, b * n_c, w_small)
    nqk = 2 * DIFF_HEADS * DIFF_DK
    tabs = _rope_tables(t, DIFF_DK)
    qscale = DIFF_DK ** -0.5
    qd_x = _headnorm_rope(z_x, 0, nqk, q_norm, DIFF_DK, qscale, tabs, t)
    kd_x = _headnorm_rope(z_x, nqk, nqk, k_norm, DIFF_DK, 1.0, tabs, t)
    qd_c = _headnorm_rope(z_c, 0, nqk, q_norm, DIFF_DK, qscale, None, t)
    kd_c = _headnorm_rope(z_c, nqk, nqk, k_norm, DIFF_DK, 1.0, None, t)
    lam_init = 0.8 - 0.6 * math.exp(-0.3 * layer)
    od_x = _diff_attention(lam_vec, qd_x, kd_c, z_c, subln, lam_init, b, n_c, kx=kd_x, zx=z_x, t=t)
    od_c = _diff_attention(lam_vec, qd_c, kd_c, z_c, subln, lam_init, b, n_c)
    dn_col0 = 2 * nqk + DIFF_HEADS * DIFF_DV
    dn_x = _dn_prep(z_x, dn_col0, conv_w, t)
    dn_c = _dn_prep(z_c, dn_col0, conv_w, n_c)
    zst_x = zs_x[:, :4 * DN_HEADS].T
    zst_c = zs_c[:, :4 * DN_HEADS].T
    prm = _deltanet_params(a_log, dt_bias)
    s0 = jnp.zeros((b, DN_HEADS, DN_DK, DN_DV), F32)
    o_cf, s_cf = _deltanet(dn_c, zs_c, zst_c, prm, s0, b, n_c, False)
    o_xf, _ = _deltanet(dn_x, zs_x, zst_x, prm, s_cf, b, t, False)
    o_cb, s_cb = _deltanet(dn_c, zs_c, zst_c, prm, s0, b, n_c, True)
    o_xb, _ = _deltanet(dn_x, zs_x, zst_x, prm, s_cb, b, t, True)
    gate_col0 = dn_col0 + conv_w.shape[1]
    w_out_b = w_out.astype(BF16)
    x1 = _outproj_ab(od_x, o_xf, o_xb, z_x, gate_col0, out_norm, w_out_b, xs, mx[2], t)
    c1 = _outproj_ab(od_c, o_cf, o_cb, z_c, gate_col0, out_norm, w_out_b, cs, mc[2], b * n_c)
    return x1, c1


def _mixer_swa(xs, cs, mx, mc, b, t, n_c, nw, w_in, w_out, q_norm, k_norm, sink):
    nq = SWA_HEADS * SWA_DH
    nkv = SWA_KV_HEADS * SWA_DH
    w_b = w_in.astype(BF16)
    z_x = _norm_mod_matmul(xs, nw, mx[0], mx[1], w_b, t)
    z_c = _norm_mod_matmul(cs, nw, mc[0], mc[1], w_b[:, nq:], b * n_c)
    tabs = _rope_tables(t, SWA_DH)
    q = _headnorm_rope(z_x, 0, nq, q_norm, SWA_DH, SWA_DH ** -0.5, tabs, t)
    k_x = _headnorm_rope(z_x, nq, nkv, k_norm, SWA_DH, 1.0, tabs, t)
    k_c = _headnorm_rope(z_c, 0, nkv, k_norm, SWA_DH, 1.0, None, t)
    att = _swa_attention(sink.reshape(SWA_KV_HEADS, SWA_GROUP), q, _dup_heads(k_x, 0), _dup_heads(z_x, nq + nkv),
                         _dup_heads(k_c, 0), _dup_heads(z_c, nkv), b, t, n_c)
    return _outproj(att, w_out.astype(BF16), xs, mx[2], t)


def kernel(x, c, ctx, c_ctx, mod_w, mod_b, norm_mix, norm_ffn, ab_w_in, ab_w_out, diff_q_norm, diff_k_norm, diff_lambda, diff_subln, dn_conv, dn_a_log, dn_dt_bias, dn_out_norm, swa_w_in, swa_w_out, swa_q_norm, swa_k_norm, swa_sink, router_w, router_bias, exp_w1, exp_w3, exp_w2, shared_w1, shared_w3, shared_w2):
    b, t, d = x.shape
    n_c = ctx.shape[1]
    depth = mod_w.shape[0]
    assert depth == 2 and t % Q_BLOCK == 0 and t % DN_CHUNK == 0 and n_c % DN_CHUNK == 0
    xs = x.reshape(b * t, d)
    cs = ctx.reshape(b * n_c, d)
    n_mod = -(-(b + 1) // SUBLANES) * SUBLANES
    a_mod = jnp.zeros((n_mod, d), F32).at[0].set(c_ctx).at[1:1 + b].set(c)
    for layer in range(depth):
        with_ctx = layer < depth - 1
        p = layer // 2
        mod = _modulation(a_mod, mod_w[layer], mod_b[layer])
        mc = [mod[0:1, j * d:(j + 1) * d].reshape(1, 1, d) for j in range(6)]
        mx = [mod[1:1 + b, j * d:(j + 1) * d].reshape(b, 1, d) for j in range(6)]
        if layer % 2 == 0:
            xs, c_new = _mixer_ab(xs, cs, mx, mc, b, t, n_c, layer, norm_mix[layer], ab_w_in[p], ab_w_out[p],
                                  diff_q_norm[p], diff_k_norm[p], diff_lambda[p], diff_subln[p], dn_conv[p],
                                  dn_a_log[p], dn_dt_bias[p], dn_out_norm[p])
        else:
            assert not with_ctx
            xs = _mixer_swa(xs, cs, mx, mc, b, t, n_c, norm_mix[layer], swa_w_in[p], swa_w_out[p],
                            swa_q_norm[p], swa_k_norm[p], swa_sink[p])
            c_new = None
        moe_w = (norm_ffn[layer], router_w[layer], router_bias[layer], exp_w1[layer], exp_w3[layer],
                 exp_w2[layer], shared_w1[layer], shared_w3[layer], shared_w2[layer])
        if with_ctx:
            cs, xs = _moe([(c_new, mc[3], mc[4], mc[5], b * n_c), (xs, mx[3], mx[4], mx[5], t)], *moe_w)
        else:
            (xs,) = _moe([(xs, mx[3], mx[4], mx[5], t)], *moe_w)
    return xs.reshape(b, t, d)
```

```python
import functools
import math

import jax
import jax.numpy as jnp
from jax import lax
from jax.experimental import pallas as pl
from jax.experimental.pallas import tpu as pltpu

F32 = jnp.float32
BF16 = jnp.bfloat16
I32 = jnp.int32
U32 = jnp.uint32

EPS = 1e-6
GRID_W = 64
ROPE_BASE = 10000.0
DIFF_HEADS = 4
DIFF_DK = 128
DIFF_DV = 256
DN_HEADS = 8
DN_DK = 128
DN_DV = 128
DN_CONV = 5
DN_CHUNK = 128
SWA_HEADS = 32
SWA_KV_HEADS = 4
SWA_GROUP = SWA_HEADS // SWA_KV_HEADS
SWA_DH = 64
WINDOW = 128
Q_BLOCK = 128
N_EXPERTS = 64
TOP_K = 8
N_GROUPS = 8
GROUP_SIZE = N_EXPERTS // N_GROUPS
TOPK_GROUPS = 4
ROUTED_SCALE = 2.5
MOE_BLOCK = 256

LANES = 128
SUBLANES = 8
VMEM_LIMIT_BYTES = 56 * 1024 * 1024

NT_DIMS = (((1,), (1,)), ((), ()))


def _params(*semantics):
    return pltpu.CompilerParams(dimension_semantics=semantics, vmem_limit_bytes=VMEM_LIMIT_BYTES)


def _tile(n, pref, mult):
    if n <= pref:
        return n
    t = pref - pref % mult
    while t > mult and n % t:
        t -= mult
    assert n % t == 0, (n, pref, mult)
    return t


def _mm(a, b):
    return jnp.dot(a.astype(BF16), b.astype(BF16), preferred_element_type=F32)


def _split2(x):
    hi = x.astype(BF16)
    lo = (x - hi.astype(F32)).astype(BF16)
    return hi, lo


def _dot3(a, b, dims=None):
    if dims is None:
        dims = (((a.ndim - 1,), (0,)), ((), ()))
    ah, al = _split2(a)
    bh, bl = _split2(b)
    d = functools.partial(lax.dot_general, dimension_numbers=dims, preferred_element_type=F32)
    return d(ah, bh) + d(ah, bl) + d(al, bh)


def _silu(x):
    return x * jax.nn.sigmoid(x)


def _softplus(x):
    return jnp.maximum(x, 0.0) + jnp.log(1.0 + jnp.exp(-jnp.abs(x)))


def _pack_pairs(lo, hi):
    ulo = lax.bitcast_convert_type(lo.astype(BF16).astype(F32), U32) >> 16
    uhi = lax.bitcast_convert_type(hi.astype(BF16).astype(F32), U32) & jnp.uint32(0xFFFF0000)
    return ulo | uhi


def _unpack_pairs(u):
    lo = lax.bitcast_convert_type(u << 16, F32)
    hi = lax.bitcast_convert_type(u & jnp.uint32(0xFFFF0000), F32)
    return lo, hi


def _mod_kernel(a_ref, w_ref, b_ref, o_ref):
    o_ref[...] = _dot3(_silu(a_ref[...]), w_ref[0]) + b_ref[...]


def _modulation(a, w_all, layer, b):
    r, d = a.shape
    n = w_all.shape[2]
    tn = _tile(n, 768, LANES)
    return pl.pallas_call(
        _mod_kernel,
        grid=(n // tn,),
        in_specs=[pl.BlockSpec((r, d), lambda j: (0, 0)),
                  pl.BlockSpec((1, d, tn), lambda j: (layer, 0, j)),
                  pl.BlockSpec((1, tn), lambda j: (0, j))],
        out_specs=pl.BlockSpec((r, tn), lambda j: (0, j)),
        out_shape=jax.ShapeDtypeStruct((r, n), F32),
        compiler_params=_params("parallel"),
        name="modulation",
    )(a, w_all, b.reshape(1, n))


def _norm_mod(x, nw, sh, sc):
    y = x * lax.rsqrt(jnp.mean(x * x, axis=-1, keepdims=True) + EPS) * nw
    return y * (1.0 + sc) + sh


def _nmm_kernel(x_ref, nw_ref, sh_ref, sc_ref, w_ref, *rest, has_small):
    if has_small:
        ws_ref, o_ref, os_ref, h_ref = rest
    else:
        o_ref, h_ref = rest

    @pl.when(pl.program_id(1) == 0)
    def _():
        h = _norm_mod(x_ref[...], nw_ref[...], sh_ref[0], sc_ref[0])
        h_ref[...] = h.astype(BF16)
        if has_small:
            os_ref[...] = _dot3(h, ws_ref[...])

    o_ref[...] = jnp.dot(h_ref[...], w_ref[...], preferred_element_type=F32).astype(o_ref.dtype)


def _norm_mod_matmul(x, nw, sh, sc, w, rows_per_group, w_small=None):
    m, d = x.shape
    n = w.shape[1]
    tm = math.gcd(_tile(m, 512, SUBLANES), rows_per_group)
    tn = _tile(n, 1024, LANES)
    has_small = w_small is not None
    grp = lambda i, j: ((i * tm) // rows_per_group, 0, 0)
    in_specs = [pl.BlockSpec((tm, d), lambda i, j: (i, 0)),
                pl.BlockSpec((1, d), lambda i, j: (0, 0)),
                pl.BlockSpec((1, 1, d), grp),
                pl.BlockSpec((1, 1, d), grp),
                pl.BlockSpec((d, tn), lambda i, j: (0, j))]
    args = [x, nw.reshape(1, d), sh, sc, w]
    out_specs = [pl.BlockSpec((tm, tn), lambda i, j: (i, j))]
    out_shape = [jax.ShapeDtypeStruct((m, n), BF16)]
    if has_small:
        in_specs.append(pl.BlockSpec((d, LANES), lambda i, j: (0, 0)))
        args.append(w_small)
        out_specs.append(pl.BlockSpec((tm, LANES), lambda i, j: (i, 0)))
        out_shape.append(jax.ShapeDtypeStruct((m, LANES), F32))
    outs = pl.pallas_call(
        functools.partial(_nmm_kernel, has_small=has_small),
        grid=(m // tm, n // tn),
        in_specs=in_specs,
        out_specs=out_specs,
        out_shape=out_shape,
        scratch_shapes=[pltpu.VMEM((tm, d), BF16)],
        compiler_params=_params("parallel", "arbitrary"),
        name="norm_mod_matmul",
    )(*args)
    return outs if has_small else outs[0]


def _rope_tables(t, head_dim):
    q = head_dim // 4
    pos = jnp.arange(t, dtype=I32)
    row = (pos // GRID_W).astype(F32)
    col = (pos % GRID_W).astype(F32)
    axis_dim = head_dim // 2
    inv_freq = ROPE_BASE ** (-jnp.arange(0, axis_dim, 2, dtype=F32) / axis_dim)
    lane = jnp.arange(LANES) % head_dim
    freq = inv_freq[lane % q]
    p = jnp.where((lane < head_dim // 2)[None, :], row[:, None], col[:, None])
    ang = p * freq[None, :]
    sign = jnp.where((lane % (2 * q)) < q, -1.0, 1.0)
    return jnp.cos(ang), jnp.sin(ang) * sign[None, :]


def _hnr_kernel(z_ref, w_ref, *rest, hd, scale, rope):
    if rope:
        cos_ref, sin_ref, o_ref = rest
    else:
        (o_ref,) = rest
    tm, cb = z_ref.shape
    lane = lax.broadcasted_iota(I32, (tm, LANES), 1)
    w = w_ref[...]
    q = hd // 4
    for g in range(cb // LANES):
        sl = slice(g * LANES, (g + 1) * LANES)
        x = z_ref[:, sl].astype(F32)
        x2 = x * x
        if hd == LANES:
            ms = jnp.mean(x2, axis=-1, keepdims=True)
        else:
            lo = jnp.sum(jnp.where(lane < hd, x2, 0.0), axis=-1, keepdims=True)
            hi = jnp.sum(jnp.where(lane >= hd, x2, 0.0), axis=-1, keepdims=True)
            ms = jnp.where(lane < hd, lo, hi) * (1.0 / hd)
        y = x * lax.rsqrt(ms + EPS) * w
        if rope:
            nxt = pltpu.roll(y, LANES - q, 1)
            prv = pltpu.roll(y, q, 1)
            partner = jnp.where((lane & (2 * q - 1)) < q, nxt, prv)
            y = y * cos_ref[...] + partner * sin_ref[...]
        o_ref[:, sl] = (y * scale).astype(o_ref.dtype)


def _headnorm_rope(z, col0, ncols, w, hd, scale, tables, t):
    m = z.shape[0]
    cb = min(512, ncols)
    assert ncols % cb == 0 and col0 % cb == 0 and LANES % hd == 0
    tm = _tile(t, 512, SUBLANES) if tables is not None else _tile(m, 512, SUBLANES)
    cblk0 = col0 // cb
    rope = tables is not None
    in_specs = [pl.BlockSpec((tm, cb), lambda i, j: (i, cblk0 + j)),
                pl.BlockSpec((1, LANES), lambda i, j: (0, 0))]
    args = [z, jnp.tile(w, LANES // hd).reshape(1, LANES)]
    if rope:
        nt = t // tm
        in_specs += [pl.BlockSpec((tm, LANES), lambda i, j: (i % nt, 0))] * 2
        args += list(tables)
    return pl.pallas_call(
        functools.partial(_hnr_kernel, hd=hd, scale=scale, rope=rope),
        grid=(m // tm, ncols // cb),
        in_specs=in_specs,
        out_specs=pl.BlockSpec((tm, cb), lambda i, j: (i, j)),
        out_shape=jax.ShapeDtypeStruct((m, ncols), BF16),
        compiler_params=_params("parallel", "parallel"),
        name="headnorm_rope",
    )(*args)


def _diff_attn_kernel(lv_ref, q_ref, *rest, lam_init, has_lat):
    if has_lat:
        kx_ref, vx_ref, kc_ref, vc_ref, w_ref, o_ref = rest
    else:
        kc_ref, vc_ref, w_ref, o_ref = rest
    lv = lv_ref[...]
    lam = (jnp.exp(jnp.sum(lv[0:1] * lv[1:2], keepdims=True))
           - jnp.exp(jnp.sum(lv[2:3] * lv[3:4], keepdims=True)) + lam_init)
    q = q_ref[...]

    def probs(c):
        sl = slice(c * DIFF_DK, (c + 1) * DIFF_DK)
        qc = q[:, sl]
        s_c = lax.dot_general(qc, kc_ref[:, sl], NT_DIMS, preferred_element_type=F32)
        m = jnp.max(s_c, axis=-1, keepdims=True)
        p_x = None
        if has_lat:
            s_x = lax.dot_general(qc, kx_ref[:, sl], NT_DIMS, preferred_element_type=F32)
            m = jnp.maximum(m, jnp.max(s_x, axis=-1, keepdims=True))
            p_x = jnp.exp(s_x - m)
        p_c = jnp.exp(s_c - m)
        l = jnp.sum(p_c, axis=-1, keepdims=True)
        if has_lat:
            l = l + jnp.sum(p_x, axis=-1, keepdims=True)
        return p_x, p_c, 1.0 / l

    p1x, p1c, r1 = probs(0)
    p2x, p2c, r2 = probs(1)
    r2 = r2 * lam
    o = jnp.dot((p1c * r1 - p2c * r2).astype(BF16), vc_ref[...], preferred_element_type=F32)
    if has_lat:
        o = o + jnp.dot((p1x * r1 - p2x * r2).astype(BF16), vx_ref[...], preferred_element_type=F32)
    o = o * lax.rsqrt(jnp.mean(o * o, axis=-1, keepdims=True) + EPS)
    o_ref[...] = (o * w_ref[...] * (1.0 - lam_init)).astype(o_ref.dtype)


def _diff_attention(lam_vec, q, kc, zc, subln, lam_init, b, n_c, kx=None, zx=None, t=None):
    has_lat = kx is not None
    hw = 2 * DIFF_DK
    vblk0 = (2 * DIFF_HEADS * hw) // DIFF_DV
    tq_all = t if has_lat else n_c
    tq = _tile(tq_all, 256, SUBLANES)
    nq = tq_all // tq
    in_specs = [pl.BlockSpec((4, DIFF_DK), lambda bi, h, qi: (0, 0)),
                pl.BlockSpec((tq, hw), lambda bi, h, qi: (bi * nq + qi, h))]
    args = [lam_vec, q]
    if has_lat:
        in_specs += [pl.BlockSpec((t, hw), lambda bi, h, qi: (bi, h)),
                     pl.BlockSpec((t, DIFF_DV), lambda bi, h, qi: (bi, vblk0 + h))]
        args += [kx, zx]
    in_specs += [pl.BlockSpec((n_c, hw), lambda bi, h, qi: (bi, h)),
                 pl.BlockSpec((n_c, DIFF_DV), lambda bi, h, qi: (bi, vblk0 + h)),
                 pl.BlockSpec((1, DIFF_DV), lambda bi, h, qi: (0, 0))]
    args += [kc, zc, subln.reshape(1, DIFF_DV)]
    return pl.pallas_call(
        functools.partial(_diff_attn_kernel, lam_init=lam_init, has_lat=has_lat),
        grid=(b, DIFF_HEADS, nq),
        in_specs=in_specs,
        out_specs=pl.BlockSpec((tq, DIFF_DV), lambda bi, h, qi: (bi * nq + qi, h)),
        out_shape=jax.ShapeDtypeStruct((b * tq_all, DIFF_HEADS * DIFF_DV), BF16),
        compiler_params=_params("parallel", "parallel", "arbitrary"),
        name="diff_attention",
    )(*args)


def _dn_prep_kernel(z_ref, cw_ref, o_ref, pad_ref, *, seg):
    halo = SUBLANES
    pad_ref[0:halo, :] = jnp.zeros((halo, LANES), F32)
    pad_ref[halo + seg:2 * halo + seg, :] = jnp.zeros((halo, LANES), F32)
    pad_ref[halo:halo + seg, :] = z_ref[...].astype(F32)
    kind = pl.program_id(1) // DN_HEADS
    qk_scale = jnp.where(kind == 0, DN_DK ** -0.5, 1.0).astype(F32)
    rows = _tile(seg, 256, SUBLANES)
    for r0 in range(0, seg, rows):
        acc = jnp.zeros((rows, LANES), F32)
        for j in range(DN_CONV):
            acc = acc + cw_ref[j:j + 1, :] * pad_ref[pl.ds(halo + r0 + j - DN_CONV // 2, rows), :]
        y = _silu(acc)
        nrm = y * lax.rsqrt(jnp.sum(y * y, axis=-1, keepdims=True) + EPS) * qk_scale
        o_ref[r0:r0 + rows, :] = jnp.where(kind < 2, nrm, y).astype(o_ref.dtype)


def _dn_prep(z, col0, conv_w, seg):
    m = z.shape[0]
    ncols = conv_w.shape[1]
    cblk0 = col0 // LANES
    cw = jnp.zeros((SUBLANES, ncols), F32).at[:DN_CONV].set(conv_w)
    return pl.pallas_call(
        functools.partial(_dn_prep_kernel, seg=seg),
        grid=(m // seg, ncols // LANES),
        in_specs=[pl.BlockSpec((seg, LANES), lambda s, g: (s, cblk0 + g)),
                  pl.BlockSpec((SUBLANES, LANES), lambda s, g: (0, g))],
        out_specs=pl.BlockSpec((seg, LANES), lambda s, g: (s, g)),
        out_shape=jax.ShapeDtypeStruct((m, ncols), BF16),
        scratch_shapes=[pltpu.VMEM((seg + 2 * SUBLANES, LANES), F32)],
        compiler_params=_params("parallel", "parallel"),
        name="dn_prep",
    )(z, cw)


def _deltanet_kernel(q_ref, k_ref, v_ref, zs_ref, zst_ref, pr_ref, pca_ref, pcd_ref, s0_ref,
                     o_ref, sout_ref, s_ref, *, reverse, n_chunks):
    step = pl.program_id(1)
    c = DN_CHUNK

    @pl.when(step == 0)
    def _():
        s_ref[...] = s0_ref[0]

    ri = lax.broadcasted_iota(I32, (c, c), 0)
    ci = lax.broadcasted_iota(I32, (c, c), 1)
    if reverse:
        later, strict, later_t = ri <= ci, ri < ci, ri >= ci
    else:
        later, strict, later_t = ri >= ci, ri > ci, ri <= ci
    eye = (ri == ci).astype(F32)
    tri = later.astype(BF16)
    tri_t = later_t.astype(BF16)

    zs = zs_ref[...]
    beta_cols = jax.nn.sigmoid(zs)
    g_cols = -jnp.exp(pr_ref[0:1, :]) * _softplus(zs + pr_ref[1:2, :])
    g_hi = g_cols.astype(BF16)
    g_r1 = g_cols - g_hi.astype(F32)
    g_mid = g_r1.astype(BF16)
    g_lo = (g_r1 - g_mid.astype(F32)).astype(BF16)
    d = functools.partial(jnp.dot, preferred_element_type=F32)
    gc_cols = d(tri, g_hi) + d(tri, g_mid) + d(tri, g_lo)
    g_rows = -jnp.exp(pca_ref[...]) * _softplus(zst_ref[...] + pcd_ref[...])
    h_hi = g_rows.astype(BF16)
    h_r1 = g_rows - h_hi.astype(F32)
    h_mid = h_r1.astype(BF16)
    h_lo = (h_r1 - h_mid.astype(F32)).astype(BF16)
    gc_rows = d(h_hi, tri_t) + d(h_mid, tri_t) + d(h_lo, tri_t)

    dir_off = DN_HEADS if reverse else 0
    last = 0 if reverse else c - 1
    neg_inf = jnp.float32(-jnp.inf)
    heads = []
    for h in range(DN_HEADS):
        sl = slice(h * DN_DK, (h + 1) * DN_DK)
        cb = dir_off + h
        cg = 2 * DN_HEADS + dir_off + h
        beta = beta_cols[:, cb:cb + 1]
        gcol = gc_cols[:, cg:cg + 1]
        grow = gc_rows[cg:cg + 1, :]
        glast = grow[:, last:last + 1]
        q = q_ref[:, sl]
        k = k_ref[:, sl]
        kf = k.astype(F32)
        decay = jnp.exp(jnp.where(later, gcol - grow, neg_inf))
        kb = kf * beta
        both = lax.dot_general(jnp.concatenate([kb.astype(BF16), q], axis=0), k, NT_DIMS,
                               preferred_element_type=F32)
        lmat = jnp.where(strict, both[:c] * decay, 0.0)
        eg = jnp.exp(gcol)
        heads.append(dict(
            sl=sl, glast=glast, lmat=lmat, amat=both[c:] * decay,
            inv=eye - jnp.where((ri >> 1) == (ci >> 1), lmat, 0.0),
            rhs=jnp.concatenate([v_ref[:, sl].astype(F32) * beta, kb * eg], axis=1).astype(BF16),
            qe=(q.astype(F32) * eg).astype(BF16),
            kdec_t=(kf * jnp.exp(glast - gcol)).T.astype(BF16)))
    lev = 1
    while (1 << lev) < c:
        blk = ((ri >> (lev + 1)) == (ci >> (lev + 1))) & ((ri >> lev) != (ci >> lev))
        half = [_mm(hd["inv"], jnp.where(blk, hd["lmat"], 0.0)) for hd in heads]
        for hd, t in zip(heads, half):
            hd["inv"] = hd["inv"] - _mm(t, hd["inv"])
        lev += 1
    uws = [_mm(hd["inv"], hd["rhs"]) for hd in heads]
    states = [s_ref[h] for h in range(DN_HEADS)]
    new_states = []
    for hd, uw, s in zip(heads, uws, states):
        ws_qs = _mm(jnp.concatenate([uw[:, DN_DV:].astype(BF16), hd["qe"]], axis=0), s)
        v_new = uw[:, :DN_DV] - ws_qs[:c]
        o_ref[:, hd["sl"]] = ws_qs[c:] + _mm(hd["amat"], v_new)
        new_states.append(s * jnp.exp(hd["glast"]) + _mm(hd["kdec_t"], v_new))
    for h in range(DN_HEADS):
        s_ref[h] = new_states[h]

    @pl.when(step == n_chunks - 1)
    def _():
        sout_ref[0] = s_ref[...]


def _deltanet(dn, zs, zst, prm, s0, b, seg, reverse):
    pr, pca, pcd = prm
    n = seg // DN_CHUNK
    hw = DN_HEADS * DN_DK
    rb = (lambda bi, s: bi * n + (n - 1 - s)) if reverse else (lambda bi, s: bi * n + s)
    state = pl.BlockSpec((1, DN_HEADS, DN_DK, DN_DV), lambda bi, s: (bi, 0, 0, 0))
    return pl.pallas_call(
        functools.partial(_deltanet_kernel, reverse=reverse, n_chunks=n),
        grid=(b, n),
        in_specs=[pl.BlockSpec((DN_CHUNK, hw), lambda bi, s: (rb(bi, s), 0)),
                  pl.BlockSpec((DN_CHUNK, hw), lambda bi, s: (rb(bi, s), 1)),
                  pl.BlockSpec((DN_CHUNK, hw), lambda bi, s: (rb(bi, s), 2)),
                  pl.BlockSpec((DN_CHUNK, LANES), lambda bi, s: (rb(bi, s), 0)),
                  pl.BlockSpec((4 * DN_HEADS, DN_CHUNK), lambda bi, s: (0, rb(bi, s))),
                  pl.BlockSpec((SUBLANES, LANES), lambda bi, s: (0, 0)),
                  pl.BlockSpec((4 * DN_HEADS, LANES), lambda bi, s: (0, 0)),
                  pl.BlockSpec((4 * DN_HEADS, LANES), lambda bi, s: (0, 0)),
                  state],
        out_specs=[pl.BlockSpec((DN_CHUNK, hw), lambda bi, s: (rb(bi, s), 0)), state],
        out_shape=[jax.ShapeDtypeStruct((b * seg, hw), F32),
                   jax.ShapeDtypeStruct((b, DN_HEADS, DN_DK, DN_DV), F32)],
        scratch_shapes=[pltpu.VMEM((DN_HEADS, DN_DK, DN_DV), F32)],
        compiler_params=_params("parallel", "arbitrary"),
        name="deltanet_bwd" if reverse else "deltanet_fwd",
    )(dn, dn, dn, zs, zst, pr, pca, pcd, s0)


def _deltanet_params(a_log, dt_bias):
    nh = 2 * DN_HEADS
    a = a_log.reshape(nh).astype(F32)
    dtb = dt_bias.reshape(nh).astype(F32)
    pr = jnp.zeros((SUBLANES, LANES), F32).at[0, nh:2 * nh].set(a).at[1, nh:2 * nh].set(dtb)
    pca = jnp.zeros((2 * nh, LANES), F32).at[nh:].set(jnp.broadcast_to(a[:, None], (nh, LANES)))
    pcd = jnp.zeros((2 * nh, LANES), F32).at[nh:].set(jnp.broadcast_to(dtb[:, None], (nh, LANES)))
    return pr, pca, pcd


def _outproj_ab_kernel(od_ref, of_ref, ob_ref, gate_ref, nw_ref, w_ref, x_ref, g_ref, o_ref, h_ref):
    @pl.when(pl.program_id(1) == 0)
    def _():
        nd = od_ref.shape[1]
        h_ref[:, :nd] = od_ref[...]
        for h in range(DN_HEADS):
            sl = slice(h * DN_DV, (h + 1) * DN_DV)
            y = of_ref[:, sl] + ob_ref[:, sl]
            y = y * lax.rsqrt(jnp.mean(y * y, axis=-1, keepdims=True) + EPS) * nw_ref[...]
            h_ref[:, nd + h * DN_DV:nd + (h + 1) * DN_DV] = (y * _silu(gate_ref[:, sl].astype(F32))).astype(BF16)

    y = jnp.dot(h_ref[...], w_ref[...], preferred_element_type=F32)
    o_ref[...] = x_ref[...] + g_ref[0] * y


def _outproj_ab(od, o_f, o_b, z, gate_col0, out_norm, w, x, gate, rows_per_group):
    m, d = x.shape
    nd, nn = od.shape[1], o_f.shape[1]
    tm = math.gcd(_tile(m, 512, SUBLANES), rows_per_group)
    tn = _tile(d, 1024, LANES)
    grp = lambda i, j: ((i * tm) // rows_per_group, 0, j)
    gblk = gate_col0 // nn
    return pl.pallas_call(
        _outproj_ab_kernel,
        grid=(m // tm, d // tn),
        in_specs=[pl.BlockSpec((tm, nd), lambda i, j: (i, 0)),
                  pl.BlockSpec((tm, nn), lambda i, j: (i, 0)),
                  pl.BlockSpec((tm, nn), lambda i, j: (i, 0)),
                  pl.BlockSpec((tm, nn), lambda i, j: (i, gblk)),
                  pl.BlockSpec((1, DN_DV), lambda i, j: (0, 0)),
                  pl.BlockSpec((nd + nn, tn), lambda i, j: (0, j)),
                  pl.BlockSpec((tm, tn), lambda i, j: (i, j)),
                  pl.BlockSpec((1, 1, tn), grp)],
        out_specs=pl.BlockSpec((tm, tn), lambda i, j: (i, j)),
        out_shape=jax.ShapeDtypeStruct((m, d), F32),
        scratch_shapes=[pltpu.VMEM((tm, nd + nn), BF16)],
        compiler_params=_params("parallel", "arbitrary"),
        name="outproj_ab",
    )(od, o_f, o_b, z, out_norm.reshape(1, DN_DV), w, x, gate)


def _outproj_kernel(a_ref, w_ref, x_ref, g_ref, o_ref):
    o_ref[...] = x_ref[...] + g_ref[0] * jnp.dot(a_ref[...], w_ref[...], preferred_element_type=F32)


def _outproj(a, w, x, gate, rows_per_group):
    m, d = x.shape
    kdim = a.shape[1]
    tm = math.gcd(_tile(m, 512, SUBLANES), rows_per_group)
    tn = _tile(d, 1024, LANES)
    grp = lambda i, j: ((i * tm) // rows_per_group, 0, j)
    return pl.pallas_call(
        _outproj_kernel,
        grid=(m // tm, d // tn),
        in_specs=[pl.BlockSpec((tm, kdim), lambda i, j: (i, 0)),
                  pl.BlockSpec((kdim, tn), lambda i, j: (0, j)),
                  pl.BlockSpec((tm, tn), lambda i, j: (i, j)),
                  pl.BlockSpec((1, 1, tn), grp)],
        out_specs=pl.BlockSpec((tm, tn), lambda i, j: (i, j)),
        out_shape=jax.ShapeDtypeStruct((m, d), F32),
        compiler_params=_params("parallel", "arbitrary"),
        name="outproj",
    )(a, w, x, gate)


def _swa_kernel(sink_ref, q_ref, k0_ref, k1_ref, k2_ref, v0_ref, v1_ref, v2_ref, kc_ref, vc_ref, o_ref,
                *, t):
    kvh = pl.program_id(1)
    i = pl.program_id(2)
    qb = Q_BLOCK
    npair = SWA_GROUP // 2
    lane = lax.broadcasted_iota(I32, (qb, LANES), 1)
    q = q_ref[...]
    parts = []
    for p in range(npair):
        qp = q[:, p * LANES:(p + 1) * LANES]
        parts.append(jnp.where(lane < SWA_DH, qp, jnp.zeros_like(qp)))
        parts.append(jnp.where(lane >= SWA_DH, qp, jnp.zeros_like(qp)))
    qq = jnp.concatenate(parts, axis=0)
    k_lat = jnp.concatenate([k0_ref[...], k1_ref[...], k2_ref[...]], axis=0)
    v_lat = jnp.concatenate([v0_ref[...], v1_ref[...], v2_ref[...]], axis=0)
    rows = SWA_GROUP * qb
    band = 3 * qb
    s_lat = lax.dot_general(qq, k_lat, NT_DIMS, preferred_element_type=F32)
    s_ctx = lax.dot_general(qq, kc_ref[...], NT_DIMS, preferred_element_type=F32)
    r_io = lax.broadcasted_iota(I32, (rows, band), 0)
    c_io = lax.broadcasted_iota(I32, (rows, band), 1)
    qpos = i * qb + (r_io & (qb - 1))
    kpos = (i - 1) * qb + c_io
    valid = (jnp.abs(qpos - kpos) <= WINDOW) & (kpos >= 0) & (kpos < t)
    s_lat = jnp.where(valid, s_lat, -jnp.inf)
    head = lax.broadcasted_iota(I32, (rows, 1), 0) >> (qb.bit_length() - 1)
    sink = jnp.zeros((rows, 1), F32)
    for g in range(SWA_GROUP):
        sink = jnp.where(head == g, sink_ref[kvh, g], sink)
    m = jnp.maximum(jnp.maximum(jnp.max(s_lat, axis=-1, keepdims=True),
                                jnp.max(s_ctx, axis=-1, keepdims=True)), sink)
    p_lat = jnp.exp(s_lat - m)
    p_ctx = jnp.exp(s_ctx - m)
    l = (jnp.sum(p_lat, axis=-1, keepdims=True) + jnp.sum(p_ctx, axis=-1, keepdims=True)
         + jnp.exp(sink - m))
    r = 1.0 / l
    o = (jnp.dot((p_lat * r).astype(BF16), v_lat, preferred_element_type=F32)
         + jnp.dot((p_ctx * r).astype(BF16), vc_ref[...], preferred_element_type=F32))
    for p in range(npair):
        lo = o[(2 * p) * qb:(2 * p + 1) * qb]
        hi = o[(2 * p + 1) * qb:(2 * p + 2) * qb]
        o_ref[:, p * LANES:(p + 1) * LANES] = jnp.where(lane < SWA_DH, lo, hi).astype(o_ref.dtype)


def _swa_attention(sink, q, kx, vx, kc, vc, b, t, n_c):
    qb = Q_BLOCK
    nb = t // qb
    gw = SWA_GROUP * SWA_DH
    lat = lambda off: pl.BlockSpec(
        (qb, LANES), lambda bi, h, i: (bi * nb + jnp.clip(i + off, 0, nb - 1), h))
    ctx = pl.BlockSpec((n_c, LANES), lambda bi, h, i: (bi, h))
    return pl.pallas_call(
        functools.partial(_swa_kernel, t=t),
        grid=(b, SWA_KV_HEADS, nb),
        in_specs=[pl.BlockSpec(memory_space=pltpu.SMEM),
                  pl.BlockSpec((qb, gw), lambda bi, h, i: (bi * nb + i, h)),
                  lat(-1), lat(0), lat(1), lat(-1), lat(0), lat(1), ctx, ctx],
        out_specs=pl.BlockSpec((qb, gw), lambda bi, h, i: (bi * nb + i, h)),
        out_shape=jax.ShapeDtypeStruct((b * t, SWA_HEADS * SWA_DH), BF16),
        compiler_params=_params("parallel", "parallel", "arbitrary"),
        name="swa_attention",
    )(sink, q, kx, kx, kx, vx, vx, vx, kc, vc)


def _dup_heads(a, col0):
    m = a.shape[0]
    h = a[:, col0:col0 + SWA_KV_HEADS * SWA_DH].reshape(m, SWA_KV_HEADS, 1, SWA_DH)
    return jnp.broadcast_to(h, (m, SWA_KV_HEADS, LANES // SWA_DH, SWA_DH)).reshape(m, SWA_KV_HEADS * LANES)


def _first_max(vals, iota, size, axis):
    m = jnp.max(vals, axis=axis, keepdims=True)
    first = jnp.min(jnp.where(vals == m, iota, size), axis=axis, keepdims=True)
    return m, first


def _router_kernel(x_ref, nw_ref, sh_ref, sc_ref, rwt_ref, rb_ref, c0_ref,
                   hp_ref, idx_ref, wt_ref, rank_ref, cnt_ref, carry_ref):
    @pl.when(pl.program_id(0) == 0)
    def _():
        carry_ref[...] = c0_ref[...]

    h = _norm_mod(x_ref[...], nw_ref[...], sh_ref[0], sc_ref[0])
    tm, d = h.shape
    hp_ref[...] = _pack_pairs(h[:, :d // 2], h[:, d // 2:])
    scores = jax.nn.sigmoid(_dot3(rwt_ref[...], h, NT_DIMS))
    sel = scores + rb_ref[...]
    neg = jnp.float32(-jnp.inf)

    g_io = lax.broadcasted_iota(I32, (GROUP_SIZE, tm), 0)
    gs_rows = []
    for g in range(N_GROUPS):
        sg = sel[g * GROUP_SIZE:(g + 1) * GROUP_SIZE]
        m1, f1 = _first_max(sg, g_io, GROUP_SIZE, 0)
        m2 = jnp.max(jnp.where(g_io == f1, neg, sg), axis=0, keepdims=True)
        gs_rows.append(m1 + m2)
    cur = jnp.concatenate(gs_rows, axis=0)
    n_io = lax.broadcasted_iota(I32, (N_GROUPS, tm), 0)
    gmask = jnp.zeros((N_GROUPS, tm), I32)
    for _ in range(TOPK_GROUPS):
        _, f = _first_max(cur, n_io, N_GROUPS, 0)
        hit = n_io == f
        gmask = jnp.where(hit, 1, gmask)
        cur = jnp.where(hit, neg, cur)
    cur = jnp.concatenate(
        [jnp.where(gmask[g:g + 1] > 0, sel[g * GROUP_SIZE:(g + 1) * GROUP_SIZE], neg) for g in range(N_GROUPS)],
        axis=0)

    e_io = lax.broadcasted_iota(I32, (N_EXPERTS, tm), 0)
    chosen = jnp.zeros((N_EXPERTS, tm), F32)
    idx_rows, w_rows = [], []
    for _ in range(TOP_K):
        _, f = _first_max(cur, e_io, N_EXPERTS, 0)
        hit = e_io == f
        idx_rows.append(f)
        w_rows.append(jnp.sum(jnp.where(hit, scores, 0.0), axis=0, keepdims=True))
        chosen = jnp.where(hit, 1.0, chosen)
        cur = jnp.where(hit, neg, cur)
    idx = jnp.concatenate(idx_rows, axis=0)
    w = jnp.concatenate(w_rows, axis=0)
    idx_ref[...] = idx
    wt_ref[...] = w * (1.0 / jnp.sum(w, axis=0, keepdims=True)) * ROUTED_SCALE

    onehot = chosen.astype(BF16)
    before = (lax.broadcasted_iota(I32, (tm, tm), 0) < lax.broadcasted_iota(I32, (tm, tm), 1)).astype(BF16)
    base = carry_ref[:, 0:1] + jnp.dot(onehot, before, preferred_element_type=F32)
    rank_ref[...] = jnp.concatenate(
        [jnp.sum(jnp.where(e_io == idx_rows[k], base, 0.0), axis=0, keepdims=True) for k in range(TOP_K)],
        axis=0).astype(I32)
    carry_ref[...] = carry_ref[...] + jnp.sum(chosen, axis=1, keepdims=True)
    cnt_ref[...] = carry_ref[...]


def _router(x, nw, sh, sc, rwt, rb, counts0, rows_per_group):
    m, d = x.shape
    tm = math.gcd(_tile(m, 256, LANES), rows_per_group)
    grp = lambda i: ((i * tm) // rows_per_group, 0, 0)
    tok = lambda rows: pl.BlockSpec((rows, tm), lambda i: (0, i))
    return pl.pallas_call(
        _router_kernel,
        grid=(m // tm,),
        in_specs=[pl.BlockSpec((tm, d), lambda i: (i, 0)),
                  pl.BlockSpec((1, d), lambda i: (0, 0)),
                  pl.BlockSpec((1, 1, d), grp),
                  pl.BlockSpec((1, 1, d), grp),
                  pl.BlockSpec((N_EXPERTS, d), lambda i: (0, 0)),
                  pl.BlockSpec((N_EXPERTS, 1), lambda i: (0, 0)),
                  pl.BlockSpec((N_EXPERTS, LANES), lambda i: (0, 0))],
        out_specs=[pl.BlockSpec((tm, d // 2), lambda i: (i, 0)),
                   tok(TOP_K), tok(TOP_K), tok(TOP_K),
                   pl.BlockSpec((N_EXPERTS, LANES), lambda i: (0, 0))],
        out_shape=[jax.ShapeDtypeStruct((m, d // 2), U32),
                   jax.ShapeDtypeStruct((TOP_K, m), I32),
                   jax.ShapeDtypeStruct((TOP_K, m), F32),
                   jax.ShapeDtypeStruct((TOP_K, m), I32),
                   jax.ShapeDtypeStruct((N_EXPERTS, LANES), F32)],
        scratch_shapes=[pltpu.VMEM((N_EXPERTS, LANES), F32)],
        compiler_params=_params("arbitrary"),
        name="moe_router",
    )(x, nw.reshape(1, d), sh, sc, rwt, rb.reshape(N_EXPERTS, 1), counts0)


def _row_copy(src, src_row, dst, dst_row, sem):
    return pltpu.make_async_copy(src.at[pl.ds(src_row, 1)], dst.at[pl.ds(dst_row, 1)], sem)


def _dispatch_kernel(nv_ref, dest_ref, hp_ref, xs_ref, zero_ref, sem):
    tm = hp_ref.shape[0]

    @pl.when(pl.program_id(0) == 0)
    def _():
        zero_ref[...] = jnp.zeros(zero_ref.shape, U32)

        def fill(blk, carry):
            @pl.when(nv_ref[blk] < MOE_BLOCK)
            def _():
                cp = pltpu.make_async_copy(
                    zero_ref, xs_ref.at[pl.ds(blk * MOE_BLOCK, MOE_BLOCK)], sem)
                cp.start()
                cp.wait()
            return carry

        lax.fori_loop(0, nv_ref.shape[0], fill, 0)

    def start(t, carry):
        for k in range(TOP_K):
            _row_copy(hp_ref, t, xs_ref, dest_ref[k, t], sem).start()
        return carry

    def wait(t, carry):
        for k in range(TOP_K):
            _row_copy(hp_ref, t, xs_ref, dest_ref[k, t], sem).wait()
        return carry

    lax.fori_loop(0, tm, start, 0)
    lax.fori_loop(0, tm, wait, 0)


def _dispatch(block_nv, dest, hp, n_rows):
    m, dh = hp.shape
    tm = _tile(m, 256, LANES)
    return pl.pallas_call(
        _dispatch_kernel,
        grid_spec=pltpu.PrefetchScalarGridSpec(
            num_scalar_prefetch=1,
            grid=(m // tm,),
            in_specs=[pl.BlockSpec((TOP_K, tm), lambda i, nv: (0, i), memory_space=pltpu.SMEM),
                      pl.BlockSpec((tm, dh), lambda i, nv: (i, 0))],
            out_specs=pl.BlockSpec(memory_space=pl.ANY),
            scratch_shapes=[pltpu.VMEM((MOE_BLOCK, dh), U32), pltpu.SemaphoreType.DMA(())]),
        out_shape=jax.ShapeDtypeStruct((n_rows, dh), U32),
        compiler_params=_params("arbitrary"),
        name="moe_dispatch",
    )(block_nv, dest, hp)


def _gffn_kernel(be_ref, nv_ref, xs_ref, w1_ref, w3_ref, w2_ref, ys_ref, w1b_ref, w3b_ref, w2b_ref):
    i = pl.program_id(0)
    nv = nv_ref[i]
    prev = be_ref[jnp.maximum(i - 1, 0)]

    @pl.when((i == 0) | (be_ref[i] != prev))
    def _():
        w1b_ref[...] = w1_ref[0, 0].astype(BF16)
        w3b_ref[...] = w3_ref[0, 0].astype(BF16)
        w2b_ref[...] = w2_ref[0, 0].astype(BF16)

    @pl.when(nv > 0)
    def _():
        lo, hi = _unpack_pairs(xs_ref[...])
        dh = lo.shape[1]
        lo = lo.astype(BF16)
        hi = hi.astype(BF16)
        d = functools.partial(jnp.dot, preferred_element_type=F32)
        a = d(lo, w1b_ref[:dh, :]) + d(hi, w1b_ref[dh:, :])
        g = d(lo, w3b_ref[:dh, :]) + d(hi, w3b_ref[dh:, :])
        y = d((_silu(a) * g).astype(BF16), w2b_ref[...])
        ys_ref[...] = _pack_pairs(y[:, :dh], y[:, dh:])

    @pl.when(nv == 0)
    def _():
        ys_ref[...] = jnp.zeros(ys_ref.shape, U32)


def _grouped_ffn(block_e, block_nv, xs, w1, w3, w2, layer):
    n_rows, dh = xs.shape
    _, _, d, f = w1.shape
    nb = n_rows // MOE_BLOCK
    return pl.pallas_call(
        _gffn_kernel,
        grid_spec=pltpu.PrefetchScalarGridSpec(
            num_scalar_prefetch=2,
            grid=(nb,),
            in_specs=[pl.BlockSpec((MOE_BLOCK, dh), lambda i, be, nv: (i, 0)),
                      pl.BlockSpec((1, 1, d, f), lambda i, be, nv: (layer, be[i], 0, 0)),
                      pl.BlockSpec((1, 1, d, f), lambda i, be, nv: (layer, be[i], 0, 0)),
                      pl.BlockSpec((1, 1, f, d), lambda i, be, nv: (layer, be[i], 0, 0))],
            out_specs=pl.BlockSpec((MOE_BLOCK, dh), lambda i, be, nv: (i, 0)),
            scratch_shapes=[pltpu.VMEM((d, f), BF16), pltpu.VMEM((d, f), BF16), pltpu.VMEM((f, d), BF16)]),
        out_shape=jax.ShapeDtypeStruct((n_rows, dh), U32),
        compiler_params=_params("arbitrary"),
        name="moe_grouped_ffn",
    )(block_e, block_nv, xs, w1, w3, w2)


def _combine_kernel(dest_ref, x_ref, hp_ref, wt_ref, g_ref, ws1_ref, ws3_ref, ws2_ref, ys_ref,
                    o_ref, buf_ref, sem):
    tm, dh = hp_ref.shape

    def start(t, carry):
        for k in range(TOP_K):
            _row_copy(ys_ref, dest_ref[k, t], buf_ref.at[k], t, sem).start()
        return carry

    def wait(t, carry):
        for k in range(TOP_K):
            _row_copy(ys_ref, dest_ref[k, t], buf_ref.at[k], t, sem).wait()
        return carry

    lax.fori_loop(0, tm, start, 0)
    lo, hi = _unpack_pairs(hp_ref[...])
    lo = lo.astype(BF16)
    hi = hi.astype(BF16)
    d = functools.partial(jnp.dot, preferred_element_type=F32)
    a = d(lo, ws1_ref[:dh, :]) + d(hi, ws1_ref[dh:, :])
    g = d(lo, ws3_ref[:dh, :]) + d(hi, ws3_ref[dh:, :])
    shared = d((_silu(a) * g).astype(BF16), ws2_ref[...])
    lax.fori_loop(0, tm, wait, 0)
    acc_lo = jnp.zeros((tm, dh), F32)
    acc_hi = jnp.zeros((tm, dh), F32)
    for k in range(TOP_K):
        ylo, yhi = _unpack_pairs(buf_ref[k])
        wk = wt_ref[:, k:k + 1]
        acc_lo = acc_lo + wk * ylo
        acc_hi = acc_hi + wk * yhi
    o_ref[:, :dh] = x_ref[:, :dh] + g_ref[0][:, :dh] * (acc_lo + shared[:, :dh])
    o_ref[:, dh:] = x_ref[:, dh:] + g_ref[0][:, dh:] * (acc_hi + shared[:, dh:])


def _combine(dest, x, hp, wt, gate, ws1, ws3, ws2, ys, rows_per_group):
    m, d = x.shape
    dh = d // 2
    f = ws1.shape[1]
    tm = math.gcd(_tile(m, 128, LANES), rows_per_group)
    grp = lambda i: ((i * tm) // rows_per_group, 0, 0)
    return pl.pallas_call(
        _combine_kernel,
        grid=(m // tm,),
        in_specs=[pl.BlockSpec((TOP_K, tm), lambda i: (0, i), memory_space=pltpu.SMEM),
                  pl.BlockSpec((tm, d), lambda i: (i, 0)),
                  pl.BlockSpec((tm, dh), lambda i: (i, 0)),
                  pl.BlockSpec((tm, TOP_K), lambda i: (i, 0)),
                  pl.BlockSpec((1, 1, d), grp),
                  pl.BlockSpec((d, f), lambda i: (0, 0)),
                  pl.BlockSpec((d, f), lambda i: (0, 0)),
                  pl.BlockSpec((f, d), lambda i: (0, 0)),
                  pl.BlockSpec(memory_space=pl.ANY)],
        out_specs=pl.BlockSpec((tm, d), lambda i: (i, 0)),
        out_shape=jax.ShapeDtypeStruct((m, d), F32),
        scratch_shapes=[pltpu.VMEM((TOP_K, tm, dh), U32), pltpu.SemaphoreType.DMA(())],
        compiler_params=_params("arbitrary"),
        name="moe_combine",
    )(dest, x, hp, wt, gate, ws1, ws3, ws2, ys)


def _moe(streams, layer, nw, rw, rb, w1, w3, w2, ws1, ws3, ws2):
    d = streams[0][0].shape[1]
    rwt = rw.T
    counts = jnp.zeros((N_EXPERTS, LANES), F32)
    routed = []
    for x, sh, sc, _, rpg in streams:
        hp, idx, wt, rank, counts = _router(x, nw, sh, sc, rwt, rb, counts, rpg)
        routed.append((hp, idx, wt, rank))
    n_assign = sum(s[0].shape[0] for s in streams) * TOP_K
    n_blocks = (n_assign + N_EXPERTS * (MOE_BLOCK - 1) + MOE_BLOCK - 1) // MOE_BLOCK
    cnt = counts[:, 0].astype(I32)
    padded = (cnt + MOE_BLOCK - 1) // MOE_BLOCK * MOE_BLOCK
    pad_end = jnp.cumsum(padded)
    pad_start = pad_end - padded
    bstart = jnp.arange(n_blocks, dtype=I32) * MOE_BLOCK
    block_e = jnp.minimum(jnp.sum((bstart[:, None] >= pad_end[None, :]).astype(I32), axis=1), N_EXPERTS - 1)
    block_nv = jnp.clip(cnt[block_e] - (bstart - pad_start[block_e]), 0, MOE_BLOCK).astype(I32)
    e_ar = jnp.arange(N_EXPERTS, dtype=I32)
    dests = [jnp.sum(jnp.where(idx[:, :, None] == e_ar, pad_start, 0), axis=-1) + rank
             for _, idx, _, rank in routed]
    if len(streams) > 1:
        hp_all = jnp.concatenate([r[0] for r in routed], axis=0)
        dest_all = jnp.concatenate(dests, axis=1)
    else:
        hp_all, dest_all = routed[0][0], dests[0]
    xs = _dispatch(block_nv, dest_all, hp_all, n_blocks * MOE_BLOCK)
    ys = _grouped_ffn(block_e, block_nv, xs, w1, w3, w2, layer)
    ws1b, ws3b, ws2b = ws1.astype(BF16), ws3.astype(BF16), ws2.astype(BF16)
    return [_combine(dest, x, hp, wt.T, gate, ws1b, ws3b, ws2b, ys, rpg)
            for (x, _, _, gate, rpg), (hp, _, wt, _), dest in zip(streams, routed, dests)]


def _mixer_ab(xs, cs, mx, mc, b, t, n_c, layer, nw, w_in, w_out, q_norm, k_norm, lam_vec, subln, conv_w,
              a_log, dt_bias, out_norm):
    n_main = w_in.shape[1] - 4 * DN_HEADS
    assert n_main % LANES == 0
    w_main = w_in[:, :n_main].astype(BF16)
    w_small = jnp.zeros((w_in.shape[0], LANES), F32).at[:, :4 * DN_HEADS].set(w_in[:, n_main:])
    z_x, zs_x = _norm_mod_matmul(xs, nw, mx[0], mx[1], w_main, t, w_small)
    z_c, zs_c = _norm_mod_matmul(cs, nw, mc[0], mc[1], w_main, b * n_c, w_small)
    nqk = 2 * DIFF_HEADS * DIFF_DK
    tabs = _rope_tables(t, DIFF_DK)
    qscale = DIFF_DK ** -0.5
    qd_x = _headnorm_rope(z_x, 0, nqk, q_norm, DIFF_DK, qscale, tabs, t)
    kd_x = _headnorm_rope(z_x, nqk, nqk, k_norm, DIFF_DK, 1.0, tabs, t)
    qd_c = _headnorm_rope(z_c, 0, nqk, q_norm, DIFF_DK, qscale, None, t)
    kd_c = _headnorm_rope(z_c, nqk, nqk, k_norm, DIFF_DK, 1.0, None, t)
    lam_init = 0.8 - 0.6 * math.exp(-0.3 * layer)
    od_x = _diff_attention(lam_vec, qd_x, kd_c, z_c, subln, lam_init, b, n_c, kx=kd_x, zx=z_x, t=t)
    od_c = _diff_attention(lam_vec, qd_c, kd_c, z_c, subln, lam_init, b, n_c)
    dn_col0 = 2 * nqk + DIFF_HEADS * DIFF_DV
    dn_x = _dn_prep(z_x, dn_col0, conv_w, t)
    dn_c = _dn_prep(z_c, dn_col0, conv_w, n_c)
    zst_x = zs_x[:, :4 * DN_HEADS].T
    zst_c = zs_c[:, :4 * DN_HEADS].T
    prm = _deltanet_params(a_log, dt_bias)
    s0 = jnp.zeros((b, DN_HEADS, DN_DK, DN_DV), F32)
    o_cf, s_cf = _deltanet(dn_c, zs_c, zst_c, prm, s0, b, n_c, False)
    o_xf, _ = _deltanet(dn_x, zs_x, zst_x, prm, s_cf, b, t, False)
    o_cb, s_cb = _deltanet(dn_c, zs_c, zst_c, prm, s0, b, n_c, True)
    o_xb, _ = _deltanet(dn_x, zs_x, zst_x, prm, s_cb, b, t, True)
    gate_col0 = dn_col0 + conv_w.shape[1]
    w_out_b = w_out.astype(BF16)
    x1 = _outproj_ab(od_x, o_xf, o_xb, z_x, gate_col0, out_norm, w_out_b, xs, mx[2], t)
    c1 = _outproj_ab(od_c, o_cf, o_cb, z_c, gate_col0, out_norm, w_out_b, cs, mc[2], b * n_c)
    return x1, c1


def _mixer_swa(xs, cs, mx, mc, b, t, n_c, nw, w_in, w_out, q_norm, k_norm, sink):
    nq = SWA_HEADS * SWA_DH
    nkv = SWA_KV_HEADS * SWA_DH
    w_b = w_in.astype(BF16)
    z_x = _norm_mod_matmul(xs, nw, mx[0], mx[1], w_b, t)
    z_c = _norm_mod_matmul(cs, nw, mc[0], mc[1], w_b[:, nq:], b * n_c)
    tabs = _rope_tables(t, SWA_DH)
    q = _headnorm_rope(z_x, 0, nq, q_norm, SWA_DH, SWA_DH ** -0.5, tabs, t)
    k_x = _headnorm_rope(z_x, nq, nkv, k_norm, SWA_DH, 1.0, tabs, t)
    k_c = _headnorm_rope(z_c, 0, nkv, k_norm, SWA_DH, 1.0, None, t)
    att = _swa_attention(sink.reshape(SWA_KV_HEADS, SWA_GROUP), q, _dup_heads(k_x, 0), _dup_heads(z_x, nq + nkv),
                         _dup_heads(k_c, 0), _dup_heads(z_c, nkv), b, t, n_c)
    return _outproj(att, w_out.astype(BF16), xs, mx[2], t)


def kernel(x, c, ctx, c_ctx, mod_w, mod_b, norm_mix, norm_ffn, ab_w_in, ab_w_out, diff_q_norm, diff_k_norm, diff_lambda, diff_subln, dn_conv, dn_a_log, dn_dt_bias, dn_out_norm, swa_w_in, swa_w_out, swa_q_norm, swa_k_norm, swa_sink, router_w, router_bias, exp_w1, exp_w3, exp_w2, shared_w1, shared_w3, shared_w2):
    b, t, d = x.shape
    n_c = ctx.shape[1]
    depth = mod_w.shape[0]
    assert depth == 2 and t % Q_BLOCK == 0 and t % DN_CHUNK == 0 and n_c % DN_CHUNK == 0
    xs = x.reshape(b * t, d)
    cs = ctx.reshape(b * n_c, d)
    n_mod = -(-(b + 1) // SUBLANES) * SUBLANES
    a_mod = jnp.zeros((n_mod, d), F32).at[0].set(c_ctx).at[1:1 + b].set(c)
    for layer in range(depth):
        with_ctx = layer < depth - 1
        p = layer // 2
        mod = _modulation(a_mod, mod_w, layer, mod_b[layer])
        mc = [mod[0:1, j * d:(j + 1) * d].reshape(1, 1, d) for j in range(6)]
        mx = [mod[1:1 + b, j * d:(j + 1) * d].reshape(b, 1, d) for j in range(6)]
        if layer % 2 == 0:
            xs, c_new = _mixer_ab(xs, cs, mx, mc, b, t, n_c, layer, norm_mix[layer], ab_w_in[p], ab_w_out[p],
                                  diff_q_norm[p], diff_k_norm[p], diff_lambda[p], diff_subln[p], dn_conv[p],
                                  dn_a_log[p], dn_dt_bias[p], dn_out_norm[p])
        else:
            assert not with_ctx
            xs = _mixer_swa(xs, cs, mx, mc, b, t, n_c, norm_mix[layer], swa_w_in[p], swa_w_out[p],
                            swa_q_norm[p], swa_k_norm[p], swa_sink[p])
            c_new = None
        moe_w = (layer, norm_ffn[layer], router_w[layer], router_bias[layer], exp_w1, exp_w3, exp_w2,
                 shared_w1[layer], shared_w3[layer], shared_w2[layer])
        if with_ctx:
            cs, xs = _moe([(c_new, mc[3], mc[4], mc[5], b * n_c), (xs, mx[3], mx[4], mx[5], t)], *moe_w)
        else:
            (xs,) = _moe([(xs, mx[3], mx[4], mx[5], t)], *moe_w)
    return xs.reshape(b, t, d)
```

```python
import functools
import math

import jax
import jax.numpy as jnp
import numpy as np
from jax import lax
from jax.experimental import pallas as pl
from jax.experimental.pallas import tpu as pltpu

F32 = jnp.float32
BF16 = jnp.bfloat16
I32 = jnp.int32
U32 = jnp.uint32

EPS = 1e-6
GRID_W = 64
ROPE_BASE = 10000.0
DIFF_HEADS = 4
DIFF_DK = 128
DIFF_DV = 256
DN_HEADS = 8
DN_DK = 128
DN_DV = 128
DN_CONV = 5
DN_CHUNK = 128
SWA_HEADS = 32
SWA_KV_HEADS = 4
SWA_GROUP = SWA_HEADS // SWA_KV_HEADS
SWA_DH = 64
WINDOW = 128
Q_BLOCK = 128
N_EXPERTS = 64
TOP_K = 8
N_GROUPS = 8
GROUP_SIZE = N_EXPERTS // N_GROUPS
TOPK_GROUPS = 4
ROUTED_SCALE = 2.5
MOE_BLOCK = 256

LANES = 128
SUBLANES = 8
VMEM_LIMIT_BYTES = 56 * 1024 * 1024

NT_DIMS = (((1,), (1,)), ((), ()))
LOG2E = math.log2(math.e)


def _params(*semantics):
    return pltpu.CompilerParams(dimension_semantics=semantics, vmem_limit_bytes=VMEM_LIMIT_BYTES)


def _tile(n, pref, mult):
    if n <= pref:
        return n
    t = pref - pref % mult
    while t > mult and n % t:
        t -= mult
    assert n % t == 0, (n, pref, mult)
    return t


def _mm(a, b):
    return jnp.dot(a.astype(BF16), b.astype(BF16), preferred_element_type=F32)


def _split2(x):
    hi = x.astype(BF16)
    lo = (x - hi.astype(F32)).astype(BF16)
    return hi, lo


def _dot3(a, b, dims=None):
    if dims is None:
        dims = (((a.ndim - 1,), (0,)), ((), ()))
    ah, al = _split2(a)
    bh, bl = _split2(b)
    d = functools.partial(lax.dot_general, dimension_numbers=dims, preferred_element_type=F32)
    return d(ah, bh) + d(ah, bl) + d(al, bh)


def _silu(x):
    return x * jax.nn.sigmoid(x)


def _softplus(x):
    return jnp.maximum(x, 0.0) + jnp.log(1.0 + jnp.exp(-jnp.abs(x)))


def _pack_pairs(lo, hi):
    ulo = lax.bitcast_convert_type(lo.astype(BF16).astype(F32), U32) >> 16
    uhi = lax.bitcast_convert_type(hi.astype(BF16).astype(F32), U32) & jnp.uint32(0xFFFF0000)
    return ulo | uhi


def _unpack_pairs(u):
    lo = lax.bitcast_convert_type(u << 16, F32)
    hi = lax.bitcast_convert_type(u & jnp.uint32(0xFFFF0000), F32)
    return lo, hi


def _store_token_tiles(ref, base, packed):
    n, width = packed.shape
    tr = width // LANES
    for j in range(tr):
        ref[pl.ds(base + j, n, stride=tr), :] = packed[:, j * LANES:(j + 1) * LANES]


def _load_token_tiles(ref, base, n, tr):
    return jnp.concatenate([ref[pl.ds(base + j, n, stride=tr), :] for j in range(tr)], axis=1)


def _tile_copy(src, src_row, dst, dst_row, tr, sem):
    return pltpu.make_async_copy(src.at[pl.ds(pl.multiple_of(src_row, tr), tr)],
                                 dst.at[pl.ds(pl.multiple_of(dst_row, tr), tr)], sem)


def _mod_kernel(a_ref, w_ref, b_ref, o_ref):
    o_ref[...] = _dot3(_silu(a_ref[...]), w_ref[0]) + b_ref[...]


def _modulation(a, w_all, layer, b):
    r, d = a.shape
    n = w_all.shape[2]
    tn = _tile(n, 768, LANES)
    return pl.pallas_call(
        _mod_kernel,
        grid=(n // tn,),
        in_specs=[pl.BlockSpec((r, d), lambda j: (0, 0)),
                  pl.BlockSpec((1, d, tn), lambda j: (layer, 0, j)),
                  pl.BlockSpec((1, tn), lambda j: (0, j))],
        out_specs=pl.BlockSpec((r, tn), lambda j: (0, j)),
        out_shape=jax.ShapeDtypeStruct((r, n), F32),
        compiler_params=_params("parallel"),
        name="modulation",
    )(a, w_all, b.reshape(1, n))


def _norm_mod(x, nw, sh, sc):
    y = x * lax.rsqrt(jnp.mean(x * x, axis=-1, keepdims=True) + EPS) * nw
    return y * (1.0 + sc) + sh


def _nmm_kernel(x_ref, nw_ref, sh_ref, sc_ref, w_ref, *rest, has_small):
    if has_small:
        ws_ref, o_ref, os_ref, h_ref = rest
    else:
        o_ref, h_ref = rest

    @pl.when(pl.program_id(1) == 0)
    def _():
        h = _norm_mod(x_ref[...], nw_ref[...], sh_ref[0], sc_ref[0])
        h_ref[...] = h.astype(BF16)
        if has_small:
            os_ref[...] = _dot3(h, ws_ref[...])

    o_ref[...] = jnp.dot(h_ref[...], w_ref[...], preferred_element_type=F32).astype(o_ref.dtype)


def _norm_mod_matmul(x, nw, sh, sc, w, rows_per_group, w_small=None):
    m, d = x.shape
    n = w.shape[1]
    tm = math.gcd(_tile(m, 512, SUBLANES), rows_per_group)
    tn = _tile(n, 1024, LANES)
    has_small = w_small is not None
    grp = lambda i, j: ((i * tm) // rows_per_group, 0, 0)
    in_specs = [pl.BlockSpec((tm, d), lambda i, j: (i, 0)),
                pl.BlockSpec((1, d), lambda i, j: (0, 0)),
                pl.BlockSpec((1, 1, d), grp),
                pl.BlockSpec((1, 1, d), grp),
                pl.BlockSpec((d, tn), lambda i, j: (0, j))]
    args = [x, nw.reshape(1, d), sh, sc, w]
    out_specs = [pl.BlockSpec((tm, tn), lambda i, j: (i, j))]
    out_shape = [jax.ShapeDtypeStruct((m, n), BF16)]
    if has_small:
        in_specs.append(pl.BlockSpec((d, LANES), lambda i, j: (0, 0)))
        args.append(w_small)
        out_specs.append(pl.BlockSpec((tm, LANES), lambda i, j: (i, 0)))
        out_shape.append(jax.ShapeDtypeStruct((m, LANES), F32))
    outs = pl.pallas_call(
        functools.partial(_nmm_kernel, has_small=has_small),
        grid=(m // tm, n // tn),
        in_specs=in_specs,
        out_specs=out_specs,
        out_shape=out_shape,
        scratch_shapes=[pltpu.VMEM((tm, d), BF16)],
        compiler_params=_params("parallel", "arbitrary"),
        name="norm_mod_matmul",
    )(*args)
    return outs if has_small else outs[0]


def _rope_tables(t, head_dim):
    q = head_dim // 4
    pos = jnp.arange(t, dtype=I32)
    row = (pos // GRID_W).astype(F32)
    col = (pos % GRID_W).astype(F32)
    axis_dim = head_dim // 2
    inv_freq = ROPE_BASE ** (-jnp.arange(0, axis_dim, 2, dtype=F32) / axis_dim)
    lane = jnp.arange(LANES) % head_dim
    freq = inv_freq[lane % q]
    p = jnp.where((lane < head_dim // 2)[None, :], row[:, None], col[:, None])
    ang = p * freq[None, :]
    sign = jnp.where((lane % (2 * q)) < q, -1.0, 1.0)
    return jnp.cos(ang), jnp.sin(ang) * sign[None, :]


def _hnr_kernel(z_ref, w_ref, *rest, hd, scale, rope):
    if rope:
        cos_ref, sin_ref, o_ref = rest
    else:
        (o_ref,) = rest
    tm, cb = z_ref.shape
    lane = lax.broadcasted_iota(I32, (tm, LANES), 1)
    w = w_ref[...]
    q = hd // 4
    for g in range(cb // LANES):
        sl = slice(g * LANES, (g + 1) * LANES)
        x = z_ref[:, sl].astype(F32)
        x2 = x * x
        if hd == LANES:
            ms = jnp.mean(x2, axis=-1, keepdims=True)
        else:
            lo = jnp.sum(jnp.where(lane < hd, x2, 0.0), axis=-1, keepdims=True)
            hi = jnp.sum(jnp.where(lane >= hd, x2, 0.0), axis=-1, keepdims=True)
            ms = jnp.where(lane < hd, lo, hi) * (1.0 / hd)
        y = x * lax.rsqrt(ms + EPS) * w
        if rope:
            nxt = pltpu.roll(y, LANES - q, 1)
            prv = pltpu.roll(y, q, 1)
            partner = jnp.where((lane & (2 * q - 1)) < q, nxt, prv)
            y = y * cos_ref[...] + partner * sin_ref[...]
        o_ref[:, sl] = (y * scale).astype(o_ref.dtype)


def _headnorm_rope(z, col0, ncols, w, hd, scale, tables, t):
    m = z.shape[0]
    cb = min(512, ncols)
    assert ncols % cb == 0 and col0 % cb == 0 and LANES % hd == 0
    tm = _tile(t, 512, SUBLANES) if tables is not None else _tile(m, 512, SUBLANES)
    cblk0 = col0 // cb
    rope = tables is not None
    in_specs = [pl.BlockSpec((tm, cb), lambda i, j: (i, cblk0 + j)),
                pl.BlockSpec((1, LANES), lambda i, j: (0, 0))]
    args = [z, jnp.tile(w, LANES // hd).reshape(1, LANES)]
    if rope:
        nt = t // tm
        in_specs += [pl.BlockSpec((tm, LANES), lambda i, j: (i % nt, 0))] * 2
        args += list(tables)
    return pl.pallas_call(
        functools.partial(_hnr_kernel, hd=hd, scale=scale, rope=rope),
        grid=(m // tm, ncols // cb),
        in_specs=in_specs,
        out_specs=pl.BlockSpec((tm, cb), lambda i, j: (i, j)),
        out_shape=jax.ShapeDtypeStruct((m, ncols), BF16),
        compiler_params=_params("parallel", "parallel"),
        name="headnorm_rope",
    )(*args)


def _diff_attn_kernel(lv_ref, q_ref, *rest, lam_init, has_lat):
    if has_lat:
        kx_ref, vx_ref, kc_ref, vc_ref, w_ref, o_ref = rest
    else:
        kc_ref, vc_ref, w_ref, o_ref = rest
    lv = lv_ref[...]
    lam = (jnp.exp(jnp.sum(lv[0:1] * lv[1:2], keepdims=True))
           - jnp.exp(jnp.sum(lv[2:3] * lv[3:4], keepdims=True)) + lam_init)
    q = q_ref[...]

    def probs(c):
        sl = slice(c * DIFF_DK, (c + 1) * DIFF_DK)
        qc = q[:, sl]
        s_c = lax.dot_general(qc, kc_ref[:, sl], NT_DIMS, preferred_element_type=F32)
        m = jnp.max(s_c, axis=-1, keepdims=True)
        p_x = None
        if has_lat:
            s_x = lax.dot_general(qc, kx_ref[:, sl], NT_DIMS, preferred_element_type=F32)
            m = jnp.maximum(m, jnp.max(s_x, axis=-1, keepdims=True))
            p_x = jnp.exp(s_x - m)
        p_c = jnp.exp(s_c - m)
        l = jnp.sum(p_c, axis=-1, keepdims=True)
        if has_lat:
            l = l + jnp.sum(p_x, axis=-1, keepdims=True)
        return p_x, p_c, 1.0 / l

    p1x, p1c, r1 = probs(0)
    p2x, p2c, r2 = probs(1)
    r2 = r2 * lam
    o = jnp.dot((p1c * r1 - p2c * r2).astype(BF16), vc_ref[...], preferred_element_type=F32)
    if has_lat:
        o = o + jnp.dot((p1x * r1 - p2x * r2).astype(BF16), vx_ref[...], preferred_element_type=F32)
    o = o * lax.rsqrt(jnp.mean(o * o, axis=-1, keepdims=True) + EPS)
    o_ref[...] = (o * w_ref[...] * (1.0 - lam_init)).astype(o_ref.dtype)


def _diff_attention(lam_vec, q, kc, zc, subln, lam_init, b, n_c, kx=None, zx=None, t=None):
    has_lat = kx is not None
    hw = 2 * DIFF_DK
    vblk0 = (2 * DIFF_HEADS * hw) // DIFF_DV
    tq_all = t if has_lat else n_c
    tq = _tile(tq_all, 256, SUBLANES)
    nq = tq_all // tq
    in_specs = [pl.BlockSpec((4, DIFF_DK), lambda bi, h, qi: (0, 0)),
                pl.BlockSpec((tq, hw), lambda bi, h, qi: (bi * nq + qi, h))]
    args = [lam_vec, q]
    if has_lat:
        in_specs += [pl.BlockSpec((t, hw), lambda bi, h, qi: (bi, h)),
                     pl.BlockSpec((t, DIFF_DV), lambda bi, h, qi: (bi, vblk0 + h))]
        args += [kx, zx]
    in_specs += [pl.BlockSpec((n_c, hw), lambda bi, h, qi: (bi, h)),
                 pl.BlockSpec((n_c, DIFF_DV), lambda bi, h, qi: (bi, vblk0 + h)),
                 pl.BlockSpec((1, DIFF_DV), lambda bi, h, qi: (0, 0))]
    args += [kc, zc, subln.reshape(1, DIFF_DV)]
    return pl.pallas_call(
        functools.partial(_diff_attn_kernel, lam_init=lam_init, has_lat=has_lat),
        grid=(b, DIFF_HEADS, nq),
        in_specs=in_specs,
        out_specs=pl.BlockSpec((tq, DIFF_DV), lambda bi, h, qi: (bi * nq + qi, h)),
        out_shape=jax.ShapeDtypeStruct((b * tq_all, DIFF_HEADS * DIFF_DV), BF16),
        compiler_params=_params("parallel", "parallel", "arbitrary"),
        name="diff_attention",
    )(*args)


def _dn_prep_kernel(z_ref, cw_ref, o_ref, pad_ref, *, seg):
    halo = SUBLANES
    pad_ref[0:halo, :] = jnp.zeros((halo, LANES), F32)
    pad_ref[halo + seg:2 * halo + seg, :] = jnp.zeros((halo, LANES), F32)
    pad_ref[halo:halo + seg, :] = z_ref[...].astype(F32)
    kind = pl.program_id(1) // DN_HEADS
    qk_scale = jnp.where(kind == 0, DN_DK ** -0.5, 1.0).astype(F32)
    rows = _tile(seg, 256, SUBLANES)
    for r0 in range(0, seg, rows):
        acc = jnp.zeros((rows, LANES), F32)
        for j in range(DN_CONV):
            acc = acc + cw_ref[j:j + 1, :] * pad_ref[pl.ds(halo + r0 + j - DN_CONV // 2, rows), :]
        y = _silu(acc)
        nrm = y * lax.rsqrt(jnp.sum(y * y, axis=-1, keepdims=True) + EPS) * qk_scale
        o_ref[r0:r0 + rows, :] = jnp.where(kind < 2, nrm, y).astype(o_ref.dtype)


def _dn_prep(z, col0, conv_w, seg):
    m = z.shape[0]
    ncols = conv_w.shape[1]
    cblk0 = col0 // LANES
    cw = jnp.zeros((SUBLANES, ncols), F32).at[:DN_CONV].set(conv_w)
    return pl.pallas_call(
        functools.partial(_dn_prep_kernel, seg=seg),
        grid=(m // seg, ncols // LANES),
        in_specs=[pl.BlockSpec((seg, LANES), lambda s, g: (s, cblk0 + g)),
                  pl.BlockSpec((SUBLANES, LANES), lambda s, g: (0, g))],
        out_specs=pl.BlockSpec((seg, LANES), lambda s, g: (s, g)),
        out_shape=jax.ShapeDtypeStruct((m, ncols), BF16),
        scratch_shapes=[pltpu.VMEM((seg + 2 * SUBLANES, LANES), F32)],
        compiler_params=_params("parallel", "parallel"),
        name="dn_prep",
    )(z, cw)


def _deltanet_kernel(q_ref, k_ref, v_ref, zs_ref, zst_ref, pr_ref, pca_ref, pcd_ref, s0_ref,
                     o_ref, sout_ref, s_ref, *, reverse, n_chunks):
    step = pl.program_id(1)
    c = DN_CHUNK

    @pl.when(step == 0)
    def _():
        s_ref[...] = s0_ref[0]

    ri = lax.broadcasted_iota(I32, (c, c), 0)
    ci = lax.broadcasted_iota(I32, (c, c), 1)
    if reverse:
        later, strict, later_t = ri <= ci, ri < ci, ri >= ci
    else:
        later, strict, later_t = ri >= ci, ri > ci, ri <= ci
    eye = (ri == ci).astype(F32)
    tri = later.astype(BF16)
    tri_t = later_t.astype(BF16)

    zs = zs_ref[...]
    beta_cols = jax.nn.sigmoid(zs)
    g_cols = -jnp.exp(pr_ref[0:1, :]) * _softplus(zs + pr_ref[1:2, :])
    g_hi = g_cols.astype(BF16)
    g_r1 = g_cols - g_hi.astype(F32)
    g_mid = g_r1.astype(BF16)
    g_lo = (g_r1 - g_mid.astype(F32)).astype(BF16)
    d = functools.partial(jnp.dot, preferred_element_type=F32)
    gc_cols = d(tri, g_hi) + d(tri, g_mid) + d(tri, g_lo)
    g_rows = -jnp.exp(pca_ref[...]) * _softplus(zst_ref[...] + pcd_ref[...])
    h_hi = g_rows.astype(BF16)
    h_r1 = g_rows - h_hi.astype(F32)
    h_mid = h_r1.astype(BF16)
    h_lo = (h_r1 - h_mid.astype(F32)).astype(BF16)
    gc_rows = d(h_hi, tri_t) + d(h_mid, tri_t) + d(h_lo, tri_t)

    dir_off = DN_HEADS if reverse else 0
    last = 0 if reverse else c - 1
    neg_inf = jnp.float32(-jnp.inf)
    heads = []
    for h in range(DN_HEADS):
        sl = slice(h * DN_DK, (h + 1) * DN_DK)
        cb = dir_off + h
        cg = 2 * DN_HEADS + dir_off + h
        beta = beta_cols[:, cb:cb + 1]
        gcol = gc_cols[:, cg:cg + 1]
        grow = gc_rows[cg:cg + 1, :]
        glast = grow[:, last:last + 1]
        q = q_ref[:, sl]
        k = k_ref[:, sl]
        kf = k.astype(F32)
        decay = jnp.exp(jnp.where(later, gcol - grow, neg_inf))
        kb = kf * beta
        both = lax.dot_general(jnp.concatenate([kb.astype(BF16), q], axis=0), k, NT_DIMS,
                               preferred_element_type=F32)
        lmat = jnp.where(strict, both[:c] * decay, 0.0)
        eg = jnp.exp(gcol)
        heads.append(dict(
            sl=sl, glast=glast, lmat=lmat, amat=both[c:] * decay,
            inv=eye - jnp.where((ri >> 1) == (ci >> 1), lmat, 0.0),
            rhs=jnp.concatenate([v_ref[:, sl].astype(F32) * beta, kb * eg], axis=1).astype(BF16),
            qe=(q.astype(F32) * eg).astype(BF16),
            kdec_t=(kf * jnp.exp(glast - gcol)).T.astype(BF16)))
    lev = 1
    while (1 << lev) < c:
        blk = ((ri >> (lev + 1)) == (ci >> (lev + 1))) & ((ri >> lev) != (ci >> lev))
        half = [_mm(hd["inv"], jnp.where(blk, hd["lmat"], 0.0)) for hd in heads]
        for hd, t in zip(heads, half):
            hd["inv"] = hd["inv"] - _mm(t, hd["inv"])
        lev += 1
    uws = [_mm(hd["inv"], hd["rhs"]) for hd in heads]
    states = [s_ref[h] for h in range(DN_HEADS)]
    new_states = []
    for hd, uw, s in zip(heads, uws, states):
        ws_qs = _mm(jnp.concatenate([uw[:, DN_DV:].astype(BF16), hd["qe"]], axis=0), s)
        v_new = uw[:, :DN_DV] - ws_qs[:c]
        o_ref[:, hd["sl"]] = ws_qs[c:] + _mm(hd["amat"], v_new)
        new_states.append(s * jnp.exp(hd["glast"]) + _mm(hd["kdec_t"], v_new))
    for h in range(DN_HEADS):
        s_ref[h] = new_states[h]

    @pl.when(step == n_chunks - 1)
    def _():
        sout_ref[0] = s_ref[...]


def _deltanet(dn, zs, zst, prm, s0, b, seg, reverse):
    pr, pca, pcd = prm
    n = seg // DN_CHUNK
    hw = DN_HEADS * DN_DK
    rb = (lambda bi, s: bi * n + (n - 1 - s)) if reverse else (lambda bi, s: bi * n + s)
    state = pl.BlockSpec((1, DN_HEADS, DN_DK, DN_DV), lambda bi, s: (bi, 0, 0, 0))
    return pl.pallas_call(
        functools.partial(_deltanet_kernel, reverse=reverse, n_chunks=n),
        grid=(b, n),
        in_specs=[pl.BlockSpec((DN_CHUNK, hw), lambda bi, s: (rb(bi, s), 0)),
                  pl.BlockSpec((DN_CHUNK, hw), lambda bi, s: (rb(bi, s), 1)),
                  pl.BlockSpec((DN_CHUNK, hw), lambda bi, s: (rb(bi, s), 2)),
                  pl.BlockSpec((DN_CHUNK, LANES), lambda bi, s: (rb(bi, s), 0)),
                  pl.BlockSpec((4 * DN_HEADS, DN_CHUNK), lambda bi, s: (0, rb(bi, s))),
                  pl.BlockSpec((SUBLANES, LANES), lambda bi, s: (0, 0)),
                  pl.BlockSpec((4 * DN_HEADS, LANES), lambda bi, s: (0, 0)),
                  pl.BlockSpec((4 * DN_HEADS, LANES), lambda bi, s: (0, 0)),
                  state],
        out_specs=[pl.BlockSpec((DN_CHUNK, hw), lambda bi, s: (rb(bi, s), 0)), state],
        out_shape=[jax.ShapeDtypeStruct((b * seg, hw), F32),
                   jax.ShapeDtypeStruct((b, DN_HEADS, DN_DK, DN_DV), F32)],
        scratch_shapes=[pltpu.VMEM((DN_HEADS, DN_DK, DN_DV), F32)],
        compiler_params=_params("parallel", "arbitrary"),
        name="deltanet_bwd" if reverse else "deltanet_fwd",
    )(dn, dn, dn, zs, zst, pr, pca, pcd, s0)


def _deltanet_params(a_log, dt_bias):
    nh = 2 * DN_HEADS
    a = a_log.reshape(nh).astype(F32)
    dtb = dt_bias.reshape(nh).astype(F32)
    pr = jnp.zeros((SUBLANES, LANES), F32).at[0, nh:2 * nh].set(a).at[1, nh:2 * nh].set(dtb)
    pca = jnp.zeros((2 * nh, LANES), F32).at[nh:].set(jnp.broadcast_to(a[:, None], (nh, LANES)))
    pcd = jnp.zeros((2 * nh, LANES), F32).at[nh:].set(jnp.broadcast_to(dtb[:, None], (nh, LANES)))
    return pr, pca, pcd


def _outproj_ab_kernel(od_ref, of_ref, ob_ref, gate_ref, nw_ref, w_ref, x_ref, g_ref, o_ref, h_ref):
    @pl.when(pl.program_id(1) == 0)
    def _():
        nd = od_ref.shape[1]
        h_ref[:, :nd] = od_ref[...]
        for h in range(DN_HEADS):
            sl = slice(h * DN_DV, (h + 1) * DN_DV)
            y = of_ref[:, sl] + ob_ref[:, sl]
            y = y * lax.rsqrt(jnp.mean(y * y, axis=-1, keepdims=True) + EPS) * nw_ref[...]
            h_ref[:, nd + h * DN_DV:nd + (h + 1) * DN_DV] = (y * _silu(gate_ref[:, sl].astype(F32))).astype(BF16)

    y = jnp.dot(h_ref[...], w_ref[...], preferred_element_type=F32)
    o_ref[...] = x_ref[...] + g_ref[0] * y


def _outproj_ab(od, o_f, o_b, z, gate_col0, out_norm, w, x, gate, rows_per_group):
    m, d = x.shape
    nd, nn = od.shape[1], o_f.shape[1]
    tm = math.gcd(_tile(m, 512, SUBLANES), rows_per_group)
    tn = _tile(d, 1024, LANES)
    grp = lambda i, j: ((i * tm) // rows_per_group, 0, j)
    gblk = gate_col0 // nn
    return pl.pallas_call(
        _outproj_ab_kernel,
        grid=(m // tm, d // tn),
        in_specs=[pl.BlockSpec((tm, nd), lambda i, j: (i, 0)),
                  pl.BlockSpec((tm, nn), lambda i, j: (i, 0)),
                  pl.BlockSpec((tm, nn), lambda i, j: (i, 0)),
                  pl.BlockSpec((tm, nn), lambda i, j: (i, gblk)),
                  pl.BlockSpec((1, DN_DV), lambda i, j: (0, 0)),
                  pl.BlockSpec((nd + nn, tn), lambda i, j: (0, j)),
                  pl.BlockSpec((tm, tn), lambda i, j: (i, j)),
                  pl.BlockSpec((1, 1, tn), grp)],
        out_specs=pl.BlockSpec((tm, tn), lambda i, j: (i, j)),
        out_shape=jax.ShapeDtypeStruct((m, d), F32),
        scratch_shapes=[pltpu.VMEM((tm, nd + nn), BF16)],
        compiler_params=_params("parallel", "arbitrary"),
        name="outproj_ab",
    )(od, o_f, o_b, z, out_norm.reshape(1, DN_DV), w, x, gate)


def _outproj_kernel(a_ref, w_ref, x_ref, g_ref, o_ref):
    o_ref[...] = x_ref[...] + g_ref[0] * jnp.dot(a_ref[...], w_ref[...], preferred_element_type=F32)


def _outproj(a, w, x, gate, rows_per_group):
    m, d = x.shape
    kdim = a.shape[1]
    tm = math.gcd(_tile(m, 512, SUBLANES), rows_per_group)
    tn = _tile(d, 1024, LANES)
    grp = lambda i, j: ((i * tm) // rows_per_group, 0, j)
    return pl.pallas_call(
        _outproj_kernel,
        grid=(m // tm, d // tn),
        in_specs=[pl.BlockSpec((tm, kdim), lambda i, j: (i, 0)),
                  pl.BlockSpec((kdim, tn), lambda i, j: (0, j)),
                  pl.BlockSpec((tm, tn), lambda i, j: (i, j)),
                  pl.BlockSpec((1, 1, tn), grp)],
        out_specs=pl.BlockSpec((tm, tn), lambda i, j: (i, j)),
        out_shape=jax.ShapeDtypeStruct((m, d), F32),
        compiler_params=_params("parallel", "arbitrary"),
        name="outproj",
    )(a, w, x, gate)


def _swa_kernel(sink_ref, bias_ref, q_ref, k0_ref, k1_ref, k2_ref, v0_ref, v1_ref, v2_ref, kc_ref, vc_ref,
                o_ref):
    kvh = pl.program_id(1)
    qb = Q_BLOCK
    npair = SWA_GROUP // 2
    lane = lax.broadcasted_iota(I32, (qb, LANES), 1)
    q = q_ref[...]
    parts = []
    for p in range(npair):
        qp = q[:, p * LANES:(p + 1) * LANES]
        parts.append(jnp.where(lane < SWA_DH, qp, jnp.zeros_like(qp)))
        parts.append(jnp.where(lane >= SWA_DH, qp, jnp.zeros_like(qp)))
    qq = jnp.concatenate(parts, axis=0)
    k_lat = jnp.concatenate([k0_ref[...], k1_ref[...], k2_ref[...]], axis=0)
    v_lat = jnp.concatenate([v0_ref[...], v1_ref[...], v2_ref[...]], axis=0)
    s_lat = lax.dot_general(qq, k_lat, NT_DIMS, preferred_element_type=F32)
    s_ctx = lax.dot_general(qq, kc_ref[...], NT_DIMS, preferred_element_type=F32)
    bias = bias_ref[0]
    vc = vc_ref[...]
    outs = []
    for g in range(SWA_GROUP):
        rs = slice(g * qb, (g + 1) * qb)
        sl = s_lat[rs] + bias
        sc = s_ctx[rs]
        sink = sink_ref[kvh, g] * LOG2E
        m = jnp.maximum(jnp.maximum(jnp.max(sl, axis=-1, keepdims=True),
                                    jnp.max(sc, axis=-1, keepdims=True)), sink)
        el = jnp.exp2(sl - m)
        ec = jnp.exp2(sc - m)
        l = jnp.sum(el, axis=-1, keepdims=True) + jnp.sum(ec, axis=-1, keepdims=True) + jnp.exp2(sink - m)
        o = (jnp.dot(el.astype(BF16), v_lat, preferred_element_type=F32)
             + jnp.dot(ec.astype(BF16), vc, preferred_element_type=F32))
        outs.append(o * (1.0 / l))
    for p in range(npair):
        o_ref[:, p * LANES:(p + 1) * LANES] = jnp.where(lane < SWA_DH, outs[2 * p], outs[2 * p + 1]).astype(o_ref.dtype)


def _swa_attention(sink, q, kx, vx, kc, vc, b, t, n_c):
    qb = Q_BLOCK
    nb = t // qb
    gw = SWA_GROUP * SWA_DH
    lat = lambda off: pl.BlockSpec(
        (qb, LANES), lambda bi, h, i: (bi * nb + jnp.clip(i + off, 0, nb - 1), h))
    ctx = pl.BlockSpec((n_c, LANES), lambda bi, h, i: (bi, h))
    r_io = np.arange(qb)[:, None]
    c_io = np.arange(3 * qb)[None, :]
    inside = np.abs(r_io + qb - c_io) <= WINDOW
    variants = [inside & ((c_io >= qb) | (v & 1 == 0)) & ((c_io < 2 * qb) | (v & 2 == 0)) for v in range(4)]
    bias = jnp.asarray(np.where(np.stack(variants), 0.0, -np.inf), F32)
    return pl.pallas_call(
        _swa_kernel,
        grid=(b, SWA_KV_HEADS, nb),
        in_specs=[pl.BlockSpec(memory_space=pltpu.SMEM),
                  pl.BlockSpec((1, qb, 3 * qb),
                               lambda bi, h, i: ((i == 0).astype(I32) + 2 * (i == nb - 1).astype(I32), 0, 0)),
                  pl.BlockSpec((qb, gw), lambda bi, h, i: (bi * nb + i, h)),
                  lat(-1), lat(0), lat(1), lat(-1), lat(0), lat(1), ctx, ctx],
        out_specs=pl.BlockSpec((qb, gw), lambda bi, h, i: (bi * nb + i, h)),
        out_shape=jax.ShapeDtypeStruct((b * t, SWA_HEADS * SWA_DH), BF16),
        compiler_params=_params("parallel", "parallel", "arbitrary"),
        name="swa_attention",
    )(sink, bias, q, kx, kx, kx, vx, vx, vx, kc, vc)


def _dup_heads(a, col0):
    m = a.shape[0]
    h = a[:, col0:col0 + SWA_KV_HEADS * SWA_DH].reshape(m, SWA_KV_HEADS, 1, SWA_DH)
    return jnp.broadcast_to(h, (m, SWA_KV_HEADS, LANES // SWA_DH, SWA_DH)).reshape(m, SWA_KV_HEADS * LANES)


def _first_max(vals, iota, size, axis):
    m = jnp.max(vals, axis=axis, keepdims=True)
    first = jnp.min(jnp.where(vals == m, iota, size), axis=axis, keepdims=True)
    return m, first


def _router_kernel(x_ref, nw_ref, sh_ref, sc_ref, rwt_ref, rb_ref, c0_ref,
                   hp_ref, idx_ref, wt_ref, rank_ref, cnt_ref, carry_ref):
    @pl.when(pl.program_id(0) == 0)
    def _():
        carry_ref[...] = c0_ref[...]

    h = _norm_mod(x_ref[...], nw_ref[...], sh_ref[0], sc_ref[0])
    tm, d = h.shape
    _store_token_tiles(hp_ref, 0, _pack_pairs(h[:, :d // 2], h[:, d // 2:]))
    scores = jax.nn.sigmoid(_dot3(rwt_ref[...], h, NT_DIMS))
    sel = scores + rb_ref[...]
    neg = jnp.float32(-jnp.inf)

    g_io = lax.broadcasted_iota(I32, (GROUP_SIZE, tm), 0)
    gs_rows = []
    for g in range(N_GROUPS):
        sg = sel[g * GROUP_SIZE:(g + 1) * GROUP_SIZE]
        m1, f1 = _first_max(sg, g_io, GROUP_SIZE, 0)
        m2 = jnp.max(jnp.where(g_io == f1, neg, sg), axis=0, keepdims=True)
        gs_rows.append(m1 + m2)
    cur = jnp.concatenate(gs_rows, axis=0)
    n_io = lax.broadcasted_iota(I32, (N_GROUPS, tm), 0)
    gmask = jnp.zeros((N_GROUPS, tm), I32)
    for _ in range(TOPK_GROUPS):
        _, f = _first_max(cur, n_io, N_GROUPS, 0)
        hit = n_io == f
        gmask = jnp.where(hit, 1, gmask)
        cur = jnp.where(hit, neg, cur)
    cur = jnp.concatenate(
        [jnp.where(gmask[g:g + 1] > 0, sel[g * GROUP_SIZE:(g + 1) * GROUP_SIZE], neg) for g in range(N_GROUPS)],
        axis=0)

    e_io = lax.broadcasted_iota(I32, (N_EXPERTS, tm), 0)
    chosen = jnp.zeros((N_EXPERTS, tm), F32)
    idx_rows, w_rows = [], []
    for _ in range(TOP_K):
        _, f = _first_max(cur, e_io, N_EXPERTS, 0)
        hit = e_io == f
        idx_rows.append(f)
        w_rows.append(jnp.sum(jnp.where(hit, scores, 0.0), axis=0, keepdims=True))
        chosen = jnp.where(hit, 1.0, chosen)
        cur = jnp.where(hit, neg, cur)
    idx = jnp.concatenate(idx_rows, axis=0)
    w = jnp.concatenate(w_rows, axis=0)
    idx_ref[...] = idx
    wt_ref[...] = w * (1.0 / jnp.sum(w, axis=0, keepdims=True)) * ROUTED_SCALE

    onehot = chosen.astype(BF16)
    before = (lax.broadcasted_iota(I32, (tm, tm), 0) < lax.broadcasted_iota(I32, (tm, tm), 1)).astype(BF16)
    base = carry_ref[:, 0:1] + jnp.dot(onehot, before, preferred_element_type=F32)
    rank_ref[...] = jnp.concatenate(
        [jnp.sum(jnp.where(e_io == idx_rows[k], base, 0.0), axis=0, keepdims=True) for k in range(TOP_K)],
        axis=0).astype(I32)
    carry_ref[...] = carry_ref[...] + jnp.sum(chosen, axis=1, keepdims=True)
    cnt_ref[...] = carry_ref[...]


def _router(x, nw, sh, sc, rwt, rb, counts0, rows_per_group):
    m, d = x.shape
    tm = math.gcd(_tile(m, 256, LANES), rows_per_group)
    assert (d // 2) % LANES == 0
    tr = d // 2 // LANES
    grp = lambda i: ((i * tm) // rows_per_group, 0, 0)
    tok = lambda rows: pl.BlockSpec((rows, tm), lambda i: (0, i))
    return pl.pallas_call(
        _router_kernel,
        grid=(m // tm,),
        in_specs=[pl.BlockSpec((tm, d), lambda i: (i, 0)),
                  pl.BlockSpec((1, d), lambda i: (0, 0)),
                  pl.BlockSpec((1, 1, d), grp),
                  pl.BlockSpec((1, 1, d), grp),
                  pl.BlockSpec((N_EXPERTS, d), lambda i: (0, 0)),
                  pl.BlockSpec((N_EXPERTS, 1), lambda i: (0, 0)),
                  pl.BlockSpec((N_EXPERTS, LANES), lambda i: (0, 0))],
        out_specs=[pl.BlockSpec((tm * tr, LANES), lambda i: (i, 0)),
                   tok(TOP_K), tok(TOP_K), tok(TOP_K),
                   pl.BlockSpec((N_EXPERTS, LANES), lambda i: (0, 0))],
        out_shape=[jax.ShapeDtypeStruct((m * tr, LANES), U32),
                   jax.ShapeDtypeStruct((TOP_K, m), I32),
                   jax.ShapeDtypeStruct((TOP_K, m), F32),
                   jax.ShapeDtypeStruct((TOP_K, m), I32),
                   jax.ShapeDtypeStruct((N_EXPERTS, LANES), F32)],
        scratch_shapes=[pltpu.VMEM((N_EXPERTS, LANES), F32)],
        compiler_params=_params("arbitrary"),
        name="moe_router",
    )(x, nw.reshape(1, d), sh, sc, rwt, rb.reshape(N_EXPERTS, 1), counts0)


def _dispatch_kernel(nv_ref, dest_ref, hp_ref, xs_ref, zero_ref, sem, *, tr):
    tm = hp_ref.shape[0] // tr
    blk_rows = MOE_BLOCK * tr

    @pl.when(pl.program_id(0) == 0)
    def _():
        zero_ref[...] = jnp.zeros(zero_ref.shape, U32)

        def fill(blk, carry):
            @pl.when(nv_ref[blk] < MOE_BLOCK)
            def _():
                cp = pltpu.make_async_copy(
                    zero_ref, xs_ref.at[pl.ds(pl.multiple_of(blk * blk_rows, blk_rows), blk_rows)], sem)
                cp.start()
                cp.wait()
            return carry

        lax.fori_loop(0, nv_ref.shape[0], fill, 0)

    def copy(t, k):
        return _tile_copy(hp_ref, t * tr, xs_ref, dest_ref[t * TOP_K + k], tr, sem)

    def start(t, carry):
        for k in range(TOP_K):
            copy(t, k).start(priority=k % 2)
        return carry

    def wait(t, carry):
        for k in range(TOP_K):
            copy(t, k).wait()
        return carry

    lax.fori_loop(0, tm, start, 0)
    lax.fori_loop(0, tm, wait, 0)


def _dispatch(block_nv, dest, hp, n_rows, tr):
    m = hp.shape[0] // tr
    tm = _tile(m, 512, LANES)
    return pl.pallas_call(
        functools.partial(_dispatch_kernel, tr=tr),
        grid_spec=pltpu.PrefetchScalarGridSpec(
            num_scalar_prefetch=1,
            grid=(m // tm,),
            in_specs=[pl.BlockSpec((tm * TOP_K,), lambda i, nv: (i,), memory_space=pltpu.SMEM),
                      pl.BlockSpec((tm * tr, LANES), lambda i, nv: (i, 0))],
            out_specs=pl.BlockSpec(memory_space=pl.ANY),
            scratch_shapes=[pltpu.VMEM((MOE_BLOCK * tr, LANES), U32), pltpu.SemaphoreType.DMA(())]),
        out_shape=jax.ShapeDtypeStruct((n_rows * tr, LANES), U32),
        compiler_params=_params("arbitrary"),
        name="moe_dispatch",
    )(block_nv, dest, hp)


def _gffn_kernel(be_ref, nv_ref, xs_ref, w1_ref, w3_ref, w2_ref, ys_ref, w1b_ref, w3b_ref, w2b_ref, *, tr):
    i = pl.program_id(0)
    nv = nv_ref[i]
    prev = be_ref[jnp.maximum(i - 1, 0)]

    @pl.when((i == 0) | (be_ref[i] != prev))
    def _():
        w1b_ref[...] = w1_ref[0, 0].astype(BF16)
        w3b_ref[...] = w3_ref[0, 0].astype(BF16)
        w2b_ref[...] = w2_ref[0, 0].astype(BF16)

    @pl.when(nv > 0)
    def _():
        lo, hi = _unpack_pairs(_load_token_tiles(xs_ref, 0, MOE_BLOCK, tr))
        dh = lo.shape[1]
        lo = lo.astype(BF16)
        hi = hi.astype(BF16)
        d = functools.partial(jnp.dot, preferred_element_type=F32)
        a = d(lo, w1b_ref[:dh, :]) + d(hi, w1b_ref[dh:, :])
        g = d(lo, w3b_ref[:dh, :]) + d(hi, w3b_ref[dh:, :])
        y = d((_silu(a) * g).astype(BF16), w2b_ref[...])
        _store_token_tiles(ys_ref, 0, _pack_pairs(y[:, :dh], y[:, dh:]))

    @pl.when(nv == 0)
    def _():
        ys_ref[...] = jnp.zeros(ys_ref.shape, U32)


def _grouped_ffn(block_e, block_nv, xs, w1, w3, w2, layer, tr):
    _, _, d, f = w1.shape
    blk_rows = MOE_BLOCK * tr
    nb = xs.shape[0] // blk_rows
    return pl.pallas_call(
        functools.partial(_gffn_kernel, tr=tr),
        grid_spec=pltpu.PrefetchScalarGridSpec(
            num_scalar_prefetch=2,
            grid=(nb,),
            in_specs=[pl.BlockSpec((blk_rows, LANES), lambda i, be, nv: (i, 0)),
                      pl.BlockSpec((1, 1, d, f), lambda i, be, nv: (layer, be[i], 0, 0)),
                      pl.BlockSpec((1, 1, d, f), lambda i, be, nv: (layer, be[i], 0, 0)),
                      pl.BlockSpec((1, 1, f, d), lambda i, be, nv: (layer, be[i], 0, 0))],
            out_specs=pl.BlockSpec((blk_rows, LANES), lambda i, be, nv: (i, 0)),
            scratch_shapes=[pltpu.VMEM((d, f), BF16), pltpu.VMEM((d, f), BF16), pltpu.VMEM((f, d), BF16)]),
        out_shape=jax.ShapeDtypeStruct(xs.shape, U32),
        compiler_params=_params("arbitrary"),
        name="moe_grouped_ffn",
    )(block_e, block_nv, xs, w1, w3, w2)


def _combine_kernel(dest_ref, x_ref, hp_ref, wt_ref, g_ref, ws1_ref, ws3_ref, ws2_ref, ys_ref,
                    o_ref, buf_ref, sem, *, tr):
    tm = x_ref.shape[0]
    dh = tr * LANES

    def copy(t, k):
        return _tile_copy(ys_ref, dest_ref[t * TOP_K + k], buf_ref, (k * tm + t) * tr, tr, sem)

    def start(t, carry):
        for k in range(TOP_K):
            copy(t, k).start(priority=k % 2)
        return carry

    def wait(t, carry):
        for k in range(TOP_K):
            copy(t, k).wait()
        return carry

    lax.fori_loop(0, tm, start, 0)
    lo, hi = _unpack_pairs(_load_token_tiles(hp_ref, 0, tm, tr))
    lo = lo.astype(BF16)
    hi = hi.astype(BF16)
    d = functools.partial(jnp.dot, preferred_element_type=F32)
    a = d(lo, ws1_ref[:dh, :]) + d(hi, ws1_ref[dh:, :])
    g = d(lo, ws3_ref[:dh, :]) + d(hi, ws3_ref[dh:, :])
    shared = d((_silu(a) * g).astype(BF16), ws2_ref[...])
    lax.fori_loop(0, tm, wait, 0)
    acc_lo = jnp.zeros((tm, dh), F32)
    acc_hi = jnp.zeros((tm, dh), F32)
    for k in range(TOP_K):
        ylo, yhi = _unpack_pairs(_load_token_tiles(buf_ref, k * tm * tr, tm, tr))
        wk = wt_ref[:, k:k + 1]
        acc_lo = acc_lo + wk * ylo
        acc_hi = acc_hi + wk * yhi
    o_ref[:, :dh] = x_ref[:, :dh] + g_ref[0][:, :dh] * (acc_lo + shared[:, :dh])
    o_ref[:, dh:] = x_ref[:, dh:] + g_ref[0][:, dh:] * (acc_hi + shared[:, dh:])


def _combine(dest, x, hp, wt, gate, ws1, ws3, ws2, ys, rows_per_group, tr):
    m, d = x.shape
    f = ws1.shape[1]
    tm = math.gcd(_tile(m, 128, LANES), rows_per_group)
    grp = lambda i: ((i * tm) // rows_per_group, 0, 0)
    return pl.pallas_call(
        functools.partial(_combine_kernel, tr=tr),
        grid=(m // tm,),
        in_specs=[pl.BlockSpec((tm * TOP_K,), lambda i: (i,), memory_space=pltpu.SMEM),
                  pl.BlockSpec((tm, d), lambda i: (i, 0)),
                  pl.BlockSpec((tm * tr, LANES), lambda i: (i, 0)),
                  pl.BlockSpec((tm, TOP_K), lambda i: (i, 0)),
                  pl.BlockSpec((1, 1, d), grp),
                  pl.BlockSpec((d, f), lambda i: (0, 0)),
                  pl.BlockSpec((d, f), lambda i: (0, 0)),
                  pl.BlockSpec((f, d), lambda i: (0, 0)),
                  pl.BlockSpec(memory_space=pl.ANY)],
        out_specs=pl.BlockSpec((tm, d), lambda i: (i, 0)),
        out_shape=jax.ShapeDtypeStruct((m, d), F32),
        scratch_shapes=[pltpu.VMEM((TOP_K * tm * tr, LANES), U32), pltpu.SemaphoreType.DMA(())],
        compiler_params=_params("arbitrary"),
        name="moe_combine",
    )(dest, x, hp, wt, gate, ws1, ws3, ws2, ys)


def _moe(streams, layer, nw, rw, rb, w1, w3, w2, ws1, ws3, ws2):
    d = streams[0][0].shape[1]
    rwt = rw.T
    counts = jnp.zeros((N_EXPERTS, LANES), F32)
    routed = []
    for x, sh, sc, _, rpg in streams:
        hp, idx, wt, rank, counts = _router(x, nw, sh, sc, rwt, rb, counts, rpg)
        routed.append((hp, idx, wt, rank))
    n_assign = sum(s[0].shape[0] for s in streams) * TOP_K
    n_blocks = (n_assign + N_EXPERTS * (MOE_BLOCK - 1) + MOE_BLOCK - 1) // MOE_BLOCK
    cnt = counts[:, 0].astype(I32)
    padded = (cnt + MOE_BLOCK - 1) // MOE_BLOCK * MOE_BLOCK
    pad_end = jnp.cumsum(padded)
    pad_start = pad_end - padded
    bstart = jnp.arange(n_blocks, dtype=I32) * MOE_BLOCK
    block_e = jnp.minimum(jnp.sum((bstart[:, None] >= pad_end[None, :]).astype(I32), axis=1), N_EXPERTS - 1)
    block_nv = jnp.clip(cnt[block_e] - (bstart - pad_start[block_e]), 0, MOE_BLOCK).astype(I32)
    e_ar = jnp.arange(N_EXPERTS, dtype=I32)
    tr = d // 2 // LANES
    dests = [((jnp.sum(jnp.where(idx[:, :, None] == e_ar, pad_start, 0), axis=-1) + rank) * tr).T.reshape(-1)
             for _, idx, _, rank in routed]
    if len(streams) > 1:
        hp_all = jnp.concatenate([r[0] for r in routed], axis=0)
        dest_all = jnp.concatenate(dests, axis=0)
    else:
        hp_all, dest_all = routed[0][0], dests[0]
    xs = _dispatch(block_nv, dest_all, hp_all, n_blocks * MOE_BLOCK, tr)
    ys = _grouped_ffn(block_e, block_nv, xs, w1, w3, w2, layer, tr)
    ws1b, ws3b, ws2b = ws1.astype(BF16), ws3.astype(BF16), ws2.astype(BF16)
    return [_combine(dest, x, hp, wt.T, gate, ws1b, ws3b, ws2b, ys, rpg, tr)
            for (x, _, _, gate, rpg), (hp, _, wt, _), dest in zip(streams, routed, dests)]


def _mixer_ab(xs, cs, mx, mc, b, t, n_c, layer, nw, w_in, w_out, q_norm, k_norm, lam_vec, subln, conv_w,
              a_log, dt_bias, out_norm):
    n_main = w_in.shape[1] - 4 * DN_HEADS
    assert n_main % LANES == 0
    w_main = w_in[:, :n_main].astype(BF16)
    w_small = jnp.zeros((w_in.shape[0], LANES), F32).at[:, :4 * DN_HEADS].set(w_in[:, n_main:])
    z_x, zs_x = _norm_mod_matmul(xs, nw, mx[0], mx[1], w_main, t, w_small)
    z_c, zs_c = _norm_mod_matmul(cs, nw, mc[0], mc[1], w_main, b * n_c, w_small)
    nqk = 2 * DIFF_HEADS * DIFF_DK
    tabs = _rope_tables(t, DIFF_DK)
    qscale = DIFF_DK ** -0.5
    qd_x = _headnorm_rope(z_x, 0, nqk, q_norm, DIFF_DK, qscale, tabs, t)
    kd_x = _headnorm_rope(z_x, nqk, nqk, k_norm, DIFF_DK, 1.0, tabs, t)
    qd_c = _headnorm_rope(z_c, 0, nqk, q_norm, DIFF_DK, qscale, None, t)
    kd_c = _headnorm_rope(z_c, nqk, nqk, k_norm, DIFF_DK, 1.0, None, t)
    lam_init = 0.8 - 0.6 * math.exp(-0.3 * layer)
    od_x = _diff_attention(lam_vec, qd_x, kd_c, z_c, subln, lam_init, b, n_c, kx=kd_x, zx=z_x, t=t)
    od_c = _diff_attention(lam_vec, qd_c, kd_c, z_c, subln, lam_init, b, n_c)
    dn_col0 = 2 * nqk + DIFF_HEADS * DIFF_DV
    dn_x = _dn_prep(z_x, dn_col0, conv_w, t)
    dn_c = _dn_prep(z_c, dn_col0, conv_w, n_c)
    zst_x = zs_x[:, :4 * DN_HEADS].T
    zst_c = zs_c[:, :4 * DN_HEADS].T
    prm = _deltanet_params(a_log, dt_bias)
    s0 = jnp.zeros((b, DN_HEADS, DN_DK, DN_DV), F32)
    o_cf, s_cf = _deltanet(dn_c, zs_c, zst_c, prm, s0, b, n_c, False)
    o_xf, _ = _deltanet(dn_x, zs_x, zst_x, prm, s_cf, b, t, False)
    o_cb, s_cb = _deltanet(dn_c, zs_c, zst_c, prm, s0, b, n_c, True)
    o_xb, _ = _deltanet(dn_x, zs_x, zst_x, prm, s_cb, b, t, True)
    gate_col0 = dn_col0 + conv_w.shape[1]
    w_out_b = w_out.astype(BF16)
    x1 = _outproj_ab(od_x, o_xf, o_xb, z_x, gate_col0, out_norm, w_out_b, xs, mx[2], t)
    c1 = _outproj_ab(od_c, o_cf, o_cb, z_c, gate_col0, out_norm, w_out_b, cs, mc[2], b * n_c)
    return x1, c1


def _mixer_swa(xs, cs, mx, mc, b, t, n_c, nw, w_in, w_out, q_norm, k_norm, sink):
    nq = SWA_HEADS * SWA_DH
    nkv = SWA_KV_HEADS * SWA_DH
    w_b = w_in.astype(BF16)
    z_x = _norm_mod_matmul(xs, nw, mx[0], mx[1], w_b, t)
    z_c = _norm_mod_matmul(cs, nw, mc[0], mc[1], w_b[:, nq:], b * n_c)
    tabs = _rope_tables(t, SWA_DH)
    q = _headnorm_rope(z_x, 0, nq, q_norm, SWA_DH, SWA_DH ** -0.5 * LOG2E, tabs, t)
    k_x = _headnorm_rope(z_x, nq, nkv, k_norm, SWA_DH, 1.0, tabs, t)
    k_c = _headnorm_rope(z_c, 0, nkv, k_norm, SWA_DH, 1.0, None, t)
    att = _swa_attention(sink.reshape(SWA_KV_HEADS, SWA_GROUP), q, _dup_heads(k_x, 0), _dup_heads(z_x, nq + nkv),
                         _dup_heads(k_c, 0), _dup_heads(z_c, nkv), b, t, n_c)
    return _outproj(att, w_out.astype(BF16), xs, mx[2], t)


def kernel(x, c, ctx, c_ctx, mod_w, mod_b, norm_mix, norm_ffn, ab_w_in, ab_w_out, diff_q_norm, diff_k_norm, diff_lambda, diff_subln, dn_conv, dn_a_log, dn_dt_bias, dn_out_norm, swa_w_in, swa_w_out, swa_q_norm, swa_k_norm, swa_sink, router_w, router_bias, exp_w1, exp_w3, exp_w2, shared_w1, shared_w3, shared_w2):
    b, t, d = x.shape
    n_c = ctx.shape[1]
    depth = mod_w.shape[0]
    assert depth == 2 and t % Q_BLOCK == 0 and t % DN_CHUNK == 0 and n_c % DN_CHUNK == 0
    xs = x.reshape(b * t, d)
    cs = ctx.reshape(b * n_c, d)
    n_mod = -(-(b + 1) // SUBLANES) * SUBLANES
    a_mod = jnp.zeros((n_mod, d), F32).at[0].set(c_ctx).at[1:1 + b].set(c)
    for layer in range(depth):
        with_ctx = layer < depth - 1
        p = layer // 2
        mod = _modulation(a_mod, mod_w, layer, mod_b[layer])
        mc = [mod[0:1, j * d:(j + 1) * d].reshape(1, 1, d) for j in range(6)]
        mx = [mod[1:1 + b, j * d:(j + 1) * d].reshape(b, 1, d) for j in range(6)]
        if layer % 2 == 0:
            xs, c_new = _mixer_ab(xs, cs, mx, mc, b, t, n_c, layer, norm_mix[layer], ab_w_in[p], ab_w_out[p],
                                  diff_q_norm[p], diff_k_norm[p], diff_lambda[p], diff_subln[p], dn_conv[p],
                                  dn_a_log[p], dn_dt_bias[p], dn_out_norm[p])
        else:
            assert not with_ctx
            xs = _mixer_swa(xs, cs, mx, mc, b, t, n_c, norm_mix[layer], swa_w_in[p], swa_w_out[p],
                            swa_q_norm[p], swa_k_norm[p], swa_sink[p])
            c_new = None
        moe_w = (layer, norm_ffn[layer], router_w[layer], router_bias[layer], exp_w1, exp_w3, exp_w2,
                 shared_w1[layer], shared_w3[layer], shared_w2[layer])
        if with_ctx:
            cs, xs = _moe([(c_new, mc[3], mc[4], mc[5], b * n_c), (xs, mx[3], mx[4], mx[5], t)], *moe_w)
        else:
            (xs,) = _moe([(xs, mx[3], mx[4], mx[5], t)], *moe_w)
    return xs.reshape(b, t, d)
```

```python
import functools
import math

import jax
import jax.numpy as jnp
import numpy as np
from jax import lax
from jax.experimental import pallas as pl
from jax.experimental.pallas import tpu as pltpu

F32 = jnp.float32
BF16 = jnp.bfloat16
I32 = jnp.int32
U32 = jnp.uint32

EPS = 1e-6
GRID_W = 64
ROPE_BASE = 10000.0
DIFF_HEADS = 4
DIFF_DK = 128
DIFF_DV = 256
DN_HEADS = 8
DN_DK = 128
DN_DV = 128
DN_CONV = 5
DN_CHUNK = 128
SWA_HEADS = 32
SWA_KV_HEADS = 4
SWA_GROUP = SWA_HEADS // SWA_KV_HEADS
SWA_DH = 64
WINDOW = 128
Q_BLOCK = 128
N_EXPERTS = 64
TOP_K = 8
N_GROUPS = 8
GROUP_SIZE = N_EXPERTS // N_GROUPS
TOPK_GROUPS = 4
ROUTED_SCALE = 2.5
MOE_BLOCK = 512

LANES = 128
SUBLANES = 8
VMEM_LIMIT_BYTES = 56 * 1024 * 1024

NT_DIMS = (((1,), (1,)), ((), ()))
LOG2E = math.log2(math.e)


def _params(*semantics):
    return pltpu.CompilerParams(dimension_semantics=semantics, vmem_limit_bytes=VMEM_LIMIT_BYTES)


def _tile(n, pref, mult):
    if n <= pref:
        return n
    t = pref - pref % mult
    while t > mult and n % t:
        t -= mult
    assert n % t == 0, (n, pref, mult)
    return t


def _mm(a, b):
    return jnp.dot(a.astype(BF16), b.astype(BF16), preferred_element_type=F32)


def _split2(x):
    hi = x.astype(BF16)
    lo = (x - hi.astype(F32)).astype(BF16)
    return hi, lo


def _dot3(a, b, dims=None):
    if dims is None:
        dims = (((a.ndim - 1,), (0,)), ((), ()))
    ah, al = _split2(a)
    bh, bl = _split2(b)
    d = functools.partial(lax.dot_general, dimension_numbers=dims, preferred_element_type=F32)
    return d(ah, bh) + d(ah, bl) + d(al, bh)


def _silu(x):
    return x * jax.nn.sigmoid(x)


def _softplus(x):
    return jnp.maximum(x, 0.0) + jnp.log(1.0 + jnp.exp(-jnp.abs(x)))


def _pack_pairs(lo, hi):
    ulo = lax.bitcast_convert_type(lo.astype(BF16).astype(F32), U32) >> 16
    uhi = lax.bitcast_convert_type(hi.astype(BF16).astype(F32), U32) & jnp.uint32(0xFFFF0000)
    return ulo | uhi


def _unpack_pairs(u):
    lo = lax.bitcast_convert_type(u << 16, F32)
    hi = lax.bitcast_convert_type(u & jnp.uint32(0xFFFF0000), F32)
    return lo, hi


def _store_token_tiles(ref, base, packed):
    n, width = packed.shape
    tr = width // LANES
    for j in range(tr):
        ref[pl.ds(base + j, n, stride=tr), :] = packed[:, j * LANES:(j + 1) * LANES]


def _load_token_tiles(ref, base, n, tr):
    return jnp.concatenate([ref[pl.ds(base + j, n, stride=tr), :] for j in range(tr)], axis=1)


def _tile_copy(src, src_row, dst, dst_row, tr, sem):
    return pltpu.make_async_copy(src.at[pl.ds(pl.multiple_of(src_row, tr), tr)],
                                 dst.at[pl.ds(pl.multiple_of(dst_row, tr), tr)], sem)


def _mod_kernel(a_ref, w_ref, b_ref, o_ref):
    o_ref[...] = _dot3(_silu(a_ref[...]), w_ref[0]) + b_ref[...]


def _modulation(a, w_all, layer, b):
    r, d = a.shape
    n = w_all.shape[2]
    tn = _tile(n, 768, LANES)
    return pl.pallas_call(
        _mod_kernel,
        grid=(n // tn,),
        in_specs=[pl.BlockSpec((r, d), lambda j: (0, 0)),
                  pl.BlockSpec((1, d, tn), lambda j: (layer, 0, j)),
                  pl.BlockSpec((1, tn), lambda j: (0, j))],
        out_specs=pl.BlockSpec((r, tn), lambda j: (0, j)),
        out_shape=jax.ShapeDtypeStruct((r, n), F32),
        compiler_params=_params("parallel"),
        name="modulation",
    )(a, w_all, b.reshape(1, n))


def _norm_mod(x, nw, sh, sc):
    y = x * lax.rsqrt(jnp.mean(x * x, axis=-1, keepdims=True) + EPS) * nw
    return y * (1.0 + sc) + sh


def _nmm_kernel(x_ref, nw_ref, sh_ref, sc_ref, w_ref, *rest, has_small):
    if has_small:
        ws_ref, o_ref, os_ref, h_ref = rest
    else:
        o_ref, h_ref = rest

    @pl.when(pl.program_id(1) == 0)
    def _():
        h = _norm_mod(x_ref[...], nw_ref[...], sh_ref[0], sc_ref[0])
        h_ref[...] = h.astype(BF16)
        if has_small:
            os_ref[...] = _dot3(h, ws_ref[...])

    o_ref[...] = jnp.dot(h_ref[...], w_ref[...], preferred_element_type=F32).astype(o_ref.dtype)


def _norm_mod_matmul(x, nw, sh, sc, w, rows_per_group, w_small=None):
    m, d = x.shape
    n = w.shape[1]
    tm = math.gcd(_tile(m, 512, SUBLANES), rows_per_group)
    tn = _tile(n, 1024, LANES)
    has_small = w_small is not None
    grp = lambda i, j: ((i * tm) // rows_per_group, 0, 0)
    in_specs = [pl.BlockSpec((tm, d), lambda i, j: (i, 0)),
                pl.BlockSpec((1, d), lambda i, j: (0, 0)),
                pl.BlockSpec((1, 1, d), grp),
                pl.BlockSpec((1, 1, d), grp),
                pl.BlockSpec((d, tn), lambda i, j: (0, j))]
    args = [x, nw.reshape(1, d), sh, sc, w]
    out_specs = [pl.BlockSpec((tm, tn), lambda i, j: (i, j))]
    out_shape = [jax.ShapeDtypeStruct((m, n), BF16)]
    if has_small:
        in_specs.append(pl.BlockSpec((d, LANES), lambda i, j: (0, 0)))
        args.append(w_small)
        out_specs.append(pl.BlockSpec((tm, LANES), lambda i, j: (i, 0)))
        out_shape.append(jax.ShapeDtypeStruct((m, LANES), F32))
    outs = pl.pallas_call(
        functools.partial(_nmm_kernel, has_small=has_small),
        grid=(m // tm, n // tn),
        in_specs=in_specs,
        out_specs=out_specs,
        out_shape=out_shape,
        scratch_shapes=[pltpu.VMEM((tm, d), BF16)],
        compiler_params=_params("parallel", "arbitrary"),
        name="norm_mod_matmul",
    )(*args)
    return outs if has_small else outs[0]


def _rope_tables(t, head_dim):
    q = head_dim // 4
    pos = jnp.arange(t, dtype=I32)
    row = (pos // GRID_W).astype(F32)
    col = (pos % GRID_W).astype(F32)
    axis_dim = head_dim // 2
    inv_freq = ROPE_BASE ** (-jnp.arange(0, axis_dim, 2, dtype=F32) / axis_dim)
    lane = jnp.arange(LANES) % head_dim
    freq = inv_freq[lane % q]
    p = jnp.where((lane < head_dim // 2)[None, :], row[:, None], col[:, None])
    ang = p * freq[None, :]
    sign = jnp.where((lane % (2 * q)) < q, -1.0, 1.0)
    return jnp.cos(ang), jnp.sin(ang) * sign[None, :]


def _hnr_kernel(z_ref, w_ref, *rest, hd, scale, rope):
    if rope:
        cos_ref, sin_ref, o_ref = rest
    else:
        (o_ref,) = rest
    tm, cb = z_ref.shape
    lane = lax.broadcasted_iota(I32, (tm, LANES), 1)
    w = w_ref[...]
    q = hd // 4
    for g in range(cb // LANES):
        sl = slice(g * LANES, (g + 1) * LANES)
        x = z_ref[:, sl].astype(F32)
        x2 = x * x
        if hd == LANES:
            ms = jnp.mean(x2, axis=-1, keepdims=True)
        else:
            lo = jnp.sum(jnp.where(lane < hd, x2, 0.0), axis=-1, keepdims=True)
            hi = jnp.sum(jnp.where(lane >= hd, x2, 0.0), axis=-1, keepdims=True)
            ms = jnp.where(lane < hd, lo, hi) * (1.0 / hd)
        y = x * lax.rsqrt(ms + EPS) * w
        if rope:
            nxt = pltpu.roll(y, LANES - q, 1)
            prv = pltpu.roll(y, q, 1)
            partner = jnp.where((lane & (2 * q - 1)) < q, nxt, prv)
            y = y * cos_ref[...] + partner * sin_ref[...]
        o_ref[:, sl] = (y * scale).astype(o_ref.dtype)


def _headnorm_rope(z, col0, ncols, w, hd, scale, tables, t):
    m = z.shape[0]
    cb = min(512, ncols)
    assert ncols % cb == 0 and col0 % cb == 0 and LANES % hd == 0
    tm = _tile(t, 512, SUBLANES) if tables is not None else _tile(m, 512, SUBLANES)
    cblk0 = col0 // cb
    rope = tables is not None
    in_specs = [pl.BlockSpec((tm, cb), lambda i, j: (i, cblk0 + j)),
                pl.BlockSpec((1, LANES), lambda i, j: (0, 0))]
    args = [z, jnp.tile(w, LANES // hd).reshape(1, LANES)]
    if rope:
        nt = t // tm
        in_specs += [pl.BlockSpec((tm, LANES), lambda i, j: (i % nt, 0))] * 2
        args += list(tables)
    return pl.pallas_call(
        functools.partial(_hnr_kernel, hd=hd, scale=scale, rope=rope),
        grid=(m // tm, ncols // cb),
        in_specs=in_specs,
        out_specs=pl.BlockSpec((tm, cb), lambda i, j: (i, j)),
        out_shape=jax.ShapeDtypeStruct((m, ncols), BF16),
        compiler_params=_params("parallel", "parallel"),
        name="headnorm_rope",
    )(*args)


def _diff_attn_kernel(lv_ref, q_ref, *rest, lam_init, has_lat):
    if has_lat:
        kx_ref, vx_ref, kc_ref, vc_ref, w_ref, o_ref = rest
    else:
        kc_ref, vc_ref, w_ref, o_ref = rest
    lv = lv_ref[...]
    lam = (jnp.exp(jnp.sum(lv[0:1] * lv[1:2], keepdims=True))
           - jnp.exp(jnp.sum(lv[2:3] * lv[3:4], keepdims=True)) + lam_init)
    q = q_ref[...]

    def probs(c):
        sl = slice(c * DIFF_DK, (c + 1) * DIFF_DK)
        qc = q[:, sl]
        s_c = lax.dot_general(qc, kc_ref[:, sl], NT_DIMS, preferred_element_type=F32)
        m = jnp.max(s_c, axis=-1, keepdims=True)
        p_x = None
        if has_lat:
            s_x = lax.dot_general(qc, kx_ref[:, sl], NT_DIMS, preferred_element_type=F32)
            m = jnp.maximum(m, jnp.max(s_x, axis=-1, keepdims=True))
            p_x = jnp.exp2(s_x - m)
        p_c = jnp.exp2(s_c - m)
        l = jnp.sum(p_c, axis=-1, keepdims=True)
        if has_lat:
            l = l + jnp.sum(p_x, axis=-1, keepdims=True)
        return p_x, p_c, l

    p1x, p1c, l1 = probs(0)
    p2x, p2c, l2 = probs(1)
    ratio = lam * l1 * (1.0 / l2)
    o = jnp.dot((p1c - p2c * ratio).astype(BF16), vc_ref[...], preferred_element_type=F32)
    if has_lat:
        o = o + jnp.dot((p1x - p2x * ratio).astype(BF16), vx_ref[...], preferred_element_type=F32)
    o = o * (1.0 / l1)
    o = o * lax.rsqrt(jnp.mean(o * o, axis=-1, keepdims=True) + EPS)
    o_ref[...] = (o * w_ref[...] * (1.0 - lam_init)).astype(o_ref.dtype)


def _diff_attention(lam_vec, q, kc, zc, subln, lam_init, b, n_c, kx=None, zx=None, t=None):
    has_lat = kx is not None
    hw = 2 * DIFF_DK
    vblk0 = (2 * DIFF_HEADS * hw) // DIFF_DV
    tq_all = t if has_lat else n_c
    tq = _tile(tq_all, 256, SUBLANES)
    nq = tq_all // tq
    in_specs = [pl.BlockSpec((4, DIFF_DK), lambda bi, h, qi: (0, 0)),
                pl.BlockSpec((tq, hw), lambda bi, h, qi: (bi * nq + qi, h))]
    args = [lam_vec, q]
    if has_lat:
        in_specs += [pl.BlockSpec((t, hw), lambda bi, h, qi: (bi, h)),
                     pl.BlockSpec((t, DIFF_DV), lambda bi, h, qi: (bi, vblk0 + h))]
        args += [kx, zx]
    in_specs += [pl.BlockSpec((n_c, hw), lambda bi, h, qi: (bi, h)),
                 pl.BlockSpec((n_c, DIFF_DV), lambda bi, h, qi: (bi, vblk0 + h)),
                 pl.BlockSpec((1, DIFF_DV), lambda bi, h, qi: (0, 0))]
    args += [kc, zc, subln.reshape(1, DIFF_DV)]
    return pl.pallas_call(
        functools.partial(_diff_attn_kernel, lam_init=lam_init, has_lat=has_lat),
        grid=(b, DIFF_HEADS, nq),
        in_specs=in_specs,
        out_specs=pl.BlockSpec((tq, DIFF_DV), lambda bi, h, qi: (bi * nq + qi, h)),
        out_shape=jax.ShapeDtypeStruct((b * tq_all, DIFF_HEADS * DIFF_DV), BF16),
        compiler_params=_params("parallel", "parallel", "arbitrary"),
        name="diff_attention",
    )(*args)


def _dn_prep_kernel(z_ref, cw_ref, o_ref, pad_ref, *, seg):
    halo = SUBLANES
    pad_ref[0:halo, :] = jnp.zeros((halo, LANES), F32)
    pad_ref[halo + seg:2 * halo + seg, :] = jnp.zeros((halo, LANES), F32)
    pad_ref[halo:halo + seg, :] = z_ref[...].astype(F32)
    kind = pl.program_id(1) // DN_HEADS
    qk_scale = jnp.where(kind == 0, DN_DK ** -0.5, 1.0).astype(F32)
    rows = _tile(seg, 256, SUBLANES)
    for r0 in range(0, seg, rows):
        acc = jnp.zeros((rows, LANES), F32)
        for j in range(DN_CONV):
            acc = acc + cw_ref[j:j + 1, :] * pad_ref[pl.ds(halo + r0 + j - DN_CONV // 2, rows), :]
        y = _silu(acc)
        nrm = y * lax.rsqrt(jnp.sum(y * y, axis=-1, keepdims=True) + EPS) * qk_scale
        o_ref[r0:r0 + rows, :] = jnp.where(kind < 2, nrm, y).astype(o_ref.dtype)


def _dn_prep(z, col0, conv_w, seg):
    m = z.shape[0]
    ncols = conv_w.shape[1]
    cblk0 = col0 // LANES
    cw = jnp.zeros((SUBLANES, ncols), F32).at[:DN_CONV].set(conv_w)
    return pl.pallas_call(
        functools.partial(_dn_prep_kernel, seg=seg),
        grid=(m // seg, ncols // LANES),
        in_specs=[pl.BlockSpec((seg, LANES), lambda s, g: (s, cblk0 + g)),
                  pl.BlockSpec((SUBLANES, LANES), lambda s, g: (0, g))],
        out_specs=pl.BlockSpec((seg, LANES), lambda s, g: (s, g)),
        out_shape=jax.ShapeDtypeStruct((m, ncols), BF16),
        scratch_shapes=[pltpu.VMEM((seg + 2 * SUBLANES, LANES), F32)],
        compiler_params=_params("parallel", "parallel"),
        name="dn_prep",
    )(z, cw)


def _deltanet_kernel(q_ref, k_ref, v_ref, zs_ref, zst_ref, pr_ref, pca_ref, pcd_ref, s0_ref,
                     o_ref, sout_ref, s_ref, *, reverse, n_chunks):
    step = pl.program_id(1)
    c = DN_CHUNK

    @pl.when(step == 0)
    def _():
        s_ref[...] = s0_ref[0]

    ri = lax.broadcasted_iota(I32, (c, c), 0)
    ci = lax.broadcasted_iota(I32, (c, c), 1)
    if reverse:
        later, strict, later_t = ri <= ci, ri < ci, ri >= ci
    else:
        later, strict, later_t = ri >= ci, ri > ci, ri <= ci
    eye = (ri == ci).astype(F32)
    tri = later.astype(BF16)
    tri_t = later_t.astype(BF16)

    zs = zs_ref[...]
    beta_cols = jax.nn.sigmoid(zs)
    g_cols = -jnp.exp(pr_ref[0:1, :]) * _softplus(zs + pr_ref[1:2, :])
    g_hi = g_cols.astype(BF16)
    g_r1 = g_cols - g_hi.astype(F32)
    g_mid = g_r1.astype(BF16)
    g_lo = (g_r1 - g_mid.astype(F32)).astype(BF16)
    d = functools.partial(jnp.dot, preferred_element_type=F32)
    gc_cols = d(tri, g_hi) + d(tri, g_mid) + d(tri, g_lo)
    g_rows = -jnp.exp(pca_ref[...]) * _softplus(zst_ref[...] + pcd_ref[...])
    h_hi = g_rows.astype(BF16)
    h_r1 = g_rows - h_hi.astype(F32)
    h_mid = h_r1.astype(BF16)
    h_lo = (h_r1 - h_mid.astype(F32)).astype(BF16)
    gc_rows = d(h_hi, tri_t) + d(h_mid, tri_t) + d(h_lo, tri_t)

    dir_off = DN_HEADS if reverse else 0
    last = 0 if reverse else c - 1
    neg_inf = jnp.float32(-jnp.inf)
    heads = []
    for h in range(DN_HEADS):
        sl = slice(h * DN_DK, (h + 1) * DN_DK)
        cb = dir_off + h
        cg = 2 * DN_HEADS + dir_off + h
        beta = beta_cols[:, cb:cb + 1]
        gcol = gc_cols[:, cg:cg + 1]
        grow = gc_rows[cg:cg + 1, :]
        glast = grow[:, last:last + 1]
        q = q_ref[:, sl]
        k = k_ref[:, sl]
        kf = k.astype(F32)
        decay = jnp.exp(jnp.where(later, gcol - grow, neg_inf))
        kb = kf * beta
        both = lax.dot_general(jnp.concatenate([kb.astype(BF16), q], axis=0), k, NT_DIMS,
                               preferred_element_type=F32)
        lmat = jnp.where(strict, both[:c] * decay, 0.0)
        eg = jnp.exp(gcol)
        heads.append(dict(
            sl=sl, glast=glast, lmat=lmat, amat=both[c:] * decay,
            inv=eye - jnp.where((ri >> 1) == (ci >> 1), lmat, 0.0),
            rhs=jnp.concatenate([v_ref[:, sl].astype(F32) * beta, kb * eg], axis=1).astype(BF16),
            qe=(q.astype(F32) * eg).astype(BF16),
            kdec_t=(kf * jnp.exp(glast - gcol)).T.astype(BF16)))
    lev = 1
    while (1 << lev) < c:
        blk = ((ri >> (lev + 1)) == (ci >> (lev + 1))) & ((ri >> lev) != (ci >> lev))
        half = [_mm(hd["inv"], jnp.where(blk, hd["lmat"], 0.0)) for hd in heads]
        for hd, t in zip(heads, half):
            hd["inv"] = hd["inv"] - _mm(t, hd["inv"])
        lev += 1
    uws = [_mm(hd["inv"], hd["rhs"]) for hd in heads]
    states = [s_ref[h] for h in range(DN_HEADS)]
    new_states = []
    for hd, uw, s in zip(heads, uws, states):
        ws_qs = _mm(jnp.concatenate([uw[:, DN_DV:].astype(BF16), hd["qe"]], axis=0), s)
        v_new = uw[:, :DN_DV] - ws_qs[:c]
        o_ref[:, hd["sl"]] = ws_qs[c:] + _mm(hd["amat"], v_new)
        new_states.append(s * jnp.exp(hd["glast"]) + _mm(hd["kdec_t"], v_new))
    for h in range(DN_HEADS):
        s_ref[h] = new_states[h]

    @pl.when(step == n_chunks - 1)
    def _():
        sout_ref[0] = s_ref[...]


def _deltanet(dn, zs, zst, prm, s0, b, seg, reverse):
    pr, pca, pcd = prm
    n = seg // DN_CHUNK
    hw = DN_HEADS * DN_DK
    rb = (lambda bi, s: bi * n + (n - 1 - s)) if reverse else (lambda bi, s: bi * n + s)
    state = pl.BlockSpec((1, DN_HEADS, DN_DK, DN_DV), lambda bi, s: (bi, 0, 0, 0))
    return pl.pallas_call(
        functools.partial(_deltanet_kernel, reverse=reverse, n_chunks=n),
        grid=(b, n),
        in_specs=[pl.BlockSpec((DN_CHUNK, hw), lambda bi, s: (rb(bi, s), 0)),
                  pl.BlockSpec((DN_CHUNK, hw), lambda bi, s: (rb(bi, s), 1)),
                  pl.BlockSpec((DN_CHUNK, hw), lambda bi, s: (rb(bi, s), 2)),
                  pl.BlockSpec((DN_CHUNK, LANES), lambda bi, s: (rb(bi, s), 0)),
                  pl.BlockSpec((4 * DN_HEADS, DN_CHUNK), lambda bi, s: (0, rb(bi, s))),
                  pl.BlockSpec((SUBLANES, LANES), lambda bi, s: (0, 0)),
                  pl.BlockSpec((4 * DN_HEADS, LANES), lambda bi, s: (0, 0)),
                  pl.BlockSpec((4 * DN_HEADS, LANES), lambda bi, s: (0, 0)),
                  state],
        out_specs=[pl.BlockSpec((DN_CHUNK, hw), lambda bi, s: (rb(bi, s), 0)), state],
        out_shape=[jax.ShapeDtypeStruct((b * seg, hw), F32),
                   jax.ShapeDtypeStruct((b, DN_HEADS, DN_DK, DN_DV), F32)],
        scratch_shapes=[pltpu.VMEM((DN_HEADS, DN_DK, DN_DV), F32)],
        compiler_params=_params("parallel", "arbitrary"),
        name="deltanet_bwd" if reverse else "deltanet_fwd",
    )(dn, dn, dn, zs, zst, pr, pca, pcd, s0)


def _deltanet_params(a_log, dt_bias):
    nh = 2 * DN_HEADS
    a = a_log.reshape(nh).astype(F32)
    dtb = dt_bias.reshape(nh).astype(F32)
    pr = jnp.zeros((SUBLANES, LANES), F32).at[0, nh:2 * nh].set(a).at[1, nh:2 * nh].set(dtb)
    pca = jnp.zeros((2 * nh, LANES), F32).at[nh:].set(jnp.broadcast_to(a[:, None], (nh, LANES)))
    pcd = jnp.zeros((2 * nh, LANES), F32).at[nh:].set(jnp.broadcast_to(dtb[:, None], (nh, LANES)))
    return pr, pca, pcd


def _outproj_ab_kernel(od_ref, of_ref, ob_ref, gate_ref, nw_ref, w_ref, x_ref, g_ref, o_ref, h_ref):
    @pl.when(pl.program_id(1) == 0)
    def _():
        nd = od_ref.shape[1]
        h_ref[:, :nd] = od_ref[...]
        for h in range(DN_HEADS):
            sl = slice(h * DN_DV, (h + 1) * DN_DV)
            y = of_ref[:, sl] + ob_ref[:, sl]
            y = y * lax.rsqrt(jnp.mean(y * y, axis=-1, keepdims=True) + EPS) * nw_ref[...]
            h_ref[:, nd + h * DN_DV:nd + (h + 1) * DN_DV] = (y * _silu(gate_ref[:, sl].astype(F32))).astype(BF16)

    y = jnp.dot(h_ref[...], w_ref[...], preferred_element_type=F32)
    o_ref[...] = x_ref[...] + g_ref[0] * y


def _outproj_ab(od, o_f, o_b, z, gate_col0, out_norm, w, x, gate, rows_per_group):
    m, d = x.shape
    nd, nn = od.shape[1], o_f.shape[1]
    tm = math.gcd(_tile(m, 512, SUBLANES), rows_per_group)
    tn = _tile(d, 1024, LANES)
    grp = lambda i, j: ((i * tm) // rows_per_group, 0, j)
    gblk = gate_col0 // nn
    return pl.pallas_call(
        _outproj_ab_kernel,
        grid=(m // tm, d // tn),
        in_specs=[pl.BlockSpec((tm, nd), lambda i, j: (i, 0)),
                  pl.BlockSpec((tm, nn), lambda i, j: (i, 0)),
                  pl.BlockSpec((tm, nn), lambda i, j: (i, 0)),
                  pl.BlockSpec((tm, nn), lambda i, j: (i, gblk)),
                  pl.BlockSpec((1, DN_DV), lambda i, j: (0, 0)),
                  pl.BlockSpec((nd + nn, tn), lambda i, j: (0, j)),
                  pl.BlockSpec((tm, tn), lambda i, j: (i, j)),
                  pl.BlockSpec((1, 1, tn), grp)],
        out_specs=pl.BlockSpec((tm, tn), lambda i, j: (i, j)),
        out_shape=jax.ShapeDtypeStruct((m, d), F32),
        scratch_shapes=[pltpu.VMEM((tm, nd + nn), BF16)],
        compiler_params=_params("parallel", "arbitrary"),
        name="outproj_ab",
    )(od, o_f, o_b, z, out_norm.reshape(1, DN_DV), w, x, gate)


def _outproj_kernel(a_ref, w_ref, x_ref, g_ref, o_ref):
    o_ref[...] = x_ref[...] + g_ref[0] * jnp.dot(a_ref[...], w_ref[...], preferred_element_type=F32)


def _outproj(a, w, x, gate, rows_per_group):
    m, d = x.shape
    kdim = a.shape[1]
    tm = math.gcd(_tile(m, 512, SUBLANES), rows_per_group)
    tn = _tile(d, 1024, LANES)
    grp = lambda i, j: ((i * tm) // rows_per_group, 0, j)
    return pl.pallas_call(
        _outproj_kernel,
        grid=(m // tm, d // tn),
        in_specs=[pl.BlockSpec((tm, kdim), lambda i, j: (i, 0)),
                  pl.BlockSpec((kdim, tn), lambda i, j: (0, j)),
                  pl.BlockSpec((tm, tn), lambda i, j: (i, j)),
                  pl.BlockSpec((1, 1, tn), grp)],
        out_specs=pl.BlockSpec((tm, tn), lambda i, j: (i, j)),
        out_shape=jax.ShapeDtypeStruct((m, d), F32),
        compiler_params=_params("parallel", "arbitrary"),
        name="outproj",
    )(a, w, x, gate)


def _swa_kernel(sink_ref, bias_ref, q_ref, k0_ref, k1_ref, k2_ref, v0_ref, v1_ref, v2_ref, kc_ref, vc_ref,
                o_ref):
    kvh = pl.program_id(1)
    qb = Q_BLOCK
    npair = SWA_GROUP // 2
    lane = lax.broadcasted_iota(I32, (qb, LANES), 1)
    q = q_ref[...]
    parts = []
    for p in range(npair):
        qp = q[:, p * LANES:(p + 1) * LANES]
        parts.append(jnp.where(lane < SWA_DH, qp, jnp.zeros_like(qp)))
        parts.append(jnp.where(lane >= SWA_DH, qp, jnp.zeros_like(qp)))
    qq = jnp.concatenate(parts, axis=0)
    k_lat = jnp.concatenate([k0_ref[...], k1_ref[...], k2_ref[...]], axis=0)
    v_lat = jnp.concatenate([v0_ref[...], v1_ref[...], v2_ref[...]], axis=0)
    s_lat = lax.dot_general(qq, k_lat, NT_DIMS, preferred_element_type=F32)
    s_ctx = lax.dot_general(qq, kc_ref[...], NT_DIMS, preferred_element_type=F32)
    bias = bias_ref[0]
    vc = vc_ref[...]
    outs = []
    for g in range(SWA_GROUP):
        rs = slice(g * qb, (g + 1) * qb)
        sl = s_lat[rs] + bias
        sc = s_ctx[rs]
        sink = sink_ref[kvh, g] * LOG2E
        m = jnp.maximum(jnp.maximum(jnp.max(sl, axis=-1, keepdims=True),
                                    jnp.max(sc, axis=-1, keepdims=True)), sink)
        el = jnp.exp2(sl - m)
        ec = jnp.exp2(sc - m)
        l = jnp.sum(el, axis=-1, keepdims=True) + jnp.sum(ec, axis=-1, keepdims=True) + jnp.exp2(sink - m)
        o = (jnp.dot(el.astype(BF16), v_lat, preferred_element_type=F32)
             + jnp.dot(ec.astype(BF16), vc, preferred_element_type=F32))
        outs.append(o * (1.0 / l))
    for p in range(npair):
        o_ref[:, p * LANES:(p + 1) * LANES] = jnp.where(lane < SWA_DH, outs[2 * p], outs[2 * p + 1]).astype(o_ref.dtype)


def _swa_attention(sink, q, kx, vx, kc, vc, b, t, n_c):
    qb = Q_BLOCK
    nb = t // qb
    gw = SWA_GROUP * SWA_DH
    lat = lambda off: pl.BlockSpec(
        (qb, LANES), lambda bi, h, i: (bi * nb + jnp.clip(i + off, 0, nb - 1), h))
    ctx = pl.BlockSpec((n_c, LANES), lambda bi, h, i: (bi, h))
    r_io = np.arange(qb)[:, None]
    c_io = np.arange(3 * qb)[None, :]
    inside = np.abs(r_io + qb - c_io) <= WINDOW
    variants = [inside & ((c_io >= qb) | (v & 1 == 0)) & ((c_io < 2 * qb) | (v & 2 == 0)) for v in range(4)]
    bias = jnp.asarray(np.where(np.stack(variants), 0.0, -np.inf), F32)
    return pl.pallas_call(
        _swa_kernel,
        grid=(b, SWA_KV_HEADS, nb),
        in_specs=[pl.BlockSpec(memory_space=pltpu.SMEM),
                  pl.BlockSpec((1, qb, 3 * qb),
                               lambda bi, h, i: ((i == 0).astype(I32) + 2 * (i == nb - 1).astype(I32), 0, 0)),
                  pl.BlockSpec((qb, gw), lambda bi, h, i: (bi * nb + i, h)),
                  lat(-1), lat(0), lat(1), lat(-1), lat(0), lat(1), ctx, ctx],
        out_specs=pl.BlockSpec((qb, gw), lambda bi, h, i: (bi * nb + i, h)),
        out_shape=jax.ShapeDtypeStruct((b * t, SWA_HEADS * SWA_DH), BF16),
        compiler_params=_params("parallel", "parallel", "arbitrary"),
        name="swa_attention",
    )(sink, bias, q, kx, kx, kx, vx, vx, vx, kc, vc)


def _dup_heads(a, col0):
    m = a.shape[0]
    h = a[:, col0:col0 + SWA_KV_HEADS * SWA_DH].reshape(m, SWA_KV_HEADS, 1, SWA_DH)
    return jnp.broadcast_to(h, (m, SWA_KV_HEADS, LANES // SWA_DH, SWA_DH)).reshape(m, SWA_KV_HEADS * LANES)


def _first_max(vals, iota, size, axis):
    m = jnp.max(vals, axis=axis, keepdims=True)
    first = jnp.min(jnp.where(vals == m, iota, size), axis=axis, keepdims=True)
    return m, first


def _router_kernel(x_ref, nw_ref, sh_ref, sc_ref, rwt_ref, rb_ref, c0_ref,
                   hp_ref, idx_ref, wt_ref, rank_ref, cnt_ref, carry_ref):
    @pl.when(pl.program_id(0) == 0)
    def _():
        carry_ref[...] = c0_ref[...]

    h = _norm_mod(x_ref[...], nw_ref[...], sh_ref[0], sc_ref[0])
    tm, d = h.shape
    _store_token_tiles(hp_ref, 0, _pack_pairs(h[:, :d // 2], h[:, d // 2:]))
    scores = jax.nn.sigmoid(_dot3(rwt_ref[...], h, NT_DIMS))
    sel = scores + rb_ref[...]
    neg = jnp.float32(-jnp.inf)

    g_io = lax.broadcasted_iota(I32, (GROUP_SIZE, tm), 0)
    gs_rows = []
    for g in range(N_GROUPS):
        sg = sel[g * GROUP_SIZE:(g + 1) * GROUP_SIZE]
        m1, f1 = _first_max(sg, g_io, GROUP_SIZE, 0)
        m2 = jnp.max(jnp.where(g_io == f1, neg, sg), axis=0, keepdims=True)
        gs_rows.append(m1 + m2)
    cur = jnp.concatenate(gs_rows, axis=0)
    n_io = lax.broadcasted_iota(I32, (N_GROUPS, tm), 0)
    gmask = jnp.zeros((N_GROUPS, tm), I32)
    for _ in range(TOPK_GROUPS):
        _, f = _first_max(cur, n_io, N_GROUPS, 0)
        hit = n_io == f
        gmask = jnp.where(hit, 1, gmask)
        cur = jnp.where(hit, neg, cur)
    cur = jnp.concatenate(
        [jnp.where(gmask[g:g + 1] > 0, sel[g * GROUP_SIZE:(g + 1) * GROUP_SIZE], neg) for g in range(N_GROUPS)],
        axis=0)

    e_io = lax.broadcasted_iota(I32, (N_EXPERTS, tm), 0)
    chosen = jnp.zeros((N_EXPERTS, tm), F32)
    idx_rows, w_rows = [], []
    for _ in range(TOP_K):
        _, f = _first_max(cur, e_io, N_EXPERTS, 0)
        hit = e_io == f
        idx_rows.append(f)
        w_rows.append(jnp.sum(jnp.where(hit, scores, 0.0), axis=0, keepdims=True))
        chosen = jnp.where(hit, 1.0, chosen)
        cur = jnp.where(hit, neg, cur)
    idx = jnp.concatenate(idx_rows, axis=0)
    w = jnp.concatenate(w_rows, axis=0)
    idx_ref[...] = idx
    wt_ref[...] = w * (1.0 / jnp.sum(w, axis=0, keepdims=True)) * ROUTED_SCALE

    onehot = chosen.astype(BF16)
    before = (lax.broadcasted_iota(I32, (tm, tm), 0) < lax.broadcasted_iota(I32, (tm, tm), 1)).astype(BF16)
    base = carry_ref[:, 0:1] + jnp.dot(onehot, before, preferred_element_type=F32)
    rank_ref[...] = jnp.concatenate(
        [jnp.sum(jnp.where(e_io == idx_rows[k], base, 0.0), axis=0, keepdims=True) for k in range(TOP_K)],
        axis=0).astype(I32)
    carry_ref[...] = carry_ref[...] + jnp.sum(chosen, axis=1, keepdims=True)
    cnt_ref[...] = carry_ref[...]


def _router(x, nw, sh, sc, rwt, rb, counts0, rows_per_group):
    m, d = x.shape
    tm = math.gcd(_tile(m, 256, LANES), rows_per_group)
    assert (d // 2) % LANES == 0
    tr = d // 2 // LANES
    grp = lambda i: ((i * tm) // rows_per_group, 0, 0)
    tok = lambda rows: pl.BlockSpec((rows, tm), lambda i: (0, i))
    return pl.pallas_call(
        _router_kernel,
        grid=(m // tm,),
        in_specs=[pl.BlockSpec((tm, d), lambda i: (i, 0)),
                  pl.BlockSpec((1, d), lambda i: (0, 0)),
                  pl.BlockSpec((1, 1, d), grp),
                  pl.BlockSpec((1, 1, d), grp),
                  pl.BlockSpec((N_EXPERTS, d), lambda i: (0, 0)),
                  pl.BlockSpec((N_EXPERTS, 1), lambda i: (0, 0)),
                  pl.BlockSpec((N_EXPERTS, LANES), lambda i: (0, 0))],
        out_specs=[pl.BlockSpec((tm * tr, LANES), lambda i: (i, 0)),
                   tok(TOP_K), tok(TOP_K), tok(TOP_K),
                   pl.BlockSpec((N_EXPERTS, LANES), lambda i: (0, 0))],
        out_shape=[jax.ShapeDtypeStruct((m * tr, LANES), U32),
                   jax.ShapeDtypeStruct((TOP_K, m), I32),
                   jax.ShapeDtypeStruct((TOP_K, m), F32),
                   jax.ShapeDtypeStruct((TOP_K, m), I32),
                   jax.ShapeDtypeStruct((N_EXPERTS, LANES), F32)],
        scratch_shapes=[pltpu.VMEM((N_EXPERTS, LANES), F32)],
        compiler_params=_params("arbitrary"),
        name="moe_router",
    )(x, nw.reshape(1, d), sh, sc, rwt, rb.reshape(N_EXPERTS, 1), counts0)


def _dispatch_kernel(nv_ref, dest_ref, hp_ref, xs_ref, zero_ref, sem, *, tr):
    tm = hp_ref.shape[0] // tr
    blk_rows = MOE_BLOCK * tr

    @pl.when(pl.program_id(0) == 0)
    def _():
        zero_ref[...] = jnp.zeros(zero_ref.shape, U32)

        def fill(blk, carry):
            @pl.when(nv_ref[blk] < MOE_BLOCK)
            def _():
                cp = pltpu.make_async_copy(
                    zero_ref, xs_ref.at[pl.ds(pl.multiple_of(blk * blk_rows, blk_rows), blk_rows)], sem)
                cp.start()
                cp.wait()
            return carry

        lax.fori_loop(0, nv_ref.shape[0], fill, 0)

    def copy(t, k):
        return _tile_copy(hp_ref, t * tr, xs_ref, dest_ref[t * TOP_K + k], tr, sem)

    def start(t, carry):
        for k in range(TOP_K):
            copy(t, k).start(priority=k % 2)
        return carry

    def wait(t, carry):
        for k in range(TOP_K):
            copy(t, k).wait()
        return carry

    lax.fori_loop(0, tm, start, 0)
    lax.fori_loop(0, tm, wait, 0)


def _dispatch(block_nv, dest, hp, n_rows, tr):
    m = hp.shape[0] // tr
    tm = _tile(m, 512, LANES)
    return pl.pallas_call(
        functools.partial(_dispatch_kernel, tr=tr),
        grid_spec=pltpu.PrefetchScalarGridSpec(
            num_scalar_prefetch=1,
            grid=(m // tm,),
            in_specs=[pl.BlockSpec((tm * TOP_K,), lambda i, nv: (i,), memory_space=pltpu.SMEM),
                      pl.BlockSpec((tm * tr, LANES), lambda i, nv: (i, 0))],
            out_specs=pl.BlockSpec(memory_space=pl.ANY),
            scratch_shapes=[pltpu.VMEM((MOE_BLOCK * tr, LANES), U32), pltpu.SemaphoreType.DMA(())]),
        out_shape=jax.ShapeDtypeStruct((n_rows * tr, LANES), U32),
        compiler_params=_params("arbitrary"),
        name="moe_dispatch",
    )(block_nv, dest, hp)


def _gffn_kernel(be_ref, nv_ref, xs_ref, w1_ref, w3_ref, w2_ref, ys_ref, w1b_ref, w3b_ref, w2b_ref, *, tr):
    i = pl.program_id(0)
    nv = nv_ref[i]
    prev = be_ref[jnp.maximum(i - 1, 0)]

    @pl.when((i == 0) | (be_ref[i] != prev))
    def _():
        w1b_ref[...] = w1_ref[0, 0].astype(BF16)
        w3b_ref[...] = w3_ref[0, 0].astype(BF16)
        w2b_ref[...] = w2_ref[0, 0].astype(BF16)

    @pl.when(nv > 0)
    def _():
        lo, hi = _unpack_pairs(_load_token_tiles(xs_ref, 0, MOE_BLOCK, tr))
        dh = lo.shape[1]
        lo = lo.astype(BF16)
        hi = hi.astype(BF16)
        d = functools.partial(jnp.dot, preferred_element_type=F32)
        a = d(lo, w1b_ref[:dh, :]) + d(hi, w1b_ref[dh:, :])
        g = d(lo, w3b_ref[:dh, :]) + d(hi, w3b_ref[dh:, :])
        y = d((_silu(a) * g).astype(BF16), w2b_ref[...])
        _store_token_tiles(ys_ref, 0, _pack_pairs(y[:, :dh], y[:, dh:]))

    @pl.when(nv == 0)
    def _():
        ys_ref[...] = jnp.zeros(ys_ref.shape, U32)


def _grouped_ffn(block_e, block_nv, xs, w1, w3, w2, layer, tr):
    _, _, d, f = w1.shape
    blk_rows = MOE_BLOCK * tr
    nb = xs.shape[0] // blk_rows
    return pl.pallas_call(
        functools.partial(_gffn_kernel, tr=tr),
        grid_spec=pltpu.PrefetchScalarGridSpec(
            num_scalar_prefetch=2,
            grid=(nb,),
            in_specs=[pl.BlockSpec((blk_rows, LANES), lambda i, be, nv: (i, 0)),
                      pl.BlockSpec((1, 1, d, f), lambda i, be, nv: (layer, be[i], 0, 0)),
                      pl.BlockSpec((1, 1, d, f), lambda i, be, nv: (layer, be[i], 0, 0)),
                      pl.BlockSpec((1, 1, f, d), lambda i, be, nv: (layer, be[i], 0, 0))],
            out_specs=pl.BlockSpec((blk_rows, LANES), lambda i, be, nv: (i, 0)),
            scratch_shapes=[pltpu.VMEM((d, f), BF16), pltpu.VMEM((d, f), BF16), pltpu.VMEM((f, d), BF16)]),
        out_shape=jax.ShapeDtypeStruct(xs.shape, U32),
        compiler_params=_params("arbitrary"),
        name="moe_grouped_ffn",
    )(block_e, block_nv, xs, w1, w3, w2)


def _combine_kernel(dest_ref, dest_next_ref, x_ref, hp_ref, wt_ref, g_ref, ws1_ref, ws3_ref, ws2_ref, ys_ref,
                    o_ref, buf_a, buf_b, sem_a, sem_b, *, tr):
    i = pl.program_id(0)
    tm = x_ref.shape[0]
    dh = tr * LANES

    def copy(dref, t, k, buf, sem):
        return _tile_copy(ys_ref, dref[t * TOP_K + k], buf, (k * tm + t) * tr, tr, sem)

    def wait_all(buf, sem):
        def wait(t, carry):
            for k in range(TOP_K):
                copy(dest_ref, t, k, buf, sem).wait()
            return carry
        lax.fori_loop(0, tm, wait, 0)

    @pl.when(i == 0)
    def _():
        def start(t, carry):
            for k in range(TOP_K):
                copy(dest_ref, t, k, buf_a, sem_a).start(priority=k % 2)
            return carry
        lax.fori_loop(0, tm, start, 0)

    def step(buf, sem, buf_next, sem_next):
        wait_all(buf, sem)
        for t in range(tm):
            for k in range(TOP_K):
                copy(dest_next_ref, t, k, buf_next, sem_next).start(priority=k % 2)
        lo, hi = _unpack_pairs(_load_token_tiles(hp_ref, 0, tm, tr))
        lo = lo.astype(BF16)
        hi = hi.astype(BF16)
        d = functools.partial(jnp.dot, preferred_element_type=F32)
        a = d(lo, ws1_ref[:dh, :]) + d(hi, ws1_ref[dh:, :])
        g = d(lo, ws3_ref[:dh, :]) + d(hi, ws3_ref[dh:, :])
        shared = d((_silu(a) * g).astype(BF16), ws2_ref[...])
        acc_lo = jnp.zeros((tm, dh), F32)
        acc_hi = jnp.zeros((tm, dh), F32)
        for k in range(TOP_K):
            ylo, yhi = _unpack_pairs(_load_token_tiles(buf, k * tm * tr, tm, tr))
            wk = wt_ref[:, k:k + 1]
            acc_lo = acc_lo + wk * ylo
            acc_hi = acc_hi + wk * yhi
        o_ref[:, :dh] = x_ref[:, :dh] + g_ref[0][:, :dh] * (acc_lo + shared[:, :dh])
        o_ref[:, dh:] = x_ref[:, dh:] + g_ref[0][:, dh:] * (acc_hi + shared[:, dh:])

        @pl.when(i == pl.num_programs(0) - 1)
        def _():
            wait_all(buf_next, sem_next)

    @pl.when(i % 2 == 0)
    def _():
        step(buf_a, sem_a, buf_b, sem_b)

    @pl.when(i % 2 == 1)
    def _():
        step(buf_b, sem_b, buf_a, sem_a)


def _combine(dest, x, hp, wt, gate, ws1, ws3, ws2, ys, rows_per_group, tr):
    m, d = x.shape
    f = ws1.shape[1]
    tm = math.gcd(_tile(m, 128, LANES), rows_per_group)
    nt = m // tm
    grp = lambda i: ((i * tm) // rows_per_group, 0, 0)
    buf = pltpu.VMEM((TOP_K * tm * tr, LANES), U32)
    return pl.pallas_call(
        functools.partial(_combine_kernel, tr=tr),
        grid=(nt,),
        in_specs=[pl.BlockSpec((tm * TOP_K,), lambda i: (i,), memory_space=pltpu.SMEM),
                  pl.BlockSpec((tm * TOP_K,), lambda i: (jnp.minimum(i + 1, nt - 1),), memory_space=pltpu.SMEM),
                  pl.BlockSpec((tm, d), lambda i: (i, 0)),
                  pl.BlockSpec((tm * tr, LANES), lambda i: (i, 0)),
                  pl.BlockSpec((tm, TOP_K), lambda i: (i, 0)),
                  pl.BlockSpec((1, 1, d), grp),
                  pl.BlockSpec((d, f), lambda i: (0, 0)),
                  pl.BlockSpec((d, f), lambda i: (0, 0)),
                  pl.BlockSpec((f, d), lambda i: (0, 0)),
                  pl.BlockSpec(memory_space=pl.ANY)],
        out_specs=pl.BlockSpec((tm, d), lambda i: (i, 0)),
        out_shape=jax.ShapeDtypeStruct((m, d), F32),
        scratch_shapes=[buf, buf, pltpu.SemaphoreType.DMA(()), pltpu.SemaphoreType.DMA(())],
        compiler_params=_params("arbitrary"),
        name="moe_combine",
    )(dest, dest, x, hp, wt, gate, ws1, ws3, ws2, ys)


def _moe(streams, layer, nw, rw, rb, w1, w3, w2, ws1, ws3, ws2):
    d = streams[0][0].shape[1]
    rwt = rw.T
    counts = jnp.zeros((N_EXPERTS, LANES), F32)
    routed = []
    for x, sh, sc, _, rpg in streams:
        hp, idx, wt, rank, counts = _router(x, nw, sh, sc, rwt, rb, counts, rpg)
        routed.append((hp, idx, wt, rank))
    n_assign = sum(s[0].shape[0] for s in streams) * TOP_K
    n_blocks = (n_assign + N_EXPERTS * (MOE_BLOCK - 1) + MOE_BLOCK - 1) // MOE_BLOCK
    cnt = counts[:, 0].astype(I32)
    padded = (cnt + MOE_BLOCK - 1) // MOE_BLOCK * MOE_BLOCK
    pad_end = jnp.cumsum(padded)
    pad_start = pad_end - padded
    bstart = jnp.arange(n_blocks, dtype=I32) * MOE_BLOCK
    block_e = jnp.minimum(jnp.sum((bstart[:, None] >= pad_end[None, :]).astype(I32), axis=1), N_EXPERTS - 1)
    block_nv = jnp.clip(cnt[block_e] - (bstart - pad_start[block_e]), 0, MOE_BLOCK).astype(I32)
    e_ar = jnp.arange(N_EXPERTS, dtype=I32)
    tr = d // 2 // LANES
    dests = [((jnp.sum(jnp.where(idx[:, :, None] == e_ar, pad_start, 0), axis=-1) + rank) * tr).T.reshape(-1)
             for _, idx, _, rank in routed]
    if len(streams) > 1:
        hp_all = jnp.concatenate([r[0] for r in routed], axis=0)
        dest_all = jnp.concatenate(dests, axis=0)
    else:
        hp_all, dest_all = routed[0][0], dests[0]
    xs = _dispatch(block_nv, dest_all, hp_all, n_blocks * MOE_BLOCK, tr)
    ys = _grouped_ffn(block_e, block_nv, xs, w1, w3, w2, layer, tr)
    ws1b, ws3b, ws2b = ws1.astype(BF16), ws3.astype(BF16), ws2.astype(BF16)
    return [_combine(dest, x, hp, wt.T, gate, ws1b, ws3b, ws2b, ys, rpg, tr)
            for (x, _, _, gate, rpg), (hp, _, wt, _), dest in zip(streams, routed, dests)]


def _mixer_ab(xs, cs, mx, mc, b, t, n_c, layer, nw, w_in, w_out, q_norm, k_norm, lam_vec, subln, conv_w,
              a_log, dt_bias, out_norm):
    n_main = w_in.shape[1] - 4 * DN_HEADS
    assert n_main % LANES == 0
    w_main = w_in[:, :n_main].astype(BF16)
    w_small = jnp.zeros((w_in.shape[0], LANES), F32).at[:, :4 * DN_HEADS].set(w_in[:, n_main:])
    z_x, zs_x = _norm_mod_matmul(xs, nw, mx[0], mx[1], w_main, t, w_small)
    z_c, zs_c = _norm_mod_matmul(cs, nw, mc[0], mc[1], w_main, b * n_c, w_small)
    nqk = 2 * DIFF_HEADS * DIFF_DK
    tabs = _rope_tables(t, DIFF_DK)
    qscale = DIFF_DK ** -0.5 * LOG2E
    qd_x = _headnorm_rope(z_x, 0, nqk, q_norm, DIFF_DK, qscale, tabs, t)
    kd_x = _headnorm_rope(z_x, nqk, nqk, k_norm, DIFF_DK, 1.0, tabs, t)
    qd_c = _headnorm_rope(z_c, 0, nqk, q_norm, DIFF_DK, qscale, None, t)
    kd_c = _headnorm_rope(z_c, nqk, nqk, k_norm, DIFF_DK, 1.0, None, t)
    lam_init = 0.8 - 0.6 * math.exp(-0.3 * layer)
    od_x = _diff_attention(lam_vec, qd_x, kd_c, z_c, subln, lam_init, b, n_c, kx=kd_x, zx=z_x, t=t)
    od_c = _diff_attention(lam_vec, qd_c, kd_c, z_c, subln, lam_init, b, n_c)
    dn_col0 = 2 * nqk + DIFF_HEADS * DIFF_DV
    dn_x = _dn_prep(z_x, dn_col0, conv_w, t)
    dn_c = _dn_prep(z_c, dn_col0, conv_w, n_c)
    zst_x = zs_x[:, :4 * DN_HEADS].T
    zst_c = zs_c[:, :4 * DN_HEADS].T
    prm = _deltanet_params(a_log, dt_bias)
    s0 = jnp.zeros((b, DN_HEADS, DN_DK, DN_DV), F32)
    o_cf, s_cf = _deltanet(dn_c, zs_c, zst_c, prm, s0, b, n_c, False)
    o_xf, _ = _deltanet(dn_x, zs_x, zst_x, prm, s_cf, b, t, False)
    o_cb, s_cb = _deltanet(dn_c, zs_c, zst_c, prm, s0, b, n_c, True)
    o_xb, _ = _deltanet(dn_x, zs_x, zst_x, prm, s_cb, b, t, True)
    gate_col0 = dn_col0 + conv_w.shape[1]
    w_out_b = w_out.astype(BF16)
    x1 = _outproj_ab(od_x, o_xf, o_xb, z_x, gate_col0, out_norm, w_out_b, xs, mx[2], t)
    c1 = _outproj_ab(od_c, o_cf, o_cb, z_c, gate_col0, out_norm, w_out_b, cs, mc[2], b * n_c)
    return x1, c1


def _mixer_swa(xs, cs, mx, mc, b, t, n_c, nw, w_in, w_out, q_norm, k_norm, sink):
    nq = SWA_HEADS * SWA_DH
    nkv = SWA_KV_HEADS * SWA_DH
    w_b = w_in.astype(BF16)
    z_x = _norm_mod_matmul(xs, nw, mx[0], mx[1], w_b, t)
    z_c = _norm_mod_matmul(cs, nw, mc[0], mc[1], w_b[:, nq:], b * n_c)
    tabs = _rope_tables(t, SWA_DH)
    q = _headnorm_rope(z_x, 0, nq, q_norm, SWA_DH, SWA_DH ** -0.5 * LOG2E, tabs, t)
    k_x = _headnorm_rope(z_x, nq, nkv, k_norm, SWA_DH, 1.0, tabs, t)
    k_c = _headnorm_rope(z_c, 0, nkv, k_norm, SWA_DH, 1.0, None, t)
    att = _swa_attention(sink.reshape(SWA_KV_HEADS, SWA_GROUP), q, _dup_heads(k_x, 0), _dup_heads(z_x, nq + nkv),
                         _dup_heads(k_c, 0), _dup_heads(z_c, nkv), b, t, n_c)
    return _outproj(att, w_out.astype(BF16), xs, mx[2], t)


def kernel(x, c, ctx, c_ctx, mod_w, mod_b, norm_mix, norm_ffn, ab_w_in, ab_w_out, diff_q_norm, diff_k_norm, diff_lambda, diff_subln, dn_conv, dn_a_log, dn_dt_bias, dn_out_norm, swa_w_in, swa_w_out, swa_q_norm, swa_k_norm, swa_sink, router_w, router_bias, exp_w1, exp_w3, exp_w2, shared_w1, shared_w3, shared_w2):
    b, t, d = x.shape
    n_c = ctx.shape[1]
    depth = mod_w.shape[0]
    assert depth == 2 and t % Q_BLOCK == 0 and t % DN_CHUNK == 0 and n_c % DN_CHUNK == 0
    xs = x.reshape(b * t, d)
    cs = ctx.reshape(b * n_c, d)
    n_mod = -(-(b + 1) // SUBLANES) * SUBLANES
    a_mod = jnp.zeros((n_mod, d), F32).at[0].set(c_ctx).at[1:1 + b].set(c)
    for layer in range(depth):
        with_ctx = layer < depth - 1
        p = layer // 2
        mod = _modulation(a_mod, mod_w, layer, mod_b[layer])
        mc = [mod[0:1, j * d:(j + 1) * d].reshape(1, 1, d) for j in range(6)]
        mx = [mod[1:1 + b, j * d:(j + 1) * d].reshape(b, 1, d) for j in range(6)]
        if layer % 2 == 0:
            xs, c_new = _mixer_ab(xs, cs, mx, mc, b, t, n_c, layer, norm_mix[layer], ab_w_in[p], ab_w_out[p],
                                  diff_q_norm[p], diff_k_norm[p], diff_lambda[p], diff_subln[p], dn_conv[p],
                                  dn_a_log[p], dn_dt_bias[p], dn_out_norm[p])
        else:
            assert not with_ctx
            xs = _mixer_swa(xs, cs, mx, mc, b, t, n_c, norm_mix[layer], swa_w_in[p], swa_w_out[p],
                            swa_q_norm[p], swa_k_norm[p], swa_sink[p])
            c_new = None
        moe_w = (layer, norm_ffn[layer], router_w[layer], router_bias[layer], exp_w1, exp_w3, exp_w2,
                 shared_w1[layer], shared_w3[layer], shared_w2[layer])
        if with_ctx:
            cs, xs = _moe([(c_new, mc[3], mc[4], mc[5], b * n_c), (xs, mx[3], mx[4], mx[5], t)], *moe_w)
        else:
            (xs,) = _moe([(xs, mx[3], mx[4], mx[5], t)], *moe_w)
    return xs.reshape(b, t, d)
```

```python
import functools
import math

import jax
import jax.numpy as jnp
import numpy as np
from jax import lax
from jax.experimental import pallas as pl
from jax.experimental.pallas import tpu as pltpu

F32 = jnp.float32
BF16 = jnp.bfloat16
I32 = jnp.int32
U32 = jnp.uint32

EPS = 1e-6
GRID_W = 64
ROPE_BASE = 10000.0
DIFF_HEADS = 4
DIFF_DK = 128
DIFF_DV = 256
DN_HEADS = 8
DN_DK = 128
DN_DV = 128
DN_CONV = 5
DN_CHUNK = 128
SWA_HEADS = 32
SWA_KV_HEADS = 4
SWA_GROUP = SWA_HEADS // SWA_KV_HEADS
SWA_DH = 64
WINDOW = 128
Q_BLOCK = 128
N_EXPERTS = 64
TOP_K = 8
N_GROUPS = 8
GROUP_SIZE = N_EXPERTS // N_GROUPS
TOPK_GROUPS = 4
ROUTED_SCALE = 2.5
MOE_BLOCK = 512

LANES = 128
SUBLANES = 8
VMEM_LIMIT_BYTES = 56 * 1024 * 1024

NT_DIMS = (((1,), (1,)), ((), ()))
LOG2E = math.log2(math.e)


def _params(*semantics):
    return pltpu.CompilerParams(dimension_semantics=semantics, vmem_limit_bytes=VMEM_LIMIT_BYTES)


def _tile(n, pref, mult):
    if n <= pref:
        return n
    t = pref - pref % mult
    while t > mult and n % t:
        t -= mult
    assert n % t == 0, (n, pref, mult)
    return t


def _mm(a, b):
    return jnp.dot(a.astype(BF16), b.astype(BF16), preferred_element_type=F32)


def _split2(x):
    hi = x.astype(BF16)
    lo = (x - hi.astype(F32)).astype(BF16)
    return hi, lo


def _dot3(a, b, dims=None):
    if dims is None:
        dims = (((a.ndim - 1,), (0,)), ((), ()))
    ah, al = _split2(a)
    bh, bl = _split2(b)
    d = functools.partial(lax.dot_general, dimension_numbers=dims, preferred_element_type=F32)
    return d(ah, bh) + d(ah, bl) + d(al, bh)


def _silu(x):
    return x * jax.nn.sigmoid(x)


def _softplus(x):
    return jnp.maximum(x, 0.0) + jnp.log(1.0 + jnp.exp(-jnp.abs(x)))


def _pack_pairs(lo, hi):
    ulo = lax.bitcast_convert_type(lo.astype(BF16).astype(F32), U32) >> 16
    uhi = lax.bitcast_convert_type(hi.astype(BF16).astype(F32), U32) & jnp.uint32(0xFFFF0000)
    return ulo | uhi


def _unpack_pairs(u):
    lo = lax.bitcast_convert_type(u << 16, F32)
    hi = lax.bitcast_convert_type(u & jnp.uint32(0xFFFF0000), F32)
    return lo, hi


def _store_token_tiles(ref, base, packed):
    n, width = packed.shape
    tr = width // LANES
    for j in range(tr):
        ref[pl.ds(base + j, n, stride=tr), :] = packed[:, j * LANES:(j + 1) * LANES]


def _load_token_tiles(ref, base, n, tr):
    return jnp.concatenate([ref[pl.ds(base + j, n, stride=tr), :] for j in range(tr)], axis=1)


def _tile_copy(src, src_row, dst, dst_row, tr, sem):
    return pltpu.make_async_copy(src.at[pl.ds(pl.multiple_of(src_row, tr), tr)],
                                 dst.at[pl.ds(pl.multiple_of(dst_row, tr), tr)], sem)


def _mod_kernel(a_ref, w_ref, b_ref, o_ref):
    o_ref[...] = _dot3(_silu(a_ref[...]), w_ref[0]) + b_ref[...]


def _modulation(a, w_all, layer, b):
    r, d = a.shape
    n = w_all.shape[2]
    tn = _tile(n, 768, LANES)
    return pl.pallas_call(
        _mod_kernel,
        grid=(n // tn,),
        in_specs=[pl.BlockSpec((r, d), lambda j: (0, 0)),
                  pl.BlockSpec((1, d, tn), lambda j: (layer, 0, j)),
                  pl.BlockSpec((1, tn), lambda j: (0, j))],
        out_specs=pl.BlockSpec((r, tn), lambda j: (0, j)),
        out_shape=jax.ShapeDtypeStruct((r, n), F32),
        compiler_params=_params("parallel"),
        name="modulation",
    )(a, w_all, b.reshape(1, n))


def _norm_mod(x, nw, sh, sc):
    y = x * lax.rsqrt(jnp.mean(x * x, axis=-1, keepdims=True) + EPS) * nw
    return y * (1.0 + sc) + sh


def _head_norm_rope(x, w, hd, lane, cos, sin):
    x2 = x * x
    if hd == LANES:
        ms = jnp.mean(x2, axis=-1, keepdims=True)
    else:
        lo = jnp.sum(jnp.where(lane < hd, x2, 0.0), axis=-1, keepdims=True)
        hi = jnp.sum(jnp.where(lane >= hd, x2, 0.0), axis=-1, keepdims=True)
        ms = jnp.where(lane < hd, lo, hi) * (1.0 / hd)
    y = x * lax.rsqrt(ms + EPS) * w
    if cos is not None:
        q = hd // 4
        nxt = pltpu.roll(y, LANES - q, 1)
        prv = pltpu.roll(y, q, 1)
        partner = jnp.where((lane & (2 * q - 1)) < q, nxt, prv)
        y = y * cos + partner * sin
    return y


def _nmm_kernel(x_ref, nw_ref, sh_ref, sc_ref, w_ref, *rest, has_small, head_blocks, hd, rope):
    rest = list(rest)
    ws_ref = rest.pop(0) if has_small else None
    hw_ref = rest.pop(0) if head_blocks else None
    cos_ref, sin_ref = (rest.pop(0), rest.pop(0)) if rope else (None, None)
    o_ref = rest.pop(0)
    os_ref = rest.pop(0) if has_small else None
    (h_ref,) = rest
    j = pl.program_id(1)

    @pl.when(j == 0)
    def _():
        h = _norm_mod(x_ref[...], nw_ref[...], sh_ref[0], sc_ref[0])
        h_ref[...] = h.astype(BF16)
        if has_small:
            os_ref[...] = _dot3(h, ws_ref[...])

    r = jnp.dot(h_ref[...], w_ref[...], preferred_element_type=F32)
    if not head_blocks:
        o_ref[...] = r.astype(o_ref.dtype)
        return
    tm, tn = r.shape
    plain = j >= 0
    for lo, hi, rows in head_blocks:
        inside = (j >= lo) & (j < hi)
        plain = plain & jnp.logical_not(inside)

        @pl.when(inside)
        def _(rows=rows):
            lane = lax.broadcasted_iota(I32, (tm, LANES), 1)
            cos = cos_ref[...] if rope else None
            sin = sin_ref[...] if rope else None
            for g in range(tn // LANES):
                sl = slice(g * LANES, (g + 1) * LANES)
                y = r[:, sl]
                if rows[g] is not None:
                    y = _head_norm_rope(y, hw_ref[rows[g]], hd, lane, cos, sin)
                o_ref[:, sl] = y.astype(o_ref.dtype)

    @pl.when(plain)
    def _():
        o_ref[...] = r.astype(o_ref.dtype)


def _norm_mod_matmul(x, nw, sh, sc, w, rows_per_group, w_small=None, tn=1024, heads=None):
    m, d = x.shape
    n = w.shape[1]
    tm = math.gcd(_tile(m, 512, SUBLANES), rows_per_group)
    tn = _tile(n, tn, LANES)
    has_small = w_small is not None
    grp = lambda i, j: ((i * tm) // rows_per_group, 0, 0)
    in_specs = [pl.BlockSpec((tm, d), lambda i, j: (i, 0)),
                pl.BlockSpec((1, d), lambda i, j: (0, 0)),
                pl.BlockSpec((1, 1, d), grp),
                pl.BlockSpec((1, 1, d), grp),
                pl.BlockSpec((d, tn), lambda i, j: (0, j))]
    args = [x, nw.reshape(1, d), sh, sc, w]
    out_specs = [pl.BlockSpec((tm, tn), lambda i, j: (i, j))]
    out_shape = [jax.ShapeDtypeStruct((m, n), BF16)]
    if has_small:
        in_specs.append(pl.BlockSpec((d, LANES), lambda i, j: (0, 0)))
        args.append(w_small)
        out_specs.append(pl.BlockSpec((tm, LANES), lambda i, j: (i, 0)))
        out_shape.append(jax.ShapeDtypeStruct((m, LANES), F32))
    head_blocks, hd, rope = (), LANES, False
    if heads is not None:
        head_blocks, hd = tuple(heads["blocks"]), heads["hd"]
        hw = heads["weights"]
        in_specs.append(pl.BlockSpec((hw.shape[0], 1, LANES), lambda i, j: (0, 0, 0)))
        args.append(hw.reshape(hw.shape[0], 1, LANES))
        if heads["tables"] is not None:
            rope = True
            nt = heads["t"] // tm
            assert heads["t"] % tm == 0
            in_specs += [pl.BlockSpec((tm, LANES), lambda i, j: (i % nt, 0))] * 2
            args += list(heads["tables"])
    outs = pl.pallas_call(
        functools.partial(_nmm_kernel, has_small=has_small, head_blocks=head_blocks, hd=hd, rope=rope),
        grid=(m // tm, n // tn),
        in_specs=in_specs,
        out_specs=out_specs,
        out_shape=out_shape,
        scratch_shapes=[pltpu.VMEM((tm, d), BF16)],
        compiler_params=_params("parallel", "arbitrary"),
        name="norm_mod_matmul",
    )(*args)
    return outs if has_small else outs[0]


def _rope_tables(t, head_dim):
    q = head_dim // 4
    pos = jnp.arange(t, dtype=I32)
    row = (pos // GRID_W).astype(F32)
    col = (pos % GRID_W).astype(F32)
    axis_dim = head_dim // 2
    inv_freq = ROPE_BASE ** (-jnp.arange(0, axis_dim, 2, dtype=F32) / axis_dim)
    lane = jnp.arange(LANES) % head_dim
    freq = inv_freq[lane % q]
    p = jnp.where((lane < head_dim // 2)[None, :], row[:, None], col[:, None])
    ang = p * freq[None, :]
    sign = jnp.where((lane % (2 * q)) < q, -1.0, 1.0)
    return jnp.cos(ang), jnp.sin(ang) * sign[None, :]


def _diff_attn_kernel(lv_ref, q_ref, *rest, lam_init, has_lat):
    if has_lat:
        kx_ref, vx_ref, kc_ref, vc_ref, w_ref, o_ref = rest
    else:
        kc_ref, vc_ref, w_ref, o_ref = rest
    lv = lv_ref[...]
    lam = (jnp.exp(jnp.sum(lv[0:1] * lv[1:2], keepdims=True))
           - jnp.exp(jnp.sum(lv[2:3] * lv[3:4], keepdims=True)) + lam_init)
    q = q_ref[...]

    def probs(c):
        sl = slice(c * DIFF_DK, (c + 1) * DIFF_DK)
        qc = q[:, sl]
        s_c = lax.dot_general(qc, kc_ref[:, sl], NT_DIMS, preferred_element_type=F32)
        m = jnp.max(s_c, axis=-1, keepdims=True)
        p_x = None
        if has_lat:
            s_x = lax.dot_general(qc, kx_ref[:, sl], NT_DIMS, preferred_element_type=F32)
            m = jnp.maximum(m, jnp.max(s_x, axis=-1, keepdims=True))
            p_x = jnp.exp2(s_x - m)
        p_c = jnp.exp2(s_c - m)
        l = jnp.sum(p_c, axis=-1, keepdims=True)
        if has_lat:
            l = l + jnp.sum(p_x, axis=-1, keepdims=True)
        return p_x, p_c, l

    p1x, p1c, l1 = probs(0)
    p2x, p2c, l2 = probs(1)
    ratio = lam * l1 * (1.0 / l2)
    o = jnp.dot((p1c - p2c * ratio).astype(BF16), vc_ref[...], preferred_element_type=F32)
    if has_lat:
        o = o + jnp.dot((p1x - p2x * ratio).astype(BF16), vx_ref[...], preferred_element_type=F32)
    o = o * (1.0 / l1)
    o = o * lax.rsqrt(jnp.mean(o * o, axis=-1, keepdims=True) + EPS)
    o_ref[...] = (o * w_ref[...] * (1.0 - lam_init)).astype(o_ref.dtype)


def _diff_attention(lam_vec, q, kc, zc, subln, lam_init, b, n_c, kx=None, zx=None, t=None):
    has_lat = kx is not None
    hw = 2 * DIFF_DK
    kblk0 = DIFF_HEADS
    vblk0 = (2 * DIFF_HEADS * hw) // DIFF_DV
    tq_all = t if has_lat else n_c
    tq = _tile(tq_all, 256, SUBLANES)
    nq = tq_all // tq
    in_specs = [pl.BlockSpec((4, DIFF_DK), lambda bi, h, qi: (0, 0)),
                pl.BlockSpec((tq, hw), lambda bi, h, qi: (bi * nq + qi, h))]
    args = [lam_vec, q]
    if has_lat:
        in_specs += [pl.BlockSpec((t, hw), lambda bi, h, qi: (bi, kblk0 + h)),
                     pl.BlockSpec((t, DIFF_DV), lambda bi, h, qi: (bi, vblk0 + h))]
        args += [kx, zx]
    in_specs += [pl.BlockSpec((n_c, hw), lambda bi, h, qi: (bi, kblk0 + h)),
                 pl.BlockSpec((n_c, DIFF_DV), lambda bi, h, qi: (bi, vblk0 + h)),
                 pl.BlockSpec((1, DIFF_DV), lambda bi, h, qi: (0, 0))]
    args += [kc, zc, subln.reshape(1, DIFF_DV)]
    return pl.pallas_call(
        functools.partial(_diff_attn_kernel, lam_init=lam_init, has_lat=has_lat),
        grid=(b, DIFF_HEADS, nq),
        in_specs=in_specs,
        out_specs=pl.BlockSpec((tq, DIFF_DV), lambda bi, h, qi: (bi * nq + qi, h)),
        out_shape=jax.ShapeDtypeStruct((b * tq_all, DIFF_HEADS * DIFF_DV), BF16),
        compiler_params=_params("parallel", "parallel", "arbitrary"),
        name="diff_attention",
    )(*args)


def _dn_prep_kernel(z_ref, cw_ref, o_ref, pad_ref, *, seg):
    halo = SUBLANES
    pad_ref[0:halo, :] = jnp.zeros((halo, LANES), F32)
    pad_ref[halo + seg:2 * halo + seg, :] = jnp.zeros((halo, LANES), F32)
    pad_ref[halo:halo + seg, :] = z_ref[...].astype(F32)
    kind = pl.program_id(1) // DN_HEADS
    qk_scale = jnp.where(kind == 0, DN_DK ** -0.5, 1.0).astype(F32)
    rows = _tile(seg, 256, SUBLANES)
    for r0 in range(0, seg, rows):
        acc = jnp.zeros((rows, LANES), F32)
        for j in range(DN_CONV):
            acc = acc + cw_ref[j:j + 1, :] * pad_ref[pl.ds(halo + r0 + j - DN_CONV // 2, rows), :]
        y = _silu(acc)
        nrm = y * lax.rsqrt(jnp.sum(y * y, axis=-1, keepdims=True) + EPS) * qk_scale
        o_ref[r0:r0 + rows, :] = jnp.where(kind < 2, nrm, y).astype(o_ref.dtype)


def _dn_prep(z, col0, conv_w, seg):
    m = z.shape[0]
    ncols = conv_w.shape[1]
    cblk0 = col0 // LANES
    cw = jnp.zeros((SUBLANES, ncols), F32).at[:DN_CONV].set(conv_w)
    return pl.pallas_call(
        functools.partial(_dn_prep_kernel, seg=seg),
        grid=(m // seg, ncols // LANES),
        in_specs=[pl.BlockSpec((seg, LANES), lambda s, g: (s, cblk0 + g)),
                  pl.BlockSpec((SUBLANES, LANES), lambda s, g: (0, g))],
        out_specs=pl.BlockSpec((seg, LANES), lambda s, g: (s, g)),
        out_shape=jax.ShapeDtypeStruct((m, ncols), BF16),
        scratch_shapes=[pltpu.VMEM((seg + 2 * SUBLANES, LANES), F32)],
        compiler_params=_params("parallel", "parallel"),
        name="dn_prep",
    )(z, cw)


def _deltanet_kernel(q_ref, k_ref, v_ref, zs_ref, zst_ref, pr_ref, pca_ref, pcd_ref, s0_ref,
                     o_ref, sout_ref, s_ref, *, reverse, n_chunks):
    step = pl.program_id(1)
    c = DN_CHUNK

    @pl.when(step == 0)
    def _():
        s_ref[...] = s0_ref[0]

    ri = lax.broadcasted_iota(I32, (c, c), 0)
    ci = lax.broadcasted_iota(I32, (c, c), 1)
    if reverse:
        later, strict, later_t = ri <= ci, ri < ci, ri >= ci
    else:
        later, strict, later_t = ri >= ci, ri > ci, ri <= ci
    eye = (ri == ci).astype(F32)
    tri = later.astype(BF16)
    tri_t = later_t.astype(BF16)

    zs = zs_ref[...]
    beta_cols = jax.nn.sigmoid(zs)
    g_cols = -jnp.exp(pr_ref[0:1, :]) * _softplus(zs + pr_ref[1:2, :])
    g_hi = g_cols.astype(BF16)
    g_r1 = g_cols - g_hi.astype(F32)
    g_mid = g_r1.astype(BF16)
    g_lo = (g_r1 - g_mid.astype(F32)).astype(BF16)
    d = functools.partial(jnp.dot, preferred_element_type=F32)
    gc_cols = d(tri, g_hi) + d(tri, g_mid) + d(tri, g_lo)
    g_rows = -jnp.exp(pca_ref[...]) * _softplus(zst_ref[...] + pcd_ref[...])
    h_hi = g_rows.astype(BF16)
    h_r1 = g_rows - h_hi.astype(F32)
    h_mid = h_r1.astype(BF16)
    h_lo = (h_r1 - h_mid.astype(F32)).astype(BF16)
    gc_rows = d(h_hi, tri_t) + d(h_mid, tri_t) + d(h_lo, tri_t)

    dir_off = DN_HEADS if reverse else 0
    last = 0 if reverse else c - 1
    neg_inf = jnp.float32(-jnp.inf)
    heads = []
    for h in range(DN_HEADS):
        sl = slice(h * DN_DK, (h + 1) * DN_DK)
        cb = dir_off + h
        cg = 2 * DN_HEADS + dir_off + h
        beta = beta_cols[:, cb:cb + 1]
        gcol = gc_cols[:, cg:cg + 1]
        grow = gc_rows[cg:cg + 1, :]
        glast = grow[:, last:last + 1]
        q = q_ref[:, sl]
        k = k_ref[:, sl]
        kf = k.astype(F32)
        decay = jnp.exp(jnp.where(later, gcol - grow, neg_inf))
        kb = kf * beta
        both = lax.dot_general(jnp.concatenate([kb.astype(BF16), q], axis=0), k, NT_DIMS,
                               preferred_element_type=F32)
        lmat = jnp.where(strict, both[:c] * decay, 0.0)
        eg = jnp.exp(gcol)
        heads.append(dict(
            sl=sl, glast=glast, lmat=lmat, amat=both[c:] * decay,
            inv=eye - jnp.where((ri >> 1) == (ci >> 1), lmat, 0.0),
            rhs=jnp.concatenate([v_ref[:, sl].astype(F32) * beta, kb * eg], axis=1).astype(BF16),
            qe=(q.astype(F32) * eg).astype(BF16),
            kdec_t=(kf * jnp.exp(glast - gcol)).T.astype(BF16)))
    lev = 1
    while (1 << lev) < c:
        blk = ((ri >> (lev + 1)) == (ci >> (lev + 1))) & ((ri >> lev) != (ci >> lev))
        half = [_mm(hd["inv"], jnp.where(blk, hd["lmat"], 0.0)) for hd in heads]
        for hd, t in zip(heads, half):
            hd["inv"] = hd["inv"] - _mm(t, hd["inv"])
        lev += 1
    uws = [_mm(hd["inv"], hd["rhs"]) for hd in heads]
    states = [s_ref[h] for h in range(DN_HEADS)]
    new_states = []
    for hd, uw, s in zip(heads, uws, states):
        ws_qs = _mm(jnp.concatenate([uw[:, DN_DV:].astype(BF16), hd["qe"]], axis=0), s)
        v_new = uw[:, :DN_DV] - ws_qs[:c]
        o_ref[:, hd["sl"]] = ws_qs[c:] + _mm(hd["amat"], v_new)
        new_states.append(s * jnp.exp(hd["glast"]) + _mm(hd["kdec_t"], v_new))
    for h in range(DN_HEADS):
        s_ref[h] = new_states[h]

    @pl.when(step == n_chunks - 1)
    def _():
        sout_ref[0] = s_ref[...]


def _deltanet(dn, zs, zst, prm, s0, b, seg, reverse):
    pr, pca, pcd = prm
    n = seg // DN_CHUNK
    hw = DN_HEADS * DN_DK
    rb = (lambda bi, s: bi * n + (n - 1 - s)) if reverse else (lambda bi, s: bi * n + s)
    state = pl.BlockSpec((1, DN_HEADS, DN_DK, DN_DV), lambda bi, s: (bi, 0, 0, 0))
    return pl.pallas_call(
        functools.partial(_deltanet_kernel, reverse=reverse, n_chunks=n),
        grid=(b, n),
        in_specs=[pl.BlockSpec((DN_CHUNK, hw), lambda bi, s: (rb(bi, s), 0)),
                  pl.BlockSpec((DN_CHUNK, hw), lambda bi, s: (rb(bi, s), 1)),
                  pl.BlockSpec((DN_CHUNK, hw), lambda bi, s: (rb(bi, s), 2)),
                  pl.BlockSpec((DN_CHUNK, LANES), lambda bi, s: (rb(bi, s), 0)),
                  pl.BlockSpec((4 * DN_HEADS, DN_CHUNK), lambda bi, s: (0, rb(bi, s))),
                  pl.BlockSpec((SUBLANES, LANES), lambda bi, s: (0, 0)),
                  pl.BlockSpec((4 * DN_HEADS, LANES), lambda bi, s: (0, 0)),
                  pl.BlockSpec((4 * DN_HEADS, LANES), lambda bi, s: (0, 0)),
                  state],
        out_specs=[pl.BlockSpec((DN_CHUNK, hw), lambda bi, s: (rb(bi, s), 0)), state],
        out_shape=[jax.ShapeDtypeStruct((b * seg, hw), F32),
                   jax.ShapeDtypeStruct((b, DN_HEADS, DN_DK, DN_DV), F32)],
        scratch_shapes=[pltpu.VMEM((DN_HEADS, DN_DK, DN_DV), F32)],
        compiler_params=_params("parallel", "arbitrary"),
        name="deltanet_bwd" if reverse else "deltanet_fwd",
    )(dn, dn, dn, zs, zst, pr, pca, pcd, s0)


def _deltanet_params(a_log, dt_bias):
    nh = 2 * DN_HEADS
    a = a_log.reshape(nh).astype(F32)
    dtb = dt_bias.reshape(nh).astype(F32)
    pr = jnp.zeros((SUBLANES, LANES), F32).at[0, nh:2 * nh].set(a).at[1, nh:2 * nh].set(dtb)
    pca = jnp.zeros((2 * nh, LANES), F32).at[nh:].set(jnp.broadcast_to(a[:, None], (nh, LANES)))
    pcd = jnp.zeros((2 * nh, LANES), F32).at[nh:].set(jnp.broadcast_to(dtb[:, None], (nh, LANES)))
    return pr, pca, pcd


def _outproj_ab_kernel(od_ref, of_ref, ob_ref, gate_ref, nw_ref, w_ref, x_ref, g_ref, o_ref, h_ref):
    @pl.when(pl.program_id(1) == 0)
    def _():
        nd = od_ref.shape[1]
        h_ref[:, :nd] = od_ref[...]
        for h in range(DN_HEADS):
            sl = slice(h * DN_DV, (h + 1) * DN_DV)
            y = of_ref[:, sl] + ob_ref[:, sl]
            y = y * lax.rsqrt(jnp.mean(y * y, axis=-1, keepdims=True) + EPS) * nw_ref[...]
            h_ref[:, nd + h * DN_DV:nd + (h + 1) * DN_DV] = (y * _silu(gate_ref[:, sl].astype(F32))).astype(BF16)

    y = jnp.dot(h_ref[...], w_ref[...], preferred_element_type=F32)
    o_ref[...] = x_ref[...] + g_ref[0] * y


def _outproj_ab(od, o_f, o_b, z, gate_col0, out_norm, w, x, gate, rows_per_group):
    m, d = x.shape
    nd, nn = od.shape[1], o_f.shape[1]
    tm = math.gcd(_tile(m, 512, SUBLANES), rows_per_group)
    tn = _tile(d, 1024, LANES)
    grp = lambda i, j: ((i * tm) // rows_per_group, 0, j)
    gblk = gate_col0 // nn
    return pl.pallas_call(
        _outproj_ab_kernel,
        grid=(m // tm, d // tn),
        in_specs=[pl.BlockSpec((tm, nd), lambda i, j: (i, 0)),
                  pl.BlockSpec((tm, nn), lambda i, j: (i, 0)),
                  pl.BlockSpec((tm, nn), lambda i, j: (i, 0)),
                  pl.BlockSpec((tm, nn), lambda i, j: (i, gblk)),
                  pl.BlockSpec((1, DN_DV), lambda i, j: (0, 0)),
                  pl.BlockSpec((nd + nn, tn), lambda i, j: (0, j)),
                  pl.BlockSpec((tm, tn), lambda i, j: (i, j)),
                  pl.BlockSpec((1, 1, tn), grp)],
        out_specs=pl.BlockSpec((tm, tn), lambda i, j: (i, j)),
        out_shape=jax.ShapeDtypeStruct((m, d), F32),
        scratch_shapes=[pltpu.VMEM((tm, nd + nn), BF16)],
        compiler_params=_params("parallel", "arbitrary"),
        name="outproj_ab",
    )(od, o_f, o_b, z, out_norm.reshape(1, DN_DV), w, x, gate)


def _outproj_kernel(a_ref, w_ref, x_ref, g_ref, o_ref):
    o_ref[...] = x_ref[...] + g_ref[0] * jnp.dot(a_ref[...], w_ref[...], preferred_element_type=F32)


def _outproj(a, w, x, gate, rows_per_group):
    m, d = x.shape
    kdim = a.shape[1]
    tm = math.gcd(_tile(m, 512, SUBLANES), rows_per_group)
    tn = _tile(d, 1024, LANES)
    grp = lambda i, j: ((i * tm) // rows_per_group, 0, j)
    return pl.pallas_call(
        _outproj_kernel,
        grid=(m // tm, d // tn),
        in_specs=[pl.BlockSpec((tm, kdim), lambda i, j: (i, 0)),
                  pl.BlockSpec((kdim, tn), lambda i, j: (0, j)),
                  pl.BlockSpec((tm, tn), lambda i, j: (i, j)),
                  pl.BlockSpec((1, 1, tn), grp)],
        out_specs=pl.BlockSpec((tm, tn), lambda i, j: (i, j)),
        out_shape=jax.ShapeDtypeStruct((m, d), F32),
        compiler_params=_params("parallel", "arbitrary"),
        name="outproj",
    )(a, w, x, gate)


def _swa_kernel(sink_ref, bias_ref, q_ref, k0_ref, k1_ref, k2_ref, v0_ref, v1_ref, v2_ref, kc_ref, vc_ref,
                o_ref):
    kvh = pl.program_id(1)
    qb = Q_BLOCK
    npair = SWA_GROUP // 2
    lane = lax.broadcasted_iota(I32, (qb, LANES), 1)
    q = q_ref[...]
    parts = []
    for p in range(npair):
        qp = q[:, p * LANES:(p + 1) * LANES]
        parts.append(jnp.where(lane < SWA_DH, qp, jnp.zeros_like(qp)))
        parts.append(jnp.where(lane >= SWA_DH, qp, jnp.zeros_like(qp)))
    qq = jnp.concatenate(parts, axis=0)
    k_lat = jnp.concatenate([k0_ref[...], k1_ref[...], k2_ref[...]], axis=0)
    v_lat = jnp.concatenate([v0_ref[...], v1_ref[...], v2_ref[...]], axis=0)
    s_lat = lax.dot_general(qq, k_lat, NT_DIMS, preferred_element_type=F32)
    s_ctx = lax.dot_general(qq, kc_ref[...], NT_DIMS, preferred_element_type=F32)
    bias = bias_ref[0]
    vc = vc_ref[...]
    outs = []
    for g in range(SWA_GROUP):
        rs = slice(g * qb, (g + 1) * qb)
        sl = s_lat[rs] + bias
        sc = s_ctx[rs]
        sink = sink_ref[kvh, g] * LOG2E
        m = jnp.maximum(jnp.maximum(jnp.max(sl, axis=-1, keepdims=True),
                                    jnp.max(sc, axis=-1, keepdims=True)), sink)
        el = jnp.exp2(sl - m)
        ec = jnp.exp2(sc - m)
        l = jnp.sum(el, axis=-1, keepdims=True) + jnp.sum(ec, axis=-1, keepdims=True) + jnp.exp2(sink - m)
        o = (jnp.dot(el.astype(BF16), v_lat, preferred_element_type=F32)
             + jnp.dot(ec.astype(BF16), vc, preferred_element_type=F32))
        outs.append(o * (1.0 / l))
    for p in range(npair):
        o_ref[:, p * LANES:(p + 1) * LANES] = jnp.where(lane < SWA_DH, outs[2 * p], outs[2 * p + 1]).astype(o_ref.dtype)


def _swa_attention(sink, q, kx, vx, kc, vc, b, t, n_c):
    qb = Q_BLOCK
    nb = t // qb
    gw = SWA_GROUP * SWA_DH
    lat = lambda off: pl.BlockSpec(
        (qb, LANES), lambda bi, h, i: (bi * nb + jnp.clip(i + off, 0, nb - 1), h))
    ctx = pl.BlockSpec((n_c, LANES), lambda bi, h, i: (bi, h))
    r_io = np.arange(qb)[:, None]
    c_io = np.arange(3 * qb)[None, :]
    inside = np.abs(r_io + qb - c_io) <= WINDOW
    variants = [inside & ((c_io >= qb) | (v & 1 == 0)) & ((c_io < 2 * qb) | (v & 2 == 0)) for v in range(4)]
    bias = jnp.asarray(np.where(np.stack(variants), 0.0, -np.inf), F32)
    return pl.pallas_call(
        _swa_kernel,
        grid=(b, SWA_KV_HEADS, nb),
        in_specs=[pl.BlockSpec(memory_space=pltpu.SMEM),
                  pl.BlockSpec((1, qb, 3 * qb),
                               lambda bi, h, i: ((i == 0).astype(I32) + 2 * (i == nb - 1).astype(I32), 0, 0)),
                  pl.BlockSpec((qb, gw), lambda bi, h, i: (bi * nb + i, h)),
                  lat(-1), lat(0), lat(1), lat(-1), lat(0), lat(1), ctx, ctx],
        out_specs=pl.BlockSpec((qb, gw), lambda bi, h, i: (bi * nb + i, h)),
        out_shape=jax.ShapeDtypeStruct((b * t, SWA_HEADS * SWA_DH), BF16),
        compiler_params=_params("parallel", "parallel", "arbitrary"),
        name="swa_attention",
    )(sink, bias, q, kx, kx, kx, vx, vx, vx, kc, vc)


def _dup_heads(a, col0):
    m = a.shape[0]
    h = a[:, col0:col0 + SWA_KV_HEADS * SWA_DH].reshape(m, SWA_KV_HEADS, 1, SWA_DH)
    return jnp.broadcast_to(h, (m, SWA_KV_HEADS, LANES // SWA_DH, SWA_DH)).reshape(m, SWA_KV_HEADS * LANES)


def _first_max(vals, iota, size, axis):
    m = jnp.max(vals, axis=axis, keepdims=True)
    first = jnp.min(jnp.where(vals == m, iota, size), axis=axis, keepdims=True)
    return m, first


def _router_kernel(x_ref, nw_ref, sh_ref, sc_ref, rwt_ref, rb_ref, c0_ref,
                   hp_ref, idx_ref, wt_ref, rank_ref, cnt_ref, carry_ref):
    @pl.when(pl.program_id(0) == 0)
    def _():
        carry_ref[...] = c0_ref[...]

    h = _norm_mod(x_ref[...], nw_ref[...], sh_ref[0], sc_ref[0])
    tm, d = h.shape
    _store_token_tiles(hp_ref, 0, _pack_pairs(h[:, :d // 2], h[:, d // 2:]))
    scores = jax.nn.sigmoid(_dot3(rwt_ref[...], h, NT_DIMS))
    sel = scores + rb_ref[...]
    neg = jnp.float32(-jnp.inf)

    g_io = lax.broadcasted_iota(I32, (GROUP_SIZE, tm), 0)
    gs_rows = []
    for g in range(N_GROUPS):
        sg = sel[g * GROUP_SIZE:(g + 1) * GROUP_SIZE]
        m1, f1 = _first_max(sg, g_io, GROUP_SIZE, 0)
        m2 = jnp.max(jnp.where(g_io == f1, neg, sg), axis=0, keepdims=True)
        gs_rows.append(m1 + m2)
    cur = jnp.concatenate(gs_rows, axis=0)
    n_io = lax.broadcasted_iota(I32, (N_GROUPS, tm), 0)
    gmask = jnp.zeros((N_GROUPS, tm), I32)
    for _ in range(TOPK_GROUPS):
        _, f = _first_max(cur, n_io, N_GROUPS, 0)
        hit = n_io == f
        gmask = jnp.where(hit, 1, gmask)
        cur = jnp.where(hit, neg, cur)
    cur = jnp.concatenate(
        [jnp.where(gmask[g:g + 1] > 0, sel[g * GROUP_SIZE:(g + 1) * GROUP_SIZE], neg) for g in range(N_GROUPS)],
        axis=0)

    e_io = lax.broadcasted_iota(I32, (N_EXPERTS, tm), 0)
    chosen = jnp.zeros((N_EXPERTS, tm), F32)
    idx_rows, w_rows = [], []
    for _ in range(TOP_K):
        _, f = _first_max(cur, e_io, N_EXPERTS, 0)
        hit = e_io == f
        idx_rows.append(f)
        w_rows.append(jnp.sum(jnp.where(hit, scores, 0.0), axis=0, keepdims=True))
        chosen = jnp.where(hit, 1.0, chosen)
        cur = jnp.where(hit, neg, cur)
    idx = jnp.concatenate(idx_rows, axis=0)
    w = jnp.concatenate(w_rows, axis=0)
    idx_ref[...] = idx
    wt_ref[...] = w * (1.0 / jnp.sum(w, axis=0, keepdims=True)) * ROUTED_SCALE

    onehot = chosen.astype(BF16)
    before = (lax.broadcasted_iota(I32, (tm, tm), 0) < lax.broadcasted_iota(I32, (tm, tm), 1)).astype(BF16)
    base = carry_ref[:, 0:1] + jnp.dot(onehot, before, preferred_element_type=F32)
    rank_ref[...] = jnp.concatenate(
        [jnp.sum(jnp.where(e_io == idx_rows[k], base, 0.0), axis=0, keepdims=True) for k in range(TOP_K)],
        axis=0).astype(I32)
    carry_ref[...] = carry_ref[...] + jnp.sum(chosen, axis=1, keepdims=True)
    cnt_ref[...] = carry_ref[...]


def _router(x, nw, sh, sc, rwt, rb, counts0, rows_per_group):
    m, d = x.shape
    tm = math.gcd(_tile(m, 256, LANES), rows_per_group)
    assert (d // 2) % LANES == 0
    tr = d // 2 // LANES
    grp = lambda i: ((i * tm) // rows_per_group, 0, 0)
    tok = lambda rows: pl.BlockSpec((rows, tm), lambda i: (0, i))
    return pl.pallas_call(
        _router_kernel,
        grid=(m // tm,),
        in_specs=[pl.BlockSpec((tm, d), lambda i: (i, 0)),
                  pl.BlockSpec((1, d), lambda i: (0, 0)),
                  pl.BlockSpec((1, 1, d), grp),
                  pl.BlockSpec((1, 1, d), grp),
                  pl.BlockSpec((N_EXPERTS, d), lambda i: (0, 0)),
                  pl.BlockSpec((N_EXPERTS, 1), lambda i: (0, 0)),
                  pl.BlockSpec((N_EXPERTS, LANES), lambda i: (0, 0))],
        out_specs=[pl.BlockSpec((tm * tr, LANES), lambda i: (i, 0)),
                   tok(TOP_K), tok(TOP_K), tok(TOP_K),
                   pl.BlockSpec((N_EXPERTS, LANES), lambda i: (0, 0))],
        out_shape=[jax.ShapeDtypeStruct((m * tr, LANES), U32),
                   jax.ShapeDtypeStruct((TOP_K, m), I32),
                   jax.ShapeDtypeStruct((TOP_K, m), F32),
                   jax.ShapeDtypeStruct((TOP_K, m), I32),
                   jax.ShapeDtypeStruct((N_EXPERTS, LANES), F32)],
        scratch_shapes=[pltpu.VMEM((N_EXPERTS, LANES), F32)],
        compiler_params=_params("arbitrary"),
        name="moe_router",
    )(x, nw.reshape(1, d), sh, sc, rwt, rb.reshape(N_EXPERTS, 1), counts0)


def _dispatch_kernel(nv_ref, dest_ref, hp_ref, xs_ref, zero_ref, sem, *, tr):
    tm = hp_ref.shape[0] // tr
    blk_rows = MOE_BLOCK * tr

    @pl.when(pl.program_id(0) == 0)
    def _():
        zero_ref[...] = jnp.zeros(zero_ref.shape, U32)

        def fill(blk, carry):
            @pl.when(nv_ref[blk] < MOE_BLOCK)
            def _():
                cp = pltpu.make_async_copy(
                    zero_ref, xs_ref.at[pl.ds(pl.multiple_of(blk * blk_rows, blk_rows), blk_rows)], sem)
                cp.start()
                cp.wait()
            return carry

        lax.fori_loop(0, nv_ref.shape[0], fill, 0)

    def copy(t, k):
        return _tile_copy(hp_ref, t * tr, xs_ref, dest_ref[t * TOP_K + k], tr, sem)

    def start(t, carry):
        for k in range(TOP_K):
            copy(t, k).start(priority=k % 2)
        return carry

    def wait(t, carry):
        for k in range(TOP_K):
            copy(t, k).wait()
        return carry

    lax.fori_loop(0, tm, start, 0)
    lax.fori_loop(0, tm, wait, 0)


def _dispatch(block_nv, dest, hp, n_rows, tr):
    m = hp.shape[0] // tr
    tm = _tile(m, 512, LANES)
    return pl.pallas_call(
        functools.partial(_dispatch_kernel, tr=tr),
        grid_spec=pltpu.PrefetchScalarGridSpec(
            num_scalar_prefetch=1,
            grid=(m // tm,),
            in_specs=[pl.BlockSpec((tm * TOP_K,), lambda i, nv: (i,), memory_space=pltpu.SMEM),
                      pl.BlockSpec((tm * tr, LANES), lambda i, nv: (i, 0))],
            out_specs=pl.BlockSpec(memory_space=pl.ANY),
            scratch_shapes=[pltpu.VMEM((MOE_BLOCK * tr, LANES), U32), pltpu.SemaphoreType.DMA(())]),
        out_shape=jax.ShapeDtypeStruct((n_rows * tr, LANES), U32),
        compiler_params=_params("arbitrary"),
        name="moe_dispatch",
    )(block_nv, dest, hp)


def _gffn_kernel(be_ref, nv_ref, xs_ref, w1_ref, w3_ref, w2_ref, ys_ref, w1b_ref, w3b_ref, w2b_ref, *, tr):
    i = pl.program_id(0)
    nv = nv_ref[i]
    prev = be_ref[jnp.maximum(i - 1, 0)]

    @pl.when((i == 0) | (be_ref[i] != prev))
    def _():
        w1b_ref[...] = w1_ref[0, 0].astype(BF16)
        w3b_ref[...] = w3_ref[0, 0].astype(BF16)
        w2b_ref[...] = w2_ref[0, 0].astype(BF16)

    @pl.when(nv > 0)
    def _():
        lo, hi = _unpack_pairs(_load_token_tiles(xs_ref, 0, MOE_BLOCK, tr))
        dh = lo.shape[1]
        lo = lo.astype(BF16)
        hi = hi.astype(BF16)
        d = functools.partial(jnp.dot, preferred_element_type=F32)
        a = d(lo, w1b_ref[:dh, :]) + d(hi, w1b_ref[dh:, :])
        g = d(lo, w3b_ref[:dh, :]) + d(hi, w3b_ref[dh:, :])
        y = d((_silu(a) * g).astype(BF16), w2b_ref[...])
        _store_token_tiles(ys_ref, 0, _pack_pairs(y[:, :dh], y[:, dh:]))

    @pl.when(nv == 0)
    def _():
        ys_ref[...] = jnp.zeros(ys_ref.shape, U32)


def _grouped_ffn(block_e, block_nv, xs, w1, w3, w2, layer, tr):
    _, _, d, f = w1.shape
    blk_rows = MOE_BLOCK * tr
    nb = xs.shape[0] // blk_rows
    return pl.pallas_call(
        functools.partial(_gffn_kernel, tr=tr),
        grid_spec=pltpu.PrefetchScalarGridSpec(
            num_scalar_prefetch=2,
            grid=(nb,),
            in_specs=[pl.BlockSpec((blk_rows, LANES), lambda i, be, nv: (i, 0)),
                      pl.BlockSpec((1, 1, d, f), lambda i, be, nv: (layer, be[i], 0, 0)),
                      pl.BlockSpec((1, 1, d, f), lambda i, be, nv: (layer, be[i], 0, 0)),
                      pl.BlockSpec((1, 1, f, d), lambda i, be, nv: (layer, be[i], 0, 0))],
            out_specs=pl.BlockSpec((blk_rows, LANES), lambda i, be, nv: (i, 0)),
            scratch_shapes=[pltpu.VMEM((d, f), BF16), pltpu.VMEM((d, f), BF16), pltpu.VMEM((f, d), BF16)]),
        out_shape=jax.ShapeDtypeStruct(xs.shape, U32),
        compiler_params=_params("arbitrary"),
        name="moe_grouped_ffn",
    )(block_e, block_nv, xs, w1, w3, w2)


def _combine_kernel(dest_ref, dest_next_ref, x_ref, hp_ref, wt_ref, g_ref, ws1_ref, ws3_ref, ws2_ref, ys_ref,
                    o_ref, buf_a, buf_b, sem_a, sem_b, *, tr):
    i = pl.program_id(0)
    tm = x_ref.shape[0]
    dh = tr * LANES

    def copy(dref, t, k, buf, sem):
        return _tile_copy(ys_ref, dref[t * TOP_K + k], buf, (k * tm + t) * tr, tr, sem)

    def wait_all(buf, sem):
        def wait(t, carry):
            for k in range(TOP_K):
                copy(dest_ref, t, k, buf, sem).wait()
            return carry
        lax.fori_loop(0, tm, wait, 0)

    @pl.when(i == 0)
    def _():
        def start(t, carry):
            for k in range(TOP_K):
                copy(dest_ref, t, k, buf_a, sem_a).start(priority=k % 2)
            return carry
        lax.fori_loop(0, tm, start, 0)

    def step(buf, sem, buf_next, sem_next):
        wait_all(buf, sem)
        for t in range(tm):
            for k in range(TOP_K):
                copy(dest_next_ref, t, k, buf_next, sem_next).start(priority=k % 2)
        lo, hi = _unpack_pairs(_load_token_tiles(hp_ref, 0, tm, tr))
        lo = lo.astype(BF16)
        hi = hi.astype(BF16)
        d = functools.partial(jnp.dot, preferred_element_type=F32)
        a = d(lo, ws1_ref[:dh, :]) + d(hi, ws1_ref[dh:, :])
        g = d(lo, ws3_ref[:dh, :]) + d(hi, ws3_ref[dh:, :])
        shared = d((_silu(a) * g).astype(BF16), ws2_ref[...])
        acc_lo = jnp.zeros((tm, dh), F32)
        acc_hi = jnp.zeros((tm, dh), F32)
        for k in range(TOP_K):
            ylo, yhi = _unpack_pairs(_load_token_tiles(buf, k * tm * tr, tm, tr))
            wk = wt_ref[:, k:k + 1]
            acc_lo = acc_lo + wk * ylo
            acc_hi = acc_hi + wk * yhi
        o_ref[:, :dh] = x_ref[:, :dh] + g_ref[0][:, :dh] * (acc_lo + shared[:, :dh])
        o_ref[:, dh:] = x_ref[:, dh:] + g_ref[0][:, dh:] * (acc_hi + shared[:, dh:])

        @pl.when(i == pl.num_programs(0) - 1)
        def _():
            wait_all(buf_next, sem_next)

    @pl.when(i % 2 == 0)
    def _():
        step(buf_a, sem_a, buf_b, sem_b)

    @pl.when(i % 2 == 1)
    def _():
        step(buf_b, sem_b, buf_a, sem_a)


def _combine(dest, x, hp, wt, gate, ws1, ws3, ws2, ys, rows_per_group, tr):
    m, d = x.shape
    f = ws1.shape[1]
    tm = math.gcd(_tile(m, 128, LANES), rows_per_group)
    nt = m // tm
    grp = lambda i: ((i * tm) // rows_per_group, 0, 0)
    buf = pltpu.VMEM((TOP_K * tm * tr, LANES), U32)
    return pl.pallas_call(
        functools.partial(_combine_kernel, tr=tr),
        grid=(nt,),
        in_specs=[pl.BlockSpec((tm * TOP_K,), lambda i: (i,), memory_space=pltpu.SMEM),
                  pl.BlockSpec((tm * TOP_K,), lambda i: (jnp.minimum(i + 1, nt - 1),), memory_space=pltpu.SMEM),
                  pl.BlockSpec((tm, d), lambda i: (i, 0)),
                  pl.BlockSpec((tm * tr, LANES), lambda i: (i, 0)),
                  pl.BlockSpec((tm, TOP_K), lambda i: (i, 0)),
                  pl.BlockSpec((1, 1, d), grp),
                  pl.BlockSpec((d, f), lambda i: (0, 0)),
                  pl.BlockSpec((d, f), lambda i: (0, 0)),
                  pl.BlockSpec((f, d), lambda i: (0, 0)),
                  pl.BlockSpec(memory_space=pl.ANY)],
        out_specs=pl.BlockSpec((tm, d), lambda i: (i, 0)),
        out_shape=jax.ShapeDtypeStruct((m, d), F32),
        scratch_shapes=[buf, buf, pltpu.SemaphoreType.DMA(()), pltpu.SemaphoreType.DMA(())],
        compiler_params=_params("arbitrary"),
        name="moe_combine",
    )(dest, dest, x, hp, wt, gate, ws1, ws3, ws2, ys)


def _moe(streams, layer, nw, rw, rb, w1, w3, w2, ws1, ws3, ws2):
    d = streams[0][0].shape[1]
    rwt = rw.T
    counts = jnp.zeros((N_EXPERTS, LANES), F32)
    routed = []
    for x, sh, sc, _, rpg in streams:
        hp, idx, wt, rank, counts = _router(x, nw, sh, sc, rwt, rb, counts, rpg)
        routed.append((hp, idx, wt, rank))
    n_assign = sum(s[0].shape[0] for s in streams) * TOP_K
    n_blocks = (n_assign + N_EXPERTS * (MOE_BLOCK - 1) + MOE_BLOCK - 1) // MOE_BLOCK
    cnt = counts[:, 0].astype(I32)
    padded = (cnt + MOE_BLOCK - 1) // MOE_BLOCK * MOE_BLOCK
    pad_end = jnp.cumsum(padded)
    pad_start = pad_end - padded
    bstart = jnp.arange(n_blocks, dtype=I32) * MOE_BLOCK
    block_e = jnp.minimum(jnp.sum((bstart[:, None] >= pad_end[None, :]).astype(I32), axis=1), N_EXPERTS - 1)
    block_nv = jnp.clip(cnt[block_e] - (bstart - pad_start[block_e]), 0, MOE_BLOCK).astype(I32)
    e_ar = jnp.arange(N_EXPERTS, dtype=I32)
    tr = d // 2 // LANES
    dests = [((jnp.sum(jnp.where(idx[:, :, None] == e_ar, pad_start, 0), axis=-1) + rank) * tr).T.reshape(-1)
             for _, idx, _, rank in routed]
    if len(streams) > 1:
        hp_all = jnp.concatenate([r[0] for r in routed], axis=0)
        dest_all = jnp.concatenate(dests, axis=0)
    else:
        hp_all, dest_all = routed[0][0], dests[0]
    xs = _dispatch(block_nv, dest_all, hp_all, n_blocks * MOE_BLOCK, tr)
    ys = _grouped_ffn(block_e, block_nv, xs, w1, w3, w2, layer, tr)
    ws1b, ws3b, ws2b = ws1.astype(BF16), ws3.astype(BF16), ws2.astype(BF16)
    return [_combine(dest, x, hp, wt.T, gate, ws1b, ws3b, ws2b, ys, rpg, tr)
            for (x, _, _, gate, rpg), (hp, _, wt, _), dest in zip(streams, routed, dests)]


def _mixer_ab(xs, cs, mx, mc, b, t, n_c, layer, nw, w_in, w_out, q_norm, k_norm, lam_vec, subln, conv_w,
              a_log, dt_bias, out_norm):
    n_main = w_in.shape[1] - 4 * DN_HEADS
    assert n_main % LANES == 0
    w_main = w_in[:, :n_main].astype(BF16)
    w_small = jnp.zeros((w_in.shape[0], LANES), F32).at[:, :4 * DN_HEADS].set(w_in[:, n_main:])
    nqk = 2 * DIFF_HEADS * DIFF_DK
    assert n_main % nqk == 0
    tabs = _rope_tables(t, DIFF_DK)
    hw = jnp.stack([q_norm * (DIFF_DK ** -0.5 * LOG2E), k_norm])
    groups = nqk // LANES
    blocks = ((0, 1, (0,) * groups), (1, 2, (1,) * groups))
    z_x, zs_x = _norm_mod_matmul(xs, nw, mx[0], mx[1], w_main, t, w_small, tn=nqk,
                                 heads=dict(hd=DIFF_DK, weights=hw, blocks=blocks, tables=tabs, t=t))
    z_c, zs_c = _norm_mod_matmul(cs, nw, mc[0], mc[1], w_main, b * n_c, w_small, tn=nqk,
                                 heads=dict(hd=DIFF_DK, weights=hw, blocks=blocks, tables=None, t=t))
    lam_init = 0.8 - 0.6 * math.exp(-0.3 * layer)
    od_x = _diff_attention(lam_vec, z_x, z_c, z_c, subln, lam_init, b, n_c, kx=z_x, zx=z_x, t=t)
    od_c = _diff_attention(lam_vec, z_c, z_c, z_c, subln, lam_init, b, n_c)
    dn_col0 = 2 * nqk + DIFF_HEADS * DIFF_DV
    dn_x = _dn_prep(z_x, dn_col0, conv_w, t)
    dn_c = _dn_prep(z_c, dn_col0, conv_w, n_c)
    zst_x = zs_x[:, :4 * DN_HEADS].T
    zst_c = zs_c[:, :4 * DN_HEADS].T
    prm = _deltanet_params(a_log, dt_bias)
    s0 = jnp.zeros((b, DN_HEADS, DN_DK, DN_DV), F32)
    o_cf, s_cf = _deltanet(dn_c, zs_c, zst_c, prm, s0, b, n_c, False)
    o_xf, _ = _deltanet(dn_x, zs_x, zst_x, prm, s_cf, b, t, False)
    o_cb, s_cb = _deltanet(dn_c, zs_c, zst_c, prm, s0, b, n_c, True)
    o_xb, _ = _deltanet(dn_x, zs_x, zst_x, prm, s_cb, b, t, True)
    gate_col0 = dn_col0 + conv_w.shape[1]
    w_out_b = w_out.astype(BF16)
    x1 = _outproj_ab(od_x, o_xf, o_xb, z_x, gate_col0, out_norm, w_out_b, xs, mx[2], t)
    c1 = _outproj_ab(od_c, o_cf, o_cb, z_c, gate_col0, out_norm, w_out_b, cs, mc[2], b * n_c)
    return x1, c1


def _mixer_swa(xs, cs, mx, mc, b, t, n_c, nw, w_in, w_out, q_norm, k_norm, sink):
    nq = SWA_HEADS * SWA_DH
    nkv = SWA_KV_HEADS * SWA_DH
    w_b = w_in.astype(BF16)
    tn = 2 * nkv
    assert nq % tn == 0
    tabs = _rope_tables(t, SWA_DH)
    rep = LANES // SWA_DH
    hw = jnp.stack([jnp.tile(q_norm, rep) * (SWA_DH ** -0.5 * LOG2E), jnp.tile(k_norm, rep)])
    q_rows = (0,) * (tn // LANES)
    kv_rows = (1,) * (nkv // LANES) + (None,) * (nkv // LANES)
    z_x = _norm_mod_matmul(xs, nw, mx[0], mx[1], w_b, t, tn=tn,
                           heads=dict(hd=SWA_DH, weights=hw, tables=tabs, t=t,
                                      blocks=((0, nq // tn, q_rows), (nq // tn, nq // tn + 1, kv_rows))))
    z_c = _norm_mod_matmul(cs, nw, mc[0], mc[1], w_b[:, nq:], b * n_c, tn=tn,
                           heads=dict(hd=SWA_DH, weights=hw, tables=None, t=t, blocks=((0, 1, kv_rows),)))
    att = _swa_attention(sink.reshape(SWA_KV_HEADS, SWA_GROUP), z_x, _dup_heads(z_x, nq), _dup_heads(z_x, nq + nkv),
                         _dup_heads(z_c, 0), _dup_heads(z_c, nkv), b, t, n_c)
    return _outproj(att, w_out.astype(BF16), xs, mx[2], t)


def kernel(x, c, ctx, c_ctx, mod_w, mod_b, norm_mix, norm_ffn, ab_w_in, ab_w_out, diff_q_norm, diff_k_norm, diff_lambda, diff_subln, dn_conv, dn_a_log, dn_dt_bias, dn_out_norm, swa_w_in, swa_w_out, swa_q_norm, swa_k_norm, swa_sink, router_w, router_bias, exp_w1, exp_w3, exp_w2, shared_w1, shared_w3, shared_w2):
    b, t, d = x.shape
    n_c = ctx.shape[1]
    depth = mod_w.shape[0]
    assert depth == 2 and t % Q_BLOCK == 0 and t % DN_CHUNK == 0 and n_c % DN_CHUNK == 0
    xs = x.reshape(b * t, d)
    cs = ctx.reshape(b * n_c, d)
    n_mod = -(-(b + 1) // SUBLANES) * SUBLANES
    a_mod = jnp.zeros((n_mod, d), F32).at[0].set(c_ctx).at[1:1 + b].set(c)
    for layer in range(depth):
        with_ctx = layer < depth - 1
        p = layer // 2
        mod = _modulation(a_mod, mod_w, layer, mod_b[layer])
        mc = [mod[0:1, j * d:(j + 1) * d].reshape(1, 1, d) for j in range(6)]
        mx = [mod[1:1 + b, j * d:(j + 1) * d].reshape(b, 1, d) for j in range(6)]
        if layer % 2 == 0:
            xs, c_new = _mixer_ab(xs, cs, mx, mc, b, t, n_c, layer, norm_mix[layer], ab_w_in[p], ab_w_out[p],
                                  diff_q_norm[p], diff_k_norm[p], diff_lambda[p], diff_subln[p], dn_conv[p],
                                  dn_a_log[p], dn_dt_bias[p], dn_out_norm[p])
        else:
            assert not with_ctx
            xs = _mixer_swa(xs, cs, mx, mc, b, t, n_c, norm_mix[layer], swa_w_in[p], swa_w_out[p],
                            swa_q_norm[p], swa_k_norm[p], swa_sink[p])
            c_new = None
        moe_w = (layer, norm_ffn[layer], router_w[layer], router_bias[layer], exp_w1, exp_w3, exp_w2,
                 shared_w1[layer], shared_w3[layer], shared_w2[layer])
        if with_ctx:
            cs, xs = _moe([(c_new, mc[3], mc[4], mc[5], b * n_c), (xs, mx[3], mx[4], mx[5], t)], *moe_w)
        else:
            (xs,) = _moe([(xs, mx[3], mx[4], mx[5], t)], *moe_w)
    return xs.reshape(b, t, d)
```

```python
import functools
import math

import jax
import jax.numpy as jnp
import numpy as np
from jax import lax
from jax.experimental import pallas as pl
from jax.experimental.pallas import tpu as pltpu

F32 = jnp.float32
BF16 = jnp.bfloat16
I32 = jnp.int32
U32 = jnp.uint32

EPS = 1e-6
GRID_W = 64
ROPE_BASE = 10000.0
DIFF_HEADS = 4
DIFF_DK = 128
DIFF_DV = 256
DN_HEADS = 8
DN_DK = 128
DN_DV = 128
DN_CONV = 5
DN_CHUNK = 128
SWA_HEADS = 32
SWA_KV_HEADS = 4
SWA_GROUP = SWA_HEADS // SWA_KV_HEADS
SWA_DH = 64
WINDOW = 128
Q_BLOCK = 128
N_EXPERTS = 64
TOP_K = 8
N_GROUPS = 8
GROUP_SIZE = N_EXPERTS // N_GROUPS
TOPK_GROUPS = 4
ROUTED_SCALE = 2.5
MOE_BLOCK = 512

LANES = 128
SUBLANES = 8
VMEM_LIMIT_BYTES = 56 * 1024 * 1024

NT_DIMS = (((1,), (1,)), ((), ()))
LOG2E = math.log2(math.e)


def _params(*semantics):
    return pltpu.CompilerParams(dimension_semantics=semantics, vmem_limit_bytes=VMEM_LIMIT_BYTES)


def _tile(n, pref, mult):
    if n <= pref:
        return n
    t = pref - pref % mult
    while t > mult and n % t:
        t -= mult
    assert n % t == 0, (n, pref, mult)
    return t


def _mm(a, b):
    return jnp.dot(a.astype(BF16), b.astype(BF16), preferred_element_type=F32)


def _split2(x):
    hi = x.astype(BF16)
    lo = (x - hi.astype(F32)).astype(BF16)
    return hi, lo


def _dot3(a, b, dims=None):
    if dims is None:
        dims = (((a.ndim - 1,), (0,)), ((), ()))
    ah, al = _split2(a)
    bh, bl = _split2(b)
    d = functools.partial(lax.dot_general, dimension_numbers=dims, preferred_element_type=F32)
    return d(ah, bh) + d(ah, bl) + d(al, bh)


def _silu(x):
    return x * jax.nn.sigmoid(x)


def _softplus(x):
    return jnp.maximum(x, 0.0) + jnp.log(1.0 + jnp.exp(-jnp.abs(x)))


def _pack_pairs(lo, hi):
    ulo = lax.bitcast_convert_type(lo.astype(BF16).astype(F32), U32) >> 16
    uhi = lax.bitcast_convert_type(hi.astype(BF16).astype(F32), U32) & jnp.uint32(0xFFFF0000)
    return ulo | uhi


def _unpack_pairs(u):
    lo = lax.bitcast_convert_type(u << 16, F32)
    hi = lax.bitcast_convert_type(u & jnp.uint32(0xFFFF0000), F32)
    return lo, hi


def _store_token_tiles(ref, base, packed):
    n, width = packed.shape
    tr = width // LANES
    for j in range(tr):
        ref[pl.ds(base + j, n, stride=tr), :] = packed[:, j * LANES:(j + 1) * LANES]


def _load_token_tiles(ref, base, n, tr):
    return jnp.concatenate([ref[pl.ds(base + j, n, stride=tr), :] for j in range(tr)], axis=1)


def _tile_copy(src, src_row, dst, dst_row, tr, sem):
    return pltpu.make_async_copy(src.at[pl.ds(pl.multiple_of(src_row, tr), tr)],
                                 dst.at[pl.ds(pl.multiple_of(dst_row, tr), tr)], sem)


def _mod_kernel(a_ref, w_ref, b_ref, o_ref):
    o_ref[...] = _dot3(_silu(a_ref[...]), w_ref[0]) + b_ref[...]


def _modulation(a, w_all, layer, b):
    r, d = a.shape
    n = w_all.shape[2]
    tn = _tile(n, 768, LANES)
    return pl.pallas_call(
        _mod_kernel,
        grid=(n // tn,),
        in_specs=[pl.BlockSpec((r, d), lambda j: (0, 0)),
                  pl.BlockSpec((1, d, tn), lambda j: (layer, 0, j)),
                  pl.BlockSpec((1, tn), lambda j: (0, j))],
        out_specs=pl.BlockSpec((r, tn), lambda j: (0, j)),
        out_shape=jax.ShapeDtypeStruct((r, n), F32),
        compiler_params=_params("parallel"),
        name="modulation",
    )(a, w_all, b.reshape(1, n))


def _norm_mod(x, nw, sh, sc):
    y = x * lax.rsqrt(jnp.mean(x * x, axis=-1, keepdims=True) + EPS) * nw
    return y * (1.0 + sc) + sh


def _head_norm_rope(x, w, seg_ones, swap, cos, sin, hd):
    ms = jnp.dot((x * x).astype(BF16), seg_ones, preferred_element_type=F32) * (1.0 / hd)
    y = x * lax.rsqrt(ms + EPS) * w
    if cos is not None:
        partner = jnp.dot(y.astype(BF16), swap, preferred_element_type=F32)
        y = y * cos + partner * sin
    return y


def _head_matrices(hd):
    lane = np.arange(LANES)
    seg = (lane[:, None] // hd) == (lane[None, :] // hd)
    q = hd // 4
    partner = np.where((lane % (2 * q)) < q, lane + q, lane - q)
    swap = lane[:, None] == partner[None, :]
    return jnp.asarray(seg, BF16), jnp.asarray(swap, BF16)


def _nmm_kernel(x_ref, nw_ref, sh_ref, sc_ref, w_ref, *rest, has_small, head_blocks, hd, rope):
    rest = list(rest)
    ws_ref = rest.pop(0) if has_small else None
    hw_ref, seg_ref, swap_ref = (rest.pop(0), rest.pop(0), rest.pop(0)) if head_blocks else (None, None, None)
    cos_ref, sin_ref = (rest.pop(0), rest.pop(0)) if rope else (None, None)
    o_ref = rest.pop(0)
    os_ref = rest.pop(0) if has_small else None
    (h_ref,) = rest
    j = pl.program_id(1)

    @pl.when(j == 0)
    def _():
        h = _norm_mod(x_ref[...], nw_ref[...], sh_ref[0], sc_ref[0])
        h_ref[...] = h.astype(BF16)
        if has_small:
            os_ref[...] = _dot3(h, ws_ref[...])

    r = jnp.dot(h_ref[...], w_ref[...], preferred_element_type=F32)
    if not head_blocks:
        o_ref[...] = r.astype(o_ref.dtype)
        return
    tm, tn = r.shape
    plain = j >= 0
    for lo, hi, rows in head_blocks:
        inside = (j >= lo) & (j < hi)
        plain = plain & jnp.logical_not(inside)

        @pl.when(inside)
        def _(rows=rows):
            cos = cos_ref[...] if rope else None
            sin = sin_ref[...] if rope else None
            for g in range(tn // LANES):
                sl = slice(g * LANES, (g + 1) * LANES)
                y = r[:, sl]
                if rows[g] is not None:
                    y = _head_norm_rope(y, hw_ref[rows[g]], seg_ref[...], swap_ref[...], cos, sin, hd)
                o_ref[:, sl] = y.astype(o_ref.dtype)

    @pl.when(plain)
    def _():
        o_ref[...] = r.astype(o_ref.dtype)


def _norm_mod_matmul(x, nw, sh, sc, w, rows_per_group, w_small=None, tn=1024, heads=None):
    m, d = x.shape
    n = w.shape[1]
    tm = math.gcd(_tile(m, 512, SUBLANES), rows_per_group)
    tn = _tile(n, tn, LANES)
    has_small = w_small is not None
    grp = lambda i, j: ((i * tm) // rows_per_group, 0, 0)
    in_specs = [pl.BlockSpec((tm, d), lambda i, j: (i, 0)),
                pl.BlockSpec((1, d), lambda i, j: (0, 0)),
                pl.BlockSpec((1, 1, d), grp),
                pl.BlockSpec((1, 1, d), grp),
                pl.BlockSpec((d, tn), lambda i, j: (0, j))]
    args = [x, nw.reshape(1, d), sh, sc, w]
    out_specs = [pl.BlockSpec((tm, tn), lambda i, j: (i, j))]
    out_shape = [jax.ShapeDtypeStruct((m, n), BF16)]
    if has_small:
        in_specs.append(pl.BlockSpec((d, LANES), lambda i, j: (0, 0)))
        args.append(w_small)
        out_specs.append(pl.BlockSpec((tm, LANES), lambda i, j: (i, 0)))
        out_shape.append(jax.ShapeDtypeStruct((m, LANES), F32))
    head_blocks, hd, rope = (), LANES, False
    if heads is not None:
        head_blocks, hd = tuple(heads["blocks"]), heads["hd"]
        hw = heads["weights"]
        in_specs += [pl.BlockSpec((hw.shape[0], 1, LANES), lambda i, j: (0, 0, 0)),
                     pl.BlockSpec((LANES, LANES), lambda i, j: (0, 0)),
                     pl.BlockSpec((LANES, LANES), lambda i, j: (0, 0))]
        args += [hw.reshape(hw.shape[0], 1, LANES), *_head_matrices(hd)]
        if heads["tables"] is not None:
            rope = True
            nt = heads["t"] // tm
            assert heads["t"] % tm == 0
            in_specs += [pl.BlockSpec((tm, LANES), lambda i, j: (i % nt, 0))] * 2
            args += list(heads["tables"])
    outs = pl.pallas_call(
        functools.partial(_nmm_kernel, has_small=has_small, head_blocks=head_blocks, hd=hd, rope=rope),
        grid=(m // tm, n // tn),
        in_specs=in_specs,
        out_specs=out_specs,
        out_shape=out_shape,
        scratch_shapes=[pltpu.VMEM((tm, d), BF16)],
        compiler_params=_params("parallel", "arbitrary"),
        name="norm_mod_matmul",
    )(*args)
    return outs if has_small else outs[0]


def _rope_tables(t, head_dim):
    q = head_dim // 4
    pos = jnp.arange(t, dtype=I32)
    row = (pos // GRID_W).astype(F32)
    col = (pos % GRID_W).astype(F32)
    axis_dim = head_dim // 2
    inv_freq = ROPE_BASE ** (-jnp.arange(0, axis_dim, 2, dtype=F32) / axis_dim)
    lane = jnp.arange(LANES) % head_dim
    freq = inv_freq[lane % q]
    p = jnp.where((lane < head_dim // 2)[None, :], row[:, None], col[:, None])
    ang = p * freq[None, :]
    sign = jnp.where((lane % (2 * q)) < q, -1.0, 1.0)
    return jnp.cos(ang), jnp.sin(ang) * sign[None, :]


def _diff_attn_kernel(lv_ref, q_ref, *rest, lam_init, has_lat):
    if has_lat:
        kx_ref, vx_ref, kc_ref, vc_ref, w_ref, o_ref = rest
    else:
        kc_ref, vc_ref, w_ref, o_ref = rest
    lv = lv_ref[...]
    lam = (jnp.exp(jnp.sum(lv[0:1] * lv[1:2], keepdims=True))
           - jnp.exp(jnp.sum(lv[2:3] * lv[3:4], keepdims=True)) + lam_init)
    q = q_ref[...]

    def probs(c):
        sl = slice(c * DIFF_DK, (c + 1) * DIFF_DK)
        qc = q[:, sl]
        s_c = lax.dot_general(qc, kc_ref[:, sl], NT_DIMS, preferred_element_type=F32)
        m = jnp.max(s_c, axis=-1, keepdims=True)
        p_x = None
        if has_lat:
            s_x = lax.dot_general(qc, kx_ref[:, sl], NT_DIMS, preferred_element_type=F32)
            m = jnp.maximum(m, jnp.max(s_x, axis=-1, keepdims=True))
            p_x = jnp.exp2(s_x - m)
        p_c = jnp.exp2(s_c - m)
        l = jnp.sum(p_c, axis=-1, keepdims=True)
        if has_lat:
            l = l + jnp.sum(p_x, axis=-1, keepdims=True)
        return p_x, p_c, l

    p1x, p1c, l1 = probs(0)
    p2x, p2c, l2 = probs(1)
    ratio = lam * l1 * (1.0 / l2)
    o = jnp.dot((p1c - p2c * ratio).astype(BF16), vc_ref[...], preferred_element_type=F32)
    if has_lat:
        o = o + jnp.dot((p1x - p2x * ratio).astype(BF16), vx_ref[...], preferred_element_type=F32)
    o = o * (1.0 / l1)
    o = o * lax.rsqrt(jnp.mean(o * o, axis=-1, keepdims=True) + EPS)
    o_ref[...] = (o * w_ref[...] * (1.0 - lam_init)).astype(o_ref.dtype)


def _diff_attention(lam_vec, q, kc, zc, subln, lam_init, b, n_c, kx=None, zx=None, t=None):
    has_lat = kx is not None
    hw = 2 * DIFF_DK
    kblk0 = DIFF_HEADS
    vblk0 = (2 * DIFF_HEADS * hw) // DIFF_DV
    tq_all = t if has_lat else n_c
    tq = _tile(tq_all, 256, SUBLANES)
    nq = tq_all // tq
    in_specs = [pl.BlockSpec((4, DIFF_DK), lambda bi, h, qi: (0, 0)),
                pl.BlockSpec((tq, hw), lambda bi, h, qi: (bi * nq + qi, h))]
    args = [lam_vec, q]
    if has_lat:
        in_specs += [pl.BlockSpec((t, hw), lambda bi, h, qi: (bi, kblk0 + h)),
                     pl.BlockSpec((t, DIFF_DV), lambda bi, h, qi: (bi, vblk0 + h))]
        args += [kx, zx]
    in_specs += [pl.BlockSpec((n_c, hw), lambda bi, h, qi: (bi, kblk0 + h)),
                 pl.BlockSpec((n_c, DIFF_DV), lambda bi, h, qi: (bi, vblk0 + h)),
                 pl.BlockSpec((1, DIFF_DV), lambda bi, h, qi: (0, 0))]
    args += [kc, zc, subln.reshape(1, DIFF_DV)]
    return pl.pallas_call(
        functools.partial(_diff_attn_kernel, lam_init=lam_init, has_lat=has_lat),
        grid=(b, DIFF_HEADS, nq),
        in_specs=in_specs,
        out_specs=pl.BlockSpec((tq, DIFF_DV), lambda bi, h, qi: (bi * nq + qi, h)),
        out_shape=jax.ShapeDtypeStruct((b * tq_all, DIFF_HEADS * DIFF_DV), BF16),
        compiler_params=_params("parallel", "parallel", "arbitrary"),
        name="diff_attention",
    )(*args)


def _dn_prep_kernel(z_ref, cw_ref, o_ref, pad_ref, *, seg):
    halo = SUBLANES
    pad_ref[0:halo, :] = jnp.zeros((halo, LANES), F32)
    pad_ref[halo + seg:2 * halo + seg, :] = jnp.zeros((halo, LANES), F32)
    pad_ref[halo:halo + seg, :] = z_ref[...].astype(F32)
    kind = pl.program_id(1) // DN_HEADS
    qk_scale = jnp.where(kind == 0, DN_DK ** -0.5, 1.0).astype(F32)
    rows = _tile(seg, 256, SUBLANES)
    for r0 in range(0, seg, rows):
        acc = jnp.zeros((rows, LANES), F32)
        for j in range(DN_CONV):
            acc = acc + cw_ref[j:j + 1, :] * pad_ref[pl.ds(halo + r0 + j - DN_CONV // 2, rows), :]
        y = _silu(acc)
        nrm = y * lax.rsqrt(jnp.sum(y * y, axis=-1, keepdims=True) + EPS) * qk_scale
        o_ref[r0:r0 + rows, :] = jnp.where(kind < 2, nrm, y).astype(o_ref.dtype)


def _dn_prep(z, col0, conv_w, seg):
    m = z.shape[0]
    ncols = conv_w.shape[1]
    cblk0 = col0 // LANES
    cw = jnp.zeros((SUBLANES, ncols), F32).at[:DN_CONV].set(conv_w)
    return pl.pallas_call(
        functools.partial(_dn_prep_kernel, seg=seg),
        grid=(m // seg, ncols // LANES),
        in_specs=[pl.BlockSpec((seg, LANES), lambda s, g: (s, cblk0 + g)),
                  pl.BlockSpec((SUBLANES, LANES), lambda s, g: (0, g))],
        out_specs=pl.BlockSpec((seg, LANES), lambda s, g: (s, g)),
        out_shape=jax.ShapeDtypeStruct((m, ncols), BF16),
        scratch_shapes=[pltpu.VMEM((seg + 2 * SUBLANES, LANES), F32)],
        compiler_params=_params("parallel", "parallel"),
        name="dn_prep",
    )(z, cw)


def _deltanet_kernel(q_ref, k_ref, v_ref, zs_ref, zst_ref, pr_ref, pca_ref, pcd_ref, s0_ref,
                     o_ref, sout_ref, s_ref, *, reverse, n_chunks):
    step = pl.program_id(1)
    c = DN_CHUNK

    @pl.when(step == 0)
    def _():
        s_ref[...] = s0_ref[0]

    ri = lax.broadcasted_iota(I32, (c, c), 0)
    ci = lax.broadcasted_iota(I32, (c, c), 1)
    if reverse:
        later, strict, later_t = ri <= ci, ri < ci, ri >= ci
    else:
        later, strict, later_t = ri >= ci, ri > ci, ri <= ci
    eye = (ri == ci).astype(F32)
    tri = later.astype(BF16)
    tri_t = later_t.astype(BF16)

    zs = zs_ref[...]
    beta_cols = jax.nn.sigmoid(zs)
    g_cols = -jnp.exp(pr_ref[0:1, :]) * _softplus(zs + pr_ref[1:2, :])
    g_hi = g_cols.astype(BF16)
    g_r1 = g_cols - g_hi.astype(F32)
    g_mid = g_r1.astype(BF16)
    g_lo = (g_r1 - g_mid.astype(F32)).astype(BF16)
    d = functools.partial(jnp.dot, preferred_element_type=F32)
    gc_cols = d(tri, g_hi) + d(tri, g_mid) + d(tri, g_lo)
    g_rows = -jnp.exp(pca_ref[...]) * _softplus(zst_ref[...] + pcd_ref[...])
    h_hi = g_rows.astype(BF16)
    h_r1 = g_rows - h_hi.astype(F32)
    h_mid = h_r1.astype(BF16)
    h_lo = (h_r1 - h_mid.astype(F32)).astype(BF16)
    gc_rows = d(h_hi, tri_t) + d(h_mid, tri_t) + d(h_lo, tri_t)

    dir_off = DN_HEADS if reverse else 0
    last = 0 if reverse else c - 1
    neg_inf = jnp.float32(-jnp.inf)
    heads = []
    for h in range(DN_HEADS):
        sl = slice(h * DN_DK, (h + 1) * DN_DK)
        cb = dir_off + h
        cg = 2 * DN_HEADS + dir_off + h
        beta = beta_cols[:, cb:cb + 1]
        gcol = gc_cols[:, cg:cg + 1]
        grow = gc_rows[cg:cg + 1, :]
        glast = grow[:, last:last + 1]
        q = q_ref[:, sl]
        k = k_ref[:, sl]
        kf = k.astype(F32)
        decay = jnp.exp(jnp.where(later, gcol - grow, neg_inf))
        kb = kf * beta
        both = lax.dot_general(jnp.concatenate([kb.astype(BF16), q], axis=0), k, NT_DIMS,
                               preferred_element_type=F32)
        lmat = jnp.where(strict, both[:c] * decay, 0.0)
        eg = jnp.exp(gcol)
        heads.append(dict(
            sl=sl, glast=glast, lmat=lmat, amat=both[c:] * decay,
            inv=eye - jnp.where((ri >> 1) == (ci >> 1), lmat, 0.0),
            rhs=jnp.concatenate([v_ref[:, sl].astype(F32) * beta, kb * eg], axis=1).astype(BF16),
            qe=(q.astype(F32) * eg).astype(BF16),
            kdec_t=(kf * jnp.exp(glast - gcol)).T.astype(BF16)))
    lev = 1
    while (1 << lev) < c:
        blk = ((ri >> (lev + 1)) == (ci >> (lev + 1))) & ((ri >> lev) != (ci >> lev))
        half = [_mm(hd["inv"], jnp.where(blk, hd["lmat"], 0.0)) for hd in heads]
        for hd, t in zip(heads, half):
            hd["inv"] = hd["inv"] - _mm(t, hd["inv"])
        lev += 1
    uws = [_mm(hd["inv"], hd["rhs"]) for hd in heads]
    states = [s_ref[h] for h in range(DN_HEADS)]
    new_states = []
    for hd, uw, s in zip(heads, uws, states):
        ws_qs = _mm(jnp.concatenate([uw[:, DN_DV:].astype(BF16), hd["qe"]], axis=0), s)
        v_new = uw[:, :DN_DV] - ws_qs[:c]
        o_ref[:, hd["sl"]] = ws_qs[c:] + _mm(hd["amat"], v_new)
        new_states.append(s * jnp.exp(hd["glast"]) + _mm(hd["kdec_t"], v_new))
    for h in range(DN_HEADS):
        s_ref[h] = new_states[h]

    @pl.when(step == n_chunks - 1)
    def _():
        sout_ref[0] = s_ref[...]


def _deltanet(dn, zs, zst, prm, s0, b, seg, reverse):
    pr, pca, pcd = prm
    n = seg // DN_CHUNK
    hw = DN_HEADS * DN_DK
    rb = (lambda bi, s: bi * n + (n - 1 - s)) if reverse else (lambda bi, s: bi * n + s)
    state = pl.BlockSpec((1, DN_HEADS, DN_DK, DN_DV), lambda bi, s: (bi, 0, 0, 0))
    return pl.pallas_call(
        functools.partial(_deltanet_kernel, reverse=reverse, n_chunks=n),
        grid=(b, n),
        in_specs=[pl.BlockSpec((DN_CHUNK, hw), lambda bi, s: (rb(bi, s), 0)),
                  pl.BlockSpec((DN_CHUNK, hw), lambda bi, s: (rb(bi, s), 1)),
                  pl.BlockSpec((DN_CHUNK, hw), lambda bi, s: (rb(bi, s), 2)),
                  pl.BlockSpec((DN_CHUNK, LANES), lambda bi, s: (rb(bi, s), 0)),
                  pl.BlockSpec((4 * DN_HEADS, DN_CHUNK), lambda bi, s: (0, rb(bi, s))),
                  pl.BlockSpec((SUBLANES, LANES), lambda bi, s: (0, 0)),
                  pl.BlockSpec((4 * DN_HEADS, LANES), lambda bi, s: (0, 0)),
                  pl.BlockSpec((4 * DN_HEADS, LANES), lambda bi, s: (0, 0)),
                  state],
        out_specs=[pl.BlockSpec((DN_CHUNK, hw), lambda bi, s: (rb(bi, s), 0)), state],
        out_shape=[jax.ShapeDtypeStruct((b * seg, hw), F32),
                   jax.ShapeDtypeStruct((b, DN_HEADS, DN_DK, DN_DV), F32)],
        scratch_shapes=[pltpu.VMEM((DN_HEADS, DN_DK, DN_DV), F32)],
        compiler_params=_params("parallel", "arbitrary"),
        name="deltanet_bwd" if reverse else "deltanet_fwd",
    )(dn, dn, dn, zs, zst, pr, pca, pcd, s0)


def _deltanet_params(a_log, dt_bias):
    nh = 2 * DN_HEADS
    a = a_log.reshape(nh).astype(F32)
    dtb = dt_bias.reshape(nh).astype(F32)
    pr = jnp.zeros((SUBLANES, LANES), F32).at[0, nh:2 * nh].set(a).at[1, nh:2 * nh].set(dtb)
    pca = jnp.zeros((2 * nh, LANES), F32).at[nh:].set(jnp.broadcast_to(a[:, None], (nh, LANES)))
    pcd = jnp.zeros((2 * nh, LANES), F32).at[nh:].set(jnp.broadcast_to(dtb[:, None], (nh, LANES)))
    return pr, pca, pcd


def _outproj_ab_kernel(od_ref, of_ref, ob_ref, gate_ref, nw_ref, w_ref, x_ref, g_ref, o_ref, h_ref):
    @pl.when(pl.program_id(1) == 0)
    def _():
        nd = od_ref.shape[1]
        h_ref[:, :nd] = od_ref[...]
        for h in range(DN_HEADS):
            sl = slice(h * DN_DV, (h + 1) * DN_DV)
            y = of_ref[:, sl] + ob_ref[:, sl]
            y = y * lax.rsqrt(jnp.mean(y * y, axis=-1, keepdims=True) + EPS) * nw_ref[...]
            h_ref[:, nd + h * DN_DV:nd + (h + 1) * DN_DV] = (y * _silu(gate_ref[:, sl].astype(F32))).astype(BF16)

    y = jnp.dot(h_ref[...], w_ref[...], preferred_element_type=F32)
    o_ref[...] = x_ref[...] + g_ref[0] * y


def _outproj_ab(od, o_f, o_b, z, gate_col0, out_norm, w, x, gate, rows_per_group):
    m, d = x.shape
    nd, nn = od.shape[1], o_f.shape[1]
    tm = math.gcd(_tile(m, 512, SUBLANES), rows_per_group)
    tn = _tile(d, 1024, LANES)
    grp = lambda i, j: ((i * tm) // rows_per_group, 0, j)
    gblk = gate_col0 // nn
    return pl.pallas_call(
        _outproj_ab_kernel,
        grid=(m // tm, d // tn),
        in_specs=[pl.BlockSpec((tm, nd), lambda i, j: (i, 0)),
                  pl.BlockSpec((tm, nn), lambda i, j: (i, 0)),
                  pl.BlockSpec((tm, nn), lambda i, j: (i, 0)),
                  pl.BlockSpec((tm, nn), lambda i, j: (i, gblk)),
                  pl.BlockSpec((1, DN_DV), lambda i, j: (0, 0)),
                  pl.BlockSpec((nd + nn, tn), lambda i, j: (0, j)),
                  pl.BlockSpec((tm, tn), lambda i, j: (i, j)),
                  pl.BlockSpec((1, 1, tn), grp)],
        out_specs=pl.BlockSpec((tm, tn), lambda i, j: (i, j)),
        out_shape=jax.ShapeDtypeStruct((m, d), F32),
        scratch_shapes=[pltpu.VMEM((tm, nd + nn), BF16)],
        compiler_params=_params("parallel", "arbitrary"),
        name="outproj_ab",
    )(od, o_f, o_b, z, out_norm.reshape(1, DN_DV), w, x, gate)


def _outproj_kernel(a_ref, w_ref, x_ref, g_ref, o_ref):
    o_ref[...] = x_ref[...] + g_ref[0] * jnp.dot(a_ref[...], w_ref[...], preferred_element_type=F32)


def _outproj(a, w, x, gate, rows_per_group):
    m, d = x.shape
    kdim = a.shape[1]
    tm = math.gcd(_tile(m, 512, SUBLANES), rows_per_group)
    tn = _tile(d, 1024, LANES)
    grp = lambda i, j: ((i * tm) // rows_per_group, 0, j)
    return pl.pallas_call(
        _outproj_kernel,
        grid=(m // tm, d // tn),
        in_specs=[pl.BlockSpec((tm, kdim), lambda i, j: (i, 0)),
                  pl.BlockSpec((kdim, tn), lambda i, j: (0, j)),
                  pl.BlockSpec((tm, tn), lambda i, j: (i, j)),
                  pl.BlockSpec((1, 1, tn), grp)],
        out_specs=pl.BlockSpec((tm, tn), lambda i, j: (i, j)),
        out_shape=jax.ShapeDtypeStruct((m, d), F32),
        compiler_params=_params("parallel", "arbitrary"),
        name="outproj",
    )(a, w, x, gate)


def _swa_kernel(sink_ref, bias_ref, q_ref, k0_ref, k1_ref, k2_ref, v0_ref, v1_ref, v2_ref, kc_ref, vc_ref,
                o_ref):
    kvh = pl.program_id(1)
    qb = Q_BLOCK
    npair = SWA_GROUP // 2
    lane = lax.broadcasted_iota(I32, (qb, LANES), 1)
    q = q_ref[...]
    parts = []
    for p in range(npair):
        qp = q[:, p * LANES:(p + 1) * LANES]
        parts.append(jnp.where(lane < SWA_DH, qp, jnp.zeros_like(qp)))
        parts.append(jnp.where(lane >= SWA_DH, qp, jnp.zeros_like(qp)))
    qq = jnp.concatenate(parts, axis=0)
    k_lat = jnp.concatenate([k0_ref[...], k1_ref[...], k2_ref[...]], axis=0)
    v_lat = jnp.concatenate([v0_ref[...], v1_ref[...], v2_ref[...]], axis=0)
    s_lat = lax.dot_general(qq, k_lat, NT_DIMS, preferred_element_type=F32)
    s_ctx = lax.dot_general(qq, kc_ref[...], NT_DIMS, preferred_element_type=F32)
    bias = bias_ref[0]
    vc = vc_ref[...]
    outs = []
    for g in range(SWA_GROUP):
        rs = slice(g * qb, (g + 1) * qb)
        sl = s_lat[rs] + bias
        sc = s_ctx[rs]
        sink = sink_ref[kvh, g] * LOG2E
        m = jnp.maximum(jnp.maximum(jnp.max(sl, axis=-1, keepdims=True),
                                    jnp.max(sc, axis=-1, keepdims=True)), sink)
        el = jnp.exp2(sl - m)
        ec = jnp.exp2(sc - m)
        l = jnp.sum(el, axis=-1, keepdims=True) + jnp.sum(ec, axis=-1, keepdims=True) + jnp.exp2(sink - m)
        o = (jnp.dot(el.astype(BF16), v_lat, preferred_element_type=F32)
             + jnp.dot(ec.astype(BF16), vc, preferred_element_type=F32))
        outs.append(o * (1.0 / l))
    for p in range(npair):
        o_ref[:, p * LANES:(p + 1) * LANES] = jnp.where(lane < SWA_DH, outs[2 * p], outs[2 * p + 1]).astype(o_ref.dtype)


def _swa_attention(sink, q, kx, vx, kc, vc, b, t, n_c):
    qb = Q_BLOCK
    nb = t // qb
    gw = SWA_GROUP * SWA_DH
    lat = lambda off: pl.BlockSpec(
        (qb, LANES), lambda bi, h, i: (bi * nb + jnp.clip(i + off, 0, nb - 1), h))
    ctx = pl.BlockSpec((n_c, LANES), lambda bi, h, i: (bi, h))
    r_io = np.arange(qb)[:, None]
    c_io = np.arange(3 * qb)[None, :]
    inside = np.abs(r_io + qb - c_io) <= WINDOW
    variants = [inside & ((c_io >= qb) | (v & 1 == 0)) & ((c_io < 2 * qb) | (v & 2 == 0)) for v in range(4)]
    bias = jnp.asarray(np.where(np.stack(variants), 0.0, -np.inf), F32)
    return pl.pallas_call(
        _swa_kernel,
        grid=(b, SWA_KV_HEADS, nb),
        in_specs=[pl.BlockSpec(memory_space=pltpu.SMEM),
                  pl.BlockSpec((1, qb, 3 * qb),
                               lambda bi, h, i: ((i == 0).astype(I32) + 2 * (i == nb - 1).astype(I32), 0, 0)),
                  pl.BlockSpec((qb, gw), lambda bi, h, i: (bi * nb + i, h)),
                  lat(-1), lat(0), lat(1), lat(-1), lat(0), lat(1), ctx, ctx],
        out_specs=pl.BlockSpec((qb, gw), lambda bi, h, i: (bi * nb + i, h)),
        out_shape=jax.ShapeDtypeStruct((b * t, SWA_HEADS * SWA_DH), BF16),
        compiler_params=_params("parallel", "parallel", "arbitrary"),
        name="swa_attention",
    )(sink, bias, q, kx, kx, kx, vx, vx, vx, kc, vc)


def _dup_heads(a, col0):
    m = a.shape[0]
    h = a[:, col0:col0 + SWA_KV_HEADS * SWA_DH].reshape(m, SWA_KV_HEADS, 1, SWA_DH)
    return jnp.broadcast_to(h, (m, SWA_KV_HEADS, LANES // SWA_DH, SWA_DH)).reshape(m, SWA_KV_HEADS * LANES)


def _first_max(vals, iota, size, axis):
    m = jnp.max(vals, axis=axis, keepdims=True)
    first = jnp.min(jnp.where(vals == m, iota, size), axis=axis, keepdims=True)
    return m, first


def _router_kernel(x_ref, nw_ref, sh_ref, sc_ref, rwt_ref, rb_ref, c0_ref,
                   hp_ref, idx_ref, wt_ref, rank_ref, cnt_ref, carry_ref):
    @pl.when(pl.program_id(0) == 0)
    def _():
        carry_ref[...] = c0_ref[...]

    h = _norm_mod(x_ref[...], nw_ref[...], sh_ref[0], sc_ref[0])
    tm, d = h.shape
    _store_token_tiles(hp_ref, 0, _pack_pairs(h[:, :d // 2], h[:, d // 2:]))
    scores = jax.nn.sigmoid(_dot3(rwt_ref[...], h, NT_DIMS))
    sel = scores + rb_ref[...]
    neg = jnp.float32(-jnp.inf)

    g_io = lax.broadcasted_iota(I32, (GROUP_SIZE, tm), 0)
    gs_rows = []
    for g in range(N_GROUPS):
        sg = sel[g * GROUP_SIZE:(g + 1) * GROUP_SIZE]
        m1, f1 = _first_max(sg, g_io, GROUP_SIZE, 0)
        m2 = jnp.max(jnp.where(g_io == f1, neg, sg), axis=0, keepdims=True)
        gs_rows.append(m1 + m2)
    cur = jnp.concatenate(gs_rows, axis=0)
    n_io = lax.broadcasted_iota(I32, (N_GROUPS, tm), 0)
    gmask = jnp.zeros((N_GROUPS, tm), I32)
    for _ in range(TOPK_GROUPS):
        _, f = _first_max(cur, n_io, N_GROUPS, 0)
        hit = n_io == f
        gmask = jnp.where(hit, 1, gmask)
        cur = jnp.where(hit, neg, cur)
    cur = jnp.concatenate(
        [jnp.where(gmask[g:g + 1] > 0, sel[g * GROUP_SIZE:(g + 1) * GROUP_SIZE], neg) for g in range(N_GROUPS)],
        axis=0)

    e_io = lax.broadcasted_iota(I32, (N_EXPERTS, tm), 0)
    chosen = jnp.zeros((N_EXPERTS, tm), F32)
    idx_rows, w_rows = [], []
    for _ in range(TOP_K):
        _, f = _first_max(cur, e_io, N_EXPERTS, 0)
        hit = e_io == f
        idx_rows.append(f)
        w_rows.append(jnp.sum(jnp.where(hit, scores, 0.0), axis=0, keepdims=True))
        chosen = jnp.where(hit, 1.0, chosen)
        cur = jnp.where(hit, neg, cur)
    idx = jnp.concatenate(idx_rows, axis=0)
    w = jnp.concatenate(w_rows, axis=0)
    idx_ref[...] = idx
    wt_ref[...] = w * (1.0 / jnp.sum(w, axis=0, keepdims=True)) * ROUTED_SCALE

    onehot = chosen.astype(BF16)
    before = (lax.broadcasted_iota(I32, (tm, tm), 0) < lax.broadcasted_iota(I32, (tm, tm), 1)).astype(BF16)
    base = carry_ref[:, 0:1] + jnp.dot(onehot, before, preferred_element_type=F32)
    rank_ref[...] = jnp.concatenate(
        [jnp.sum(jnp.where(e_io == idx_rows[k], base, 0.0), axis=0, keepdims=True) for k in range(TOP_K)],
        axis=0).astype(I32)
    carry_ref[...] = carry_ref[...] + jnp.sum(chosen, axis=1, keepdims=True)
    cnt_ref[...] = carry_ref[...]


def _router(x, nw, sh, sc, rwt, rb, counts0, rows_per_group):
    m, d = x.shape
    tm = math.gcd(_tile(m, 256, LANES), rows_per_group)
    assert (d // 2) % LANES == 0
    tr = d // 2 // LANES
    grp = lambda i: ((i * tm) // rows_per_group, 0, 0)
    tok = lambda rows: pl.BlockSpec((rows, tm), lambda i: (0, i))
    return pl.pallas_call(
        _router_kernel,
        grid=(m // tm,),
        in_specs=[pl.BlockSpec((tm, d), lambda i: (i, 0)),
                  pl.BlockSpec((1, d), lambda i: (0, 0)),
                  pl.BlockSpec((1, 1, d), grp),
                  pl.BlockSpec((1, 1, d), grp),
                  pl.BlockSpec((N_EXPERTS, d), lambda i: (0, 0)),
                  pl.BlockSpec((N_EXPERTS, 1), lambda i: (0, 0)),
                  pl.BlockSpec((N_EXPERTS, LANES), lambda i: (0, 0))],
        out_specs=[pl.BlockSpec((tm * tr, LANES), lambda i: (i, 0)),
                   tok(TOP_K), tok(TOP_K), tok(TOP_K),
                   pl.BlockSpec((N_EXPERTS, LANES), lambda i: (0, 0))],
        out_shape=[jax.ShapeDtypeStruct((m * tr, LANES), U32),
                   jax.ShapeDtypeStruct((TOP_K, m), I32),
                   jax.ShapeDtypeStruct((TOP_K, m), F32),
                   jax.ShapeDtypeStruct((TOP_K, m), I32),
                   jax.ShapeDtypeStruct((N_EXPERTS, LANES), F32)],
        scratch_shapes=[pltpu.VMEM((N_EXPERTS, LANES), F32)],
        compiler_params=_params("arbitrary"),
        name="moe_router",
    )(x, nw.reshape(1, d), sh, sc, rwt, rb.reshape(N_EXPERTS, 1), counts0)


def _dispatch_kernel(nv_ref, dest_ref, hp_ref, xs_ref, zero_ref, sem, *, tr):
    tm = hp_ref.shape[0] // tr
    blk_rows = MOE_BLOCK * tr

    @pl.when(pl.program_id(0) == 0)
    def _():
        zero_ref[...] = jnp.zeros(zero_ref.shape, U32)

        def fill(blk, carry):
            @pl.when(nv_ref[blk] < MOE_BLOCK)
            def _():
                cp = pltpu.make_async_copy(
                    zero_ref, xs_ref.at[pl.ds(pl.multiple_of(blk * blk_rows, blk_rows), blk_rows)], sem)
                cp.start()
                cp.wait()
            return carry

        lax.fori_loop(0, nv_ref.shape[0], fill, 0)

    def copy(t, k):
        return _tile_copy(hp_ref, t * tr, xs_ref, dest_ref[t * TOP_K + k], tr, sem)

    def start(t, carry):
        for k in range(TOP_K):
            copy(t, k).start(priority=k % 2)
        return carry

    def wait(t, carry):
        for k in range(TOP_K):
            copy(t, k).wait()
        return carry

    lax.fori_loop(0, tm, start, 0)
    lax.fori_loop(0, tm, wait, 0)


def _dispatch(block_nv, dest, hp, n_rows, tr):
    m = hp.shape[0] // tr
    tm = _tile(m, 512, LANES)
    return pl.pallas_call(
        functools.partial(_dispatch_kernel, tr=tr),
        grid_spec=pltpu.PrefetchScalarGridSpec(
            num_scalar_prefetch=1,
            grid=(m // tm,),
            in_specs=[pl.BlockSpec((tm * TOP_K,), lambda i, nv: (i,), memory_space=pltpu.SMEM),
                      pl.BlockSpec((tm * tr, LANES), lambda i, nv: (i, 0))],
            out_specs=pl.BlockSpec(memory_space=pl.ANY),
            scratch_shapes=[pltpu.VMEM((MOE_BLOCK * tr, LANES), U32), pltpu.SemaphoreType.DMA(())]),
        out_shape=jax.ShapeDtypeStruct((n_rows * tr, LANES), U32),
        compiler_params=_params("arbitrary"),
        name="moe_dispatch",
    )(block_nv, dest, hp)


def _gffn_kernel(be_ref, nv_ref, xs_ref, w1_ref, w3_ref, w2_ref, ys_ref, w1b_ref, w3b_ref, w2b_ref, *, tr):
    i = pl.program_id(0)
    nv = nv_ref[i]
    prev = be_ref[jnp.maximum(i - 1, 0)]

    @pl.when((i == 0) | (be_ref[i] != prev))
    def _():
        w1b_ref[...] = w1_ref[0, 0].astype(BF16)
        w3b_ref[...] = w3_ref[0, 0].astype(BF16)
        w2b_ref[...] = w2_ref[0, 0].astype(BF16)

    @pl.when(nv > 0)
    def _():
        lo, hi = _unpack_pairs(_load_token_tiles(xs_ref, 0, MOE_BLOCK, tr))
        dh = lo.shape[1]
        lo = lo.astype(BF16)
        hi = hi.astype(BF16)
        d = functools.partial(jnp.dot, preferred_element_type=F32)
        a = d(lo, w1b_ref[:dh, :]) + d(hi, w1b_ref[dh:, :])
        g = d(lo, w3b_ref[:dh, :]) + d(hi, w3b_ref[dh:, :])
        y = d((_silu(a) * g).astype(BF16), w2b_ref[...])
        _store_token_tiles(ys_ref, 0, _pack_pairs(y[:, :dh], y[:, dh:]))

    @pl.when(nv == 0)
    def _():
        ys_ref[...] = jnp.zeros(ys_ref.shape, U32)


def _grouped_ffn(block_e, block_nv, xs, w1, w3, w2, layer, tr):
    _, _, d, f = w1.shape
    blk_rows = MOE_BLOCK * tr
    nb = xs.shape[0] // blk_rows
    return pl.pallas_call(
        functools.partial(_gffn_kernel, tr=tr),
        grid_spec=pltpu.PrefetchScalarGridSpec(
            num_scalar_prefetch=2,
            grid=(nb,),
            in_specs=[pl.BlockSpec((blk_rows, LANES), lambda i, be, nv: (i, 0)),
                      pl.BlockSpec((1, 1, d, f), lambda i, be, nv: (layer, be[i], 0, 0)),
                      pl.BlockSpec((1, 1, d, f), lambda i, be, nv: (layer, be[i], 0, 0)),
                      pl.BlockSpec((1, 1, f, d), lambda i, be, nv: (layer, be[i], 0, 0))],
            out_specs=pl.BlockSpec((blk_rows, LANES), lambda i, be, nv: (i, 0)),
            scratch_shapes=[pltpu.VMEM((d, f), BF16), pltpu.VMEM((d, f), BF16), pltpu.VMEM((f, d), BF16)]),
        out_shape=jax.ShapeDtypeStruct(xs.shape, U32),
        compiler_params=_params("arbitrary"),
        name="moe_grouped_ffn",
    )(block_e, block_nv, xs, w1, w3, w2)


def _combine_kernel(dest_ref, dest_next_ref, x_ref, hp_ref, wt_ref, g_ref, ws1_ref, ws3_ref, ws2_ref, ys_ref,
                    o_ref, buf_a, buf_b, sem_a, sem_b, *, tr):
    i = pl.program_id(0)
    tm = x_ref.shape[0]
    dh = tr * LANES

    def copy(dref, t, k, buf, sem):
        return _tile_copy(ys_ref, dref[t * TOP_K + k], buf, (k * tm + t) * tr, tr, sem)

    def wait_all(buf, sem):
        def wait(t, carry):
            for k in range(TOP_K):
                copy(dest_ref, t, k, buf, sem).wait()
            return carry
        lax.fori_loop(0, tm, wait, 0)

    @pl.when(i == 0)
    def _():
        def start(t, carry):
            for k in range(TOP_K):
                copy(dest_ref, t, k, buf_a, sem_a).start(priority=k % 2)
            return carry
        lax.fori_loop(0, tm, start, 0)

    def step(buf, sem, buf_next, sem_next):
        wait_all(buf, sem)
        for t in range(tm):
            for k in range(TOP_K):
                copy(dest_next_ref, t, k, buf_next, sem_next).start(priority=k % 2)
        lo, hi = _unpack_pairs(_load_token_tiles(hp_ref, 0, tm, tr))
        lo = lo.astype(BF16)
        hi = hi.astype(BF16)
        d = functools.partial(jnp.dot, preferred_element_type=F32)
        a = d(lo, ws1_ref[:dh, :]) + d(hi, ws1_ref[dh:, :])
        g = d(lo, ws3_ref[:dh, :]) + d(hi, ws3_ref[dh:, :])
        shared = d((_silu(a) * g).astype(BF16), ws2_ref[...])
        acc_lo = jnp.zeros((tm, dh), F32)
        acc_hi = jnp.zeros((tm, dh), F32)
        for k in range(TOP_K):
            ylo, yhi = _unpack_pairs(_load_token_tiles(buf, k * tm * tr, tm, tr))
            wk = wt_ref[:, k:k + 1]
            acc_lo = acc_lo + wk * ylo
            acc_hi = acc_hi + wk * yhi
        o_ref[:, :dh] = x_ref[:, :dh] + g_ref[0][:, :dh] * (acc_lo + shared[:, :dh])
        o_ref[:, dh:] = x_ref[:, dh:] + g_ref[0][:, dh:] * (acc_hi + shared[:, dh:])

        @pl.when(i == pl.num_programs(0) - 1)
        def _():
            wait_all(buf_next, sem_next)

    @pl.when(i % 2 == 0)
    def _():
        step(buf_a, sem_a, buf_b, sem_b)

    @pl.when(i % 2 == 1)
    def _():
        step(buf_b, sem_b, buf_a, sem_a)


def _combine(dest, x, hp, wt, gate, ws1, ws3, ws2, ys, rows_per_group, tr):
    m, d = x.shape
    f = ws1.shape[1]
    tm = math.gcd(_tile(m, 128, LANES), rows_per_group)
    nt = m // tm
    grp = lambda i: ((i * tm) // rows_per_group, 0, 0)
    buf = pltpu.VMEM((TOP_K * tm * tr, LANES), U32)
    return pl.pallas_call(
        functools.partial(_combine_kernel, tr=tr),
        grid=(nt,),
        in_specs=[pl.BlockSpec((tm * TOP_K,), lambda i: (i,), memory_space=pltpu.SMEM),
                  pl.BlockSpec((tm * TOP_K,), lambda i: (jnp.minimum(i + 1, nt - 1),), memory_space=pltpu.SMEM),
                  pl.BlockSpec((tm, d), lambda i: (i, 0)),
                  pl.BlockSpec((tm * tr, LANES), lambda i: (i, 0)),
                  pl.BlockSpec((tm, TOP_K), lambda i: (i, 0)),
                  pl.BlockSpec((1, 1, d), grp),
                  pl.BlockSpec((d, f), lambda i: (0, 0)),
                  pl.BlockSpec((d, f), lambda i: (0, 0)),
                  pl.BlockSpec((f, d), lambda i: (0, 0)),
                  pl.BlockSpec(memory_space=pl.ANY)],
        out_specs=pl.BlockSpec((tm, d), lambda i: (i, 0)),
        out_shape=jax.ShapeDtypeStruct((m, d), F32),
        scratch_shapes=[buf, buf, pltpu.SemaphoreType.DMA(()), pltpu.SemaphoreType.DMA(())],
        compiler_params=_params("arbitrary"),
        name="moe_combine",
    )(dest, dest, x, hp, wt, gate, ws1, ws3, ws2, ys)


def _moe(streams, layer, nw, rw, rb, w1, w3, w2, ws1, ws3, ws2):
    d = streams[0][0].shape[1]
    rwt = rw.T
    counts = jnp.zeros((N_EXPERTS, LANES), F32)
    routed = []
    for x, sh, sc, _, rpg in streams:
        hp, idx, wt, rank, counts = _router(x, nw, sh, sc, rwt, rb, counts, rpg)
        routed.append((hp, idx, wt, rank))
    n_assign = sum(s[0].shape[0] for s in streams) * TOP_K
    n_blocks = (n_assign + N_EXPERTS * (MOE_BLOCK - 1) + MOE_BLOCK - 1) // MOE_BLOCK
    cnt = counts[:, 0].astype(I32)
    padded = (cnt + MOE_BLOCK - 1) // MOE_BLOCK * MOE_BLOCK
    pad_end = jnp.cumsum(padded)
    pad_start = pad_end - padded
    bstart = jnp.arange(n_blocks, dtype=I32) * MOE_BLOCK
    block_e = jnp.minimum(jnp.sum((bstart[:, None] >= pad_end[None, :]).astype(I32), axis=1), N_EXPERTS - 1)
    block_nv = jnp.clip(cnt[block_e] - (bstart - pad_start[block_e]), 0, MOE_BLOCK).astype(I32)
    e_ar = jnp.arange(N_EXPERTS, dtype=I32)
    tr = d // 2 // LANES
    dests = [((jnp.sum(jnp.where(idx[:, :, None] == e_ar, pad_start, 0), axis=-1) + rank) * tr).T.reshape(-1)
             for _, idx, _, rank in routed]
    if len(streams) > 1:
        hp_all = jnp.concatenate([r[0] for r in routed], axis=0)
        dest_all = jnp.concatenate(dests, axis=0)
    else:
        hp_all, dest_all = routed[0][0], dests[0]
    xs = _dispatch(block_nv, dest_all, hp_all, n_blocks * MOE_BLOCK, tr)
    ys = _grouped_ffn(block_e, block_nv, xs, w1, w3, w2, layer, tr)
    ws1b, ws3b, ws2b = ws1.astype(BF16), ws3.astype(BF16), ws2.astype(BF16)
    return [_combine(dest, x, hp, wt.T, gate, ws1b, ws3b, ws2b, ys, rpg, tr)
            for (x, _, _, gate, rpg), (hp, _, wt, _), dest in zip(streams, routed, dests)]


def _mixer_ab(xs, cs, mx, mc, b, t, n_c, layer, nw, w_in, w_out, q_norm, k_norm, lam_vec, subln, conv_w,
              a_log, dt_bias, out_norm):
    n_main = w_in.shape[1] - 4 * DN_HEADS
    assert n_main % LANES == 0
    w_main = w_in[:, :n_main].astype(BF16)
    w_small = jnp.zeros((w_in.shape[0], LANES), F32).at[:, :4 * DN_HEADS].set(w_in[:, n_main:])
    nqk = 2 * DIFF_HEADS * DIFF_DK
    assert n_main % nqk == 0
    tabs = _rope_tables(t, DIFF_DK)
    hw = jnp.stack([q_norm * (DIFF_DK ** -0.5 * LOG2E), k_norm])
    groups = nqk // LANES
    blocks = ((0, 1, (0,) * groups), (1, 2, (1,) * groups))
    z_x, zs_x = _norm_mod_matmul(xs, nw, mx[0], mx[1], w_main, t, w_small, tn=nqk,
                                 heads=dict(hd=DIFF_DK, weights=hw, blocks=blocks, tables=tabs, t=t))
    z_c, zs_c = _norm_mod_matmul(cs, nw, mc[0], mc[1], w_main, b * n_c, w_small, tn=nqk,
                                 heads=dict(hd=DIFF_DK, weights=hw, blocks=blocks, tables=None, t=t))
    lam_init = 0.8 - 0.6 * math.exp(-0.3 * layer)
    od_x = _diff_attention(lam_vec, z_x, z_c, z_c, subln, lam_init, b, n_c, kx=z_x, zx=z_x, t=t)
    od_c = _diff_attention(lam_vec, z_c, z_c, z_c, subln, lam_init, b, n_c)
    dn_col0 = 2 * nqk + DIFF_HEADS * DIFF_DV
    dn_x = _dn_prep(z_x, dn_col0, conv_w, t)
    dn_c = _dn_prep(z_c, dn_col0, conv_w, n_c)
    zst_x = zs_x[:, :4 * DN_HEADS].T
    zst_c = zs_c[:, :4 * DN_HEADS].T
    prm = _deltanet_params(a_log, dt_bias)
    s0 = jnp.zeros((b, DN_HEADS, DN_DK, DN_DV), F32)
    o_cf, s_cf = _deltanet(dn_c, zs_c, zst_c, prm, s0, b, n_c, False)
    o_xf, _ = _deltanet(dn_x, zs_x, zst_x, prm, s_cf, b, t, False)
    o_cb, s_cb = _deltanet(dn_c, zs_c, zst_c, prm, s0, b, n_c, True)
    o_xb, _ = _deltanet(dn_x, zs_x, zst_x, prm, s_cb, b, t, True)
    gate_col0 = dn_col0 + conv_w.shape[1]
    w_out_b = w_out.astype(BF16)
    x1 = _outproj_ab(od_x, o_xf, o_xb, z_x, gate_col0, out_norm, w_out_b, xs, mx[2], t)
    c1 = _outproj_ab(od_c, o_cf, o_cb, z_c, gate_col0, out_norm, w_out_b, cs, mc[2], b * n_c)
    return x1, c1


def _mixer_swa(xs, cs, mx, mc, b, t, n_c, nw, w_in, w_out, q_norm, k_norm, sink):
    nq = SWA_HEADS * SWA_DH
    nkv = SWA_KV_HEADS * SWA_DH
    w_b = w_in.astype(BF16)
    tn = 2 * nkv
    assert nq % tn == 0
    tabs = _rope_tables(t, SWA_DH)
    rep = LANES // SWA_DH
    hw = jnp.stack([jnp.tile(q_norm, rep) * (SWA_DH ** -0.5 * LOG2E), jnp.tile(k_norm, rep)])
    q_rows = (0,) * (tn // LANES)
    kv_rows = (1,) * (nkv // LANES) + (None,) * (nkv // LANES)
    z_x = _norm_mod_matmul(xs, nw, mx[0], mx[1], w_b, t, tn=tn,
                           heads=dict(hd=SWA_DH, weights=hw, tables=tabs, t=t,
                                      blocks=((0, nq // tn, q_rows), (nq // tn, nq // tn + 1, kv_rows))))
    z_c = _norm_mod_matmul(cs, nw, mc[0], mc[1], w_b[:, nq:], b * n_c, tn=tn,
                           heads=dict(hd=SWA_DH, weights=hw, tables=None, t=t, blocks=((0, 1, kv_rows),)))
    att = _swa_attention(sink.reshape(SWA_KV_HEADS, SWA_GROUP), z_x, _dup_heads(z_x, nq), _dup_heads(z_x, nq + nkv),
                         _dup_heads(z_c, 0), _dup_heads(z_c, nkv), b, t, n_c)
    return _outproj(att, w_out.astype(BF16), xs, mx[2], t)


def kernel(x, c, ctx, c_ctx, mod_w, mod_b, norm_mix, norm_ffn, ab_w_in, ab_w_out, diff_q_norm, diff_k_norm, diff_lambda, diff_subln, dn_conv, dn_a_log, dn_dt_bias, dn_out_norm, swa_w_in, swa_w_out, swa_q_norm, swa_k_norm, swa_sink, router_w, router_bias, exp_w1, exp_w3, exp_w2, shared_w1, shared_w3, shared_w2):
    b, t, d = x.shape
    n_c = ctx.shape[1]
    depth = mod_w.shape[0]
    assert depth == 2 and t % Q_BLOCK == 0 and t % DN_CHUNK == 0 and n_c % DN_CHUNK == 0
    xs = x.reshape(b * t, d)
    cs = ctx.reshape(b * n_c, d)
    n_mod = -(-(b + 1) // SUBLANES) * SUBLANES
    a_mod = jnp.zeros((n_mod, d), F32).at[0].set(c_ctx).at[1:1 + b].set(c)
    for layer in range(depth):
        with_ctx = layer < depth - 1
        p = layer // 2
        mod = _modulation(a_mod, mod_w, layer, mod_b[layer])
        mc = [mod[0:1, j * d:(j + 1) * d].reshape(1, 1, d) for j in range(6)]
        mx = [mod[1:1 + b, j * d:(j + 1) * d].reshape(b, 1, d) for j in range(6)]
        if layer % 2 == 0:
            xs, c_new = _mixer_ab(xs, cs, mx, mc, b, t, n_c, layer, norm_mix[layer], ab_w_in[p], ab_w_out[p],
                                  diff_q_norm[p], diff_k_norm[p], diff_lambda[p], diff_subln[p], dn_conv[p],
                                  dn_a_log[p], dn_dt_bias[p], dn_out_norm[p])
        else:
            assert not with_ctx
            xs = _mixer_swa(xs, cs, mx, mc, b, t, n_c, norm_mix[layer], swa_w_in[p], swa_w_out[p],
                            swa_q_norm[p], swa_k_norm[p], swa_sink[p])
            c_new = None
        moe_w = (layer, norm_ffn[layer], router_w[layer], router_bias[layer], exp_w1, exp_w3, exp_w2,
                 shared_w1[layer], shared_w3[layer], shared_w2[layer])
        if with_ctx:
            cs, xs = _moe([(c_new, mc[3], mc[4], mc[5], b * n_c), (xs, mx[3], mx[4], mx[5], t)], *moe_w)
        else:
            (xs,) = _moe([(xs, mx[3], mx[4], mx[5], t)], *moe_w)
    return xs.reshape(b, t, d)
```

```python
import functools
import math

import jax
import jax.numpy as jnp
import numpy as np
from jax import lax
from jax.experimental import pallas as pl
from jax.experimental.pallas import tpu as pltpu

F32 = jnp.float32
BF16 = jnp.bfloat16
I32 = jnp.int32
U32 = jnp.uint32

EPS = 1e-6
GRID_W = 64
ROPE_BASE = 10000.0
DIFF_HEADS = 4
DIFF_DK = 128
DIFF_DV = 256
DN_HEADS = 8
DN_DK = 128
DN_DV = 128
DN_CONV = 5
DN_CHUNK = 128
SWA_HEADS = 32
SWA_KV_HEADS = 4
SWA_GROUP = SWA_HEADS // SWA_KV_HEADS
SWA_DH = 64
WINDOW = 128
Q_BLOCK = 128
N_EXPERTS = 64
TOP_K = 8
N_GROUPS = 8
GROUP_SIZE = N_EXPERTS // N_GROUPS
TOPK_GROUPS = 4
ROUTED_SCALE = 2.5
MOE_BLOCK = 512

LANES = 128
SUBLANES = 8
VMEM_LIMIT_BYTES = 56 * 1024 * 1024

NT_DIMS = (((1,), (1,)), ((), ()))
LOG2E = math.log2(math.e)


def _params(*semantics):
    return pltpu.CompilerParams(dimension_semantics=semantics, vmem_limit_bytes=VMEM_LIMIT_BYTES)


def _tile(n, pref, mult):
    if n <= pref:
        return n
    t = pref - pref % mult
    while t > mult and n % t:
        t -= mult
    assert n % t == 0, (n, pref, mult)
    return t


def _mm(a, b):
    return jnp.dot(a.astype(BF16), b.astype(BF16), preferred_element_type=F32)


def _split2(x):
    hi = x.astype(BF16)
    lo = (x - hi.astype(F32)).astype(BF16)
    return hi, lo


def _dot3(a, b, dims=None):
    if dims is None:
        dims = (((a.ndim - 1,), (0,)), ((), ()))
    ah, al = _split2(a)
    bh, bl = _split2(b)
    d = functools.partial(lax.dot_general, dimension_numbers=dims, preferred_element_type=F32)
    return d(ah, bh) + d(ah, bl) + d(al, bh)


def _silu(x):
    return x * jax.nn.sigmoid(x)


def _softplus(x):
    return jnp.maximum(x, 0.0) + jnp.log(1.0 + jnp.exp(-jnp.abs(x)))


def _pack_pairs(lo, hi):
    ulo = lax.bitcast_convert_type(lo.astype(BF16).astype(F32), U32) >> 16
    uhi = lax.bitcast_convert_type(hi.astype(BF16).astype(F32), U32) & jnp.uint32(0xFFFF0000)
    return ulo | uhi


def _unpack_pairs(u):
    lo = lax.bitcast_convert_type(u << 16, F32)
    hi = lax.bitcast_convert_type(u & jnp.uint32(0xFFFF0000), F32)
    return lo, hi


def _store_token_tiles(ref, base, packed):
    n, width = packed.shape
    tr = width // LANES
    for j in range(tr):
        ref[pl.ds(base + j, n, stride=tr), :] = packed[:, j * LANES:(j + 1) * LANES]


def _load_token_tiles(ref, base, n, tr):
    return jnp.concatenate([ref[pl.ds(base + j, n, stride=tr), :] for j in range(tr)], axis=1)


def _tile_copy(src, src_row, dst, dst_row, tr, sem):
    return pltpu.make_async_copy(src.at[pl.ds(pl.multiple_of(src_row, tr), tr)],
                                 dst.at[pl.ds(pl.multiple_of(dst_row, tr), tr)], sem)


def _mod_kernel(a_ref, w_ref, b_ref, o_ref):
    o_ref[...] = _dot3(_silu(a_ref[...]), w_ref[0]) + b_ref[...]


def _modulation(a, w_all, layer, b):
    r, d = a.shape
    n = w_all.shape[2]
    tn = _tile(n, 768, LANES)
    return pl.pallas_call(
        _mod_kernel,
        grid=(n // tn,),
        in_specs=[pl.BlockSpec((r, d), lambda j: (0, 0)),
                  pl.BlockSpec((1, d, tn), lambda j: (layer, 0, j)),
                  pl.BlockSpec((1, tn), lambda j: (0, j))],
        out_specs=pl.BlockSpec((r, tn), lambda j: (0, j)),
        out_shape=jax.ShapeDtypeStruct((r, n), F32),
        compiler_params=_params("parallel"),
        name="modulation",
    )(a, w_all, b.reshape(1, n))


def _norm_mod(x, nw, sh, sc):
    y = x * lax.rsqrt(jnp.mean(x * x, axis=-1, keepdims=True) + EPS) * nw
    return y * (1.0 + sc) + sh


def _head_norm_rope(x, w, seg_ones, swap, cos, sin, hd):
    ms = jnp.dot((x * x).astype(BF16), seg_ones, preferred_element_type=F32) * (1.0 / hd)
    y = x * lax.rsqrt(ms + EPS) * w
    if cos is not None:
        partner = jnp.dot(y.astype(BF16), swap, preferred_element_type=F32)
        y = y * cos + partner * sin
    return y


def _head_matrices(hd):
    lane = np.arange(LANES)
    seg = (lane[:, None] // hd) == (lane[None, :] // hd)
    q = hd // 4
    partner = np.where((lane % (2 * q)) < q, lane + q, lane - q)
    swap = lane[:, None] == partner[None, :]
    return jnp.asarray(seg, BF16), jnp.asarray(swap, BF16)


def _nmm_kernel(x_ref, nw_ref, sh_ref, sc_ref, w_ref, *rest, has_small, head_blocks, hd, rope):
    rest = list(rest)
    ws_ref = rest.pop(0) if has_small else None
    hw_ref, seg_ref, swap_ref = (rest.pop(0), rest.pop(0), rest.pop(0)) if head_blocks else (None, None, None)
    cos_ref, sin_ref = (rest.pop(0), rest.pop(0)) if rope else (None, None)
    o_ref = rest.pop(0)
    os_ref = rest.pop(0) if has_small else None
    (h_ref,) = rest
    j = pl.program_id(1)

    @pl.when(j == 0)
    def _():
        h = _norm_mod(x_ref[...], nw_ref[...], sh_ref[0], sc_ref[0])
        h_ref[...] = h.astype(BF16)
        if has_small:
            os_ref[...] = _dot3(h, ws_ref[...])

    r = jnp.dot(h_ref[...], w_ref[...], preferred_element_type=F32)
    if not head_blocks:
        o_ref[...] = r.astype(o_ref.dtype)
        return
    tm, tn = r.shape
    plain = j >= 0
    for lo, hi, rows in head_blocks:
        inside = (j >= lo) & (j < hi)
        plain = plain & jnp.logical_not(inside)

        @pl.when(inside)
        def _(rows=rows):
            cos = cos_ref[...] if rope else None
            sin = sin_ref[...] if rope else None
            for g in range(tn // LANES):
                sl = slice(g * LANES, (g + 1) * LANES)
                y = r[:, sl]
                if rows[g] is not None:
                    y = _head_norm_rope(y, hw_ref[rows[g]], seg_ref[...], swap_ref[...], cos, sin, hd)
                o_ref[:, sl] = y.astype(o_ref.dtype)

    @pl.when(plain)
    def _():
        o_ref[...] = r.astype(o_ref.dtype)


def _norm_mod_matmul(x, nw, sh, sc, w, rows_per_group, w_small=None, tn=1024, heads=None):
    m, d = x.shape
    n = w.shape[1]
    tm = math.gcd(_tile(m, 512, SUBLANES), rows_per_group)
    tn = _tile(n, tn, LANES)
    has_small = w_small is not None
    grp = lambda i, j: ((i * tm) // rows_per_group, 0, 0)
    in_specs = [pl.BlockSpec((tm, d), lambda i, j: (i, 0)),
                pl.BlockSpec((1, d), lambda i, j: (0, 0)),
                pl.BlockSpec((1, 1, d), grp),
                pl.BlockSpec((1, 1, d), grp),
                pl.BlockSpec((d, tn), lambda i, j: (0, j))]
    args = [x, nw.reshape(1, d), sh, sc, w]
    out_specs = [pl.BlockSpec((tm, tn), lambda i, j: (i, j))]
    out_shape = [jax.ShapeDtypeStruct((m, n), BF16)]
    if has_small:
        in_specs.append(pl.BlockSpec((d, LANES), lambda i, j: (0, 0)))
        args.append(w_small)
        out_specs.append(pl.BlockSpec((tm, LANES), lambda i, j: (i, 0)))
        out_shape.append(jax.ShapeDtypeStruct((m, LANES), F32))
    head_blocks, hd, rope = (), LANES, False
    if heads is not None:
        head_blocks, hd = tuple(heads["blocks"]), heads["hd"]
        hw = heads["weights"]
        in_specs += [pl.BlockSpec((hw.shape[0], 1, LANES), lambda i, j: (0, 0, 0)),
                     pl.BlockSpec((LANES, LANES), lambda i, j: (0, 0)),
                     pl.BlockSpec((LANES, LANES), lambda i, j: (0, 0))]
        args += [hw.reshape(hw.shape[0], 1, LANES), *_head_matrices(hd)]
        if heads["tables"] is not None:
            rope = True
            nt = heads["t"] // tm
            assert heads["t"] % tm == 0
            in_specs += [pl.BlockSpec((tm, LANES), lambda i, j: (i % nt, 0))] * 2
            args += list(heads["tables"])
    outs = pl.pallas_call(
        functools.partial(_nmm_kernel, has_small=has_small, head_blocks=head_blocks, hd=hd, rope=rope),
        grid=(m // tm, n // tn),
        in_specs=in_specs,
        out_specs=out_specs,
        out_shape=out_shape,
        scratch_shapes=[pltpu.VMEM((tm, d), BF16)],
        compiler_params=_params("parallel", "arbitrary"),
        name="norm_mod_matmul",
    )(*args)
    return outs if has_small else outs[0]


def _rope_tables(t, head_dim):
    q = head_dim // 4
    pos = jnp.arange(t, dtype=I32)
    row = (pos // GRID_W).astype(F32)
    col = (pos % GRID_W).astype(F32)
    axis_dim = head_dim // 2
    inv_freq = ROPE_BASE ** (-jnp.arange(0, axis_dim, 2, dtype=F32) / axis_dim)
    lane = jnp.arange(LANES) % head_dim
    freq = inv_freq[lane % q]
    p = jnp.where((lane < head_dim // 2)[None, :], row[:, None], col[:, None])
    ang = p * freq[None, :]
    sign = jnp.where((lane % (2 * q)) < q, -1.0, 1.0)
    return jnp.cos(ang), jnp.sin(ang) * sign[None, :]


def _diff_attn_kernel(lv_ref, q_ref, *rest, lam_init, has_lat):
    if has_lat:
        kx_ref, vx_ref, kc_ref, vc_ref, w_ref, o_ref = rest
    else:
        kc_ref, vc_ref, w_ref, o_ref = rest
    lv = lv_ref[...]
    lam = (jnp.exp(jnp.sum(lv[0:1] * lv[1:2], keepdims=True))
           - jnp.exp(jnp.sum(lv[2:3] * lv[3:4], keepdims=True)) + lam_init)
    q = q_ref[...]

    def probs(c):
        sl = slice(c * DIFF_DK, (c + 1) * DIFF_DK)
        qc = q[:, sl]
        s_c = lax.dot_general(qc, kc_ref[:, sl], NT_DIMS, preferred_element_type=F32)
        m = jnp.max(s_c, axis=-1, keepdims=True)
        p_x = None
        if has_lat:
            s_x = lax.dot_general(qc, kx_ref[:, sl], NT_DIMS, preferred_element_type=F32)
            m = jnp.maximum(m, jnp.max(s_x, axis=-1, keepdims=True))
            p_x = jnp.exp2(s_x - m)
        p_c = jnp.exp2(s_c - m)
        l = jnp.sum(p_c, axis=-1, keepdims=True)
        if has_lat:
            l = l + jnp.sum(p_x, axis=-1, keepdims=True)
        return p_x, p_c, l

    p1x, p1c, l1 = probs(0)
    p2x, p2c, l2 = probs(1)
    ratio = lam * l1 * (1.0 / l2)
    o = jnp.dot((p1c - p2c * ratio).astype(BF16), vc_ref[...], preferred_element_type=F32)
    if has_lat:
        o = o + jnp.dot((p1x - p2x * ratio).astype(BF16), vx_ref[...], preferred_element_type=F32)
    o = o * (1.0 / l1)
    o = o * lax.rsqrt(jnp.mean(o * o, axis=-1, keepdims=True) + EPS)
    o_ref[...] = (o * w_ref[...] * (1.0 - lam_init)).astype(o_ref.dtype)


def _diff_attention(lam_vec, q, kc, zc, subln, lam_init, b, n_c, kx=None, zx=None, t=None):
    has_lat = kx is not None
    hw = 2 * DIFF_DK
    kblk0 = DIFF_HEADS
    vblk0 = (2 * DIFF_HEADS * hw) // DIFF_DV
    tq_all = t if has_lat else n_c
    tq = _tile(tq_all, 256, SUBLANES)
    nq = tq_all // tq
    in_specs = [pl.BlockSpec((4, DIFF_DK), lambda bi, h, qi: (0, 0)),
                pl.BlockSpec((tq, hw), lambda bi, h, qi: (bi * nq + qi, h))]
    args = [lam_vec, q]
    if has_lat:
        in_specs += [pl.BlockSpec((t, hw), lambda bi, h, qi: (bi, kblk0 + h)),
                     pl.BlockSpec((t, DIFF_DV), lambda bi, h, qi: (bi, vblk0 + h))]
        args += [kx, zx]
    in_specs += [pl.BlockSpec((n_c, hw), lambda bi, h, qi: (bi, kblk0 + h)),
                 pl.BlockSpec((n_c, DIFF_DV), lambda bi, h, qi: (bi, vblk0 + h)),
                 pl.BlockSpec((1, DIFF_DV), lambda bi, h, qi: (0, 0))]
    args += [kc, zc, subln.reshape(1, DIFF_DV)]
    return pl.pallas_call(
        functools.partial(_diff_attn_kernel, lam_init=lam_init, has_lat=has_lat),
        grid=(b, DIFF_HEADS, nq),
        in_specs=in_specs,
        out_specs=pl.BlockSpec((tq, DIFF_DV), lambda bi, h, qi: (bi * nq + qi, h)),
        out_shape=jax.ShapeDtypeStruct((b * tq_all, DIFF_HEADS * DIFF_DV), BF16),
        compiler_params=_params("parallel", "parallel", "arbitrary"),
        name="diff_attention",
    )(*args)


def _dn_prep_kernel(z_ref, cw_ref, o_ref, pad_ref, *, seg):
    halo = SUBLANES
    pad_ref[0:halo, :] = jnp.zeros((halo, LANES), F32)
    pad_ref[halo + seg:2 * halo + seg, :] = jnp.zeros((halo, LANES), F32)
    pad_ref[halo:halo + seg, :] = z_ref[...].astype(F32)
    kind = pl.program_id(1) // DN_HEADS
    qk_scale = jnp.where(kind == 0, DN_DK ** -0.5, 1.0).astype(F32)
    rows = _tile(seg, 256, SUBLANES)
    for r0 in range(0, seg, rows):
        acc = jnp.zeros((rows, LANES), F32)
        for j in range(DN_CONV):
            acc = acc + cw_ref[j:j + 1, :] * pad_ref[pl.ds(halo + r0 + j - DN_CONV // 2, rows), :]
        y = _silu(acc)
        nrm = y * lax.rsqrt(jnp.sum(y * y, axis=-1, keepdims=True) + EPS) * qk_scale
        o_ref[r0:r0 + rows, :] = jnp.where(kind < 2, nrm, y).astype(o_ref.dtype)


def _dn_prep(z, col0, conv_w, seg):
    m = z.shape[0]
    ncols = conv_w.shape[1]
    cblk0 = col0 // LANES
    cw = jnp.zeros((SUBLANES, ncols), F32).at[:DN_CONV].set(conv_w)
    return pl.pallas_call(
        functools.partial(_dn_prep_kernel, seg=seg),
        grid=(m // seg, ncols // LANES),
        in_specs=[pl.BlockSpec((seg, LANES), lambda s, g: (s, cblk0 + g)),
                  pl.BlockSpec((SUBLANES, LANES), lambda s, g: (0, g))],
        out_specs=pl.BlockSpec((seg, LANES), lambda s, g: (s, g)),
        out_shape=jax.ShapeDtypeStruct((m, ncols), BF16),
        scratch_shapes=[pltpu.VMEM((seg + 2 * SUBLANES, LANES), F32)],
        compiler_params=_params("parallel", "parallel"),
        name="dn_prep",
    )(z, cw)


def _dn_heads(reverse, q_ref, k_ref, v_ref, zs_ref, zst_ref, pr_ref, pca_ref, pcd_ref, o_ref, s_ref, ri, ci):
    c = DN_CHUNK
    if reverse:
        later, strict, later_t = ri <= ci, ri < ci, ri >= ci
    else:
        later, strict, later_t = ri >= ci, ri > ci, ri <= ci
    eye = (ri == ci).astype(F32)
    tri = later.astype(BF16)
    tri_t = later_t.astype(BF16)

    zs = zs_ref[...]
    beta_cols = jax.nn.sigmoid(zs)
    g_cols = -jnp.exp(pr_ref[0:1, :]) * _softplus(zs + pr_ref[1:2, :])
    g_hi = g_cols.astype(BF16)
    g_r1 = g_cols - g_hi.astype(F32)
    g_mid = g_r1.astype(BF16)
    g_lo = (g_r1 - g_mid.astype(F32)).astype(BF16)
    d = functools.partial(jnp.dot, preferred_element_type=F32)
    gc_cols = d(tri, g_hi) + d(tri, g_mid) + d(tri, g_lo)
    g_rows = -jnp.exp(pca_ref[...]) * _softplus(zst_ref[...] + pcd_ref[...])
    h_hi = g_rows.astype(BF16)
    h_r1 = g_rows - h_hi.astype(F32)
    h_mid = h_r1.astype(BF16)
    h_lo = (h_r1 - h_mid.astype(F32)).astype(BF16)
    gc_rows = d(h_hi, tri_t) + d(h_mid, tri_t) + d(h_lo, tri_t)

    dir_off = DN_HEADS if reverse else 0
    last = 0 if reverse else c - 1
    neg_inf = jnp.float32(-jnp.inf)
    heads = []
    for h in range(DN_HEADS):
        sl = slice(h * DN_DK, (h + 1) * DN_DK)
        cb = dir_off + h
        cg = 2 * DN_HEADS + dir_off + h
        beta = beta_cols[:, cb:cb + 1]
        gcol = gc_cols[:, cg:cg + 1]
        grow = gc_rows[cg:cg + 1, :]
        glast = grow[:, last:last + 1]
        q = q_ref[:, sl]
        k = k_ref[:, sl]
        kf = k.astype(F32)
        decay = jnp.exp(jnp.where(later, gcol - grow, neg_inf))
        kb = kf * beta
        both = lax.dot_general(jnp.concatenate([kb.astype(BF16), q], axis=0), k, NT_DIMS,
                               preferred_element_type=F32)
        lmat = jnp.where(strict, both[:c] * decay, 0.0)
        eg = jnp.exp(gcol)
        heads.append(dict(
            sl=sl, h=h, o_ref=o_ref, s_ref=s_ref, glast=glast, lmat=lmat, amat=both[c:] * decay,
            inv=eye - jnp.where((ri >> 1) == (ci >> 1), lmat, 0.0),
            rhs=jnp.concatenate([v_ref[:, sl].astype(F32) * beta, kb * eg], axis=1).astype(BF16),
            qe=(q.astype(F32) * eg).astype(BF16),
            kdec_t=(kf * jnp.exp(glast - gcol)).T.astype(BF16)))
    return heads


def _deltanet_kernel(qf_ref, kf_ref, vf_ref, zsf_ref, zstf_ref, qb_ref, kb_ref, vb_ref, zsb_ref, zstb_ref,
                     pr_ref, pca_ref, pcd_ref, s0f_ref, s0b_ref,
                     of_ref, ob_ref, soutf_ref, soutb_ref, sf_ref, sb_ref, *, n_chunks):
    step = pl.program_id(1)
    c = DN_CHUNK

    @pl.when(step == 0)
    def _():
        sf_ref[...] = s0f_ref[0]
        sb_ref[...] = s0b_ref[0]

    ri = lax.broadcasted_iota(I32, (c, c), 0)
    ci = lax.broadcasted_iota(I32, (c, c), 1)
    prm = (pr_ref, pca_ref, pcd_ref)
    heads = (_dn_heads(False, qf_ref, kf_ref, vf_ref, zsf_ref, zstf_ref, *prm, of_ref, sf_ref, ri, ci)
             + _dn_heads(True, qb_ref, kb_ref, vb_ref, zsb_ref, zstb_ref, *prm, ob_ref, sb_ref, ri, ci))
    lev = 1
    while (1 << lev) < c:
        blk = ((ri >> (lev + 1)) == (ci >> (lev + 1))) & ((ri >> lev) != (ci >> lev))
        half = [_mm(hd["inv"], jnp.where(blk, hd["lmat"], 0.0)) for hd in heads]
        for hd, t in zip(heads, half):
            hd["inv"] = hd["inv"] - _mm(t, hd["inv"])
        lev += 1
    uws = [_mm(hd["inv"], hd["rhs"]) for hd in heads]
    states = [hd["s_ref"][hd["h"]] for hd in heads]
    new_states = []
    for hd, uw, s in zip(heads, uws, states):
        ws_qs = _mm(jnp.concatenate([uw[:, DN_DV:].astype(BF16), hd["qe"]], axis=0), s)
        v_new = uw[:, :DN_DV] - ws_qs[:c]
        hd["o_ref"][:, hd["sl"]] = (ws_qs[c:] + _mm(hd["amat"], v_new)).astype(hd["o_ref"].dtype)
        new_states.append(s * jnp.exp(hd["glast"]) + _mm(hd["kdec_t"], v_new))
    for hd, s in zip(heads, new_states):
        hd["s_ref"][hd["h"]] = s

    @pl.when(step == n_chunks - 1)
    def _():
        soutf_ref[0] = sf_ref[...]
        soutb_ref[0] = sb_ref[...]


def _deltanet(dn, zs, zst, prm, s0f, s0b, b, seg):
    pr, pca, pcd = prm
    n = seg // DN_CHUNK
    hw = DN_HEADS * DN_DK
    fw = lambda bi, s: bi * n + s
    bw = lambda bi, s: bi * n + (n - 1 - s)
    state = pl.BlockSpec((1, DN_HEADS, DN_DK, DN_DV), lambda bi, s: (bi, 0, 0, 0))

    def chunk_specs(rb):
        return [pl.BlockSpec((DN_CHUNK, hw), lambda bi, s: (rb(bi, s), 0)),
                pl.BlockSpec((DN_CHUNK, hw), lambda bi, s: (rb(bi, s), 1)),
                pl.BlockSpec((DN_CHUNK, hw), lambda bi, s: (rb(bi, s), 2)),
                pl.BlockSpec((DN_CHUNK, LANES), lambda bi, s: (rb(bi, s), 0)),
                pl.BlockSpec((4 * DN_HEADS, DN_CHUNK), lambda bi, s: (0, rb(bi, s)))]

    out = lambda rb: pl.BlockSpec((DN_CHUNK, hw), lambda bi, s: (rb(bi, s), 0))
    o_shape = jax.ShapeDtypeStruct((b * seg, hw), BF16)
    s_shape = jax.ShapeDtypeStruct((b, DN_HEADS, DN_DK, DN_DV), F32)
    s_scratch = pltpu.VMEM((DN_HEADS, DN_DK, DN_DV), F32)
    return pl.pallas_call(
        functools.partial(_deltanet_kernel, n_chunks=n),
        grid=(b, n),
        in_specs=chunk_specs(fw) + chunk_specs(bw) + [
            pl.BlockSpec((SUBLANES, LANES), lambda bi, s: (0, 0)),
            pl.BlockSpec((4 * DN_HEADS, LANES), lambda bi, s: (0, 0)),
            pl.BlockSpec((4 * DN_HEADS, LANES), lambda bi, s: (0, 0)),
            state, state],
        out_specs=[out(fw), out(bw), state, state],
        out_shape=[o_shape, o_shape, s_shape, s_shape],
        scratch_shapes=[s_scratch, s_scratch],
        compiler_params=_params("parallel", "arbitrary"),
        name="deltanet",
    )(dn, dn, dn, zs, zst, dn, dn, dn, zs, zst, pr, pca, pcd, s0f, s0b)


def _deltanet_params(a_log, dt_bias):
    nh = 2 * DN_HEADS
    a = a_log.reshape(nh).astype(F32)
    dtb = dt_bias.reshape(nh).astype(F32)
    pr = jnp.zeros((SUBLANES, LANES), F32).at[0, nh:2 * nh].set(a).at[1, nh:2 * nh].set(dtb)
    pca = jnp.zeros((2 * nh, LANES), F32).at[nh:].set(jnp.broadcast_to(a[:, None], (nh, LANES)))
    pcd = jnp.zeros((2 * nh, LANES), F32).at[nh:].set(jnp.broadcast_to(dtb[:, None], (nh, LANES)))
    return pr, pca, pcd


def _outproj_ab_kernel(od_ref, of_ref, ob_ref, gate_ref, nw_ref, w_ref, x_ref, g_ref, o_ref):
    nd = od_ref.shape[1]
    y = jnp.dot(od_ref[...], w_ref[:nd, :], preferred_element_type=F32)
    gated = []
    for h in range(DN_HEADS):
        sl = slice(h * DN_DV, (h + 1) * DN_DV)
        o = of_ref[:, sl].astype(F32) + ob_ref[:, sl].astype(F32)
        o = o * lax.rsqrt(jnp.mean(o * o, axis=-1, keepdims=True) + EPS) * nw_ref[...]
        gated.append((o * _silu(gate_ref[:, sl].astype(F32))).astype(BF16))
    y = y + jnp.dot(jnp.concatenate(gated, axis=1), w_ref[nd:, :], preferred_element_type=F32)
    o_ref[...] = x_ref[...] + g_ref[0] * y


def _outproj_ab(od, o_f, o_b, z, gate_col0, out_norm, w, x, gate, rows_per_group):
    m, d = x.shape
    nd, nn = od.shape[1], o_f.shape[1]
    tm = math.gcd(_tile(m, 256, SUBLANES), rows_per_group)
    grp = lambda i: ((i * tm) // rows_per_group, 0, 0)
    gblk = gate_col0 // nn
    return pl.pallas_call(
        _outproj_ab_kernel,
        grid=(m // tm,),
        in_specs=[pl.BlockSpec((tm, nd), lambda i: (i, 0)),
                  pl.BlockSpec((tm, nn), lambda i: (i, 0)),
                  pl.BlockSpec((tm, nn), lambda i: (i, 0)),
                  pl.BlockSpec((tm, nn), lambda i: (i, gblk)),
                  pl.BlockSpec((1, DN_DV), lambda i: (0, 0)),
                  pl.BlockSpec((nd + nn, d), lambda i: (0, 0)),
                  pl.BlockSpec((tm, d), lambda i: (i, 0)),
                  pl.BlockSpec((1, 1, d), grp)],
        out_specs=pl.BlockSpec((tm, d), lambda i: (i, 0)),
        out_shape=jax.ShapeDtypeStruct((m, d), F32),
        compiler_params=_params("parallel"),
        name="outproj_ab",
    )(od, o_f, o_b, z, out_norm.reshape(1, DN_DV), w, x, gate)


def _outproj_kernel(a_ref, w_ref, x_ref, g_ref, o_ref):
    o_ref[...] = x_ref[...] + g_ref[0] * jnp.dot(a_ref[...], w_ref[...], preferred_element_type=F32)


def _outproj(a, w, x, gate, rows_per_group):
    m, d = x.shape
    kdim = a.shape[1]
    tm = math.gcd(_tile(m, 512, SUBLANES), rows_per_group)
    tn = _tile(d, 2048, LANES)
    grp = lambda i, j: ((i * tm) // rows_per_group, 0, j)
    return pl.pallas_call(
        _outproj_kernel,
        grid=(m // tm, d // tn),
        in_specs=[pl.BlockSpec((tm, kdim), lambda i, j: (i, 0)),
                  pl.BlockSpec((kdim, tn), lambda i, j: (0, j)),
                  pl.BlockSpec((tm, tn), lambda i, j: (i, j)),
                  pl.BlockSpec((1, 1, tn), grp)],
        out_specs=pl.BlockSpec((tm, tn), lambda i, j: (i, j)),
        out_shape=jax.ShapeDtypeStruct((m, d), F32),
        compiler_params=_params("parallel", "arbitrary"),
        name="outproj",
    )(a, w, x, gate)


def _swa_kernel(sink_ref, bias_ref, q_ref, k0_ref, k1_ref, k2_ref, v0_ref, v1_ref, v2_ref, kc_ref, vc_ref,
                o_ref):
    kvh = pl.program_id(1)
    qb = Q_BLOCK
    npair = SWA_GROUP // 2
    lane = lax.broadcasted_iota(I32, (qb, LANES), 1)
    q = q_ref[...]
    parts = []
    for p in range(npair):
        qp = q[:, p * LANES:(p + 1) * LANES]
        parts.append(jnp.where(lane < SWA_DH, qp, jnp.zeros_like(qp)))
        parts.append(jnp.where(lane >= SWA_DH, qp, jnp.zeros_like(qp)))
    qq = jnp.concatenate(parts, axis=0)
    k_lat = jnp.concatenate([k0_ref[...], k1_ref[...], k2_ref[...]], axis=0)
    v_lat = jnp.concatenate([v0_ref[...], v1_ref[...], v2_ref[...]], axis=0)
    s_lat = lax.dot_general(qq, k_lat, NT_DIMS, preferred_element_type=F32)
    s_ctx = lax.dot_general(qq, kc_ref[...], NT_DIMS, preferred_element_type=F32)
    bias = bias_ref[0]
    vc = vc_ref[...]
    outs = []
    for g in range(SWA_GROUP):
        rs = slice(g * qb, (g + 1) * qb)
        sl = s_lat[rs] + bias
        sc = s_ctx[rs]
        sink = sink_ref[kvh, g] * LOG2E
        m = jnp.maximum(jnp.maximum(jnp.max(sl, axis=-1, keepdims=True),
                                    jnp.max(sc, axis=-1, keepdims=True)), sink)
        el = jnp.exp2(sl - m)
        ec = jnp.exp2(sc - m)
        l = jnp.sum(el, axis=-1, keepdims=True) + jnp.sum(ec, axis=-1, keepdims=True) + jnp.exp2(sink - m)
        o = (jnp.dot(el.astype(BF16), v_lat, preferred_element_type=F32)
             + jnp.dot(ec.astype(BF16), vc, preferred_element_type=F32))
        outs.append(o * (1.0 / l))
    for p in range(npair):
        o_ref[:, p * LANES:(p + 1) * LANES] = jnp.where(lane < SWA_DH, outs[2 * p], outs[2 * p + 1]).astype(o_ref.dtype)


def _swa_attention(sink, q, kx, vx, kc, vc, b, t, n_c):
    qb = Q_BLOCK
    nb = t // qb
    gw = SWA_GROUP * SWA_DH
    lat = lambda off: pl.BlockSpec(
        (qb, LANES), lambda bi, h, i: (bi * nb + jnp.clip(i + off, 0, nb - 1), h))
    ctx = pl.BlockSpec((n_c, LANES), lambda bi, h, i: (bi, h))
    r_io = np.arange(qb)[:, None]
    c_io = np.arange(3 * qb)[None, :]
    inside = np.abs(r_io + qb - c_io) <= WINDOW
    variants = [inside & ((c_io >= qb) | (v & 1 == 0)) & ((c_io < 2 * qb) | (v & 2 == 0)) for v in range(4)]
    bias = jnp.asarray(np.where(np.stack(variants), 0.0, -np.inf), F32)
    return pl.pallas_call(
        _swa_kernel,
        grid=(b, SWA_KV_HEADS, nb),
        in_specs=[pl.BlockSpec(memory_space=pltpu.SMEM),
                  pl.BlockSpec((1, qb, 3 * qb),
                               lambda bi, h, i: ((i == 0).astype(I32) + 2 * (i == nb - 1).astype(I32), 0, 0)),
                  pl.BlockSpec((qb, gw), lambda bi, h, i: (bi * nb + i, h)),
                  lat(-1), lat(0), lat(1), lat(-1), lat(0), lat(1), ctx, ctx],
        out_specs=pl.BlockSpec((qb, gw), lambda bi, h, i: (bi * nb + i, h)),
        out_shape=jax.ShapeDtypeStruct((b * t, SWA_HEADS * SWA_DH), BF16),
        compiler_params=_params("parallel", "parallel", "arbitrary"),
        name="swa_attention",
    )(sink, bias, q, kx, kx, kx, vx, vx, vx, kc, vc)


def _dup_heads(a, col0):
    m = a.shape[0]
    h = a[:, col0:col0 + SWA_KV_HEADS * SWA_DH].reshape(m, SWA_KV_HEADS, 1, SWA_DH)
    return jnp.broadcast_to(h, (m, SWA_KV_HEADS, LANES // SWA_DH, SWA_DH)).reshape(m, SWA_KV_HEADS * LANES)


def _first_max(vals, iota, size, axis):
    m = jnp.max(vals, axis=axis, keepdims=True)
    first = jnp.min(jnp.where(vals == m, iota, size), axis=axis, keepdims=True)
    return m, first


def _router_kernel(x_ref, nw_ref, sh_ref, sc_ref, rwt_ref, rb_ref, c0_ref,
                   hp_ref, idx_ref, wt_ref, rank_ref, cnt_ref, carry_ref):
    @pl.when(pl.program_id(0) == 0)
    def _():
        carry_ref[...] = c0_ref[...]

    h = _norm_mod(x_ref[...], nw_ref[...], sh_ref[0], sc_ref[0])
    tm, d = h.shape
    _store_token_tiles(hp_ref, 0, _pack_pairs(h[:, :d // 2], h[:, d // 2:]))
    scores = jax.nn.sigmoid(_dot3(rwt_ref[...], h, NT_DIMS))
    sel = scores + rb_ref[...]
    neg = jnp.float32(-jnp.inf)

    g_io = lax.broadcasted_iota(I32, (GROUP_SIZE, tm), 0)
    gs_rows = []
    for g in range(N_GROUPS):
        sg = sel[g * GROUP_SIZE:(g + 1) * GROUP_SIZE]
        m1, f1 = _first_max(sg, g_io, GROUP_SIZE, 0)
        m2 = jnp.max(jnp.where(g_io == f1, neg, sg), axis=0, keepdims=True)
        gs_rows.append(m1 + m2)
    cur = jnp.concatenate(gs_rows, axis=0)
    n_io = lax.broadcasted_iota(I32, (N_GROUPS, tm), 0)
    gmask = jnp.zeros((N_GROUPS, tm), I32)
    for _ in range(TOPK_GROUPS):
        _, f = _first_max(cur, n_io, N_GROUPS, 0)
        hit = n_io == f
        gmask = jnp.where(hit, 1, gmask)
        cur = jnp.where(hit, neg, cur)
    cur = jnp.concatenate(
        [jnp.where(gmask[g:g + 1] > 0, sel[g * GROUP_SIZE:(g + 1) * GROUP_SIZE], neg) for g in range(N_GROUPS)],
        axis=0)

    e_io = lax.broadcasted_iota(I32, (N_EXPERTS, tm), 0)
    chosen = jnp.zeros((N_EXPERTS, tm), F32)
    idx_rows, w_rows = [], []
    for _ in range(TOP_K):
        _, f = _first_max(cur, e_io, N_EXPERTS, 0)
        hit = e_io == f
        idx_rows.append(f)
        w_rows.append(jnp.sum(jnp.where(hit, scores, 0.0), axis=0, keepdims=True))
        chosen = jnp.where(hit, 1.0, chosen)
        cur = jnp.where(hit, neg, cur)
    idx = jnp.concatenate(idx_rows, axis=0)
    w = jnp.concatenate(w_rows, axis=0)
    idx_ref[...] = idx
    wt_ref[...] = w * (1.0 / jnp.sum(w, axis=0, keepdims=True)) * ROUTED_SCALE

    onehot = chosen.astype(BF16)
    before = (lax.broadcasted_iota(I32, (tm, tm), 0) < lax.broadcasted_iota(I32, (tm, tm), 1)).astype(BF16)
    base = carry_ref[:, 0:1] + jnp.dot(onehot, before, preferred_element_type=F32)
    rank_ref[...] = jnp.concatenate(
        [jnp.sum(jnp.where(e_io == idx_rows[k], base, 0.0), axis=0, keepdims=True) for k in range(TOP_K)],
        axis=0).astype(I32)
    carry_ref[...] = carry_ref[...] + jnp.sum(chosen, axis=1, keepdims=True)
    cnt_ref[...] = carry_ref[...]


def _router(x, nw, sh, sc, rwt, rb, counts0, rows_per_group):
    m, d = x.shape
    tm = math.gcd(_tile(m, 256, LANES), rows_per_group)
    assert (d // 2) % LANES == 0
    tr = d // 2 // LANES
    grp = lambda i: ((i * tm) // rows_per_group, 0, 0)
    tok = lambda rows: pl.BlockSpec((rows, tm), lambda i: (0, i))
    return pl.pallas_call(
        _router_kernel,
        grid=(m // tm,),
        in_specs=[pl.BlockSpec((tm, d), lambda i: (i, 0)),
                  pl.BlockSpec((1, d), lambda i: (0, 0)),
                  pl.BlockSpec((1, 1, d), grp),
                  pl.BlockSpec((1, 1, d), grp),
                  pl.BlockSpec((N_EXPERTS, d), lambda i: (0, 0)),
                  pl.BlockSpec((N_EXPERTS, 1), lambda i: (0, 0)),
                  pl.BlockSpec((N_EXPERTS, LANES), lambda i: (0, 0))],
        out_specs=[pl.BlockSpec((tm * tr, LANES), lambda i: (i, 0)),
                   tok(TOP_K), tok(TOP_K), tok(TOP_K),
                   pl.BlockSpec((N_EXPERTS, LANES), lambda i: (0, 0))],
        out_shape=[jax.ShapeDtypeStruct((m * tr, LANES), U32),
                   jax.ShapeDtypeStruct((TOP_K, m), I32),
                   jax.ShapeDtypeStruct((TOP_K, m), F32),
                   jax.ShapeDtypeStruct((TOP_K, m), I32),
                   jax.ShapeDtypeStruct((N_EXPERTS, LANES), F32)],
        scratch_shapes=[pltpu.VMEM((N_EXPERTS, LANES), F32)],
        compiler_params=_params("arbitrary"),
        name="moe_router",
    )(x, nw.reshape(1, d), sh, sc, rwt, rb.reshape(N_EXPERTS, 1), counts0)


def _dispatch_kernel(nv_ref, dest_ref, hp_ref, xs_ref, zero_ref, sem, *, tr):
    tm = hp_ref.shape[0] // tr
    blk_rows = MOE_BLOCK * tr

    @pl.when(pl.program_id(0) == 0)
    def _():
        zero_ref[...] = jnp.zeros(zero_ref.shape, U32)

        def fill(blk, carry):
            @pl.when(nv_ref[blk] < MOE_BLOCK)
            def _():
                cp = pltpu.make_async_copy(
                    zero_ref, xs_ref.at[pl.ds(pl.multiple_of(blk * blk_rows, blk_rows), blk_rows)], sem)
                cp.start()
                cp.wait()
            return carry

        lax.fori_loop(0, nv_ref.shape[0], fill, 0)

    def copy(t, k):
        return _tile_copy(hp_ref, t * tr, xs_ref, dest_ref[t * TOP_K + k], tr, sem)

    def start(t, carry):
        for k in range(TOP_K):
            copy(t, k).start(priority=k % 2)
        return carry

    def wait(t, carry):
        for k in range(TOP_K):
            copy(t, k).wait()
        return carry

    lax.fori_loop(0, tm, start, 0)
    lax.fori_loop(0, tm, wait, 0)


def _dispatch(block_nv, dest, hp, n_rows, tr):
    m = hp.shape[0] // tr
    tm = _tile(m, 512, LANES)
    return pl.pallas_call(
        functools.partial(_dispatch_kernel, tr=tr),
        grid_spec=pltpu.PrefetchScalarGridSpec(
            num_scalar_prefetch=1,
            grid=(m // tm,),
            in_specs=[pl.BlockSpec((tm * TOP_K,), lambda i, nv: (i,), memory_space=pltpu.SMEM),
                      pl.BlockSpec((tm * tr, LANES), lambda i, nv: (i, 0))],
            out_specs=pl.BlockSpec(memory_space=pl.ANY),
            scratch_shapes=[pltpu.VMEM((MOE_BLOCK * tr, LANES), U32), pltpu.SemaphoreType.DMA(())]),
        out_shape=jax.ShapeDtypeStruct((n_rows * tr, LANES), U32),
        compiler_params=_params("arbitrary"),
        name="moe_dispatch",
    )(block_nv, dest, hp)


def _gffn_kernel(be_ref, nv_ref, xs_ref, w1_ref, w3_ref, w2_ref, ys_ref, w1b_ref, w3b_ref, w2b_ref, *, tr):
    i = pl.program_id(0)
    nv = nv_ref[i]
    prev = be_ref[jnp.maximum(i - 1, 0)]

    @pl.when((i == 0) | (be_ref[i] != prev))
    def _():
        w1b_ref[...] = w1_ref[0, 0].astype(BF16)
        w3b_ref[...] = w3_ref[0, 0].astype(BF16)
        w2b_ref[...] = w2_ref[0, 0].astype(BF16)

    @pl.when(nv > 0)
    def _():
        lo, hi = _unpack_pairs(_load_token_tiles(xs_ref, 0, MOE_BLOCK, tr))
        dh = lo.shape[1]
        lo = lo.astype(BF16)
        hi = hi.astype(BF16)
        d = functools.partial(jnp.dot, preferred_element_type=F32)
        a = d(lo, w1b_ref[:dh, :]) + d(hi, w1b_ref[dh:, :])
        g = d(lo, w3b_ref[:dh, :]) + d(hi, w3b_ref[dh:, :])
        y = d((_silu(a) * g).astype(BF16), w2b_ref[...])
        _store_token_tiles(ys_ref, 0, _pack_pairs(y[:, :dh], y[:, dh:]))

    @pl.when(nv == 0)
    def _():
        ys_ref[...] = jnp.zeros(ys_ref.shape, U32)


def _grouped_ffn(block_e, block_nv, xs, w1, w3, w2, layer, tr):
    _, _, d, f = w1.shape
    blk_rows = MOE_BLOCK * tr
    nb = xs.shape[0] // blk_rows
    return pl.pallas_call(
        functools.partial(_gffn_kernel, tr=tr),
        grid_spec=pltpu.PrefetchScalarGridSpec(
            num_scalar_prefetch=2,
            grid=(nb,),
            in_specs=[pl.BlockSpec((blk_rows, LANES), lambda i, be, nv: (i, 0)),
                      pl.BlockSpec((1, 1, d, f), lambda i, be, nv: (layer, be[i], 0, 0)),
                      pl.BlockSpec((1, 1, d, f), lambda i, be, nv: (layer, be[i], 0, 0)),
                      pl.BlockSpec((1, 1, f, d), lambda i, be, nv: (layer, be[i], 0, 0))],
            out_specs=pl.BlockSpec((blk_rows, LANES), lambda i, be, nv: (i, 0)),
            scratch_shapes=[pltpu.VMEM((d, f), BF16), pltpu.VMEM((d, f), BF16), pltpu.VMEM((f, d), BF16)]),
        out_shape=jax.ShapeDtypeStruct(xs.shape, U32),
        compiler_params=_params("arbitrary"),
        name="moe_grouped_ffn",
    )(block_e, block_nv, xs, w1, w3, w2)


def _combine_kernel(dest_ref, dest_next_ref, x_ref, hp_ref, wt_ref, g_ref, ws1_ref, ws3_ref, ws2_ref, ys_ref,
                    o_ref, buf_a, buf_b, sem_a, sem_b, *, tr):
    i = pl.program_id(0)
    tm = x_ref.shape[0]
    dh = tr * LANES

    def copy(dref, t, k, buf, sem):
        return _tile_copy(ys_ref, dref[t * TOP_K + k], buf, (k * tm + t) * tr, tr, sem)

    def wait_all(buf, sem):
        def wait(t, carry):
            for k in range(TOP_K):
                copy(dest_ref, t, k, buf, sem).wait()
            return carry
        lax.fori_loop(0, tm, wait, 0)

    @pl.when(i == 0)
    def _():
        def start(t, carry):
            for k in range(TOP_K):
                copy(dest_ref, t, k, buf_a, sem_a).start(priority=k % 2)
            return carry
        lax.fori_loop(0, tm, start, 0)

    def step(buf, sem, buf_next, sem_next):
        wait_all(buf, sem)
        for t in range(tm):
            for k in range(TOP_K):
                copy(dest_next_ref, t, k, buf_next, sem_next).start(priority=k % 2)
        lo, hi = _unpack_pairs(_load_token_tiles(hp_ref, 0, tm, tr))
        lo = lo.astype(BF16)
        hi = hi.astype(BF16)
        d = functools.partial(jnp.dot, preferred_element_type=F32)
        a = d(lo, ws1_ref[:dh, :]) + d(hi, ws1_ref[dh:, :])
        g = d(lo, ws3_ref[:dh, :]) + d(hi, ws3_ref[dh:, :])
        shared = d((_silu(a) * g).astype(BF16), ws2_ref[...])
        acc_lo = jnp.zeros((tm, dh), F32)
        acc_hi = jnp.zeros((tm, dh), F32)
        for k in range(TOP_K):
            ylo, yhi = _unpack_pairs(_load_token_tiles(buf, k * tm * tr, tm, tr))
            wk = wt_ref[:, k:k + 1]
            acc_lo = acc_lo + wk * ylo
            acc_hi = acc_hi + wk * yhi
        o_ref[:, :dh] = x_ref[:, :dh] + g_ref[0][:, :dh] * (acc_lo + shared[:, :dh])
        o_ref[:, dh:] = x_ref[:, dh:] + g_ref[0][:, dh:] * (acc_hi + shared[:, dh:])

        @pl.when(i == pl.num_programs(0) - 1)
        def _():
            wait_all(buf_next, sem_next)

    @pl.when(i % 2 == 0)
    def _():
        step(buf_a, sem_a, buf_b, sem_b)

    @pl.when(i % 2 == 1)
    def _():
        step(buf_b, sem_b, buf_a, sem_a)


def _combine(dest, x, hp, wt, gate, ws1, ws3, ws2, ys, rows_per_group, tr):
    m, d = x.shape
    f = ws1.shape[1]
    tm = math.gcd(_tile(m, 128, LANES), rows_per_group)
    nt = m // tm
    grp = lambda i: ((i * tm) // rows_per_group, 0, 0)
    buf = pltpu.VMEM((TOP_K * tm * tr, LANES), U32)
    return pl.pallas_call(
        functools.partial(_combine_kernel, tr=tr),
        grid=(nt,),
        in_specs=[pl.BlockSpec((tm * TOP_K,), lambda i: (i,), memory_space=pltpu.SMEM),
                  pl.BlockSpec((tm * TOP_K,), lambda i: (jnp.minimum(i + 1, nt - 1),), memory_space=pltpu.SMEM),
                  pl.BlockSpec((tm, d), lambda i: (i, 0)),
                  pl.BlockSpec((tm * tr, LANES), lambda i: (i, 0)),
                  pl.BlockSpec((tm, TOP_K), lambda i: (i, 0)),
                  pl.BlockSpec((1, 1, d), grp),
                  pl.BlockSpec((d, f), lambda i: (0, 0)),
                  pl.BlockSpec((d, f), lambda i: (0, 0)),
                  pl.BlockSpec((f, d), lambda i: (0, 0)),
                  pl.BlockSpec(memory_space=pl.ANY)],
        out_specs=pl.BlockSpec((tm, d), lambda i: (i, 0)),
        out_shape=jax.ShapeDtypeStruct((m, d), F32),
        scratch_shapes=[buf, buf, pltpu.SemaphoreType.DMA(()), pltpu.SemaphoreType.DMA(())],
        compiler_params=_params("arbitrary"),
        name="moe_combine",
    )(dest, dest, x, hp, wt, gate, ws1, ws3, ws2, ys)


def _moe(streams, layer, nw, rw, rb, w1, w3, w2, ws1, ws3, ws2):
    d = streams[0][0].shape[1]
    rwt = rw.T
    counts = jnp.zeros((N_EXPERTS, LANES), F32)
    routed = []
    for x, sh, sc, _, rpg in streams:
        hp, idx, wt, rank, counts = _router(x, nw, sh, sc, rwt, rb, counts, rpg)
        routed.append((hp, idx, wt, rank))
    n_assign = sum(s[0].shape[0] for s in streams) * TOP_K
    n_blocks = (n_assign + N_EXPERTS * (MOE_BLOCK - 1) + MOE_BLOCK - 1) // MOE_BLOCK
    cnt = counts[:, 0].astype(I32)
    padded = (cnt + MOE_BLOCK - 1) // MOE_BLOCK * MOE_BLOCK
    pad_end = jnp.cumsum(padded)
    pad_start = pad_end - padded
    bstart = jnp.arange(n_blocks, dtype=I32) * MOE_BLOCK
    block_e = jnp.minimum(jnp.sum((bstart[:, None] >= pad_end[None, :]).astype(I32), axis=1), N_EXPERTS - 1)
    block_nv = jnp.clip(cnt[block_e] - (bstart - pad_start[block_e]), 0, MOE_BLOCK).astype(I32)
    e_ar = jnp.arange(N_EXPERTS, dtype=I32)
    tr = d // 2 // LANES
    dests = [((jnp.sum(jnp.where(idx[:, :, None] == e_ar, pad_start, 0), axis=-1) + rank) * tr).T.reshape(-1)
             for _, idx, _, rank in routed]
    if len(streams) > 1:
        hp_all = jnp.concatenate([r[0] for r in routed], axis=0)
        dest_all = jnp.concatenate(dests, axis=0)
    else:
        hp_all, dest_all = routed[0][0], dests[0]
    xs = _dispatch(block_nv, dest_all, hp_all, n_blocks * MOE_BLOCK, tr)
    ys = _grouped_ffn(block_e, block_nv, xs, w1, w3, w2, layer, tr)
    ws1b, ws3b, ws2b = ws1.astype(BF16), ws3.astype(BF16), ws2.astype(BF16)
    return [_combine(dest, x, hp, wt.T, gate, ws1b, ws3b, ws2b, ys, rpg, tr)
            for (x, _, _, gate, rpg), (hp, _, wt, _), dest in zip(streams, routed, dests)]


def _mixer_ab(xs, cs, mx, mc, b, t, n_c, layer, nw, w_in, w_out, q_norm, k_norm, lam_vec, subln, conv_w,
              a_log, dt_bias, out_norm):
    n_main = w_in.shape[1] - 4 * DN_HEADS
    assert n_main % LANES == 0
    w_main = w_in[:, :n_main].astype(BF16)
    w_small = jnp.zeros((w_in.shape[0], LANES), F32).at[:, :4 * DN_HEADS].set(w_in[:, n_main:])
    nqk = 2 * DIFF_HEADS * DIFF_DK
    assert n_main % nqk == 0
    tabs = _rope_tables(t, DIFF_DK)
    hw = jnp.stack([q_norm * (DIFF_DK ** -0.5 * LOG2E), k_norm])
    groups = nqk // LANES
    blocks = ((0, 1, (0,) * groups), (1, 2, (1,) * groups))
    z_x, zs_x = _norm_mod_matmul(xs, nw, mx[0], mx[1], w_main, t, w_small, tn=nqk,
                                 heads=dict(hd=DIFF_DK, weights=hw, blocks=blocks, tables=tabs, t=t))
    z_c, zs_c = _norm_mod_matmul(cs, nw, mc[0], mc[1], w_main, b * n_c, w_small, tn=nqk,
                                 heads=dict(hd=DIFF_DK, weights=hw, blocks=blocks, tables=None, t=t))
    lam_init = 0.8 - 0.6 * math.exp(-0.3 * layer)
    od_x = _diff_attention(lam_vec, z_x, z_c, z_c, subln, lam_init, b, n_c, kx=z_x, zx=z_x, t=t)
    od_c = _diff_attention(lam_vec, z_c, z_c, z_c, subln, lam_init, b, n_c)
    dn_col0 = 2 * nqk + DIFF_HEADS * DIFF_DV
    dn_x = _dn_prep(z_x, dn_col0, conv_w, t)
    dn_c = _dn_prep(z_c, dn_col0, conv_w, n_c)
    zst_x = zs_x[:, :4 * DN_HEADS].T
    zst_c = zs_c[:, :4 * DN_HEADS].T
    prm = _deltanet_params(a_log, dt_bias)
    s0 = jnp.zeros((b, DN_HEADS, DN_DK, DN_DV), F32)
    o_cf, o_cb, s_cf, s_cb = _deltanet(dn_c, zs_c, zst_c, prm, s0, s0, b, n_c)
    o_xf, o_xb, _, _ = _deltanet(dn_x, zs_x, zst_x, prm, s_cf, s_cb, b, t)
    gate_col0 = dn_col0 + conv_w.shape[1]
    w_out_b = w_out.astype(BF16)
    x1 = _outproj_ab(od_x, o_xf, o_xb, z_x, gate_col0, out_norm, w_out_b, xs, mx[2], t)
    c1 = _outproj_ab(od_c, o_cf, o_cb, z_c, gate_col0, out_norm, w_out_b, cs, mc[2], b * n_c)
    return x1, c1


def _mixer_swa(xs, cs, mx, mc, b, t, n_c, nw, w_in, w_out, q_norm, k_norm, sink):
    nq = SWA_HEADS * SWA_DH
    nkv = SWA_KV_HEADS * SWA_DH
    w_b = w_in.astype(BF16)
    tn = 2 * nkv
    assert nq % tn == 0
    tabs = _rope_tables(t, SWA_DH)
    rep = LANES // SWA_DH
    hw = jnp.stack([jnp.tile(q_norm, rep) * (SWA_DH ** -0.5 * LOG2E), jnp.tile(k_norm, rep)])
    q_rows = (0,) * (tn // LANES)
    kv_rows = (1,) * (nkv // LANES) + (None,) * (nkv // LANES)
    z_x = _norm_mod_matmul(xs, nw, mx[0], mx[1], w_b, t, tn=tn,
                           heads=dict(hd=SWA_DH, weights=hw, tables=tabs, t=t,
                                      blocks=((0, nq // tn, q_rows), (nq // tn, nq // tn + 1, kv_rows))))
    z_c = _norm_mod_matmul(cs, nw, mc[0], mc[1], w_b[:, nq:], b * n_c, tn=tn,
                           heads=dict(hd=SWA_DH, weights=hw, tables=None, t=t, blocks=((0, 1, kv_rows),)))
    att = _swa_attention(sink.reshape(SWA_KV_HEADS, SWA_GROUP), z_x, _dup_heads(z_x, nq), _dup_heads(z_x, nq + nkv),
                         _dup_heads(z_c, 0), _dup_heads(z_c, nkv), b, t, n_c)
    return _outproj(att, w_out.astype(BF16), xs, mx[2], t)


def kernel(x, c, ctx, c_ctx, mod_w, mod_b, norm_mix, norm_ffn, ab_w_in, ab_w_out, diff_q_norm, diff_k_norm, diff_lambda, diff_subln, dn_conv, dn_a_log, dn_dt_bias, dn_out_norm, swa_w_in, swa_w_out, swa_q_norm, swa_k_norm, swa_sink, router_w, router_bias, exp_w1, exp_w3, exp_w2, shared_w1, shared_w3, shared_w2):
    b, t, d = x.shape
    n_c = ctx.shape[1]
    depth = mod_w.shape[0]
    assert depth == 2 and t % Q_BLOCK == 0 and t % DN_CHUNK == 0 and n_c % DN_CHUNK == 0
    xs = x.reshape(b * t, d)
    cs = ctx.reshape(b * n_c, d)
    n_mod = -(-(b + 1) // SUBLANES) * SUBLANES
    a_mod = jnp.zeros((n_mod, d), F32).at[0].set(c_ctx).at[1:1 + b].set(c)
    for layer in range(depth):
        with_ctx = layer < depth - 1
        p = layer // 2
        mod = _modulation(a_mod, mod_w, layer, mod_b[layer])
        mc = [mod[0:1, j * d:(j + 1) * d].reshape(1, 1, d) for j in range(6)]
        mx = [mod[1:1 + b, j * d:(j + 1) * d].reshape(b, 1, d) for j in range(6)]
        if layer % 2 == 0:
            xs, c_new = _mixer_ab(xs, cs, mx, mc, b, t, n_c, layer, norm_mix[layer], ab_w_in[p], ab_w_out[p],
                                  diff_q_norm[p], diff_k_norm[p], diff_lambda[p], diff_subln[p], dn_conv[p],
                                  dn_a_log[p], dn_dt_bias[p], dn_out_norm[p])
        else:
            assert not with_ctx
            xs = _mixer_swa(xs, cs, mx, mc, b, t, n_c, norm_mix[layer], swa_w_in[p], swa_w_out[p],
                            swa_q_norm[p], swa_k_norm[p], swa_sink[p])
            c_new = None
        moe_w = (layer, norm_ffn[layer], router_w[layer], router_bias[layer], exp_w1, exp_w3, exp_w2,
                 shared_w1[layer], shared_w3[layer], shared_w2[layer])
        if with_ctx:
            cs, xs = _moe([(c_new, mc[3], mc[4], mc[5], b * n_c), (xs, mx[3], mx[4], mx[5], t)], *moe_w)
        else:
            (xs,) = _moe([(xs, mx[3], mx[4], mx[5], t)], *moe_w)
    return xs.reshape(b, t, d)
```

```python
import functools
import math

import jax
import jax.numpy as jnp
import numpy as np
from jax import lax
from jax.experimental import pallas as pl
from jax.experimental.pallas import tpu as pltpu

F32 = jnp.float32
BF16 = jnp.bfloat16
I32 = jnp.int32
U32 = jnp.uint32

EPS = 1e-6
GRID_W = 64
ROPE_BASE = 10000.0
DIFF_HEADS = 4
DIFF_DK = 128
DIFF_DV = 256
DN_HEADS = 8
DN_DK = 128
DN_DV = 128
DN_CONV = 5
DN_CHUNK = 128
SWA_HEADS = 32
SWA_KV_HEADS = 4
SWA_GROUP = SWA_HEADS // SWA_KV_HEADS
SWA_DH = 64
WINDOW = 128
Q_BLOCK = 128
N_EXPERTS = 64
TOP_K = 8
N_GROUPS = 8
GROUP_SIZE = N_EXPERTS // N_GROUPS
TOPK_GROUPS = 4
ROUTED_SCALE = 2.5
MOE_BLOCK = 512

LANES = 128
SUBLANES = 8
VMEM_LIMIT_BYTES = 56 * 1024 * 1024

NT_DIMS = (((1,), (1,)), ((), ()))
LOG2E = math.log2(math.e)


def _params(*semantics):
    return pltpu.CompilerParams(dimension_semantics=semantics, vmem_limit_bytes=VMEM_LIMIT_BYTES)


def _tile(n, pref, mult):
    if n <= pref:
        return n
    t = pref - pref % mult
    while t > mult and n % t:
        t -= mult
    assert n % t == 0, (n, pref, mult)
    return t


def _mm(a, b):
    return jnp.dot(a.astype(BF16), b.astype(BF16), preferred_element_type=F32)


def _split2(x):
    hi = x.astype(BF16)
    lo = (x - hi.astype(F32)).astype(BF16)
    return hi, lo


def _dot3(a, b, dims=None):
    if dims is None:
        dims = (((a.ndim - 1,), (0,)), ((), ()))
    ah, al = _split2(a)
    bh, bl = _split2(b)
    d = functools.partial(lax.dot_general, dimension_numbers=dims, preferred_element_type=F32)
    return d(ah, bh) + d(ah, bl) + d(al, bh)


def _silu(x):
    return x * jax.nn.sigmoid(x)


def _softplus(x):
    return jnp.maximum(x, 0.0) + jnp.log(1.0 + jnp.exp(-jnp.abs(x)))


def _pack_pairs(lo, hi):
    ulo = lax.bitcast_convert_type(lo.astype(BF16).astype(F32), U32) >> 16
    uhi = lax.bitcast_convert_type(hi.astype(BF16).astype(F32), U32) & jnp.uint32(0xFFFF0000)
    return ulo | uhi


def _unpack_pairs(u):
    lo = lax.bitcast_convert_type(u << 16, F32)
    hi = lax.bitcast_convert_type(u & jnp.uint32(0xFFFF0000), F32)
    return lo, hi


def _store_token_tiles(ref, base, packed):
    n, width = packed.shape
    tr = width // LANES
    for j in range(tr):
        ref[pl.ds(base + j, n, stride=tr), :] = packed[:, j * LANES:(j + 1) * LANES]


def _load_token_tiles(ref, base, n, tr):
    return jnp.concatenate([ref[pl.ds(base + j, n, stride=tr), :] for j in range(tr)], axis=1)


def _tile_copy(src, src_row, dst, dst_row, tr, sem):
    return pltpu.make_async_copy(src.at[pl.ds(pl.multiple_of(src_row, tr), tr)],
                                 dst.at[pl.ds(pl.multiple_of(dst_row, tr), tr)], sem)


def _mod_kernel(a_ref, w_ref, b_ref, o_ref):
    o_ref[...] = _dot3(_silu(a_ref[...]), w_ref[0]) + b_ref[...]


def _modulation(a, w_all, layer, b):
    r, d = a.shape
    n = w_all.shape[2]
    tn = _tile(n, 768, LANES)
    return pl.pallas_call(
        _mod_kernel,
        grid=(n // tn,),
        in_specs=[pl.BlockSpec((r, d), lambda j: (0, 0)),
                  pl.BlockSpec((1, d, tn), lambda j: (layer, 0, j)),
                  pl.BlockSpec((1, tn), lambda j: (0, j))],
        out_specs=pl.BlockSpec((r, tn), lambda j: (0, j)),
        out_shape=jax.ShapeDtypeStruct((r, n), F32),
        compiler_params=_params("parallel"),
        name="modulation",
    )(a, w_all, b.reshape(1, n))


def _norm_mod(x, nw, sh, sc):
    y = x * lax.rsqrt(jnp.mean(x * x, axis=-1, keepdims=True) + EPS) * nw
    return y * (1.0 + sc) + sh


def _store_head_groups(o_ref, ys, rows, hw_ref, seg_ref, swap_ref, cos, sin, hd):
    normed = [g for g in range(len(ys)) if rows[g] is not None]
    if normed:
        seg = seg_ref[...]
        sums = {g: jnp.dot((ys[g] * ys[g]).astype(BF16), seg, preferred_element_type=F32) for g in normed}
        for g in normed:
            ys[g] = ys[g] * lax.rsqrt(sums[g] * (1.0 / hd) + EPS) * hw_ref[rows[g]]
        if cos is not None:
            swap = swap_ref[...]
            partners = {g: jnp.dot(ys[g].astype(BF16), swap, preferred_element_type=F32) for g in normed}
            for g in normed:
                ys[g] = ys[g] * cos + partners[g] * sin
    for g, y in enumerate(ys):
        o_ref[:, g * LANES:(g + 1) * LANES] = y.astype(o_ref.dtype)


def _head_matrices(hd):
    lane = np.arange(LANES)
    seg = (lane[:, None] // hd) == (lane[None, :] // hd)
    q = hd // 4
    partner = np.where((lane % (2 * q)) < q, lane + q, lane - q)
    swap = lane[:, None] == partner[None, :]
    return jnp.asarray(seg, BF16), jnp.asarray(swap, BF16)


def _nmm_kernel(x_ref, nw_ref, sh_ref, sc_ref, w_ref, *rest, has_small, head_blocks, hd, rope, single):
    rest = list(rest)
    ws_ref = rest.pop(0) if has_small else None
    hw_ref, seg_ref, swap_ref = (rest.pop(0), rest.pop(0), rest.pop(0)) if head_blocks else (None, None, None)
    cos_ref, sin_ref = (rest.pop(0), rest.pop(0)) if rope else (None, None)
    o_ref = rest.pop(0)
    os_ref = rest.pop(0) if has_small else None
    (h_ref,) = rest
    if single:
        h = _norm_mod(x_ref[...], nw_ref[...], sh_ref[0], sc_ref[0])
        hb = h.astype(BF16)
        if has_small:
            os_ref[...] = _dot3(h, ws_ref[...])
        rows = head_blocks[0][2] if head_blocks else (None,) * (o_ref.shape[1] // LANES)
        cos = cos_ref[...] if rope else None
        sin = sin_ref[...] if rope else None
        cw = 2 * LANES
        n_out = o_ref.shape[1]
        parts = [jnp.dot(hb, w_ref[:, c0:c0 + cw], preferred_element_type=F32) for c0 in range(0, n_out, cw)]
        ys = [parts[g // 2][:, (g % 2) * LANES:(g % 2 + 1) * LANES] for g in range(n_out // LANES)]
        _store_head_groups(o_ref, ys, rows, hw_ref, seg_ref, swap_ref, cos, sin, hd)
        return
    j = pl.program_id(1)

    @pl.when(j == 0)
    def _():
        h = _norm_mod(x_ref[...], nw_ref[...], sh_ref[0], sc_ref[0])
        h_ref[...] = h.astype(BF16)
        if has_small:
            os_ref[...] = _dot3(h, ws_ref[...])

    r = jnp.dot(h_ref[...], w_ref[...], preferred_element_type=F32)
    if not head_blocks:
        o_ref[...] = r.astype(o_ref.dtype)
        return
    tm, tn = r.shape
    plain = j >= 0
    for lo, hi, rows in head_blocks:
        inside = (j >= lo) & (j < hi)
        plain = plain & jnp.logical_not(inside)

        @pl.when(inside)
        def _(rows=rows):
            cos = cos_ref[...] if rope else None
            sin = sin_ref[...] if rope else None
            ys = [r[:, g * LANES:(g + 1) * LANES] for g in range(tn // LANES)]
            _store_head_groups(o_ref, ys, rows, hw_ref, seg_ref, swap_ref, cos, sin, hd)

    @pl.when(plain)
    def _():
        o_ref[...] = r.astype(o_ref.dtype)


def _norm_mod_matmul(x, nw, sh, sc, w, rows_per_group, w_small=None, tn=1024, heads=None):
    m, d = x.shape
    n = w.shape[1]
    tn = _tile(n, tn, LANES)
    single = tn == n and n % (2 * LANES) == 0
    tm = math.gcd(_tile(m, 256 if single else 512, SUBLANES), rows_per_group)
    has_small = w_small is not None
    grp = lambda i, j: ((i * tm) // rows_per_group, 0, 0)
    in_specs = [pl.BlockSpec((tm, d), lambda i, j: (i, 0)),
                pl.BlockSpec((1, d), lambda i, j: (0, 0)),
                pl.BlockSpec((1, 1, d), grp),
                pl.BlockSpec((1, 1, d), grp),
                pl.BlockSpec((d, tn), lambda i, j: (0, j))]
    args = [x, nw.reshape(1, d), sh, sc, w]
    out_specs = [pl.BlockSpec((tm, tn), lambda i, j: (i, j))]
    out_shape = [jax.ShapeDtypeStruct((m, n), BF16)]
    if has_small:
        in_specs.append(pl.BlockSpec((d, LANES), lambda i, j: (0, 0)))
        args.append(w_small)
        out_specs.append(pl.BlockSpec((tm, LANES), lambda i, j: (i, 0)))
        out_shape.append(jax.ShapeDtypeStruct((m, LANES), F32))
    head_blocks, hd, rope = (), LANES, False
    if heads is not None:
        head_blocks, hd = tuple(heads["blocks"]), heads["hd"]
        hw = heads["weights"]
        in_specs += [pl.BlockSpec((hw.shape[0], 1, LANES), lambda i, j: (0, 0, 0)),
                     pl.BlockSpec((LANES, LANES), lambda i, j: (0, 0)),
                     pl.BlockSpec((LANES, LANES), lambda i, j: (0, 0))]
        args += [hw.reshape(hw.shape[0], 1, LANES), *_head_matrices(hd)]
        if heads["tables"] is not None:
            rope = True
            nt = heads["t"] // tm
            assert heads["t"] % tm == 0
            in_specs += [pl.BlockSpec((tm, LANES), lambda i, j: (i % nt, 0))] * 2
            args += list(heads["tables"])
    outs = pl.pallas_call(
        functools.partial(_nmm_kernel, has_small=has_small, head_blocks=head_blocks, hd=hd, rope=rope,
                          single=single),
        grid=(m // tm, n // tn),
        in_specs=in_specs,
        out_specs=out_specs,
        out_shape=out_shape,
        scratch_shapes=[pltpu.VMEM((tm, d), BF16)],
        compiler_params=_params("parallel", "arbitrary"),
        name="norm_mod_matmul",
    )(*args)
    return outs if has_small else outs[0]


def _rope_tables(t, head_dim):
    q = head_dim // 4
    pos = jnp.arange(t, dtype=I32)
    row = (pos // GRID_W).astype(F32)
    col = (pos % GRID_W).astype(F32)
    axis_dim = head_dim // 2
    inv_freq = ROPE_BASE ** (-jnp.arange(0, axis_dim, 2, dtype=F32) / axis_dim)
    lane = jnp.arange(LANES) % head_dim
    freq = inv_freq[lane % q]
    p = jnp.where((lane < head_dim // 2)[None, :], row[:, None], col[:, None])
    ang = p * freq[None, :]
    sign = jnp.where((lane % (2 * q)) < q, -1.0, 1.0)
    return jnp.cos(ang), jnp.sin(ang) * sign[None, :]


def _diff_attn_kernel(lv_ref, q_ref, *rest, lam_init, has_lat):
    if has_lat:
        kx_ref, vx_ref, kc_ref, vc_ref, w_ref, o_ref = rest
    else:
        kc_ref, vc_ref, w_ref, o_ref = rest
    lv = lv_ref[...]
    lam = (jnp.exp(jnp.sum(lv[0:1] * lv[1:2], keepdims=True))
           - jnp.exp(jnp.sum(lv[2:3] * lv[3:4], keepdims=True)) + lam_init)
    q = q_ref[...]

    def probs(c):
        sl = slice(c * DIFF_DK, (c + 1) * DIFF_DK)
        qc = q[:, sl]
        s_c = lax.dot_general(qc, kc_ref[:, sl], NT_DIMS, preferred_element_type=F32)
        m = jnp.max(s_c, axis=-1, keepdims=True)
        p_x = None
        if has_lat:
            s_x = lax.dot_general(qc, kx_ref[:, sl], NT_DIMS, preferred_element_type=F32)
            m = jnp.maximum(m, jnp.max(s_x, axis=-1, keepdims=True))
            p_x = jnp.exp2(s_x - m)
        p_c = jnp.exp2(s_c - m)
        l = jnp.sum(p_c, axis=-1, keepdims=True)
        if has_lat:
            l = l + jnp.sum(p_x, axis=-1, keepdims=True)
        return p_x, p_c, l

    p1x, p1c, l1 = probs(0)
    p2x, p2c, l2 = probs(1)
    ratio = lam * l1 * (1.0 / l2)
    o = jnp.dot((p1c - p2c * ratio).astype(BF16), vc_ref[...], preferred_element_type=F32)
    if has_lat:
        o = o + jnp.dot((p1x - p2x * ratio).astype(BF16), vx_ref[...], preferred_element_type=F32)
    o = o * (1.0 / l1)
    o = o * lax.rsqrt(jnp.mean(o * o, axis=-1, keepdims=True) + EPS)
    o_ref[...] = (o * w_ref[...] * (1.0 - lam_init)).astype(o_ref.dtype)


def _diff_attention(lam_vec, q, kc, zc, subln, lam_init, b, n_c, kx=None, zx=None, t=None):
    has_lat = kx is not None
    hw = 2 * DIFF_DK
    kblk0 = DIFF_HEADS
    vblk0 = (2 * DIFF_HEADS * hw) // DIFF_DV
    tq_all = t if has_lat else n_c
    tq = _tile(tq_all, 256, SUBLANES)
    nq = tq_all // tq
    in_specs = [pl.BlockSpec((4, DIFF_DK), lambda bi, h, qi: (0, 0)),
                pl.BlockSpec((tq, hw), lambda bi, h, qi: (bi * nq + qi, h))]
    args = [lam_vec, q]
    if has_lat:
        in_specs += [pl.BlockSpec((t, hw), lambda bi, h, qi: (bi, kblk0 + h)),
                     pl.BlockSpec((t, DIFF_DV), lambda bi, h, qi: (bi, vblk0 + h))]
        args += [kx, zx]
    in_specs += [pl.BlockSpec((n_c, hw), lambda bi, h, qi: (bi, kblk0 + h)),
                 pl.BlockSpec((n_c, DIFF_DV), lambda bi, h, qi: (bi, vblk0 + h)),
                 pl.BlockSpec((1, DIFF_DV), lambda bi, h, qi: (0, 0))]
    args += [kc, zc, subln.reshape(1, DIFF_DV)]
    return pl.pallas_call(
        functools.partial(_diff_attn_kernel, lam_init=lam_init, has_lat=has_lat),
        grid=(b, DIFF_HEADS, nq),
        in_specs=in_specs,
        out_specs=pl.BlockSpec((tq, DIFF_DV), lambda bi, h, qi: (bi * nq + qi, h)),
        out_shape=jax.ShapeDtypeStruct((b * tq_all, DIFF_HEADS * DIFF_DV), BF16),
        compiler_params=_params("parallel", "parallel", "arbitrary"),
        name="diff_attention",
    )(*args)


def _dn_prep_kernel(z_ref, cw_ref, o_ref, pad_ref, *, seg):
    halo = SUBLANES
    pad_ref[0:halo, :] = jnp.zeros((halo, LANES), F32)
    pad_ref[halo + seg:2 * halo + seg, :] = jnp.zeros((halo, LANES), F32)
    pad_ref[halo:halo + seg, :] = z_ref[...].astype(F32)
    kind = pl.program_id(1) // DN_HEADS
    qk_scale = jnp.where(kind == 0, DN_DK ** -0.5, 1.0).astype(F32)
    rows = _tile(seg, 256, SUBLANES)
    for r0 in range(0, seg, rows):
        acc = jnp.zeros((rows, LANES), F32)
        for j in range(DN_CONV):
            acc = acc + cw_ref[j:j + 1, :] * pad_ref[pl.ds(halo + r0 + j - DN_CONV // 2, rows), :]
        y = _silu(acc)
        nrm = y * lax.rsqrt(jnp.sum(y * y, axis=-1, keepdims=True) + EPS) * qk_scale
        o_ref[r0:r0 + rows, :] = jnp.where(kind < 2, nrm, y).astype(o_ref.dtype)


def _dn_prep(z, col0, conv_w, seg):
    m = z.shape[0]
    ncols = conv_w.shape[1]
    cblk0 = col0 // LANES
    cw = jnp.zeros((SUBLANES, ncols), F32).at[:DN_CONV].set(conv_w)
    return pl.pallas_call(
        functools.partial(_dn_prep_kernel, seg=seg),
        grid=(m // seg, ncols // LANES),
        in_specs=[pl.BlockSpec((seg, LANES), lambda s, g: (s, cblk0 + g)),
                  pl.BlockSpec((SUBLANES, LANES), lambda s, g: (0, g))],
        out_specs=pl.BlockSpec((seg, LANES), lambda s, g: (s, g)),
        out_shape=jax.ShapeDtypeStruct((m, ncols), BF16),
        scratch_shapes=[pltpu.VMEM((seg + 2 * SUBLANES, LANES), F32)],
        compiler_params=_params("parallel", "parallel"),
        name="dn_prep",
    )(z, cw)


def _dn_heads(reverse, q_ref, k_ref, v_ref, zs_ref, zst_ref, pr_ref, pca_ref, pcd_ref, o_ref, s_ref, ri, ci):
    c = DN_CHUNK
    if reverse:
        later, strict, later_t = ri <= ci, ri < ci, ri >= ci
    else:
        later, strict, later_t = ri >= ci, ri > ci, ri <= ci
    eye = (ri == ci).astype(F32)
    tri = later.astype(BF16)
    tri_t = later_t.astype(BF16)

    zs = zs_ref[...]
    beta_cols = jax.nn.sigmoid(zs)
    g_cols = -jnp.exp(pr_ref[0:1, :]) * _softplus(zs + pr_ref[1:2, :])
    g_hi = g_cols.astype(BF16)
    g_r1 = g_cols - g_hi.astype(F32)
    g_mid = g_r1.astype(BF16)
    g_lo = (g_r1 - g_mid.astype(F32)).astype(BF16)
    d = functools.partial(jnp.dot, preferred_element_type=F32)
    gc_cols = d(tri, g_hi) + d(tri, g_mid) + d(tri, g_lo)
    g_rows = -jnp.exp(pca_ref[...]) * _softplus(zst_ref[...] + pcd_ref[...])
    h_hi = g_rows.astype(BF16)
    h_r1 = g_rows - h_hi.astype(F32)
    h_mid = h_r1.astype(BF16)
    h_lo = (h_r1 - h_mid.astype(F32)).astype(BF16)
    gc_rows = d(h_hi, tri_t) + d(h_mid, tri_t) + d(h_lo, tri_t)

    dir_off = DN_HEADS if reverse else 0
    last = 0 if reverse else c - 1
    neg_inf = jnp.float32(-jnp.inf)
    heads = []
    for h in range(DN_HEADS):
        sl = slice(h * DN_DK, (h + 1) * DN_DK)
        cb = dir_off + h
        cg = 2 * DN_HEADS + dir_off + h
        beta = beta_cols[:, cb:cb + 1]
        gcol = gc_cols[:, cg:cg + 1]
        grow = gc_rows[cg:cg + 1, :]
        glast = grow[:, last:last + 1]
        q = q_ref[:, sl]
        k = k_ref[:, sl]
        kf = k.astype(F32)
        decay = jnp.exp(jnp.where(later, gcol - grow, neg_inf))
        kb = kf * beta
        both = lax.dot_general(jnp.concatenate([kb.astype(BF16), q], axis=0), k, NT_DIMS,
                               preferred_element_type=F32)
        lmat = jnp.where(strict, both[:c] * decay, 0.0)
        eg = jnp.exp(gcol)
        heads.append(dict(
            sl=sl, h=h, o_ref=o_ref, s_ref=s_ref, glast=glast, lmat=lmat, amat=both[c:] * decay,
            inv=eye - jnp.where((ri >> 1) == (ci >> 1), lmat, 0.0),
            rhs=jnp.concatenate([v_ref[:, sl].astype(F32) * beta, kb * eg], axis=1).astype(BF16),
            qe=(q.astype(F32) * eg).astype(BF16),
            kdec_t=(kf * jnp.exp(glast - gcol)).T.astype(BF16)))
    return heads


def _deltanet_kernel(qf_ref, kf_ref, vf_ref, zsf_ref, zstf_ref, qb_ref, kb_ref, vb_ref, zsb_ref, zstb_ref,
                     pr_ref, pca_ref, pcd_ref, s0f_ref, s0b_ref,
                     of_ref, ob_ref, soutf_ref, soutb_ref, sf_ref, sb_ref, *, n_chunks):
    step = pl.program_id(1)
    c = DN_CHUNK

    @pl.when(step == 0)
    def _():
        sf_ref[...] = s0f_ref[0]
        sb_ref[...] = s0b_ref[0]

    ri = lax.broadcasted_iota(I32, (c, c), 0)
    ci = lax.broadcasted_iota(I32, (c, c), 1)
    prm = (pr_ref, pca_ref, pcd_ref)
    heads = (_dn_heads(False, qf_ref, kf_ref, vf_ref, zsf_ref, zstf_ref, *prm, of_ref, sf_ref, ri, ci)
             + _dn_heads(True, qb_ref, kb_ref, vb_ref, zsb_ref, zstb_ref, *prm, ob_ref, sb_ref, ri, ci))
    lev = 1
    while (1 << lev) < c:
        blk = ((ri >> (lev + 1)) == (ci >> (lev + 1))) & ((ri >> lev) != (ci >> lev))
        half = [_mm(hd["inv"], jnp.where(blk, hd["lmat"], 0.0)) for hd in heads]
        for hd, t in zip(heads, half):
            hd["inv"] = hd["inv"] - _mm(t, hd["inv"])
        lev += 1
    uws = [_mm(hd["inv"], hd["rhs"]) for hd in heads]
    states = [hd["s_ref"][hd["h"]] for hd in heads]
    new_states = []
    for hd, uw, s in zip(heads, uws, states):
        ws_qs = _mm(jnp.concatenate([uw[:, DN_DV:].astype(BF16), hd["qe"]], axis=0), s)
        v_new = uw[:, :DN_DV] - ws_qs[:c]
        hd["o_ref"][:, hd["sl"]] = (ws_qs[c:] + _mm(hd["amat"], v_new)).astype(hd["o_ref"].dtype)
        new_states.append(s * jnp.exp(hd["glast"]) + _mm(hd["kdec_t"], v_new))
    for hd, s in zip(heads, new_states):
        hd["s_ref"][hd["h"]] = s

    @pl.when(step == n_chunks - 1)
    def _():
        soutf_ref[0] = sf_ref[...]
        soutb_ref[0] = sb_ref[...]


def _deltanet(dn, zs, zst, prm, s0f, s0b, b, seg):
    pr, pca, pcd = prm
    n = seg // DN_CHUNK
    hw = DN_HEADS * DN_DK
    fw = lambda bi, s: bi * n + s
    bw = lambda bi, s: bi * n + (n - 1 - s)
    state = pl.BlockSpec((1, DN_HEADS, DN_DK, DN_DV), lambda bi, s: (bi, 0, 0, 0))

    def chunk_specs(rb):
        return [pl.BlockSpec((DN_CHUNK, hw), lambda bi, s: (rb(bi, s), 0)),
                pl.BlockSpec((DN_CHUNK, hw), lambda bi, s: (rb(bi, s), 1)),
                pl.BlockSpec((DN_CHUNK, hw), lambda bi, s: (rb(bi, s), 2)),
                pl.BlockSpec((DN_CHUNK, LANES), lambda bi, s: (rb(bi, s), 0)),
                pl.BlockSpec((4 * DN_HEADS, DN_CHUNK), lambda bi, s: (0, rb(bi, s)))]

    out = lambda rb: pl.BlockSpec((DN_CHUNK, hw), lambda bi, s: (rb(bi, s), 0))
    o_shape = jax.ShapeDtypeStruct((b * seg, hw), BF16)
    s_shape = jax.ShapeDtypeStruct((b, DN_HEADS, DN_DK, DN_DV), F32)
    s_scratch = pltpu.VMEM((DN_HEADS, DN_DK, DN_DV), F32)
    return pl.pallas_call(
        functools.partial(_deltanet_kernel, n_chunks=n),
        grid=(b, n),
        in_specs=chunk_specs(fw) + chunk_specs(bw) + [
            pl.BlockSpec((SUBLANES, LANES), lambda bi, s: (0, 0)),
            pl.BlockSpec((4 * DN_HEADS, LANES), lambda bi, s: (0, 0)),
            pl.BlockSpec((4 * DN_HEADS, LANES), lambda bi, s: (0, 0)),
            state, state],
        out_specs=[out(fw), out(bw), state, state],
        out_shape=[o_shape, o_shape, s_shape, s_shape],
        scratch_shapes=[s_scratch, s_scratch],
        compiler_params=_params("parallel", "arbitrary"),
        name="deltanet",
    )(dn, dn, dn, zs, zst, dn, dn, dn, zs, zst, pr, pca, pcd, s0f, s0b)


def _deltanet_params(a_log, dt_bias):
    nh = 2 * DN_HEADS
    a = a_log.reshape(nh).astype(F32)
    dtb = dt_bias.reshape(nh).astype(F32)
    pr = jnp.zeros((SUBLANES, LANES), F32).at[0, nh:2 * nh].set(a).at[1, nh:2 * nh].set(dtb)
    pca = jnp.zeros((2 * nh, LANES), F32).at[nh:].set(jnp.broadcast_to(a[:, None], (nh, LANES)))
    pcd = jnp.zeros((2 * nh, LANES), F32).at[nh:].set(jnp.broadcast_to(dtb[:, None], (nh, LANES)))
    return pr, pca, pcd


def _outproj_ab_kernel(od_ref, of_ref, ob_ref, gate_ref, nw_ref, w_ref, x_ref, g_ref, o_ref):
    nd = od_ref.shape[1]
    y = jnp.dot(od_ref[...], w_ref[:nd, :], preferred_element_type=F32)
    gated = []
    for h in range(DN_HEADS):
        sl = slice(h * DN_DV, (h + 1) * DN_DV)
        o = of_ref[:, sl].astype(F32) + ob_ref[:, sl].astype(F32)
        o = o * lax.rsqrt(jnp.mean(o * o, axis=-1, keepdims=True) + EPS) * nw_ref[...]
        gated.append((o * _silu(gate_ref[:, sl].astype(F32))).astype(BF16))
    y = y + jnp.dot(jnp.concatenate(gated, axis=1), w_ref[nd:, :], preferred_element_type=F32)
    o_ref[...] = x_ref[...] + g_ref[0] * y


def _outproj_ab(od, o_f, o_b, z, gate_col0, out_norm, w, x, gate, rows_per_group):
    m, d = x.shape
    nd, nn = od.shape[1], o_f.shape[1]
    tm = math.gcd(_tile(m, 256, SUBLANES), rows_per_group)
    grp = lambda i: ((i * tm) // rows_per_group, 0, 0)
    gblk = gate_col0 // nn
    return pl.pallas_call(
        _outproj_ab_kernel,
        grid=(m // tm,),
        in_specs=[pl.BlockSpec((tm, nd), lambda i: (i, 0)),
                  pl.BlockSpec((tm, nn), lambda i: (i, 0)),
                  pl.BlockSpec((tm, nn), lambda i: (i, 0)),
                  pl.BlockSpec((tm, nn), lambda i: (i, gblk)),
                  pl.BlockSpec((1, DN_DV), lambda i: (0, 0)),
                  pl.BlockSpec((nd + nn, d), lambda i: (0, 0)),
                  pl.BlockSpec((tm, d), lambda i: (i, 0)),
                  pl.BlockSpec((1, 1, d), grp)],
        out_specs=pl.BlockSpec((tm, d), lambda i: (i, 0)),
        out_shape=jax.ShapeDtypeStruct((m, d), F32),
        compiler_params=_params("parallel"),
        name="outproj_ab",
    )(od, o_f, o_b, z, out_norm.reshape(1, DN_DV), w, x, gate)


def _outproj_kernel(a_ref, w_ref, x_ref, g_ref, o_ref):
    o_ref[...] = x_ref[...] + g_ref[0] * jnp.dot(a_ref[...], w_ref[...], preferred_element_type=F32)


def _outproj(a, w, x, gate, rows_per_group):
    m, d = x.shape
    kdim = a.shape[1]
    tm = math.gcd(_tile(m, 512, SUBLANES), rows_per_group)
    tn = _tile(d, 2048, LANES)
    grp = lambda i, j: ((i * tm) // rows_per_group, 0, j)
    return pl.pallas_call(
        _outproj_kernel,
        grid=(m // tm, d // tn),
        in_specs=[pl.BlockSpec((tm, kdim), lambda i, j: (i, 0)),
                  pl.BlockSpec((kdim, tn), lambda i, j: (0, j)),
                  pl.BlockSpec((tm, tn), lambda i, j: (i, j)),
                  pl.BlockSpec((1, 1, tn), grp)],
        out_specs=pl.BlockSpec((tm, tn), lambda i, j: (i, j)),
        out_shape=jax.ShapeDtypeStruct((m, d), F32),
        compiler_params=_params("parallel", "arbitrary"),
        name="outproj",
    )(a, w, x, gate)


def _swa_kernel(sink_ref, bias_ref, q_ref, k0_ref, k1_ref, k2_ref, v0_ref, v1_ref, v2_ref, kc_ref, vc_ref,
                o_ref):
    kvh = pl.program_id(1)
    qb = Q_BLOCK
    npair = SWA_GROUP // 2
    lane = lax.broadcasted_iota(I32, (qb, LANES), 1)
    q = q_ref[...]
    parts = []
    for p in range(npair):
        qp = q[:, p * LANES:(p + 1) * LANES]
        parts.append(jnp.where(lane < SWA_DH, qp, jnp.zeros_like(qp)))
        parts.append(jnp.where(lane >= SWA_DH, qp, jnp.zeros_like(qp)))
    qq = jnp.concatenate(parts, axis=0)
    k_lat = jnp.concatenate([k0_ref[...], k1_ref[...], k2_ref[...]], axis=0)
    v_lat = jnp.concatenate([v0_ref[...], v1_ref[...], v2_ref[...]], axis=0)
    s_lat = lax.dot_general(qq, k_lat, NT_DIMS, preferred_element_type=F32)
    s_ctx = lax.dot_general(qq, kc_ref[...], NT_DIMS, preferred_element_type=F32)
    bias = bias_ref[0]
    vc = vc_ref[...]
    outs = []
    for g in range(SWA_GROUP):
        rs = slice(g * qb, (g + 1) * qb)
        sl = s_lat[rs] + bias
        sc = s_ctx[rs]
        sink = sink_ref[kvh, g] * LOG2E
        m = jnp.maximum(jnp.maximum(jnp.max(sl, axis=-1, keepdims=True),
                                    jnp.max(sc, axis=-1, keepdims=True)), sink)
        el = jnp.exp2(sl - m)
        ec = jnp.exp2(sc - m)
        l = jnp.sum(el, axis=-1, keepdims=True) + jnp.sum(ec, axis=-1, keepdims=True) + jnp.exp2(sink - m)
        o = (jnp.dot(el.astype(BF16), v_lat, preferred_element_type=F32)
             + jnp.dot(ec.astype(BF16), vc, preferred_element_type=F32))
        outs.append(o * (1.0 / l))
    for p in range(npair):
        o_ref[:, p * LANES:(p + 1) * LANES] = jnp.where(lane < SWA_DH, outs[2 * p], outs[2 * p + 1]).astype(o_ref.dtype)


def _swa_attention(sink, q, kx, vx, kc, vc, b, t, n_c):
    qb = Q_BLOCK
    nb = t // qb
    gw = SWA_GROUP * SWA_DH
    lat = lambda off: pl.BlockSpec(
        (qb, LANES), lambda bi, h, i: (bi * nb + jnp.clip(i + off, 0, nb - 1), h))
    ctx = pl.BlockSpec((n_c, LANES), lambda bi, h, i: (bi, h))
    r_io = np.arange(qb)[:, None]
    c_io = np.arange(3 * qb)[None, :]
    inside = np.abs(r_io + qb - c_io) <= WINDOW
    variants = [inside & ((c_io >= qb) | (v & 1 == 0)) & ((c_io < 2 * qb) | (v & 2 == 0)) for v in range(4)]
    bias = jnp.asarray(np.where(np.stack(variants), 0.0, -np.inf), F32)
    return pl.pallas_call(
        _swa_kernel,
        grid=(b, SWA_KV_HEADS, nb),
        in_specs=[pl.BlockSpec(memory_space=pltpu.SMEM),
                  pl.BlockSpec((1, qb, 3 * qb),
                               lambda bi, h, i: ((i == 0).astype(I32) + 2 * (i == nb - 1).astype(I32), 0, 0)),
                  pl.BlockSpec((qb, gw), lambda bi, h, i: (bi * nb + i, h)),
                  lat(-1), lat(0), lat(1), lat(-1), lat(0), lat(1), ctx, ctx],
        out_specs=pl.BlockSpec((qb, gw), lambda bi, h, i: (bi * nb + i, h)),
        out_shape=jax.ShapeDtypeStruct((b * t, SWA_HEADS * SWA_DH), BF16),
        compiler_params=_params("parallel", "parallel", "arbitrary"),
        name="swa_attention",
    )(sink, bias, q, kx, kx, kx, vx, vx, vx, kc, vc)


def _dup_heads(a, col0):
    m = a.shape[0]
    h = a[:, col0:col0 + SWA_KV_HEADS * SWA_DH].reshape(m, SWA_KV_HEADS, 1, SWA_DH)
    return jnp.broadcast_to(h, (m, SWA_KV_HEADS, LANES // SWA_DH, SWA_DH)).reshape(m, SWA_KV_HEADS * LANES)


def _first_max(vals, iota, size, axis):
    m = jnp.max(vals, axis=axis, keepdims=True)
    first = jnp.min(jnp.where(vals == m, iota, size), axis=axis, keepdims=True)
    return m, first


def _router_kernel(x_ref, nw_ref, sh_ref, sc_ref, rwt_ref, rb_ref, c0_ref,
                   hp_ref, idx_ref, wt_ref, rank_ref, cnt_ref, carry_ref):
    @pl.when(pl.program_id(0) == 0)
    def _():
        carry_ref[...] = c0_ref[...]

    h = _norm_mod(x_ref[...], nw_ref[...], sh_ref[0], sc_ref[0])
    tm, d = h.shape
    _store_token_tiles(hp_ref, 0, _pack_pairs(h[:, :d // 2], h[:, d // 2:]))
    scores = jax.nn.sigmoid(_dot3(rwt_ref[...], h, NT_DIMS))
    sel = scores + rb_ref[...]
    neg = jnp.float32(-jnp.inf)

    g_io = lax.broadcasted_iota(I32, (GROUP_SIZE, tm), 0)
    gs_rows = []
    for g in range(N_GROUPS):
        sg = sel[g * GROUP_SIZE:(g + 1) * GROUP_SIZE]
        m1, f1 = _first_max(sg, g_io, GROUP_SIZE, 0)
        m2 = jnp.max(jnp.where(g_io == f1, neg, sg), axis=0, keepdims=True)
        gs_rows.append(m1 + m2)
    cur = jnp.concatenate(gs_rows, axis=0)
    n_io = lax.broadcasted_iota(I32, (N_GROUPS, tm), 0)
    gmask = jnp.zeros((N_GROUPS, tm), I32)
    for _ in range(TOPK_GROUPS):
        _, f = _first_max(cur, n_io, N_GROUPS, 0)
        hit = n_io == f
        gmask = jnp.where(hit, 1, gmask)
        cur = jnp.where(hit, neg, cur)
    cur = jnp.concatenate(
        [jnp.where(gmask[g:g + 1] > 0, sel[g * GROUP_SIZE:(g + 1) * GROUP_SIZE], neg) for g in range(N_GROUPS)],
        axis=0)

    e_io = lax.broadcasted_iota(I32, (N_EXPERTS, tm), 0)
    chosen = jnp.zeros((N_EXPERTS, tm), F32)
    idx_rows, w_rows = [], []
    for _ in range(TOP_K):
        _, f = _first_max(cur, e_io, N_EXPERTS, 0)
        hit = e_io == f
        idx_rows.append(f)
        w_rows.append(jnp.sum(jnp.where(hit, scores, 0.0), axis=0, keepdims=True))
        chosen = jnp.where(hit, 1.0, chosen)
        cur = jnp.where(hit, neg, cur)
    idx = jnp.concatenate(idx_rows, axis=0)
    w = jnp.concatenate(w_rows, axis=0)
    idx_ref[...] = idx
    wt_ref[...] = w * (1.0 / jnp.sum(w, axis=0, keepdims=True)) * ROUTED_SCALE

    onehot = chosen.astype(BF16)
    before = (lax.broadcasted_iota(I32, (tm, tm), 0) < lax.broadcasted_iota(I32, (tm, tm), 1)).astype(BF16)
    base = carry_ref[:, 0:1] + jnp.dot(onehot, before, preferred_element_type=F32)
    rank_ref[...] = jnp.concatenate(
        [jnp.sum(jnp.where(e_io == idx_rows[k], base, 0.0), axis=0, keepdims=True) for k in range(TOP_K)],
        axis=0).astype(I32)
    carry_ref[...] = carry_ref[...] + jnp.sum(chosen, axis=1, keepdims=True)
    cnt_ref[...] = carry_ref[...]


def _router(x, nw, sh, sc, rwt, rb, counts0, rows_per_group):
    m, d = x.shape
    tm = math.gcd(_tile(m, 256, LANES), rows_per_group)
    assert (d // 2) % LANES == 0
    tr = d // 2 // LANES
    grp = lambda i: ((i * tm) // rows_per_group, 0, 0)
    tok = lambda rows: pl.BlockSpec((rows, tm), lambda i: (0, i))
    return pl.pallas_call(
        _router_kernel,
        grid=(m // tm,),
        in_specs=[pl.BlockSpec((tm, d), lambda i: (i, 0)),
                  pl.BlockSpec((1, d), lambda i: (0, 0)),
                  pl.BlockSpec((1, 1, d), grp),
                  pl.BlockSpec((1, 1, d), grp),
                  pl.BlockSpec((N_EXPERTS, d), lambda i: (0, 0)),
                  pl.BlockSpec((N_EXPERTS, 1), lambda i: (0, 0)),
                  pl.BlockSpec((N_EXPERTS, LANES), lambda i: (0, 0))],
        out_specs=[pl.BlockSpec((tm * tr, LANES), lambda i: (i, 0)),
                   tok(TOP_K), tok(TOP_K), tok(TOP_K),
                   pl.BlockSpec((N_EXPERTS, LANES), lambda i: (0, 0))],
        out_shape=[jax.ShapeDtypeStruct((m * tr, LANES), U32),
                   jax.ShapeDtypeStruct((TOP_K, m), I32),
                   jax.ShapeDtypeStruct((TOP_K, m), F32),
                   jax.ShapeDtypeStruct((TOP_K, m), I32),
                   jax.ShapeDtypeStruct((N_EXPERTS, LANES), F32)],
        scratch_shapes=[pltpu.VMEM((N_EXPERTS, LANES), F32)],
        compiler_params=_params("arbitrary"),
        name="moe_router",
    )(x, nw.reshape(1, d), sh, sc, rwt, rb.reshape(N_EXPERTS, 1), counts0)


def _dispatch_kernel(nv_ref, dest_ref, hp_ref, xs_ref, zero_ref, sem, *, tr):
    tm = hp_ref.shape[0] // tr
    blk_rows = MOE_BLOCK * tr

    @pl.when(pl.program_id(0) == 0)
    def _():
        zero_ref[...] = jnp.zeros(zero_ref.shape, U32)

        def fill(blk, carry):
            @pl.when(nv_ref[blk] < MOE_BLOCK)
            def _():
                cp = pltpu.make_async_copy(
                    zero_ref, xs_ref.at[pl.ds(pl.multiple_of(blk * blk_rows, blk_rows), blk_rows)], sem)
                cp.start()
                cp.wait()
            return carry

        lax.fori_loop(0, nv_ref.shape[0], fill, 0)

    def copy(t, k):
        return _tile_copy(hp_ref, t * tr, xs_ref, dest_ref[t * TOP_K + k], tr, sem)

    def start(t, carry):
        for k in range(TOP_K):
            copy(t, k).start(priority=k % 2)
        return carry

    def wait(t, carry):
        for k in range(TOP_K):
            copy(t, k).wait()
        return carry

    lax.fori_loop(0, tm, start, 0)
    lax.fori_loop(0, tm, wait, 0)


def _dispatch(block_nv, dest, hp, n_rows, tr):
    m = hp.shape[0] // tr
    tm = _tile(m, 512, LANES)
    return pl.pallas_call(
        functools.partial(_dispatch_kernel, tr=tr),
        grid_spec=pltpu.PrefetchScalarGridSpec(
            num_scalar_prefetch=1,
            grid=(m // tm,),
            in_specs=[pl.BlockSpec((tm * TOP_K,), lambda i, nv: (i,), memory_space=pltpu.SMEM),
                      pl.BlockSpec((tm * tr, LANES), lambda i, nv: (i, 0))],
            out_specs=pl.BlockSpec(memory_space=pl.ANY),
            scratch_shapes=[pltpu.VMEM((MOE_BLOCK * tr, LANES), U32), pltpu.SemaphoreType.DMA(())]),
        out_shape=jax.ShapeDtypeStruct((n_rows * tr, LANES), U32),
        compiler_params=_params("arbitrary"),
        name="moe_dispatch",
    )(block_nv, dest, hp)


def _gffn_kernel(be_ref, nv_ref, xs_ref, w1_ref, w3_ref, w2_ref, ys_ref, w1b_ref, w3b_ref, w2b_ref, *, tr):
    i = pl.program_id(0)
    nv = nv_ref[i]
    prev = be_ref[jnp.maximum(i - 1, 0)]

    @pl.when((i == 0) | (be_ref[i] != prev))
    def _():
        w1b_ref[...] = w1_ref[0, 0].astype(BF16)
        w3b_ref[...] = w3_ref[0, 0].astype(BF16)
        w2b_ref[...] = w2_ref[0, 0].astype(BF16)

    @pl.when(nv > 0)
    def _():
        lo, hi = _unpack_pairs(_load_token_tiles(xs_ref, 0, MOE_BLOCK, tr))
        dh = lo.shape[1]
        lo = lo.astype(BF16)
        hi = hi.astype(BF16)
        d = functools.partial(jnp.dot, preferred_element_type=F32)
        a = d(lo, w1b_ref[:dh, :]) + d(hi, w1b_ref[dh:, :])
        g = d(lo, w3b_ref[:dh, :]) + d(hi, w3b_ref[dh:, :])
        y = d((_silu(a) * g).astype(BF16), w2b_ref[...])
        _store_token_tiles(ys_ref, 0, _pack_pairs(y[:, :dh], y[:, dh:]))

    @pl.when(nv == 0)
    def _():
        ys_ref[...] = jnp.zeros(ys_ref.shape, U32)


def _grouped_ffn(block_e, block_nv, xs, w1, w3, w2, layer, tr):
    _, _, d, f = w1.shape
    blk_rows = MOE_BLOCK * tr
    nb = xs.shape[0] // blk_rows
    return pl.pallas_call(
        functools.partial(_gffn_kernel, tr=tr),
        grid_spec=pltpu.PrefetchScalarGridSpec(
            num_scalar_prefetch=2,
            grid=(nb,),
            in_specs=[pl.BlockSpec((blk_rows, LANES), lambda i, be, nv: (i, 0)),
                      pl.BlockSpec((1, 1, d, f), lambda i, be, nv: (layer, be[i], 0, 0)),
                      pl.BlockSpec((1, 1, d, f), lambda i, be, nv: (layer, be[i], 0, 0)),
                      pl.BlockSpec((1, 1, f, d), lambda i, be, nv: (layer, be[i], 0, 0))],
            out_specs=pl.BlockSpec((blk_rows, LANES), lambda i, be, nv: (i, 0)),
            scratch_shapes=[pltpu.VMEM((d, f), BF16), pltpu.VMEM((d, f), BF16), pltpu.VMEM((f, d), BF16)]),
        out_shape=jax.ShapeDtypeStruct(xs.shape, U32),
        compiler_params=_params("arbitrary"),
        name="moe_grouped_ffn",
    )(block_e, block_nv, xs, w1, w3, w2)


def _combine_kernel(dest_ref, dest_next_ref, x_ref, hp_ref, wt_ref, g_ref, ws1_ref, ws3_ref, ws2_ref, ys_ref,
                    o_ref, buf_a, buf_b, sem_a, sem_b, *, tr):
    i = pl.program_id(0)
    tm = x_ref.shape[0]
    dh = tr * LANES

    def copy(dref, t, k, buf, sem):
        return _tile_copy(ys_ref, dref[t * TOP_K + k], buf, (k * tm + t) * tr, tr, sem)

    def wait_all(buf, sem):
        def wait(t, carry):
            for k in range(TOP_K):
                copy(dest_ref, t, k, buf, sem).wait()
            return carry
        lax.fori_loop(0, tm, wait, 0)

    @pl.when(i == 0)
    def _():
        def start(t, carry):
            for k in range(TOP_K):
                copy(dest_ref, t, k, buf_a, sem_a).start(priority=k % 2)
            return carry
        lax.fori_loop(0, tm, start, 0)

    def step(buf, sem, buf_next, sem_next):
        wait_all(buf, sem)
        for t in range(tm):
            for k in range(TOP_K):
                copy(dest_next_ref, t, k, buf_next, sem_next).start(priority=k % 2)
        lo, hi = _unpack_pairs(_load_token_tiles(hp_ref, 0, tm, tr))
        lo = lo.astype(BF16)
        hi = hi.astype(BF16)
        d = functools.partial(jnp.dot, preferred_element_type=F32)
        a = d(lo, ws1_ref[:dh, :]) + d(hi, ws1_ref[dh:, :])
        g = d(lo, ws3_ref[:dh, :]) + d(hi, ws3_ref[dh:, :])
        shared = d((_silu(a) * g).astype(BF16), ws2_ref[...])
        acc_lo = jnp.zeros((tm, dh), F32)
        acc_hi = jnp.zeros((tm, dh), F32)
        for k in range(TOP_K):
            ylo, yhi = _unpack_pairs(_load_token_tiles(buf, k * tm * tr, tm, tr))
            wk = wt_ref[:, k:k + 1]
            acc_lo = acc_lo + wk * ylo
            acc_hi = acc_hi + wk * yhi
        o_ref[:, :dh] = x_ref[:, :dh] + g_ref[0][:, :dh] * (acc_lo + shared[:, :dh])
        o_ref[:, dh:] = x_ref[:, dh:] + g_ref[0][:, dh:] * (acc_hi + shared[:, dh:])

        @pl.when(i == pl.num_programs(0) - 1)
        def _():
            wait_all(buf_next, sem_next)

    @pl.when(i % 2 == 0)
    def _():
        step(buf_a, sem_a, buf_b, sem_b)

    @pl.when(i % 2 == 1)
    def _():
        step(buf_b, sem_b, buf_a, sem_a)


def _combine(dest, x, hp, wt, gate, ws1, ws3, ws2, ys, rows_per_group, tr):
    m, d = x.shape
    f = ws1.shape[1]
    tm = math.gcd(_tile(m, 128, LANES), rows_per_group)
    nt = m // tm
    grp = lambda i: ((i * tm) // rows_per_group, 0, 0)
    buf = pltpu.VMEM((TOP_K * tm * tr, LANES), U32)
    return pl.pallas_call(
        functools.partial(_combine_kernel, tr=tr),
        grid=(nt,),
        in_specs=[pl.BlockSpec((tm * TOP_K,), lambda i: (i,), memory_space=pltpu.SMEM),
                  pl.BlockSpec((tm * TOP_K,), lambda i: (jnp.minimum(i + 1, nt - 1),), memory_space=pltpu.SMEM),
                  pl.BlockSpec((tm, d), lambda i: (i, 0)),
                  pl.BlockSpec((tm * tr, LANES), lambda i: (i, 0)),
                  pl.BlockSpec((tm, TOP_K), lambda i: (i, 0)),
                  pl.BlockSpec((1, 1, d), grp),
                  pl.BlockSpec((d, f), lambda i: (0, 0)),
                  pl.BlockSpec((d, f), lambda i: (0, 0)),
                  pl.BlockSpec((f, d), lambda i: (0, 0)),
                  pl.BlockSpec(memory_space=pl.ANY)],
        out_specs=pl.BlockSpec((tm, d), lambda i: (i, 0)),
        out_shape=jax.ShapeDtypeStruct((m, d), F32),
        scratch_shapes=[buf, buf, pltpu.SemaphoreType.DMA(()), pltpu.SemaphoreType.DMA(())],
        compiler_params=_params("arbitrary"),
        name="moe_combine",
    )(dest, dest, x, hp, wt, gate, ws1, ws3, ws2, ys)


def _moe(streams, layer, nw, rw, rb, w1, w3, w2, ws1, ws3, ws2):
    d = streams[0][0].shape[1]
    rwt = rw.T
    counts = jnp.zeros((N_EXPERTS, LANES), F32)
    routed = []
    for x, sh, sc, _, rpg in streams:
        hp, idx, wt, rank, counts = _router(x, nw, sh, sc, rwt, rb, counts, rpg)
        routed.append((hp, idx, wt, rank))
    n_assign = sum(s[0].shape[0] for s in streams) * TOP_K
    n_blocks = (n_assign + N_EXPERTS * (MOE_BLOCK - 1) + MOE_BLOCK - 1) // MOE_BLOCK
    cnt = counts[:, 0].astype(I32)
    padded = (cnt + MOE_BLOCK - 1) // MOE_BLOCK * MOE_BLOCK
    pad_end = jnp.cumsum(padded)
    pad_start = pad_end - padded
    bstart = jnp.arange(n_blocks, dtype=I32) * MOE_BLOCK
    block_e = jnp.minimum(jnp.sum((bstart[:, None] >= pad_end[None, :]).astype(I32), axis=1), N_EXPERTS - 1)
    block_nv = jnp.clip(cnt[block_e] - (bstart - pad_start[block_e]), 0, MOE_BLOCK).astype(I32)
    e_ar = jnp.arange(N_EXPERTS, dtype=I32)
    tr = d // 2 // LANES
    dests = [((jnp.sum(jnp.where(idx[:, :, None] == e_ar, pad_start, 0), axis=-1) + rank) * tr).T.reshape(-1)
             for _, idx, _, rank in routed]
    if len(streams) > 1:
        hp_all = jnp.concatenate([r[0] for r in routed], axis=0)
        dest_all = jnp.concatenate(dests, axis=0)
    else:
        hp_all, dest_all = routed[0][0], dests[0]
    xs = _dispatch(block_nv, dest_all, hp_all, n_blocks * MOE_BLOCK, tr)
    ys = _grouped_ffn(block_e, block_nv, xs, w1, w3, w2, layer, tr)
    ws1b, ws3b, ws2b = ws1.astype(BF16), ws3.astype(BF16), ws2.astype(BF16)
    return [_combine(dest, x, hp, wt.T, gate, ws1b, ws3b, ws2b, ys, rpg, tr)
            for (x, _, _, gate, rpg), (hp, _, wt, _), dest in zip(streams, routed, dests)]


def _mixer_ab(xs, cs, mx, mc, b, t, n_c, layer, nw, w_in, w_out, q_norm, k_norm, lam_vec, subln, conv_w,
              a_log, dt_bias, out_norm):
    n_main = w_in.shape[1] - 4 * DN_HEADS
    assert n_main % LANES == 0
    w_main = w_in[:, :n_main].astype(BF16)
    w_small = jnp.zeros((w_in.shape[0], LANES), F32).at[:, :4 * DN_HEADS].set(w_in[:, n_main:])
    nqk = 2 * DIFF_HEADS * DIFF_DK
    assert n_main % nqk == 0
    tabs = _rope_tables(t, DIFF_DK)
    hw = jnp.stack([q_norm * (DIFF_DK ** -0.5 * LOG2E), k_norm])
    groups = nqk // LANES
    blocks = ((0, 1, (0,) * groups), (1, 2, (1,) * groups))
    z_x, zs_x = _norm_mod_matmul(xs, nw, mx[0], mx[1], w_main, t, w_small, tn=nqk,
                                 heads=dict(hd=DIFF_DK, weights=hw, blocks=blocks, tables=tabs, t=t))
    z_c, zs_c = _norm_mod_matmul(cs, nw, mc[0], mc[1], w_main, b * n_c, w_small, tn=nqk,
                                 heads=dict(hd=DIFF_DK, weights=hw, blocks=blocks, tables=None, t=t))
    lam_init = 0.8 - 0.6 * math.exp(-0.3 * layer)
    od_x = _diff_attention(lam_vec, z_x, z_c, z_c, subln, lam_init, b, n_c, kx=z_x, zx=z_x, t=t)
    od_c = _diff_attention(lam_vec, z_c, z_c, z_c, subln, lam_init, b, n_c)
    dn_col0 = 2 * nqk + DIFF_HEADS * DIFF_DV
    dn_x = _dn_prep(z_x, dn_col0, conv_w, t)
    dn_c = _dn_prep(z_c, dn_col0, conv_w, n_c)
    zst_x = zs_x[:, :4 * DN_HEADS].T
    zst_c = zs_c[:, :4 * DN_HEADS].T
    prm = _deltanet_params(a_log, dt_bias)
    s0 = jnp.zeros((b, DN_HEADS, DN_DK, DN_DV), F32)
    o_cf, o_cb, s_cf, s_cb = _deltanet(dn_c, zs_c, zst_c, prm, s0, s0, b, n_c)
    o_xf, o_xb, _, _ = _deltanet(dn_x, zs_x, zst_x, prm, s_cf, s_cb, b, t)
    gate_col0 = dn_col0 + conv_w.shape[1]
    w_out_b = w_out.astype(BF16)
    x1 = _outproj_ab(od_x, o_xf, o_xb, z_x, gate_col0, out_norm, w_out_b, xs, mx[2], t)
    c1 = _outproj_ab(od_c, o_cf, o_cb, z_c, gate_col0, out_norm, w_out_b, cs, mc[2], b * n_c)
    return x1, c1


def _mixer_swa(xs, cs, mx, mc, b, t, n_c, nw, w_in, w_out, q_norm, k_norm, sink):
    nq = SWA_HEADS * SWA_DH
    nkv = SWA_KV_HEADS * SWA_DH
    w_b = w_in.astype(BF16)
    tabs = _rope_tables(t, SWA_DH)
    rep = LANES // SWA_DH
    hw = jnp.stack([jnp.tile(q_norm, rep) * (SWA_DH ** -0.5 * LOG2E), jnp.tile(k_norm, rep)])
    q_rows = (0,) * (nq // LANES)
    kv_rows = (1,) * (nkv // LANES) + (None,) * (nkv // LANES)
    z_x = _norm_mod_matmul(xs, nw, mx[0], mx[1], w_b, t, tn=w_b.shape[1],
                           heads=dict(hd=SWA_DH, weights=hw, tables=tabs, t=t, blocks=((0, 1, q_rows + kv_rows),)))
    z_c = _norm_mod_matmul(cs, nw, mc[0], mc[1], w_b[:, nq:], b * n_c, tn=2 * nkv,
                           heads=dict(hd=SWA_DH, weights=hw, tables=None, t=t, blocks=((0, 1, kv_rows),)))
    att = _swa_attention(sink.reshape(SWA_KV_HEADS, SWA_GROUP), z_x, _dup_heads(z_x, nq), _dup_heads(z_x, nq + nkv),
                         _dup_heads(z_c, 0), _dup_heads(z_c, nkv), b, t, n_c)
    return _outproj(att, w_out.astype(BF16), xs, mx[2], t)


def kernel(x, c, ctx, c_ctx, mod_w, mod_b, norm_mix, norm_ffn, ab_w_in, ab_w_out, diff_q_norm, diff_k_norm, diff_lambda, diff_subln, dn_conv, dn_a_log, dn_dt_bias, dn_out_norm, swa_w_in, swa_w_out, swa_q_norm, swa_k_norm, swa_sink, router_w, router_bias, exp_w1, exp_w3, exp_w2, shared_w1, shared_w3, shared_w2):
    b, t, d = x.shape
    n_c = ctx.shape[1]
    depth = mod_w.shape[0]
    assert depth == 2 and t % Q_BLOCK == 0 and t % DN_CHUNK == 0 and n_c % DN_CHUNK == 0
    xs = x.reshape(b * t, d)
    cs = ctx.reshape(b * n_c, d)
    n_mod = -(-(b + 1) // SUBLANES) * SUBLANES
    a_mod = jnp.zeros((n_mod, d), F32).at[0].set(c_ctx).at[1:1 + b].set(c)
    for layer in range(depth):
        with_ctx = layer < depth - 1
        p = layer // 2
        mod = _modulation(a_mod, mod_w, layer, mod_b[layer])
        mc = [mod[0:1, j * d:(j + 1) * d].reshape(1, 1, d) for j in range(6)]
        mx = [mod[1:1 + b, j * d:(j + 1) * d].reshape(b, 1, d) for j in range(6)]
        if layer % 2 == 0:
            xs, c_new = _mixer_ab(xs, cs, mx, mc, b, t, n_c, layer, norm_mix[layer], ab_w_in[p], ab_w_out[p],
                                  diff_q_norm[p], diff_k_norm[p], diff_lambda[p], diff_subln[p], dn_conv[p],
                                  dn_a_log[p], dn_dt_bias[p], dn_out_norm[p])
        else:
            assert not with_ctx
            xs = _mixer_swa(xs, cs, mx, mc, b, t, n_c, norm_mix[layer], swa_w_in[p], swa_w_out[p],
                            swa_q_norm[p], swa_k_norm[p], swa_sink[p])
            c_new = None
        moe_w = (layer, norm_ffn[layer], router_w[layer], router_bias[layer], exp_w1, exp_w3, exp_w2,
                 shared_w1[layer], shared_w3[layer], shared_w2[layer])
        if with_ctx:
            cs, xs = _moe([(c_new, mc[3], mc[4], mc[5], b * n_c), (xs, mx[3], mx[4], mx[5], t)], *moe_w)
        else:
            (xs,) = _moe([(xs, mx[3], mx[4], mx[5], t)], *moe_w)
    return xs.reshape(b, t, d)
```

```python
import functools
import math

import jax
import jax.numpy as jnp
import numpy as np
from jax import lax
from jax.experimental import pallas as pl
from jax.experimental.pallas import tpu as pltpu

F32 = jnp.float32
BF16 = jnp.bfloat16
I32 = jnp.int32
U32 = jnp.uint32

EPS = 1e-6
GRID_W = 64
ROPE_BASE = 10000.0
DIFF_HEADS = 4
DIFF_DK = 128
DIFF_DV = 256
DN_HEADS = 8
DN_DK = 128
DN_DV = 128
DN_CONV = 5
DN_CHUNK = 128
SWA_HEADS = 32
SWA_KV_HEADS = 4
SWA_GROUP = SWA_HEADS // SWA_KV_HEADS
SWA_DH = 64
WINDOW = 128
Q_BLOCK = 128
N_EXPERTS = 64
TOP_K = 8
N_GROUPS = 8
GROUP_SIZE = N_EXPERTS // N_GROUPS
TOPK_GROUPS = 4
ROUTED_SCALE = 2.5
MOE_BLOCK = 512

LANES = 128
SUBLANES = 8
VMEM_LIMIT_BYTES = 56 * 1024 * 1024

ROWS_STREAMED = 1024
COLS_STREAMED = 1024
ROWS_RESIDENT = 256
ROWS_OUTPROJ = 512
MOD_COLS = 768
DIFF_Q_ROWS = 1024
CHUNK_ROWS = 256
ROUTER_TOKENS = 256
DISPATCH_TOKENS = 512
COMBINE_TOKENS = 128

NT_DIMS = (((1,), (1,)), ((), ()))
LOG2E = math.log2(math.e)


def _params(*semantics):
    return pltpu.CompilerParams(dimension_semantics=semantics, vmem_limit_bytes=VMEM_LIMIT_BYTES)


def _tile(n, pref, mult):
    if n <= pref:
        return n
    t = pref - pref % mult
    while t > mult and n % t:
        t -= mult
    assert n % t == 0, (n, pref, mult)
    return t


def _mm(a, b):
    return jnp.dot(a.astype(BF16), b.astype(BF16), preferred_element_type=F32)


def _split2(x):
    hi = x.astype(BF16)
    lo = (x - hi.astype(F32)).astype(BF16)
    return hi, lo


def _dot3(a, b, dims=None):
    if dims is None:
        dims = (((a.ndim - 1,), (0,)), ((), ()))
    ah, al = _split2(a)
    bh, bl = _split2(b)
    d = functools.partial(lax.dot_general, dimension_numbers=dims, preferred_element_type=F32)
    return d(ah, bh) + d(ah, bl) + d(al, bh)


def _silu(x):
    return x * jax.nn.sigmoid(x)


def _softplus(x):
    return jnp.maximum(x, 0.0) + jnp.log(1.0 + jnp.exp(-jnp.abs(x)))


def _pack_pairs(lo, hi):
    ulo = lax.bitcast_convert_type(lo.astype(BF16).astype(F32), U32) >> 16
    uhi = lax.bitcast_convert_type(hi.astype(BF16).astype(F32), U32) & jnp.uint32(0xFFFF0000)
    return ulo | uhi


def _unpack_pairs(u):
    lo = lax.bitcast_convert_type(u << 16, F32)
    hi = lax.bitcast_convert_type(u & jnp.uint32(0xFFFF0000), F32)
    return lo, hi


def _store_token_tiles(ref, base, packed):
    n, width = packed.shape
    tr = width // LANES
    for j in range(tr):
        ref[pl.ds(base + j, n, stride=tr), :] = packed[:, j * LANES:(j + 1) * LANES]


def _load_token_tiles(ref, base, n, tr):
    return jnp.concatenate([ref[pl.ds(base + j, n, stride=tr), :] for j in range(tr)], axis=1)


def _tile_copy(src, src_row, dst, dst_row, tr, sem):
    return pltpu.make_async_copy(src.at[pl.ds(pl.multiple_of(src_row, tr), tr)],
                                 dst.at[pl.ds(pl.multiple_of(dst_row, tr), tr)], sem)


def _mod_kernel(a_ref, w_ref, b_ref, o_ref):
    o_ref[...] = _dot3(_silu(a_ref[...]), w_ref[0]) + b_ref[...]


def _modulation(a, w_all, layer, b):
    r, d = a.shape
    n = w_all.shape[2]
    tn = _tile(n, MOD_COLS, LANES)
    return pl.pallas_call(
        _mod_kernel,
        grid=(n // tn,),
        in_specs=[pl.BlockSpec((r, d), lambda j: (0, 0)),
                  pl.BlockSpec((1, d, tn), lambda j: (layer, 0, j)),
                  pl.BlockSpec((1, tn), lambda j: (0, j))],
        out_specs=pl.BlockSpec((r, tn), lambda j: (0, j)),
        out_shape=jax.ShapeDtypeStruct((r, n), F32),
        compiler_params=_params("parallel"),
        name="modulation",
    )(a, w_all, b.reshape(1, n))


def _norm_mod(x, nw, sh, sc):
    y = x * lax.rsqrt(jnp.mean(x * x, axis=-1, keepdims=True) + EPS) * nw
    return y * (1.0 + sc) + sh


def _store_head_groups(o_ref, ys, rows, hw_ref, seg_ref, swap_ref, cos, sin, hd):
    normed = [g for g in range(len(ys)) if rows[g] is not None]
    if normed:
        seg = seg_ref[...]
        sums = {g: jnp.dot((ys[g] * ys[g]).astype(BF16), seg, preferred_element_type=F32) for g in normed}
        for g in normed:
            ys[g] = ys[g] * lax.rsqrt(sums[g] * (1.0 / hd) + EPS) * hw_ref[rows[g]]
        if cos is not None:
            swap = swap_ref[...]
            partners = {g: jnp.dot(ys[g].astype(BF16), swap, preferred_element_type=F32) for g in normed}
            for g in normed:
                ys[g] = ys[g] * cos + partners[g] * sin
    for g, y in enumerate(ys):
        o_ref[:, g * LANES:(g + 1) * LANES] = y.astype(o_ref.dtype)


def _head_matrices(hd):
    lane = np.arange(LANES)
    seg = (lane[:, None] // hd) == (lane[None, :] // hd)
    q = hd // 4
    partner = np.where((lane % (2 * q)) < q, lane + q, lane - q)
    swap = lane[:, None] == partner[None, :]
    return jnp.asarray(seg, BF16), jnp.asarray(swap, BF16)


def _nmm_kernel(x_ref, nw_ref, sh_ref, sc_ref, w_ref, *rest, has_small, head_blocks, hd, rope, single):
    rest = list(rest)
    ws_ref = rest.pop(0) if has_small else None
    hw_ref, seg_ref, swap_ref = (rest.pop(0), rest.pop(0), rest.pop(0)) if head_blocks else (None, None, None)
    cos_ref, sin_ref = (rest.pop(0), rest.pop(0)) if rope else (None, None)
    o_ref = rest.pop(0)
    os_ref = rest.pop(0) if has_small else None
    (h_ref,) = rest
    if single:
        h = _norm_mod(x_ref[...], nw_ref[...], sh_ref[0], sc_ref[0])
        hb = h.astype(BF16)
        if has_small:
            os_ref[...] = _dot3(h, ws_ref[...])
        rows = head_blocks[0][2] if head_blocks else (None,) * (o_ref.shape[1] // LANES)
        cos = cos_ref[...] if rope else None
        sin = sin_ref[...] if rope else None
        cw = 2 * LANES
        n_out = o_ref.shape[1]
        parts = [jnp.dot(hb, w_ref[:, c0:c0 + cw], preferred_element_type=F32) for c0 in range(0, n_out, cw)]
        ys = [parts[g // 2][:, (g % 2) * LANES:(g % 2 + 1) * LANES] for g in range(n_out // LANES)]
        _store_head_groups(o_ref, ys, rows, hw_ref, seg_ref, swap_ref, cos, sin, hd)
        return
    j = pl.program_id(1)

    @pl.when(j == 0)
    def _():
        rows = x_ref.shape[0]
        step = _tile(rows, CHUNK_ROWS, SUBLANES)
        for r0 in range(0, rows, step):
            h = _norm_mod(x_ref[r0:r0 + step, :], nw_ref[...], sh_ref[0], sc_ref[0])
            h_ref[r0:r0 + step, :] = h.astype(BF16)
            if has_small:
                os_ref[r0:r0 + step, :] = _dot3(h, ws_ref[...])

    r = jnp.dot(h_ref[...], w_ref[...], preferred_element_type=F32)
    if not head_blocks:
        o_ref[...] = r.astype(o_ref.dtype)
        return
    tm, tn = r.shape
    plain = j >= 0
    for lo, hi, rows in head_blocks:
        inside = (j >= lo) & (j < hi)
        plain = plain & jnp.logical_not(inside)

        @pl.when(inside)
        def _(rows=rows):
            cos = cos_ref[...] if rope else None
            sin = sin_ref[...] if rope else None
            ys = [r[:, g * LANES:(g + 1) * LANES] for g in range(tn // LANES)]
            _store_head_groups(o_ref, ys, rows, hw_ref, seg_ref, swap_ref, cos, sin, hd)

    @pl.when(plain)
    def _():
        o_ref[...] = r.astype(o_ref.dtype)


def _norm_mod_matmul(x, nw, sh, sc, w, rows_per_group, w_small=None, tn=COLS_STREAMED, heads=None):
    m, d = x.shape
    n = w.shape[1]
    tn = _tile(n, tn, LANES)
    single = tn == n and n % (2 * LANES) == 0
    tm = math.gcd(_tile(m, ROWS_RESIDENT if single else ROWS_STREAMED, SUBLANES), rows_per_group)
    has_small = w_small is not None
    grp = lambda i, j: ((i * tm) // rows_per_group, 0, 0)
    in_specs = [pl.BlockSpec((tm, d), lambda i, j: (i, 0)),
                pl.BlockSpec((1, d), lambda i, j: (0, 0)),
                pl.BlockSpec((1, 1, d), grp),
                pl.BlockSpec((1, 1, d), grp),
                pl.BlockSpec((d, tn), lambda i, j: (0, j))]
    args = [x, nw.reshape(1, d), sh, sc, w]
    out_specs = [pl.BlockSpec((tm, tn), lambda i, j: (i, j))]
    out_shape = [jax.ShapeDtypeStruct((m, n), BF16)]
    if has_small:
        in_specs.append(pl.BlockSpec((d, LANES), lambda i, j: (0, 0)))
        args.append(w_small)
        out_specs.append(pl.BlockSpec((tm, LANES), lambda i, j: (i, 0)))
        out_shape.append(jax.ShapeDtypeStruct((m, LANES), F32))
    head_blocks, hd, rope = (), LANES, False
    if heads is not None:
        head_blocks, hd = tuple(heads["blocks"]), heads["hd"]
        hw = heads["weights"]
        in_specs += [pl.BlockSpec((hw.shape[0], 1, LANES), lambda i, j: (0, 0, 0)),
                     pl.BlockSpec((LANES, LANES), lambda i, j: (0, 0)),
                     pl.BlockSpec((LANES, LANES), lambda i, j: (0, 0))]
        args += [hw.reshape(hw.shape[0], 1, LANES), *_head_matrices(hd)]
        if heads["tables"] is not None:
            rope = True
            nt = heads["t"] // tm
            assert heads["t"] % tm == 0
            in_specs += [pl.BlockSpec((tm, LANES), lambda i, j: (i % nt, 0))] * 2
            args += list(heads["tables"])
    outs = pl.pallas_call(
        functools.partial(_nmm_kernel, has_small=has_small, head_blocks=head_blocks, hd=hd, rope=rope,
                          single=single),
        grid=(m // tm, n // tn),
        in_specs=in_specs,
        out_specs=out_specs,
        out_shape=out_shape,
        scratch_shapes=[pltpu.VMEM((tm, d), BF16)],
        compiler_params=_params("parallel", "arbitrary"),
        name="norm_mod_matmul",
    )(*args)
    return outs if has_small else outs[0]


def _rope_tables(t, head_dim):
    q = head_dim // 4
    pos = jnp.arange(t, dtype=I32)
    row = (pos // GRID_W).astype(F32)
    col = (pos % GRID_W).astype(F32)
    axis_dim = head_dim // 2
    inv_freq = ROPE_BASE ** (-jnp.arange(0, axis_dim, 2, dtype=F32) / axis_dim)
    lane = jnp.arange(LANES) % head_dim
    freq = inv_freq[lane % q]
    p = jnp.where((lane < head_dim // 2)[None, :], row[:, None], col[:, None])
    ang = p * freq[None, :]
    sign = jnp.where((lane % (2 * q)) < q, -1.0, 1.0)
    return jnp.cos(ang), jnp.sin(ang) * sign[None, :]


def _diff_attn_kernel(lv_ref, q_ref, *rest, lam_init, has_lat):
    if has_lat:
        kx_ref, vx_ref, kc_ref, vc_ref, w_ref, o_ref = rest
    else:
        kc_ref, vc_ref, w_ref, o_ref = rest
    lv = lv_ref[...]
    lam = (jnp.exp(jnp.sum(lv[0:1] * lv[1:2], keepdims=True))
           - jnp.exp(jnp.sum(lv[2:3] * lv[3:4], keepdims=True)) + lam_init)
    q = q_ref[...]

    def probs(c):
        sl = slice(c * DIFF_DK, (c + 1) * DIFF_DK)
        qc = q[:, sl]
        s_c = lax.dot_general(qc, kc_ref[:, sl], NT_DIMS, preferred_element_type=F32)
        m = jnp.max(s_c, axis=-1, keepdims=True)
        p_x = None
        if has_lat:
            s_x = lax.dot_general(qc, kx_ref[:, sl], NT_DIMS, preferred_element_type=F32)
            m = jnp.maximum(m, jnp.max(s_x, axis=-1, keepdims=True))
            p_x = jnp.exp2(s_x - m)
        p_c = jnp.exp2(s_c - m)
        l = jnp.sum(p_c, axis=-1, keepdims=True)
        if has_lat:
            l = l + jnp.sum(p_x, axis=-1, keepdims=True)
        return p_x, p_c, l

    p1x, p1c, l1 = probs(0)
    p2x, p2c, l2 = probs(1)
    ratio = lam * l1 * (1.0 / l2)
    o = jnp.dot((p1c - p2c * ratio).astype(BF16), vc_ref[...], preferred_element_type=F32)
    if has_lat:
        o = o + jnp.dot((p1x - p2x * ratio).astype(BF16), vx_ref[...], preferred_element_type=F32)
    o = o * (1.0 / l1)
    o = o * lax.rsqrt(jnp.mean(o * o, axis=-1, keepdims=True) + EPS)
    o_ref[...] = (o * w_ref[...] * (1.0 - lam_init)).astype(o_ref.dtype)


def _diff_attention(lam_vec, q, kc, zc, subln, lam_init, b, n_c, kx=None, zx=None, t=None):
    has_lat = kx is not None
    hw = 2 * DIFF_DK
    kblk0 = DIFF_HEADS
    vblk0 = (2 * DIFF_HEADS * hw) // DIFF_DV
    tq_all = t if has_lat else n_c
    tq = _tile(tq_all, DIFF_Q_ROWS, SUBLANES)
    nq = tq_all // tq
    in_specs = [pl.BlockSpec((4, DIFF_DK), lambda bi, h, qi: (0, 0)),
                pl.BlockSpec((tq, hw), lambda bi, h, qi: (bi * nq + qi, h))]
    args = [lam_vec, q]
    if has_lat:
        in_specs += [pl.BlockSpec((t, hw), lambda bi, h, qi: (bi, kblk0 + h)),
                     pl.BlockSpec((t, DIFF_DV), lambda bi, h, qi: (bi, vblk0 + h))]
        args += [kx, zx]
    in_specs += [pl.BlockSpec((n_c, hw), lambda bi, h, qi: (bi, kblk0 + h)),
                 pl.BlockSpec((n_c, DIFF_DV), lambda bi, h, qi: (bi, vblk0 + h)),
                 pl.BlockSpec((1, DIFF_DV), lambda bi, h, qi: (0, 0))]
    args += [kc, zc, subln.reshape(1, DIFF_DV)]
    return pl.pallas_call(
        functools.partial(_diff_attn_kernel, lam_init=lam_init, has_lat=has_lat),
        grid=(b, DIFF_HEADS, nq),
        in_specs=in_specs,
        out_specs=pl.BlockSpec((tq, DIFF_DV), lambda bi, h, qi: (bi * nq + qi, h)),
        out_shape=jax.ShapeDtypeStruct((b * tq_all, DIFF_HEADS * DIFF_DV), BF16),
        compiler_params=_params("parallel", "parallel", "arbitrary"),
        name="diff_attention",
    )(*args)


def _dn_prep_kernel(z_ref, cw_ref, o_ref, pad_ref, *, seg):
    halo = SUBLANES
    pad_ref[0:halo, :] = jnp.zeros((halo, LANES), F32)
    pad_ref[halo + seg:2 * halo + seg, :] = jnp.zeros((halo, LANES), F32)
    pad_ref[halo:halo + seg, :] = z_ref[...].astype(F32)
    kind = pl.program_id(1) // DN_HEADS
    qk_scale = jnp.where(kind == 0, DN_DK ** -0.5, 1.0).astype(F32)
    rows = _tile(seg, CHUNK_ROWS, SUBLANES)
    for r0 in range(0, seg, rows):
        acc = jnp.zeros((rows, LANES), F32)
        for j in range(DN_CONV):
            acc = acc + cw_ref[j:j + 1, :] * pad_ref[pl.ds(halo + r0 + j - DN_CONV // 2, rows), :]
        y = _silu(acc)
        nrm = y * lax.rsqrt(jnp.sum(y * y, axis=-1, keepdims=True) + EPS) * qk_scale
        o_ref[r0:r0 + rows, :] = jnp.where(kind < 2, nrm, y).astype(o_ref.dtype)


def _dn_prep(z, col0, conv_w, seg):
    m = z.shape[0]
    ncols = conv_w.shape[1]
    cblk0 = col0 // LANES
    cw = jnp.zeros((SUBLANES, ncols), F32).at[:DN_CONV].set(conv_w)
    return pl.pallas_call(
        functools.partial(_dn_prep_kernel, seg=seg),
        grid=(m // seg, ncols // LANES),
        in_specs=[pl.BlockSpec((seg, LANES), lambda s, g: (s, cblk0 + g)),
                  pl.BlockSpec((SUBLANES, LANES), lambda s, g: (0, g))],
        out_specs=pl.BlockSpec((seg, LANES), lambda s, g: (s, g)),
        out_shape=jax.ShapeDtypeStruct((m, ncols), BF16),
        scratch_shapes=[pltpu.VMEM((seg + 2 * SUBLANES, LANES), F32)],
        compiler_params=_params("parallel", "parallel"),
        name="dn_prep",
    )(z, cw)


def _dn_heads(reverse, q_ref, k_ref, v_ref, zs_ref, zst_ref, pr_ref, pca_ref, pcd_ref, o_ref, s_ref, ri, ci):
    c = DN_CHUNK
    if reverse:
        later, strict, later_t = ri <= ci, ri < ci, ri >= ci
    else:
        later, strict, later_t = ri >= ci, ri > ci, ri <= ci
    eye = (ri == ci).astype(F32)
    tri = later.astype(BF16)
    tri_t = later_t.astype(BF16)

    zs = zs_ref[...]
    beta_cols = jax.nn.sigmoid(zs)
    g_cols = -jnp.exp(pr_ref[0:1, :]) * _softplus(zs + pr_ref[1:2, :])
    g_hi = g_cols.astype(BF16)
    g_r1 = g_cols - g_hi.astype(F32)
    g_mid = g_r1.astype(BF16)
    g_lo = (g_r1 - g_mid.astype(F32)).astype(BF16)
    d = functools.partial(jnp.dot, preferred_element_type=F32)
    gc_cols = d(tri, g_hi) + d(tri, g_mid) + d(tri, g_lo)
    g_rows = -jnp.exp(pca_ref[...]) * _softplus(zst_ref[...] + pcd_ref[...])
    h_hi = g_rows.astype(BF16)
    h_r1 = g_rows - h_hi.astype(F32)
    h_mid = h_r1.astype(BF16)
    h_lo = (h_r1 - h_mid.astype(F32)).astype(BF16)
    gc_rows = d(h_hi, tri_t) + d(h_mid, tri_t) + d(h_lo, tri_t)

    dir_off = DN_HEADS if reverse else 0
    last = 0 if reverse else c - 1
    neg_inf = jnp.float32(-jnp.inf)
    heads = []
    for h in range(DN_HEADS):
        sl = slice(h * DN_DK, (h + 1) * DN_DK)
        cb = dir_off + h
        cg = 2 * DN_HEADS + dir_off + h
        beta = beta_cols[:, cb:cb + 1]
        gcol = gc_cols[:, cg:cg + 1]
        grow = gc_rows[cg:cg + 1, :]
        glast = grow[:, last:last + 1]
        q = q_ref[:, sl]
        k = k_ref[:, sl]
        kf = k.astype(F32)
        decay = jnp.exp(jnp.where(later, gcol - grow, neg_inf))
        kb = kf * beta
        both = lax.dot_general(jnp.concatenate([kb.astype(BF16), q], axis=0), k, NT_DIMS,
                               preferred_element_type=F32)
        lmat = jnp.where(strict, both[:c] * decay, 0.0)
        eg = jnp.exp(gcol)
        heads.append(dict(
            sl=sl, h=h, o_ref=o_ref, s_ref=s_ref, glast=glast, lmat=lmat, amat=both[c:] * decay,
            inv=eye - jnp.where((ri >> 1) == (ci >> 1), lmat, 0.0),
            rhs=jnp.concatenate([v_ref[:, sl].astype(F32) * beta, kb * eg], axis=1).astype(BF16),
            qe=(q.astype(F32) * eg).astype(BF16),
            kdec_t=(kf * jnp.exp(glast - gcol)).T.astype(BF16)))
    return heads


def _deltanet_kernel(qf_ref, kf_ref, vf_ref, zsf_ref, zstf_ref, qb_ref, kb_ref, vb_ref, zsb_ref, zstb_ref,
                     pr_ref, pca_ref, pcd_ref, s0f_ref, s0b_ref,
                     of_ref, ob_ref, soutf_ref, soutb_ref, sf_ref, sb_ref, *, n_chunks):
    step = pl.program_id(1)
    c = DN_CHUNK

    @pl.when(step == 0)
    def _():
        sf_ref[...] = s0f_ref[0]
        sb_ref[...] = s0b_ref[0]

    ri = lax.broadcasted_iota(I32, (c, c), 0)
    ci = lax.broadcasted_iota(I32, (c, c), 1)
    prm = (pr_ref, pca_ref, pcd_ref)
    heads = (_dn_heads(False, qf_ref, kf_ref, vf_ref, zsf_ref, zstf_ref, *prm, of_ref, sf_ref, ri, ci)
             + _dn_heads(True, qb_ref, kb_ref, vb_ref, zsb_ref, zstb_ref, *prm, ob_ref, sb_ref, ri, ci))
    lev = 1
    while (1 << lev) < c:
        blk = ((ri >> (lev + 1)) == (ci >> (lev + 1))) & ((ri >> lev) != (ci >> lev))
        half = [_mm(hd["inv"], jnp.where(blk, hd["lmat"], 0.0)) for hd in heads]
        for hd, t in zip(heads, half):
            hd["inv"] = hd["inv"] - _mm(t, hd["inv"])
        lev += 1
    uws = [_mm(hd["inv"], hd["rhs"]) for hd in heads]
    states = [hd["s_ref"][hd["h"]] for hd in heads]
    new_states = []
    for hd, uw, s in zip(heads, uws, states):
        ws_qs = _mm(jnp.concatenate([uw[:, DN_DV:].astype(BF16), hd["qe"]], axis=0), s)
        v_new = uw[:, :DN_DV] - ws_qs[:c]
        hd["o_ref"][:, hd["sl"]] = (ws_qs[c:] + _mm(hd["amat"], v_new)).astype(hd["o_ref"].dtype)
        new_states.append(s * jnp.exp(hd["glast"]) + _mm(hd["kdec_t"], v_new))
    for hd, s in zip(heads, new_states):
        hd["s_ref"][hd["h"]] = s

    @pl.when(step == n_chunks - 1)
    def _():
        soutf_ref[0] = sf_ref[...]
        soutb_ref[0] = sb_ref[...]


def _deltanet(dn, zs, zst, prm, s0f, s0b, b, seg):
    pr, pca, pcd = prm
    n = seg // DN_CHUNK
    hw = DN_HEADS * DN_DK
    fw = lambda bi, s: bi * n + s
    bw = lambda bi, s: bi * n + (n - 1 - s)
    state = pl.BlockSpec((1, DN_HEADS, DN_DK, DN_DV), lambda bi, s: (bi, 0, 0, 0))

    def chunk_specs(rb):
        return [pl.BlockSpec((DN_CHUNK, hw), lambda bi, s: (rb(bi, s), 0)),
                pl.BlockSpec((DN_CHUNK, hw), lambda bi, s: (rb(bi, s), 1)),
                pl.BlockSpec((DN_CHUNK, hw), lambda bi, s: (rb(bi, s), 2)),
                pl.BlockSpec((DN_CHUNK, LANES), lambda bi, s: (rb(bi, s), 0)),
                pl.BlockSpec((4 * DN_HEADS, DN_CHUNK), lambda bi, s: (0, rb(bi, s)))]

    out = lambda rb: pl.BlockSpec((DN_CHUNK, hw), lambda bi, s: (rb(bi, s), 0))
    o_shape = jax.ShapeDtypeStruct((b * seg, hw), BF16)
    s_shape = jax.ShapeDtypeStruct((b, DN_HEADS, DN_DK, DN_DV), F32)
    s_scratch = pltpu.VMEM((DN_HEADS, DN_DK, DN_DV), F32)
    return pl.pallas_call(
        functools.partial(_deltanet_kernel, n_chunks=n),
        grid=(b, n),
        in_specs=chunk_specs(fw) + chunk_specs(bw) + [
            pl.BlockSpec((SUBLANES, LANES), lambda bi, s: (0, 0)),
            pl.BlockSpec((4 * DN_HEADS, LANES), lambda bi, s: (0, 0)),
            pl.BlockSpec((4 * DN_HEADS, LANES), lambda bi, s: (0, 0)),
            state, state],
        out_specs=[out(fw), out(bw), state, state],
        out_shape=[o_shape, o_shape, s_shape, s_shape],
        scratch_shapes=[s_scratch, s_scratch],
        compiler_params=_params("parallel", "arbitrary"),
        name="deltanet",
    )(dn, dn, dn, zs, zst, dn, dn, dn, zs, zst, pr, pca, pcd, s0f, s0b)


def _deltanet_params(a_log, dt_bias):
    nh = 2 * DN_HEADS
    a = a_log.reshape(nh).astype(F32)
    dtb = dt_bias.reshape(nh).astype(F32)
    pr = jnp.zeros((SUBLANES, LANES), F32).at[0, nh:2 * nh].set(a).at[1, nh:2 * nh].set(dtb)
    pca = jnp.zeros((2 * nh, LANES), F32).at[nh:].set(jnp.broadcast_to(a[:, None], (nh, LANES)))
    pcd = jnp.zeros((2 * nh, LANES), F32).at[nh:].set(jnp.broadcast_to(dtb[:, None], (nh, LANES)))
    return pr, pca, pcd


def _outproj_ab_kernel(od_ref, of_ref, ob_ref, gate_ref, nw_ref, w_ref, x_ref, g_ref, o_ref):
    nd = od_ref.shape[1]
    y = jnp.dot(od_ref[...], w_ref[:nd, :], preferred_element_type=F32)
    gated = []
    for h in range(DN_HEADS):
        sl = slice(h * DN_DV, (h + 1) * DN_DV)
        o = of_ref[:, sl].astype(F32) + ob_ref[:, sl].astype(F32)
        o = o * lax.rsqrt(jnp.mean(o * o, axis=-1, keepdims=True) + EPS) * nw_ref[...]
        gated.append((o * _silu(gate_ref[:, sl].astype(F32))).astype(BF16))
    y = y + jnp.dot(jnp.concatenate(gated, axis=1), w_ref[nd:, :], preferred_element_type=F32)
    o_ref[...] = x_ref[...] + g_ref[0] * y


def _outproj_ab(od, o_f, o_b, z, gate_col0, out_norm, w, x, gate, rows_per_group):
    m, d = x.shape
    nd, nn = od.shape[1], o_f.shape[1]
    tm = math.gcd(_tile(m, ROWS_RESIDENT, SUBLANES), rows_per_group)
    grp = lambda i: ((i * tm) // rows_per_group, 0, 0)
    gblk = gate_col0 // nn
    return pl.pallas_call(
        _outproj_ab_kernel,
        grid=(m // tm,),
        in_specs=[pl.BlockSpec((tm, nd), lambda i: (i, 0)),
                  pl.BlockSpec((tm, nn), lambda i: (i, 0)),
                  pl.BlockSpec((tm, nn), lambda i: (i, 0)),
                  pl.BlockSpec((tm, nn), lambda i: (i, gblk)),
                  pl.BlockSpec((1, DN_DV), lambda i: (0, 0)),
                  pl.BlockSpec((nd + nn, d), lambda i: (0, 0)),
                  pl.BlockSpec((tm, d), lambda i: (i, 0)),
                  pl.BlockSpec((1, 1, d), grp)],
        out_specs=pl.BlockSpec((tm, d), lambda i: (i, 0)),
        out_shape=jax.ShapeDtypeStruct((m, d), F32),
        compiler_params=_params("parallel"),
        name="outproj_ab",
    )(od, o_f, o_b, z, out_norm.reshape(1, DN_DV), w, x, gate)


def _outproj_kernel(a_ref, w_ref, x_ref, g_ref, o_ref):
    o_ref[...] = x_ref[...] + g_ref[0] * jnp.dot(a_ref[...], w_ref[...], preferred_element_type=F32)


def _outproj(a, w, x, gate, rows_per_group):
    m, d = x.shape
    kdim = a.shape[1]
    tm = math.gcd(_tile(m, ROWS_OUTPROJ, SUBLANES), rows_per_group)
    tn = d
    grp = lambda i, j: ((i * tm) // rows_per_group, 0, j)
    return pl.pallas_call(
        _outproj_kernel,
        grid=(m // tm, d // tn),
        in_specs=[pl.BlockSpec((tm, kdim), lambda i, j: (i, 0)),
                  pl.BlockSpec((kdim, tn), lambda i, j: (0, j)),
                  pl.BlockSpec((tm, tn), lambda i, j: (i, j)),
                  pl.BlockSpec((1, 1, tn), grp)],
        out_specs=pl.BlockSpec((tm, tn), lambda i, j: (i, j)),
        out_shape=jax.ShapeDtypeStruct((m, d), F32),
        compiler_params=_params("parallel", "arbitrary"),
        name="outproj",
    )(a, w, x, gate)


def _swa_kernel(sink_ref, bias_ref, q_ref, k0_ref, k1_ref, k2_ref, v0_ref, v1_ref, v2_ref, kc_ref, vc_ref,
                o_ref):
    kvh = pl.program_id(1)
    qb = Q_BLOCK
    npair = SWA_GROUP // 2
    lane = lax.broadcasted_iota(I32, (qb, LANES), 1)
    q = q_ref[...]
    parts = []
    for p in range(npair):
        qp = q[:, p * LANES:(p + 1) * LANES]
        parts.append(jnp.where(lane < SWA_DH, qp, jnp.zeros_like(qp)))
        parts.append(jnp.where(lane >= SWA_DH, qp, jnp.zeros_like(qp)))
    qq = jnp.concatenate(parts, axis=0)
    k_lat = jnp.concatenate([k0_ref[...], k1_ref[...], k2_ref[...]], axis=0)
    v_lat = jnp.concatenate([v0_ref[...], v1_ref[...], v2_ref[...]], axis=0)
    s_lat = lax.dot_general(qq, k_lat, NT_DIMS, preferred_element_type=F32)
    s_ctx = lax.dot_general(qq, kc_ref[...], NT_DIMS, preferred_element_type=F32)
    bias = bias_ref[0]
    vc = vc_ref[...]
    outs = []
    for g in range(SWA_GROUP):
        rs = slice(g * qb, (g + 1) * qb)
        sl = s_lat[rs] + bias
        sc = s_ctx[rs]
        sink = sink_ref[kvh, g] * LOG2E
        m = jnp.maximum(jnp.maximum(jnp.max(sl, axis=-1, keepdims=True),
                                    jnp.max(sc, axis=-1, keepdims=True)), sink)
        el = jnp.exp2(sl - m)
        ec = jnp.exp2(sc - m)
        l = jnp.sum(el, axis=-1, keepdims=True) + jnp.sum(ec, axis=-1, keepdims=True) + jnp.exp2(sink - m)
        o = (jnp.dot(el.astype(BF16), v_lat, preferred_element_type=F32)
             + jnp.dot(ec.astype(BF16), vc, preferred_element_type=F32))
        outs.append(o * (1.0 / l))
    for p in range(npair):
        o_ref[:, p * LANES:(p + 1) * LANES] = jnp.where(lane < SWA_DH, outs[2 * p], outs[2 * p + 1]).astype(o_ref.dtype)


def _swa_attention(sink, q, kx, vx, kc, vc, b, t, n_c):
    qb = Q_BLOCK
    nb = t // qb
    gw = SWA_GROUP * SWA_DH
    lat = lambda off: pl.BlockSpec(
        (qb, LANES), lambda bi, h, i: (bi * nb + jnp.clip(i + off, 0, nb - 1), h))
    ctx = pl.BlockSpec((n_c, LANES), lambda bi, h, i: (bi, h))
    r_io = np.arange(qb)[:, None]
    c_io = np.arange(3 * qb)[None, :]
    inside = np.abs(r_io + qb - c_io) <= WINDOW
    variants = [inside & ((c_io >= qb) | (v & 1 == 0)) & ((c_io < 2 * qb) | (v & 2 == 0)) for v in range(4)]
    bias = jnp.asarray(np.where(np.stack(variants), 0.0, -np.inf), F32)
    return pl.pallas_call(
        _swa_kernel,
        grid=(b, SWA_KV_HEADS, nb),
        in_specs=[pl.BlockSpec(memory_space=pltpu.SMEM),
                  pl.BlockSpec((1, qb, 3 * qb),
                               lambda bi, h, i: ((i == 0).astype(I32) + 2 * (i == nb - 1).astype(I32), 0, 0)),
                  pl.BlockSpec((qb, gw), lambda bi, h, i: (bi * nb + i, h)),
                  lat(-1), lat(0), lat(1), lat(-1), lat(0), lat(1), ctx, ctx],
        out_specs=pl.BlockSpec((qb, gw), lambda bi, h, i: (bi * nb + i, h)),
        out_shape=jax.ShapeDtypeStruct((b * t, SWA_HEADS * SWA_DH), BF16),
        compiler_params=_params("parallel", "parallel", "arbitrary"),
        name="swa_attention",
    )(sink, bias, q, kx, kx, kx, vx, vx, vx, kc, vc)


def _dup_heads(a, col0):
    m = a.shape[0]
    h = a[:, col0:col0 + SWA_KV_HEADS * SWA_DH].reshape(m, SWA_KV_HEADS, 1, SWA_DH)
    return jnp.broadcast_to(h, (m, SWA_KV_HEADS, LANES // SWA_DH, SWA_DH)).reshape(m, SWA_KV_HEADS * LANES)


def _first_max(vals, iota, size, axis):
    m = jnp.max(vals, axis=axis, keepdims=True)
    first = jnp.min(jnp.where(vals == m, iota, size), axis=axis, keepdims=True)
    return m, first


def _router_kernel(x_ref, nw_ref, sh_ref, sc_ref, rwt_ref, rb_ref, c0_ref,
                   hp_ref, idx_ref, wt_ref, rank_ref, cnt_ref, carry_ref):
    @pl.when(pl.program_id(0) == 0)
    def _():
        carry_ref[...] = c0_ref[...]

    h = _norm_mod(x_ref[...], nw_ref[...], sh_ref[0], sc_ref[0])
    tm, d = h.shape
    _store_token_tiles(hp_ref, 0, _pack_pairs(h[:, :d // 2], h[:, d // 2:]))
    scores = jax.nn.sigmoid(_dot3(rwt_ref[...], h, NT_DIMS))
    sel = scores + rb_ref[...]
    neg = jnp.float32(-jnp.inf)

    g_io = lax.broadcasted_iota(I32, (GROUP_SIZE, tm), 0)
    gs_rows = []
    for g in range(N_GROUPS):
        sg = sel[g * GROUP_SIZE:(g + 1) * GROUP_SIZE]
        m1, f1 = _first_max(sg, g_io, GROUP_SIZE, 0)
        m2 = jnp.max(jnp.where(g_io == f1, neg, sg), axis=0, keepdims=True)
        gs_rows.append(m1 + m2)
    cur = jnp.concatenate(gs_rows, axis=0)
    n_io = lax.broadcasted_iota(I32, (N_GROUPS, tm), 0)
    gmask = jnp.zeros((N_GROUPS, tm), I32)
    for _ in range(TOPK_GROUPS):
        _, f = _first_max(cur, n_io, N_GROUPS, 0)
        hit = n_io == f
        gmask = jnp.where(hit, 1, gmask)
        cur = jnp.where(hit, neg, cur)
    cur = jnp.concatenate(
        [jnp.where(gmask[g:g + 1] > 0, sel[g * GROUP_SIZE:(g + 1) * GROUP_SIZE], neg) for g in range(N_GROUPS)],
        axis=0)

    e_io = lax.broadcasted_iota(I32, (N_EXPERTS, tm), 0)
    chosen = jnp.zeros((N_EXPERTS, tm), F32)
    idx_rows, w_rows = [], []
    for _ in range(TOP_K):
        _, f = _first_max(cur, e_io, N_EXPERTS, 0)
        hit = e_io == f
        idx_rows.append(f)
        w_rows.append(jnp.sum(jnp.where(hit, scores, 0.0), axis=0, keepdims=True))
        chosen = jnp.where(hit, 1.0, chosen)
        cur = jnp.where(hit, neg, cur)
    idx = jnp.concatenate(idx_rows, axis=0)
    w = jnp.concatenate(w_rows, axis=0)
    idx_ref[...] = idx
    wt_ref[...] = w * (1.0 / jnp.sum(w, axis=0, keepdims=True)) * ROUTED_SCALE

    onehot = chosen.astype(BF16)
    before = (lax.broadcasted_iota(I32, (tm, tm), 0) < lax.broadcasted_iota(I32, (tm, tm), 1)).astype(BF16)
    base = carry_ref[:, 0:1] + jnp.dot(onehot, before, preferred_element_type=F32)
    rank_ref[...] = jnp.concatenate(
        [jnp.sum(jnp.where(e_io == idx_rows[k], base, 0.0), axis=0, keepdims=True) for k in range(TOP_K)],
        axis=0).astype(I32)
    carry_ref[...] = carry_ref[...] + jnp.sum(chosen, axis=1, keepdims=True)
    cnt_ref[...] = carry_ref[...]


def _router(x, nw, sh, sc, rwt, rb, counts0, rows_per_group):
    m, d = x.shape
    tm = math.gcd(_tile(m, ROUTER_TOKENS, LANES), rows_per_group)
    assert (d // 2) % LANES == 0
    tr = d // 2 // LANES
    grp = lambda i: ((i * tm) // rows_per_group, 0, 0)
    tok = lambda rows: pl.BlockSpec((rows, tm), lambda i: (0, i))
    return pl.pallas_call(
        _router_kernel,
        grid=(m // tm,),
        in_specs=[pl.BlockSpec((tm, d), lambda i: (i, 0)),
                  pl.BlockSpec((1, d), lambda i: (0, 0)),
                  pl.BlockSpec((1, 1, d), grp),
                  pl.BlockSpec((1, 1, d), grp),
                  pl.BlockSpec((N_EXPERTS, d), lambda i: (0, 0)),
                  pl.BlockSpec((N_EXPERTS, 1), lambda i: (0, 0)),
                  pl.BlockSpec((N_EXPERTS, LANES), lambda i: (0, 0))],
        out_specs=[pl.BlockSpec((tm * tr, LANES), lambda i: (i, 0)),
                   tok(TOP_K), tok(TOP_K), tok(TOP_K),
                   pl.BlockSpec((N_EXPERTS, LANES), lambda i: (0, 0))],
        out_shape=[jax.ShapeDtypeStruct((m * tr, LANES), U32),
                   jax.ShapeDtypeStruct((TOP_K, m), I32),
                   jax.ShapeDtypeStruct((TOP_K, m), F32),
                   jax.ShapeDtypeStruct((TOP_K, m), I32),
                   jax.ShapeDtypeStruct((N_EXPERTS, LANES), F32)],
        scratch_shapes=[pltpu.VMEM((N_EXPERTS, LANES), F32)],
        compiler_params=_params("arbitrary"),
        name="moe_router",
    )(x, nw.reshape(1, d), sh, sc, rwt, rb.reshape(N_EXPERTS, 1), counts0)


def _dispatch_kernel(nv_ref, dest_ref, *rest, tr, tiles):
    hp_refs = rest[:len(tiles)]
    xs_ref, zero_ref, sem = rest[len(tiles):]
    i = pl.program_id(0)
    tm = hp_refs[0].shape[0] // tr
    blk_rows = MOE_BLOCK * tr

    @pl.when(i == 0)
    def _():
        zero_ref[...] = jnp.zeros(zero_ref.shape, U32)

        def fill(blk, carry):
            @pl.when(nv_ref[blk] < MOE_BLOCK)
            def _():
                cp = pltpu.make_async_copy(
                    zero_ref, xs_ref.at[pl.ds(pl.multiple_of(blk * blk_rows, blk_rows), blk_rows)], sem)
                cp.start()
                cp.wait()
            return carry

        lax.fori_loop(0, nv_ref.shape[0], fill, 0)

    def scatter(hp_ref):
        def copy(t, k):
            return _tile_copy(hp_ref, t * tr, xs_ref, dest_ref[t * TOP_K + k], tr, sem)

        def start(t, carry):
            for k in range(TOP_K):
                copy(t, k).start(priority=k % 2)
            return carry

        def wait(t, carry):
            for k in range(TOP_K):
                copy(t, k).wait()
            return carry

        lax.fori_loop(0, tm, start, 0)
        lax.fori_loop(0, tm, wait, 0)

    first = 0
    for hp_ref, n in zip(hp_refs, tiles):
        if len(tiles) == 1:
            scatter(hp_ref)
        else:
            pl.when((i >= first) & (i < first + n))(functools.partial(scatter, hp_ref))
        first += n


def _dispatch(block_nv, dest, hps, n_rows, tr):
    tm = functools.reduce(math.gcd, [hp.shape[0] // tr for hp in hps] + [DISPATCH_TOKENS])
    tiles = tuple(hp.shape[0] // tr // tm for hp in hps)
    firsts = [sum(tiles[:s]) for s in range(len(tiles))]
    hp_specs = [pl.BlockSpec((tm * tr, LANES), lambda i, nv, f=f, n=n: (jnp.clip(i - f, 0, n - 1), 0))
                for f, n in zip(firsts, tiles)]
    return pl.pallas_call(
        functools.partial(_dispatch_kernel, tr=tr, tiles=tiles),
        grid_spec=pltpu.PrefetchScalarGridSpec(
            num_scalar_prefetch=1,
            grid=(sum(tiles),),
            in_specs=[pl.BlockSpec((tm * TOP_K,), lambda i, nv: (i,), memory_space=pltpu.SMEM)] + hp_specs,
            out_specs=pl.BlockSpec(memory_space=pl.ANY),
            scratch_shapes=[pltpu.VMEM((MOE_BLOCK * tr, LANES), U32), pltpu.SemaphoreType.DMA(())]),
        out_shape=jax.ShapeDtypeStruct((n_rows * tr, LANES), U32),
        compiler_params=_params("arbitrary"),
        name="moe_dispatch",
    )(block_nv, dest, *hps)


def _gffn_kernel(be_ref, nv_ref, xs_ref, w1_ref, w3_ref, w2_ref, ys_ref, w1b_ref, w3b_ref, w2b_ref, *, tr):
    i = pl.program_id(0)
    nv = nv_ref[i]
    prev = be_ref[jnp.maximum(i - 1, 0)]

    @pl.when((i == 0) | (be_ref[i] != prev))
    def _():
        w1b_ref[...] = w1_ref[0, 0].astype(BF16)
        w3b_ref[...] = w3_ref[0, 0].astype(BF16)
        w2b_ref[...] = w2_ref[0, 0].astype(BF16)

    @pl.when(nv > 0)
    def _():
        lo, hi = _unpack_pairs(_load_token_tiles(xs_ref, 0, MOE_BLOCK, tr))
        dh = lo.shape[1]
        lo = lo.astype(BF16)
        hi = hi.astype(BF16)
        d = functools.partial(jnp.dot, preferred_element_type=F32)
        a = d(lo, w1b_ref[:dh, :]) + d(hi, w1b_ref[dh:, :])
        g = d(lo, w3b_ref[:dh, :]) + d(hi, w3b_ref[dh:, :])
        y = d((_silu(a) * g).astype(BF16), w2b_ref[...])
        _store_token_tiles(ys_ref, 0, _pack_pairs(y[:, :dh], y[:, dh:]))

    @pl.when(nv == 0)
    def _():
        ys_ref[...] = jnp.zeros(ys_ref.shape, U32)


def _grouped_ffn(block_e, block_nv, xs, w1, w3, w2, layer, tr):
    _, _, d, f = w1.shape
    blk_rows = MOE_BLOCK * tr
    nb = xs.shape[0] // blk_rows
    return pl.pallas_call(
        functools.partial(_gffn_kernel, tr=tr),
        grid_spec=pltpu.PrefetchScalarGridSpec(
            num_scalar_prefetch=2,
            grid=(nb,),
            in_specs=[pl.BlockSpec((blk_rows, LANES), lambda i, be, nv: (i, 0)),
                      pl.BlockSpec((1, 1, d, f), lambda i, be, nv: (layer, be[i], 0, 0)),
                      pl.BlockSpec((1, 1, d, f), lambda i, be, nv: (layer, be[i], 0, 0)),
                      pl.BlockSpec((1, 1, f, d), lambda i, be, nv: (layer, be[i], 0, 0))],
            out_specs=pl.BlockSpec((blk_rows, LANES), lambda i, be, nv: (i, 0)),
            scratch_shapes=[pltpu.VMEM((d, f), BF16), pltpu.VMEM((d, f), BF16), pltpu.VMEM((f, d), BF16)]),
        out_shape=jax.ShapeDtypeStruct(xs.shape, U32),
        compiler_params=_params("arbitrary"),
        name="moe_grouped_ffn",
    )(block_e, block_nv, xs, w1, w3, w2)


def _combine_kernel(dest_ref, dest_next_ref, x_ref, hp_ref, wt_ref, g_ref, ws1_ref, ws3_ref, ws2_ref, ys_ref,
                    o_ref, buf_a, buf_b, sem_a, sem_b, *, tr):
    i = pl.program_id(0)
    tm = x_ref.shape[0]
    dh = tr * LANES

    def copy(dref, t, k, buf, sem):
        return _tile_copy(ys_ref, dref[t * TOP_K + k], buf, (k * tm + t) * tr, tr, sem)

    def wait_all(buf, sem):
        def wait(t, carry):
            for k in range(TOP_K):
                copy(dest_ref, t, k, buf, sem).wait()
            return carry
        lax.fori_loop(0, tm, wait, 0)

    @pl.when(i == 0)
    def _():
        def start(t, carry):
            for k in range(TOP_K):
                copy(dest_ref, t, k, buf_a, sem_a).start(priority=k % 2)
            return carry
        lax.fori_loop(0, tm, start, 0)

    def step(buf, sem, buf_next, sem_next):
        wait_all(buf, sem)
        for t in range(tm):
            for k in range(TOP_K):
                copy(dest_next_ref, t, k, buf_next, sem_next).start(priority=k % 2)
        lo, hi = _unpack_pairs(_load_token_tiles(hp_ref, 0, tm, tr))
        lo = lo.astype(BF16)
        hi = hi.astype(BF16)
        d = functools.partial(jnp.dot, preferred_element_type=F32)
        a = d(lo, ws1_ref[:dh, :]) + d(hi, ws1_ref[dh:, :])
        g = d(lo, ws3_ref[:dh, :]) + d(hi, ws3_ref[dh:, :])
        shared = d((_silu(a) * g).astype(BF16), ws2_ref[...])
        acc_lo = jnp.zeros((tm, dh), F32)
        acc_hi = jnp.zeros((tm, dh), F32)
        for k in range(TOP_K):
            ylo, yhi = _unpack_pairs(_load_token_tiles(buf, k * tm * tr, tm, tr))
            wk = wt_ref[:, k:k + 1]
            acc_lo = acc_lo + wk * ylo
            acc_hi = acc_hi + wk * yhi
        o_ref[:, :dh] = x_ref[:, :dh] + g_ref[0][:, :dh] * (acc_lo + shared[:, :dh])
        o_ref[:, dh:] = x_ref[:, dh:] + g_ref[0][:, dh:] * (acc_hi + shared[:, dh:])

        @pl.when(i == pl.num_programs(0) - 1)
        def _():
            wait_all(buf_next, sem_next)

    @pl.when(i % 2 == 0)
    def _():
        step(buf_a, sem_a, buf_b, sem_b)

    @pl.when(i % 2 == 1)
    def _():
        step(buf_b, sem_b, buf_a, sem_a)


def _combine(dest, x, hp, wt, gate, ws1, ws3, ws2, ys, rows_per_group, tr):
    m, d = x.shape
    f = ws1.shape[1]
    tm = math.gcd(_tile(m, COMBINE_TOKENS, LANES), rows_per_group)
    nt = m // tm
    grp = lambda i: ((i * tm) // rows_per_group, 0, 0)
    buf = pltpu.VMEM((TOP_K * tm * tr, LANES), U32)
    return pl.pallas_call(
        functools.partial(_combine_kernel, tr=tr),
        grid=(nt,),
        in_specs=[pl.BlockSpec((tm * TOP_K,), lambda i: (i,), memory_space=pltpu.SMEM),
                  pl.BlockSpec((tm * TOP_K,), lambda i: (jnp.minimum(i + 1, nt - 1),), memory_space=pltpu.SMEM),
                  pl.BlockSpec((tm, d), lambda i: (i, 0)),
                  pl.BlockSpec((tm * tr, LANES), lambda i: (i, 0)),
                  pl.BlockSpec((tm, TOP_K), lambda i: (i, 0)),
                  pl.BlockSpec((1, 1, d), grp),
                  pl.BlockSpec((d, f), lambda i: (0, 0)),
                  pl.BlockSpec((d, f), lambda i: (0, 0)),
                  pl.BlockSpec((f, d), lambda i: (0, 0)),
                  pl.BlockSpec(memory_space=pl.ANY)],
        out_specs=pl.BlockSpec((tm, d), lambda i: (i, 0)),
        out_shape=jax.ShapeDtypeStruct((m, d), F32),
        scratch_shapes=[buf, buf, pltpu.SemaphoreType.DMA(()), pltpu.SemaphoreType.DMA(())],
        compiler_params=_params("arbitrary"),
        name="moe_combine",
    )(dest, dest, x, hp, wt, gate, ws1, ws3, ws2, ys)


def _moe(streams, layer, nw, rw, rb, w1, w3, w2, ws1, ws3, ws2):
    d = streams[0][0].shape[1]
    rwt = rw.T
    counts = jnp.zeros((N_EXPERTS, LANES), F32)
    routed = []
    for x, sh, sc, _, rpg in streams:
        hp, idx, wt, rank, counts = _router(x, nw, sh, sc, rwt, rb, counts, rpg)
        routed.append((hp, idx, wt, rank))
    n_assign = sum(s[0].shape[0] for s in streams) * TOP_K
    n_blocks = (n_assign + N_EXPERTS * (MOE_BLOCK - 1) + MOE_BLOCK - 1) // MOE_BLOCK
    cnt = counts[:, 0].astype(I32)
    padded = (cnt + MOE_BLOCK - 1) // MOE_BLOCK * MOE_BLOCK
    pad_end = jnp.cumsum(padded)
    pad_start = pad_end - padded
    bstart = jnp.arange(n_blocks, dtype=I32) * MOE_BLOCK
    block_e = jnp.minimum(jnp.sum((bstart[:, None] >= pad_end[None, :]).astype(I32), axis=1), N_EXPERTS - 1)
    block_nv = jnp.clip(cnt[block_e] - (bstart - pad_start[block_e]), 0, MOE_BLOCK).astype(I32)
    e_ar = jnp.arange(N_EXPERTS, dtype=I32)
    tr = d // 2 // LANES
    dests = [((jnp.sum(jnp.where(idx[:, :, None] == e_ar, pad_start, 0), axis=-1) + rank) * tr).T.reshape(-1)
             for _, idx, _, rank in routed]
    dest_all = jnp.concatenate(dests, axis=0) if len(dests) > 1 else dests[0]
    xs = _dispatch(block_nv, dest_all, [r[0] for r in routed], n_blocks * MOE_BLOCK, tr)
    ys = _grouped_ffn(block_e, block_nv, xs, w1, w3, w2, layer, tr)
    ws1b, ws3b, ws2b = ws1.astype(BF16), ws3.astype(BF16), ws2.astype(BF16)
    return [_combine(dest, x, hp, wt.T, gate, ws1b, ws3b, ws2b, ys, rpg, tr)
            for (x, _, _, gate, rpg), (hp, _, wt, _), dest in zip(streams, routed, dests)]


def _mixer_ab(xs, cs, mx, mc, b, t, n_c, layer, nw, w_in, w_out, q_norm, k_norm, lam_vec, subln, conv_w,
              a_log, dt_bias, out_norm):
    n_main = w_in.shape[1] - 4 * DN_HEADS
    assert n_main % LANES == 0
    w_main = w_in[:, :n_main].astype(BF16)
    w_small = jnp.zeros((w_in.shape[0], LANES), F32).at[:, :4 * DN_HEADS].set(w_in[:, n_main:])
    nqk = 2 * DIFF_HEADS * DIFF_DK
    assert n_main % nqk == 0
    tabs = _rope_tables(t, DIFF_DK)
    hw = jnp.stack([q_norm * (DIFF_DK ** -0.5 * LOG2E), k_norm])
    groups = nqk // LANES
    blocks = ((0, 1, (0,) * groups), (1, 2, (1,) * groups))
    z_x, zs_x = _norm_mod_matmul(xs, nw, mx[0], mx[1], w_main, t, w_small, tn=nqk,
                                 heads=dict(hd=DIFF_DK, weights=hw, blocks=blocks, tables=tabs, t=t))
    z_c, zs_c = _norm_mod_matmul(cs, nw, mc[0], mc[1], w_main, b * n_c, w_small, tn=nqk,
                                 heads=dict(hd=DIFF_DK, weights=hw, blocks=blocks, tables=None, t=t))
    lam_init = 0.8 - 0.6 * math.exp(-0.3 * layer)
    od_x = _diff_attention(lam_vec, z_x, z_c, z_c, subln, lam_init, b, n_c, kx=z_x, zx=z_x, t=t)
    od_c = _diff_attention(lam_vec, z_c, z_c, z_c, subln, lam_init, b, n_c)
    dn_col0 = 2 * nqk + DIFF_HEADS * DIFF_DV
    dn_x = _dn_prep(z_x, dn_col0, conv_w, t)
    dn_c = _dn_prep(z_c, dn_col0, conv_w, n_c)
    zst_x = zs_x[:, :4 * DN_HEADS].T
    zst_c = zs_c[:, :4 * DN_HEADS].T
    prm = _deltanet_params(a_log, dt_bias)
    s0 = jnp.zeros((b, DN_HEADS, DN_DK, DN_DV), F32)
    o_cf, o_cb, s_cf, s_cb = _deltanet(dn_c, zs_c, zst_c, prm, s0, s0, b, n_c)
    o_xf, o_xb, _, _ = _deltanet(dn_x, zs_x, zst_x, prm, s_cf, s_cb, b, t)
    gate_col0 = dn_col0 + conv_w.shape[1]
    w_out_b = w_out.astype(BF16)
    x1 = _outproj_ab(od_x, o_xf, o_xb, z_x, gate_col0, out_norm, w_out_b, xs, mx[2], t)
    c1 = _outproj_ab(od_c, o_cf, o_cb, z_c, gate_col0, out_norm, w_out_b, cs, mc[2], b * n_c)
    return x1, c1


def _mixer_swa(xs, cs, mx, mc, b, t, n_c, nw, w_in, w_out, q_norm, k_norm, sink):
    nq = SWA_HEADS * SWA_DH
    nkv = SWA_KV_HEADS * SWA_DH
    w_b = w_in.astype(BF16)
    tabs = _rope_tables(t, SWA_DH)
    rep = LANES // SWA_DH
    hw = jnp.stack([jnp.tile(q_norm, rep) * (SWA_DH ** -0.5 * LOG2E), jnp.tile(k_norm, rep)])
    q_rows = (0,) * (nq // LANES)
    kv_rows = (1,) * (nkv // LANES) + (None,) * (nkv // LANES)
    z_x = _norm_mod_matmul(xs, nw, mx[0], mx[1], w_b, t, tn=w_b.shape[1],
                           heads=dict(hd=SWA_DH, weights=hw, tables=tabs, t=t, blocks=((0, 1, q_rows + kv_rows),)))
    z_c = _norm_mod_matmul(cs, nw, mc[0], mc[1], w_b[:, nq:], b * n_c, tn=2 * nkv,
                           heads=dict(hd=SWA_DH, weights=hw, tables=None, t=t, blocks=((0, 1, kv_rows),)))
    att = _swa_attention(sink.reshape(SWA_KV_HEADS, SWA_GROUP), z_x, _dup_heads(z_x, nq), _dup_heads(z_x, nq + nkv),
                         _dup_heads(z_c, 0), _dup_heads(z_c, nkv), b, t, n_c)
    return _outproj(att, w_out.astype(BF16), xs, mx[2], t)


def kernel(x, c, ctx, c_ctx, mod_w, mod_b, norm_mix, norm_ffn, ab_w_in, ab_w_out, diff_q_norm, diff_k_norm, diff_lambda, diff_subln, dn_conv, dn_a_log, dn_dt_bias, dn_out_norm, swa_w_in, swa_w_out, swa_q_norm, swa_k_norm, swa_sink, router_w, router_bias, exp_w1, exp_w3, exp_w2, shared_w1, shared_w3, shared_w2):
    b, t, d = x.shape
    n_c = ctx.shape[1]
    depth = mod_w.shape[0]
    assert depth == 2 and t % Q_BLOCK == 0 and t % DN_CHUNK == 0 and n_c % DN_CHUNK == 0
    xs = x.reshape(b * t, d)
    cs = ctx.reshape(b * n_c, d)
    n_mod = -(-(b + 1) // SUBLANES) * SUBLANES
    a_mod = jnp.zeros((n_mod, d), F32).at[0].set(c_ctx).at[1:1 + b].set(c)
    for layer in range(depth):
        with_ctx = layer < depth - 1
        p = layer // 2
        mod = _modulation(a_mod, mod_w, layer, mod_b[layer])
        mc = [mod[0:1, j * d:(j + 1) * d].reshape(1, 1, d) for j in range(6)]
        mx = [mod[1:1 + b, j * d:(j + 1) * d].reshape(b, 1, d) for j in range(6)]
        if layer % 2 == 0:
            xs, c_new = _mixer_ab(xs, cs, mx, mc, b, t, n_c, layer, norm_mix[layer], ab_w_in[p], ab_w_out[p],
                                  diff_q_norm[p], diff_k_norm[p], diff_lambda[p], diff_subln[p], dn_conv[p],
                                  dn_a_log[p], dn_dt_bias[p], dn_out_norm[p])
        else:
            assert not with_ctx
            xs = _mixer_swa(xs, cs, mx, mc, b, t, n_c, norm_mix[layer], swa_w_in[p], swa_w_out[p],
                            swa_q_norm[p], swa_k_norm[p], swa_sink[p])
            c_new = None
        moe_w = (layer, norm_ffn[layer], router_w[layer], router_bias[layer], exp_w1, exp_w3, exp_w2,
                 shared_w1[layer], shared_w3[layer], shared_w2[layer])
        if with_ctx:
            cs, xs = _moe([(c_new, mc[3], mc[4], mc[5], b * n_c), (xs, mx[3], mx[4], mx[5], t)], *moe_w)
        else:
            (xs,) = _moe([(xs, mx[3], mx[4], mx[5], t)], *moe_w)
    return xs.reshape(b, t, d)
```

```python
import functools
import math

import jax
import jax.numpy as jnp
import numpy as np
from jax import lax
from jax.experimental import pallas as pl
from jax.experimental.pallas import tpu as pltpu

F32 = jnp.float32
BF16 = jnp.bfloat16
I32 = jnp.int32
U32 = jnp.uint32

EPS = 1e-6
GRID_W = 64
ROPE_BASE = 10000.0
DIFF_HEADS = 4
DIFF_DK = 128
DIFF_DV = 256
DN_HEADS = 8
DN_DK = 128
DN_DV = 128
DN_CONV = 5
DN_CHUNK = 128
SWA_HEADS = 32
SWA_KV_HEADS = 4
SWA_GROUP = SWA_HEADS // SWA_KV_HEADS
SWA_DH = 64
WINDOW = 128
Q_BLOCK = 128
N_EXPERTS = 64
TOP_K = 8
N_GROUPS = 8
GROUP_SIZE = N_EXPERTS // N_GROUPS
TOPK_GROUPS = 4
ROUTED_SCALE = 2.5
MOE_BLOCK = 512

LANES = 128
SUBLANES = 8
VMEM_LIMIT_BYTES = 56 * 1024 * 1024

ROWS_STREAMED = 1024
COLS_STREAMED = 1024
ROWS_RESIDENT = 256
ROWS_OUTPROJ = 1024
MOD_COLS = 768
DIFF_Q_ROWS = 1024
CHUNK_ROWS = 256
ROUTER_TOKENS = 512
DISPATCH_TOKENS = 512
COMBINE_TOKENS = 128

NT_DIMS = (((1,), (1,)), ((), ()))
LOG2E = math.log2(math.e)


def _params(*semantics):
    return pltpu.CompilerParams(dimension_semantics=semantics, vmem_limit_bytes=VMEM_LIMIT_BYTES)


def _tile(n, pref, mult):
    if n <= pref:
        return n
    t = pref - pref % mult
    while t > mult and n % t:
        t -= mult
    assert n % t == 0, (n, pref, mult)
    return t


def _mm(a, b):
    return jnp.dot(a.astype(BF16), b.astype(BF16), preferred_element_type=F32)


def _split2(x):
    hi = x.astype(BF16)
    lo = (x - hi.astype(F32)).astype(BF16)
    return hi, lo


def _dot3(a, b, dims=None):
    if dims is None:
        dims = (((a.ndim - 1,), (0,)), ((), ()))
    ah, al = _split2(a)
    bh, bl = _split2(b)
    d = functools.partial(lax.dot_general, dimension_numbers=dims, preferred_element_type=F32)
    return d(ah, bh) + d(ah, bl) + d(al, bh)


def _silu(x):
    return x * jax.nn.sigmoid(x)


def _softplus(x):
    return jnp.maximum(x, 0.0) + jnp.log(1.0 + jnp.exp(-jnp.abs(x)))


def _pack_pairs(lo, hi):
    ulo = lax.bitcast_convert_type(lo.astype(BF16).astype(F32), U32) >> 16
    uhi = lax.bitcast_convert_type(hi.astype(BF16).astype(F32), U32) & jnp.uint32(0xFFFF0000)
    return ulo | uhi


def _unpack_pairs(u):
    lo = lax.bitcast_convert_type(u << 16, F32)
    hi = lax.bitcast_convert_type(u & jnp.uint32(0xFFFF0000), F32)
    return lo, hi


def _store_token_tiles(ref, base, packed):
    n, width = packed.shape
    tr = width // LANES
    for j in range(tr):
        ref[pl.ds(base + j, n, stride=tr), :] = packed[:, j * LANES:(j + 1) * LANES]


def _load_token_tiles(ref, base, n, tr):
    return jnp.concatenate([ref[pl.ds(base + j, n, stride=tr), :] for j in range(tr)], axis=1)


def _tile_copy(src, src_row, dst, dst_row, tr, sem):
    return pltpu.make_async_copy(src.at[pl.ds(pl.multiple_of(src_row, tr), tr)],
                                 dst.at[pl.ds(pl.multiple_of(dst_row, tr), tr)], sem)


def _mod_kernel(a_ref, w_ref, b_ref, o_ref):
    o_ref[...] = _dot3(_silu(a_ref[...]), w_ref[0]) + b_ref[...]


def _modulation(a, w_all, layer, b):
    r, d = a.shape
    n = w_all.shape[2]
    tn = _tile(n, MOD_COLS, LANES)
    return pl.pallas_call(
        _mod_kernel,
        grid=(n // tn,),
        in_specs=[pl.BlockSpec((r, d), lambda j: (0, 0)),
                  pl.BlockSpec((1, d, tn), lambda j: (layer, 0, j)),
                  pl.BlockSpec((1, tn), lambda j: (0, j))],
        out_specs=pl.BlockSpec((r, tn), lambda j: (0, j)),
        out_shape=jax.ShapeDtypeStruct((r, n), F32),
        compiler_params=_params("parallel"),
        name="modulation",
    )(a, w_all, b.reshape(1, n))


def _norm_mod(x, nw, sh, sc):
    y = x * lax.rsqrt(jnp.mean(x * x, axis=-1, keepdims=True) + EPS) * nw
    return y * (1.0 + sc) + sh


def _store_head_groups(o_ref, ys, rows, hw_ref, seg_ref, swap_ref, cos, sin, hd):
    normed = [g for g in range(len(ys)) if rows[g] is not None]
    if normed:
        seg = seg_ref[...]
        sums = {g: jnp.dot((ys[g] * ys[g]).astype(BF16), seg, preferred_element_type=F32) for g in normed}
        for g in normed:
            ys[g] = ys[g] * lax.rsqrt(sums[g] * (1.0 / hd) + EPS) * hw_ref[rows[g]]
        if cos is not None:
            swap = swap_ref[...]
            partners = {g: jnp.dot(ys[g].astype(BF16), swap, preferred_element_type=F32) for g in normed}
            for g in normed:
                ys[g] = ys[g] * cos + partners[g] * sin
    for g, y in enumerate(ys):
        o_ref[:, g * LANES:(g + 1) * LANES] = y.astype(o_ref.dtype)


def _head_matrices(hd):
    lane = np.arange(LANES)
    seg = (lane[:, None] // hd) == (lane[None, :] // hd)
    q = hd // 4
    partner = np.where((lane % (2 * q)) < q, lane + q, lane - q)
    swap = lane[:, None] == partner[None, :]
    return jnp.asarray(seg, BF16), jnp.asarray(swap, BF16)


def _nmm_kernel(x_ref, nw_ref, sh_ref, sc_ref, w_ref, *rest, has_small, head_blocks, hd, rope, single):
    rest = list(rest)
    ws_ref = rest.pop(0) if has_small else None
    hw_ref, seg_ref, swap_ref = (rest.pop(0), rest.pop(0), rest.pop(0)) if head_blocks else (None, None, None)
    cos_ref, sin_ref = (rest.pop(0), rest.pop(0)) if rope else (None, None)
    o_ref = rest.pop(0)
    os_ref = rest.pop(0) if has_small else None
    (h_ref,) = rest
    if single:
        h = _norm_mod(x_ref[...], nw_ref[...], sh_ref[0], sc_ref[0])
        hb = h.astype(BF16)
        if has_small:
            os_ref[...] = _dot3(h, ws_ref[...])
        rows = head_blocks[0][2] if head_blocks else (None,) * (o_ref.shape[1] // LANES)
        cos = cos_ref[...] if rope else None
        sin = sin_ref[...] if rope else None
        cw = 2 * LANES
        n_out = o_ref.shape[1]
        parts = [jnp.dot(hb, w_ref[:, c0:c0 + cw], preferred_element_type=F32) for c0 in range(0, n_out, cw)]
        ys = [parts[g // 2][:, (g % 2) * LANES:(g % 2 + 1) * LANES] for g in range(n_out // LANES)]
        _store_head_groups(o_ref, ys, rows, hw_ref, seg_ref, swap_ref, cos, sin, hd)
        return
    j = pl.program_id(1)

    @pl.when(j == 0)
    def _():
        rows = x_ref.shape[0]
        step = _tile(rows, CHUNK_ROWS, SUBLANES)
        for r0 in range(0, rows, step):
            h = _norm_mod(x_ref[r0:r0 + step, :], nw_ref[...], sh_ref[0], sc_ref[0])
            h_ref[r0:r0 + step, :] = h.astype(BF16)
            if has_small:
                os_ref[r0:r0 + step, :] = _dot3(h, ws_ref[...])

    r = jnp.dot(h_ref[...], w_ref[...], preferred_element_type=F32)
    if not head_blocks:
        o_ref[...] = r.astype(o_ref.dtype)
        return
    tm, tn = r.shape
    plain = j >= 0
    for lo, hi, rows in head_blocks:
        inside = (j >= lo) & (j < hi)
        plain = plain & jnp.logical_not(inside)

        @pl.when(inside)
        def _(rows=rows):
            cos = cos_ref[...] if rope else None
            sin = sin_ref[...] if rope else None
            ys = [r[:, g * LANES:(g + 1) * LANES] for g in range(tn // LANES)]
            _store_head_groups(o_ref, ys, rows, hw_ref, seg_ref, swap_ref, cos, sin, hd)

    @pl.when(plain)
    def _():
        o_ref[...] = r.astype(o_ref.dtype)


def _norm_mod_matmul(x, nw, sh, sc, w, rows_per_group, w_small=None, tn=COLS_STREAMED, heads=None):
    m, d = x.shape
    n = w.shape[1]
    tn = _tile(n, tn, LANES)
    single = tn == n and n % (2 * LANES) == 0
    tm = math.gcd(_tile(m, ROWS_RESIDENT if single else ROWS_STREAMED, SUBLANES), rows_per_group)
    has_small = w_small is not None
    grp = lambda i, j: ((i * tm) // rows_per_group, 0, 0)
    in_specs = [pl.BlockSpec((tm, d), lambda i, j: (i, 0)),
                pl.BlockSpec((1, d), lambda i, j: (0, 0)),
                pl.BlockSpec((1, 1, d), grp),
                pl.BlockSpec((1, 1, d), grp),
                pl.BlockSpec((d, tn), lambda i, j: (0, j))]
    args = [x, nw.reshape(1, d), sh, sc, w]
    out_specs = [pl.BlockSpec((tm, tn), lambda i, j: (i, j))]
    out_shape = [jax.ShapeDtypeStruct((m, n), BF16)]
    if has_small:
        in_specs.append(pl.BlockSpec((d, LANES), lambda i, j: (0, 0)))
        args.append(w_small)
        out_specs.append(pl.BlockSpec((tm, LANES), lambda i, j: (i, 0)))
        out_shape.append(jax.ShapeDtypeStruct((m, LANES), F32))
    head_blocks, hd, rope = (), LANES, False
    if heads is not None:
        head_blocks, hd = tuple(heads["blocks"]), heads["hd"]
        hw = heads["weights"]
        in_specs += [pl.BlockSpec((hw.shape[0], 1, LANES), lambda i, j: (0, 0, 0)),
                     pl.BlockSpec((LANES, LANES), lambda i, j: (0, 0)),
                     pl.BlockSpec((LANES, LANES), lambda i, j: (0, 0))]
        args += [hw.reshape(hw.shape[0], 1, LANES), *_head_matrices(hd)]
        if heads["tables"] is not None:
            rope = True
            nt = heads["t"] // tm
            assert heads["t"] % tm == 0
            in_specs += [pl.BlockSpec((tm, LANES), lambda i, j: (i % nt, 0))] * 2
            args += list(heads["tables"])
    outs = pl.pallas_call(
        functools.partial(_nmm_kernel, has_small=has_small, head_blocks=head_blocks, hd=hd, rope=rope,
                          single=single),
        grid=(m // tm, n // tn),
        in_specs=in_specs,
        out_specs=out_specs,
        out_shape=out_shape,
        scratch_shapes=[pltpu.VMEM((tm, d), BF16)],
        compiler_params=_params("parallel", "arbitrary"),
        name="norm_mod_matmul",
    )(*args)
    return outs if has_small else outs[0]


def _rope_tables(t, head_dim):
    q = head_dim // 4
    pos = jnp.arange(t, dtype=I32)
    row = (pos // GRID_W).astype(F32)
    col = (pos % GRID_W).astype(F32)
    axis_dim = head_dim // 2
    inv_freq = ROPE_BASE ** (-jnp.arange(0, axis_dim, 2, dtype=F32) / axis_dim)
    lane = jnp.arange(LANES) % head_dim
    freq = inv_freq[lane % q]
    p = jnp.where((lane < head_dim // 2)[None, :], row[:, None], col[:, None])
    ang = p * freq[None, :]
    sign = jnp.where((lane % (2 * q)) < q, -1.0, 1.0)
    return jnp.cos(ang), jnp.sin(ang) * sign[None, :]


def _diff_attn_kernel(lv_ref, q_ref, *rest, lam_init, has_lat):
    if has_lat:
        kx_ref, vx_ref, kc_ref, vc_ref, w_ref, o_ref = rest
    else:
        kc_ref, vc_ref, w_ref, o_ref = rest
    lv = lv_ref[...]
    lam = (jnp.exp(jnp.sum(lv[0:1] * lv[1:2], keepdims=True))
           - jnp.exp(jnp.sum(lv[2:3] * lv[3:4], keepdims=True)) + lam_init)
    q = q_ref[...]

    def probs(c):
        sl = slice(c * DIFF_DK, (c + 1) * DIFF_DK)
        qc = q[:, sl]
        s_c = lax.dot_general(qc, kc_ref[:, sl], NT_DIMS, preferred_element_type=F32)
        m = jnp.max(s_c, axis=-1, keepdims=True)
        p_x = None
        if has_lat:
            s_x = lax.dot_general(qc, kx_ref[:, sl], NT_DIMS, preferred_element_type=F32)
            m = jnp.maximum(m, jnp.max(s_x, axis=-1, keepdims=True))
            p_x = jnp.exp2(s_x - m)
        p_c = jnp.exp2(s_c - m)
        l = jnp.sum(p_c, axis=-1, keepdims=True)
        if has_lat:
            l = l + jnp.sum(p_x, axis=-1, keepdims=True)
        return p_x, p_c, l

    p1x, p1c, l1 = probs(0)
    p2x, p2c, l2 = probs(1)
    ratio = lam * l1 * (1.0 / l2)
    o = jnp.dot((p1c - p2c * ratio).astype(BF16), vc_ref[...], preferred_element_type=F32)
    if has_lat:
        o = o + jnp.dot((p1x - p2x * ratio).astype(BF16), vx_ref[...], preferred_element_type=F32)
    o = o * (1.0 / l1)
    o = o * lax.rsqrt(jnp.mean(o * o, axis=-1, keepdims=True) + EPS)
    o_ref[...] = (o * w_ref[...] * (1.0 - lam_init)).astype(o_ref.dtype)


def _diff_attention(lam_vec, q, kc, zc, subln, lam_init, b, n_c, kx=None, zx=None, t=None):
    has_lat = kx is not None
    hw = 2 * DIFF_DK
    kblk0 = DIFF_HEADS
    vblk0 = (2 * DIFF_HEADS * hw) // DIFF_DV
    tq_all = t if has_lat else n_c
    tq = _tile(tq_all, DIFF_Q_ROWS, SUBLANES)
    nq = tq_all // tq
    in_specs = [pl.BlockSpec((4, DIFF_DK), lambda bi, h, qi: (0, 0)),
                pl.BlockSpec((tq, hw), lambda bi, h, qi: (bi * nq + qi, h))]
    args = [lam_vec, q]
    if has_lat:
        in_specs += [pl.BlockSpec((t, hw), lambda bi, h, qi: (bi, kblk0 + h)),
                     pl.BlockSpec((t, DIFF_DV), lambda bi, h, qi: (bi, vblk0 + h))]
        args += [kx, zx]
    in_specs += [pl.BlockSpec((n_c, hw), lambda bi, h, qi: (bi, kblk0 + h)),
                 pl.BlockSpec((n_c, DIFF_DV), lambda bi, h, qi: (bi, vblk0 + h)),
                 pl.BlockSpec((1, DIFF_DV), lambda bi, h, qi: (0, 0))]
    args += [kc, zc, subln.reshape(1, DIFF_DV)]
    return pl.pallas_call(
        functools.partial(_diff_attn_kernel, lam_init=lam_init, has_lat=has_lat),
        grid=(b, DIFF_HEADS, nq),
        in_specs=in_specs,
        out_specs=pl.BlockSpec((tq, DIFF_DV), lambda bi, h, qi: (bi * nq + qi, h)),
        out_shape=jax.ShapeDtypeStruct((b * tq_all, DIFF_HEADS * DIFF_DV), BF16),
        compiler_params=_params("parallel", "parallel", "arbitrary"),
        name="diff_attention",
    )(*args)


def _dn_prep_kernel(z_ref, cw_ref, o_ref, pad_ref, *, seg):
    halo = SUBLANES
    pad_ref[0:halo, :] = jnp.zeros((halo, LANES), F32)
    pad_ref[halo + seg:2 * halo + seg, :] = jnp.zeros((halo, LANES), F32)
    pad_ref[halo:halo + seg, :] = z_ref[...].astype(F32)
    kind = pl.program_id(1) // DN_HEADS
    qk_scale = jnp.where(kind == 0, DN_DK ** -0.5, 1.0).astype(F32)
    rows = _tile(seg, CHUNK_ROWS, SUBLANES)
    for r0 in range(0, seg, rows):
        acc = jnp.zeros((rows, LANES), F32)
        for j in range(DN_CONV):
            acc = acc + cw_ref[j:j + 1, :] * pad_ref[pl.ds(halo + r0 + j - DN_CONV // 2, rows), :]
        y = _silu(acc)
        nrm = y * lax.rsqrt(jnp.sum(y * y, axis=-1, keepdims=True) + EPS) * qk_scale
        o_ref[r0:r0 + rows, :] = jnp.where(kind < 2, nrm, y).astype(o_ref.dtype)


def _dn_prep(z, col0, conv_w, seg):
    m = z.shape[0]
    ncols = conv_w.shape[1]
    cblk0 = col0 // LANES
    cw = jnp.zeros((SUBLANES, ncols), F32).at[:DN_CONV].set(conv_w)
    return pl.pallas_call(
        functools.partial(_dn_prep_kernel, seg=seg),
        grid=(m // seg, ncols // LANES),
        in_specs=[pl.BlockSpec((seg, LANES), lambda s, g: (s, cblk0 + g)),
                  pl.BlockSpec((SUBLANES, LANES), lambda s, g: (0, g))],
        out_specs=pl.BlockSpec((seg, LANES), lambda s, g: (s, g)),
        out_shape=jax.ShapeDtypeStruct((m, ncols), BF16),
        scratch_shapes=[pltpu.VMEM((seg + 2 * SUBLANES, LANES), F32)],
        compiler_params=_params("parallel", "parallel"),
        name="dn_prep",
    )(z, cw)


def _dn_heads(reverse, q_ref, k_ref, v_ref, zs_ref, zst_ref, pr_ref, pca_ref, pcd_ref, o_ref, s_ref, ri, ci):
    c = DN_CHUNK
    if reverse:
        later, strict, later_t = ri <= ci, ri < ci, ri >= ci
    else:
        later, strict, later_t = ri >= ci, ri > ci, ri <= ci
    eye = (ri == ci).astype(F32)
    tri = later.astype(BF16)
    tri_t = later_t.astype(BF16)

    zs = zs_ref[...]
    beta_cols = jax.nn.sigmoid(zs)
    g_cols = -jnp.exp(pr_ref[0:1, :]) * _softplus(zs + pr_ref[1:2, :])
    g_hi = g_cols.astype(BF16)
    g_r1 = g_cols - g_hi.astype(F32)
    g_mid = g_r1.astype(BF16)
    g_lo = (g_r1 - g_mid.astype(F32)).astype(BF16)
    d = functools.partial(jnp.dot, preferred_element_type=F32)
    gc_cols = d(tri, g_hi) + d(tri, g_mid) + d(tri, g_lo)
    g_rows = -jnp.exp(pca_ref[...]) * _softplus(zst_ref[...] + pcd_ref[...])
    h_hi = g_rows.astype(BF16)
    h_r1 = g_rows - h_hi.astype(F32)
    h_mid = h_r1.astype(BF16)
    h_lo = (h_r1 - h_mid.astype(F32)).astype(BF16)
    gc_rows = d(h_hi, tri_t) + d(h_mid, tri_t) + d(h_lo, tri_t)

    dir_off = DN_HEADS if reverse else 0
    last = 0 if reverse else c - 1
    neg_inf = jnp.float32(-jnp.inf)
    heads = []
    for h in range(DN_HEADS):
        sl = slice(h * DN_DK, (h + 1) * DN_DK)
        cb = dir_off + h
        cg = 2 * DN_HEADS + dir_off + h
        beta = beta_cols[:, cb:cb + 1]
        gcol = gc_cols[:, cg:cg + 1]
        grow = gc_rows[cg:cg + 1, :]
        glast = grow[:, last:last + 1]
        q = q_ref[:, sl]
        k = k_ref[:, sl]
        kf = k.astype(F32)
        decay = jnp.exp(jnp.where(later, gcol - grow, neg_inf))
        kb = kf * beta
        both = lax.dot_general(jnp.concatenate([kb.astype(BF16), q], axis=0), k, NT_DIMS,
                               preferred_element_type=F32)
        lmat = jnp.where(strict, both[:c] * decay, 0.0)
        eg = jnp.exp(gcol)
        heads.append(dict(
            sl=sl, h=h, o_ref=o_ref, s_ref=s_ref, glast=glast, lmat=lmat, amat=both[c:] * decay,
            inv=eye - jnp.where((ri >> 1) == (ci >> 1), lmat, 0.0),
            rhs=jnp.concatenate([v_ref[:, sl].astype(F32) * beta, kb * eg], axis=1).astype(BF16),
            qe=(q.astype(F32) * eg).astype(BF16),
            kdec_t=(kf * jnp.exp(glast - gcol)).T.astype(BF16)))
    return heads


def _deltanet_kernel(qf_ref, kf_ref, vf_ref, zsf_ref, zstf_ref, qb_ref, kb_ref, vb_ref, zsb_ref, zstb_ref,
                     pr_ref, pca_ref, pcd_ref, s0f_ref, s0b_ref,
                     of_ref, ob_ref, soutf_ref, soutb_ref, sf_ref, sb_ref, *, n_chunks):
    step = pl.program_id(1)
    c = DN_CHUNK

    @pl.when(step == 0)
    def _():
        sf_ref[...] = s0f_ref[0]
        sb_ref[...] = s0b_ref[0]

    ri = lax.broadcasted_iota(I32, (c, c), 0)
    ci = lax.broadcasted_iota(I32, (c, c), 1)
    prm = (pr_ref, pca_ref, pcd_ref)
    heads = (_dn_heads(False, qf_ref, kf_ref, vf_ref, zsf_ref, zstf_ref, *prm, of_ref, sf_ref, ri, ci)
             + _dn_heads(True, qb_ref, kb_ref, vb_ref, zsb_ref, zstb_ref, *prm, ob_ref, sb_ref, ri, ci))
    lev = 1
    while (1 << lev) < c:
        blk = ((ri >> (lev + 1)) == (ci >> (lev + 1))) & ((ri >> lev) != (ci >> lev))
        half = [_mm(hd["inv"], jnp.where(blk, hd["lmat"], 0.0)) for hd in heads]
        for hd, t in zip(heads, half):
            hd["inv"] = hd["inv"] - _mm(t, hd["inv"])
        lev += 1
    uws = [_mm(hd["inv"], hd["rhs"]) for hd in heads]
    states = [hd["s_ref"][hd["h"]] for hd in heads]
    new_states = []
    for hd, uw, s in zip(heads, uws, states):
        ws_qs = _mm(jnp.concatenate([uw[:, DN_DV:].astype(BF16), hd["qe"]], axis=0), s)
        v_new = uw[:, :DN_DV] - ws_qs[:c]
        hd["o_ref"][:, hd["sl"]] = (ws_qs[c:] + _mm(hd["amat"], v_new)).astype(hd["o_ref"].dtype)
        new_states.append(s * jnp.exp(hd["glast"]) + _mm(hd["kdec_t"], v_new))
    for hd, s in zip(heads, new_states):
        hd["s_ref"][hd["h"]] = s

    @pl.when(step == n_chunks - 1)
    def _():
        soutf_ref[0] = sf_ref[...]
        soutb_ref[0] = sb_ref[...]


def _deltanet(dn, zs, zst, prm, s0f, s0b, b, seg):
    pr, pca, pcd = prm
    n = seg // DN_CHUNK
    hw = DN_HEADS * DN_DK
    fw = lambda bi, s: bi * n + s
    bw = lambda bi, s: bi * n + (n - 1 - s)
    state = pl.BlockSpec((1, DN_HEADS, DN_DK, DN_DV), lambda bi, s: (bi, 0, 0, 0))

    def chunk_specs(rb):
        return [pl.BlockSpec((DN_CHUNK, hw), lambda bi, s: (rb(bi, s), 0)),
                pl.BlockSpec((DN_CHUNK, hw), lambda bi, s: (rb(bi, s), 1)),
                pl.BlockSpec((DN_CHUNK, hw), lambda bi, s: (rb(bi, s), 2)),
                pl.BlockSpec((DN_CHUNK, LANES), lambda bi, s: (rb(bi, s), 0)),
                pl.BlockSpec((4 * DN_HEADS, DN_CHUNK), lambda bi, s: (0, rb(bi, s)))]

    out = lambda rb: pl.BlockSpec((DN_CHUNK, hw), lambda bi, s: (rb(bi, s), 0))
    o_shape = jax.ShapeDtypeStruct((b * seg, hw), BF16)
    s_shape = jax.ShapeDtypeStruct((b, DN_HEADS, DN_DK, DN_DV), F32)
    s_scratch = pltpu.VMEM((DN_HEADS, DN_DK, DN_DV), F32)
    return pl.pallas_call(
        functools.partial(_deltanet_kernel, n_chunks=n),
        grid=(b, n),
        in_specs=chunk_specs(fw) + chunk_specs(bw) + [
            pl.BlockSpec((SUBLANES, LANES), lambda bi, s: (0, 0)),
            pl.BlockSpec((4 * DN_HEADS, LANES), lambda bi, s: (0, 0)),
            pl.BlockSpec((4 * DN_HEADS, LANES), lambda bi, s: (0, 0)),
            state, state],
        out_specs=[out(fw), out(bw), state, state],
        out_shape=[o_shape, o_shape, s_shape, s_shape],
        scratch_shapes=[s_scratch, s_scratch],
        compiler_params=_params("parallel", "arbitrary"),
        name="deltanet",
    )(dn, dn, dn, zs, zst, dn, dn, dn, zs, zst, pr, pca, pcd, s0f, s0b)


def _deltanet_params(a_log, dt_bias):
    nh = 2 * DN_HEADS
    a = a_log.reshape(nh).astype(F32)
    dtb = dt_bias.reshape(nh).astype(F32)
    pr = jnp.zeros((SUBLANES, LANES), F32).at[0, nh:2 * nh].set(a).at[1, nh:2 * nh].set(dtb)
    pca = jnp.zeros((2 * nh, LANES), F32).at[nh:].set(jnp.broadcast_to(a[:, None], (nh, LANES)))
    pcd = jnp.zeros((2 * nh, LANES), F32).at[nh:].set(jnp.broadcast_to(dtb[:, None], (nh, LANES)))
    return pr, pca, pcd


def _outproj_ab_kernel(od_ref, of_ref, ob_ref, gate_ref, nw_ref, w_ref, x_ref, g_ref, o_ref):
    nd = od_ref.shape[1]
    y = jnp.dot(od_ref[...], w_ref[:nd, :], preferred_element_type=F32)
    gated = []
    for h in range(DN_HEADS):
        sl = slice(h * DN_DV, (h + 1) * DN_DV)
        o = of_ref[:, sl].astype(F32) + ob_ref[:, sl].astype(F32)
        o = o * lax.rsqrt(jnp.mean(o * o, axis=-1, keepdims=True) + EPS) * nw_ref[...]
        gated.append((o * _silu(gate_ref[:, sl].astype(F32))).astype(BF16))
    y = y + jnp.dot(jnp.concatenate(gated, axis=1), w_ref[nd:, :], preferred_element_type=F32)
    o_ref[...] = x_ref[...] + g_ref[0] * y


def _outproj_ab(od, o_f, o_b, z, gate_col0, out_norm, w, x, gate, rows_per_group):
    m, d = x.shape
    nd, nn = od.shape[1], o_f.shape[1]
    tm = math.gcd(_tile(m, 2 * ROWS_RESIDENT, SUBLANES), rows_per_group)
    grp = lambda i: ((i * tm) // rows_per_group, 0, 0)
    gblk = gate_col0 // nn
    return pl.pallas_call(
        _outproj_ab_kernel,
        grid=(m // tm,),
        in_specs=[pl.BlockSpec((tm, nd), lambda i: (i, 0)),
                  pl.BlockSpec((tm, nn), lambda i: (i, 0)),
                  pl.BlockSpec((tm, nn), lambda i: (i, 0)),
                  pl.BlockSpec((tm, nn), lambda i: (i, gblk)),
                  pl.BlockSpec((1, DN_DV), lambda i: (0, 0)),
                  pl.BlockSpec((nd + nn, d), lambda i: (0, 0)),
                  pl.BlockSpec((tm, d), lambda i: (i, 0)),
                  pl.BlockSpec((1, 1, d), grp)],
        out_specs=pl.BlockSpec((tm, d), lambda i: (i, 0)),
        out_shape=jax.ShapeDtypeStruct((m, d), F32),
        compiler_params=_params("parallel"),
        name="outproj_ab",
    )(od, o_f, o_b, z, out_norm.reshape(1, DN_DV), w, x, gate)


def _outproj_kernel(a_ref, w_ref, x_ref, g_ref, o_ref):
    o_ref[...] = x_ref[...] + g_ref[0] * jnp.dot(a_ref[...], w_ref[...], preferred_element_type=F32)


def _outproj(a, w, x, gate, rows_per_group):
    m, d = x.shape
    kdim = a.shape[1]
    tm = math.gcd(_tile(m, ROWS_OUTPROJ, SUBLANES), rows_per_group)
    tn = d
    grp = lambda i, j: ((i * tm) // rows_per_group, 0, j)
    return pl.pallas_call(
        _outproj_kernel,
        grid=(m // tm, d // tn),
        in_specs=[pl.BlockSpec((tm, kdim), lambda i, j: (i, 0)),
                  pl.BlockSpec((kdim, tn), lambda i, j: (0, j)),
                  pl.BlockSpec((tm, tn), lambda i, j: (i, j)),
                  pl.BlockSpec((1, 1, tn), grp)],
        out_specs=pl.BlockSpec((tm, tn), lambda i, j: (i, j)),
        out_shape=jax.ShapeDtypeStruct((m, d), F32),
        compiler_params=_params("parallel", "arbitrary"),
        name="outproj",
    )(a, w, x, gate)


def _swa_kernel(sink_ref, bias_ref, q_ref, k0_ref, k1_ref, k2_ref, v0_ref, v1_ref, v2_ref, kc_ref, vc_ref,
                o_ref):
    kvh = pl.program_id(1)
    qb = Q_BLOCK
    npair = SWA_GROUP // 2
    lane = lax.broadcasted_iota(I32, (qb, LANES), 1)
    q = q_ref[...]
    parts = []
    for p in range(npair):
        qp = q[:, p * LANES:(p + 1) * LANES]
        parts.append(jnp.where(lane < SWA_DH, qp, jnp.zeros_like(qp)))
        parts.append(jnp.where(lane >= SWA_DH, qp, jnp.zeros_like(qp)))
    qq = jnp.concatenate(parts, axis=0)
    k_lat = jnp.concatenate([k0_ref[...], k1_ref[...], k2_ref[...]], axis=0)
    v_lat = jnp.concatenate([v0_ref[...], v1_ref[...], v2_ref[...]], axis=0)
    s_lat = lax.dot_general(qq, k_lat, NT_DIMS, preferred_element_type=F32)
    s_ctx = lax.dot_general(qq, kc_ref[...], NT_DIMS, preferred_element_type=F32)
    bias = bias_ref[0]
    vc = vc_ref[...]
    outs = []
    for g in range(SWA_GROUP):
        rs = slice(g * qb, (g + 1) * qb)
        sl = s_lat[rs] + bias
        sc = s_ctx[rs]
        sink = sink_ref[kvh, g] * LOG2E
        m = jnp.maximum(jnp.maximum(jnp.max(sl, axis=-1, keepdims=True),
                                    jnp.max(sc, axis=-1, keepdims=True)), sink)
        el = jnp.exp2(sl - m)
        ec = jnp.exp2(sc - m)
        l = jnp.sum(el, axis=-1, keepdims=True) + jnp.sum(ec, axis=-1, keepdims=True) + jnp.exp2(sink - m)
        o = (jnp.dot(el.astype(BF16), v_lat, preferred_element_type=F32)
             + jnp.dot(ec.astype(BF16), vc, preferred_element_type=F32))
        outs.append(o * (1.0 / l))
    for p in range(npair):
        o_ref[:, p * LANES:(p + 1) * LANES] = jnp.where(lane < SWA_DH, outs[2 * p], outs[2 * p + 1]).astype(o_ref.dtype)


def _swa_attention(sink, q, kx, vx, kc, vc, b, t, n_c):
    qb = Q_BLOCK
    nb = t // qb
    gw = SWA_GROUP * SWA_DH
    lat = lambda off: pl.BlockSpec(
        (qb, LANES), lambda bi, h, i: (bi * nb + jnp.clip(i + off, 0, nb - 1), h))
    ctx = pl.BlockSpec((n_c, LANES), lambda bi, h, i: (bi, h))
    r_io = np.arange(qb)[:, None]
    c_io = np.arange(3 * qb)[None, :]
    inside = np.abs(r_io + qb - c_io) <= WINDOW
    variants = [inside & ((c_io >= qb) | (v & 1 == 0)) & ((c_io < 2 * qb) | (v & 2 == 0)) for v in range(4)]
    bias = jnp.asarray(np.where(np.stack(variants), 0.0, -np.inf), F32)
    return pl.pallas_call(
        _swa_kernel,
        grid=(b, SWA_KV_HEADS, nb),
        in_specs=[pl.BlockSpec(memory_space=pltpu.SMEM),
                  pl.BlockSpec((1, qb, 3 * qb),
                               lambda bi, h, i: ((i == 0).astype(I32) + 2 * (i == nb - 1).astype(I32), 0, 0)),
                  pl.BlockSpec((qb, gw), lambda bi, h, i: (bi * nb + i, h)),
                  lat(-1), lat(0), lat(1), lat(-1), lat(0), lat(1), ctx, ctx],
        out_specs=pl.BlockSpec((qb, gw), lambda bi, h, i: (bi * nb + i, h)),
        out_shape=jax.ShapeDtypeStruct((b * t, SWA_HEADS * SWA_DH), BF16),
        compiler_params=_params("parallel", "parallel", "arbitrary"),
        name="swa_attention",
    )(sink, bias, q, kx, kx, kx, vx, vx, vx, kc, vc)


def _dup_heads(a, col0):
    m = a.shape[0]
    h = a[:, col0:col0 + SWA_KV_HEADS * SWA_DH].reshape(m, SWA_KV_HEADS, 1, SWA_DH)
    return jnp.broadcast_to(h, (m, SWA_KV_HEADS, LANES // SWA_DH, SWA_DH)).reshape(m, SWA_KV_HEADS * LANES)


def _first_max(vals, iota, size, axis):
    m = jnp.max(vals, axis=axis, keepdims=True)
    first = jnp.min(jnp.where(vals == m, iota, size), axis=axis, keepdims=True)
    return m, first


def _router_kernel(x_ref, nw_ref, sh_ref, sc_ref, rwt_ref, rb_ref, c0_ref,
                   hp_ref, idx_ref, wt_ref, rank_ref, cnt_ref, carry_ref):
    @pl.when(pl.program_id(0) == 0)
    def _():
        carry_ref[...] = c0_ref[...]

    h = _norm_mod(x_ref[...], nw_ref[...], sh_ref[0], sc_ref[0])
    tm, d = h.shape
    _store_token_tiles(hp_ref, 0, _pack_pairs(h[:, :d // 2], h[:, d // 2:]))
    scores = jax.nn.sigmoid(_dot3(rwt_ref[...], h, NT_DIMS))
    sel = scores + rb_ref[...]
    neg = jnp.float32(-jnp.inf)

    g_io = lax.broadcasted_iota(I32, (GROUP_SIZE, tm), 0)
    gs_rows = []
    for g in range(N_GROUPS):
        sg = sel[g * GROUP_SIZE:(g + 1) * GROUP_SIZE]
        m1, f1 = _first_max(sg, g_io, GROUP_SIZE, 0)
        m2 = jnp.max(jnp.where(g_io == f1, neg, sg), axis=0, keepdims=True)
        gs_rows.append(m1 + m2)
    cur = jnp.concatenate(gs_rows, axis=0)
    n_io = lax.broadcasted_iota(I32, (N_GROUPS, tm), 0)
    gmask = jnp.zeros((N_GROUPS, tm), I32)
    for _ in range(TOPK_GROUPS):
        _, f = _first_max(cur, n_io, N_GROUPS, 0)
        hit = n_io == f
        gmask = jnp.where(hit, 1, gmask)
        cur = jnp.where(hit, neg, cur)
    cur = jnp.concatenate(
        [jnp.where(gmask[g:g + 1] > 0, sel[g * GROUP_SIZE:(g + 1) * GROUP_SIZE], neg) for g in range(N_GROUPS)],
        axis=0)

    e_io = lax.broadcasted_iota(I32, (N_EXPERTS, tm), 0)
    chosen = jnp.zeros((N_EXPERTS, tm), F32)
    idx_rows, w_rows = [], []
    for _ in range(TOP_K):
        _, f = _first_max(cur, e_io, N_EXPERTS, 0)
        hit = e_io == f
        idx_rows.append(f)
        w_rows.append(jnp.sum(jnp.where(hit, scores, 0.0), axis=0, keepdims=True))
        chosen = jnp.where(hit, 1.0, chosen)
        cur = jnp.where(hit, neg, cur)
    idx = jnp.concatenate(idx_rows, axis=0)
    w = jnp.concatenate(w_rows, axis=0)
    idx_ref[...] = idx
    wt_ref[...] = w * (1.0 / jnp.sum(w, axis=0, keepdims=True)) * ROUTED_SCALE

    onehot = chosen.astype(BF16)
    before = (lax.broadcasted_iota(I32, (tm, tm), 0) < lax.broadcasted_iota(I32, (tm, tm), 1)).astype(BF16)
    base = carry_ref[:, 0:1] + jnp.dot(onehot, before, preferred_element_type=F32)
    rank_ref[...] = jnp.concatenate(
        [jnp.sum(jnp.where(e_io == idx_rows[k], base, 0.0), axis=0, keepdims=True) for k in range(TOP_K)],
        axis=0).astype(I32)
    carry_ref[...] = carry_ref[...] + jnp.sum(chosen, axis=1, keepdims=True)
    cnt_ref[...] = carry_ref[...]


def _router(x, nw, sh, sc, rwt, rb, counts0, rows_per_group):
    m, d = x.shape
    tm = math.gcd(_tile(m, ROUTER_TOKENS, LANES), rows_per_group)
    assert (d // 2) % LANES == 0
    tr = d // 2 // LANES
    grp = lambda i: ((i * tm) // rows_per_group, 0, 0)
    tok = lambda rows: pl.BlockSpec((rows, tm), lambda i: (0, i))
    return pl.pallas_call(
        _router_kernel,
        grid=(m // tm,),
        in_specs=[pl.BlockSpec((tm, d), lambda i: (i, 0)),
                  pl.BlockSpec((1, d), lambda i: (0, 0)),
                  pl.BlockSpec((1, 1, d), grp),
                  pl.BlockSpec((1, 1, d), grp),
                  pl.BlockSpec((N_EXPERTS, d), lambda i: (0, 0)),
                  pl.BlockSpec((N_EXPERTS, 1), lambda i: (0, 0)),
                  pl.BlockSpec((N_EXPERTS, LANES), lambda i: (0, 0))],
        out_specs=[pl.BlockSpec((tm * tr, LANES), lambda i: (i, 0)),
                   tok(TOP_K), tok(TOP_K), tok(TOP_K),
                   pl.BlockSpec((N_EXPERTS, LANES), lambda i: (0, 0))],
        out_shape=[jax.ShapeDtypeStruct((m * tr, LANES), U32),
                   jax.ShapeDtypeStruct((TOP_K, m), I32),
                   jax.ShapeDtypeStruct((TOP_K, m), F32),
                   jax.ShapeDtypeStruct((TOP_K, m), I32),
                   jax.ShapeDtypeStruct((N_EXPERTS, LANES), F32)],
        scratch_shapes=[pltpu.VMEM((N_EXPERTS, LANES), F32)],
        compiler_params=_params("arbitrary"),
        name="moe_router",
    )(x, nw.reshape(1, d), sh, sc, rwt, rb.reshape(N_EXPERTS, 1), counts0)


def _dispatch_kernel(nv_ref, dest_ref, *rest, tr, tiles):
    hp_refs = rest[:len(tiles)]
    xs_ref, zero_ref, sem = rest[len(tiles):]
    i = pl.program_id(0)
    tm = hp_refs[0].shape[0] // tr
    blk_rows = MOE_BLOCK * tr

    @pl.when(i == 0)
    def _():
        zero_ref[...] = jnp.zeros(zero_ref.shape, U32)

        def fill(blk, carry):
            @pl.when(nv_ref[blk] < MOE_BLOCK)
            def _():
                cp = pltpu.make_async_copy(
                    zero_ref, xs_ref.at[pl.ds(pl.multiple_of(blk * blk_rows, blk_rows), blk_rows)], sem)
                cp.start()
                cp.wait()
            return carry

        lax.fori_loop(0, nv_ref.shape[0], fill, 0)

    def scatter(hp_ref):
        def copy(t, k):
            return _tile_copy(hp_ref, t * tr, xs_ref, dest_ref[t * TOP_K + k], tr, sem)

        def start(t, carry):
            for k in range(TOP_K):
                copy(t, k).start(priority=k % 2)
            return carry

        def wait(t, carry):
            for k in range(TOP_K):
                copy(t, k).wait()
            return carry

        lax.fori_loop(0, tm, start, 0)
        lax.fori_loop(0, tm, wait, 0)

    first = 0
    for hp_ref, n in zip(hp_refs, tiles):
        if len(tiles) == 1:
            scatter(hp_ref)
        else:
            pl.when((i >= first) & (i < first + n))(functools.partial(scatter, hp_ref))
        first += n


def _dispatch(block_nv, dest, hps, n_rows, tr):
    tm = functools.reduce(math.gcd, [hp.shape[0] // tr for hp in hps] + [DISPATCH_TOKENS])
    tiles = tuple(hp.shape[0] // tr // tm for hp in hps)
    firsts = [sum(tiles[:s]) for s in range(len(tiles))]
    hp_specs = [pl.BlockSpec((tm * tr, LANES), lambda i, nv, f=f, n=n: (jnp.clip(i - f, 0, n - 1), 0))
                for f, n in zip(firsts, tiles)]
    return pl.pallas_call(
        functools.partial(_dispatch_kernel, tr=tr, tiles=tiles),
        grid_spec=pltpu.PrefetchScalarGridSpec(
            num_scalar_prefetch=1,
            grid=(sum(tiles),),
            in_specs=[pl.BlockSpec((tm * TOP_K,), lambda i, nv: (i,), memory_space=pltpu.SMEM)] + hp_specs,
            out_specs=pl.BlockSpec(memory_space=pl.ANY),
            scratch_shapes=[pltpu.VMEM((MOE_BLOCK * tr, LANES), U32), pltpu.SemaphoreType.DMA(())]),
        out_shape=jax.ShapeDtypeStruct((n_rows * tr, LANES), U32),
        compiler_params=_params("arbitrary"),
        name="moe_dispatch",
    )(block_nv, dest, *hps)


def _gffn_kernel(be_ref, nv_ref, xs_ref, w1_ref, w3_ref, w2_ref, ys_ref, w1b_ref, w3b_ref, w2b_ref, *, tr):
    i = pl.program_id(0)
    nv = nv_ref[i]
    prev = be_ref[jnp.maximum(i - 1, 0)]

    @pl.when((i == 0) | (be_ref[i] != prev))
    def _():
        w1b_ref[...] = w1_ref[0, 0].astype(BF16)
        w3b_ref[...] = w3_ref[0, 0].astype(BF16)
        w2b_ref[...] = w2_ref[0, 0].astype(BF16)

    @pl.when(nv > 0)
    def _():
        lo, hi = _unpack_pairs(_load_token_tiles(xs_ref, 0, MOE_BLOCK, tr))
        dh = lo.shape[1]
        lo = lo.astype(BF16)
        hi = hi.astype(BF16)
        d = functools.partial(jnp.dot, preferred_element_type=F32)
        a = d(lo, w1b_ref[:dh, :]) + d(hi, w1b_ref[dh:, :])
        g = d(lo, w3b_ref[:dh, :]) + d(hi, w3b_ref[dh:, :])
        y = d((_silu(a) * g).astype(BF16), w2b_ref[...])
        _store_token_tiles(ys_ref, 0, _pack_pairs(y[:, :dh], y[:, dh:]))

    @pl.when(nv == 0)
    def _():
        ys_ref[...] = jnp.zeros(ys_ref.shape, U32)


def _grouped_ffn(block_e, block_nv, xs, w1, w3, w2, layer, tr):
    _, _, d, f = w1.shape
    blk_rows = MOE_BLOCK * tr
    nb = xs.shape[0] // blk_rows
    return pl.pallas_call(
        functools.partial(_gffn_kernel, tr=tr),
        grid_spec=pltpu.PrefetchScalarGridSpec(
            num_scalar_prefetch=2,
            grid=(nb,),
            in_specs=[pl.BlockSpec((blk_rows, LANES), lambda i, be, nv: (i, 0)),
                      pl.BlockSpec((1, 1, d, f), lambda i, be, nv: (layer, be[i], 0, 0)),
                      pl.BlockSpec((1, 1, d, f), lambda i, be, nv: (layer, be[i], 0, 0)),
                      pl.BlockSpec((1, 1, f, d), lambda i, be, nv: (layer, be[i], 0, 0))],
            out_specs=pl.BlockSpec((blk_rows, LANES), lambda i, be, nv: (i, 0)),
            scratch_shapes=[pltpu.VMEM((d, f), BF16), pltpu.VMEM((d, f), BF16), pltpu.VMEM((f, d), BF16)]),
        out_shape=jax.ShapeDtypeStruct(xs.shape, U32),
        compiler_params=_params("arbitrary"),
        name="moe_grouped_ffn",
    )(block_e, block_nv, xs, w1, w3, w2)


def _combine_kernel(dest_ref, dest_next_ref, x_ref, hp_ref, wt_ref, g_ref, ws1_ref, ws3_ref, ws2_ref, ys_ref,
                    o_ref, buf_a, buf_b, sem_a, sem_b, *, tr):
    i = pl.program_id(0)
    tm = x_ref.shape[0]
    dh = tr * LANES

    def copy(dref, t, k, buf, sem):
        return _tile_copy(ys_ref, dref[t * TOP_K + k], buf, (k * tm + t) * tr, tr, sem)

    def wait_all(buf, sem):
        def wait(t, carry):
            for k in range(TOP_K):
                copy(dest_ref, t, k, buf, sem).wait()
            return carry
        lax.fori_loop(0, tm, wait, 0)

    @pl.when(i == 0)
    def _():
        def start(t, carry):
            for k in range(TOP_K):
                copy(dest_ref, t, k, buf_a, sem_a).start(priority=k % 2)
            return carry
        lax.fori_loop(0, tm, start, 0)

    def step(buf, sem, buf_next, sem_next):
        wait_all(buf, sem)
        for t in range(tm):
            for k in range(TOP_K):
                copy(dest_next_ref, t, k, buf_next, sem_next).start(priority=k % 2)
        lo, hi = _unpack_pairs(_load_token_tiles(hp_ref, 0, tm, tr))
        lo = lo.astype(BF16)
        hi = hi.astype(BF16)
        d = functools.partial(jnp.dot, preferred_element_type=F32)
        a = d(lo, ws1_ref[:dh, :]) + d(hi, ws1_ref[dh:, :])
        g = d(lo, ws3_ref[:dh, :]) + d(hi, ws3_ref[dh:, :])
        shared = d((_silu(a) * g).astype(BF16), ws2_ref[...])
        acc_lo = jnp.zeros((tm, dh), F32)
        acc_hi = jnp.zeros((tm, dh), F32)
        for k in range(TOP_K):
            ylo, yhi = _unpack_pairs(_load_token_tiles(buf, k * tm * tr, tm, tr))
            wk = wt_ref[:, k:k + 1]
            acc_lo = acc_lo + wk * ylo
            acc_hi = acc_hi + wk * yhi
        o_ref[:, :dh] = x_ref[:, :dh] + g_ref[0][:, :dh] * (acc_lo + shared[:, :dh])
        o_ref[:, dh:] = x_ref[:, dh:] + g_ref[0][:, dh:] * (acc_hi + shared[:, dh:])

        @pl.when(i == pl.num_programs(0) - 1)
        def _():
            wait_all(buf_next, sem_next)

    @pl.when(i % 2 == 0)
    def _():
        step(buf_a, sem_a, buf_b, sem_b)

    @pl.when(i % 2 == 1)
    def _():
        step(buf_b, sem_b, buf_a, sem_a)


def _combine(dest, x, hp, wt, gate, ws1, ws3, ws2, ys, rows_per_group, tr):
    m, d = x.shape
    f = ws1.shape[1]
    tm = math.gcd(_tile(m, COMBINE_TOKENS, LANES), rows_per_group)
    nt = m // tm
    grp = lambda i: ((i * tm) // rows_per_group, 0, 0)
    buf = pltpu.VMEM((TOP_K * tm * tr, LANES), U32)
    return pl.pallas_call(
        functools.partial(_combine_kernel, tr=tr),
        grid=(nt,),
        in_specs=[pl.BlockSpec((tm * TOP_K,), lambda i: (i,), memory_space=pltpu.SMEM),
                  pl.BlockSpec((tm * TOP_K,), lambda i: (jnp.minimum(i + 1, nt - 1),), memory_space=pltpu.SMEM),
                  pl.BlockSpec((tm, d), lambda i: (i, 0)),
                  pl.BlockSpec((tm * tr, LANES), lambda i: (i, 0)),
                  pl.BlockSpec((tm, TOP_K), lambda i: (i, 0)),
                  pl.BlockSpec((1, 1, d), grp),
                  pl.BlockSpec((d, f), lambda i: (0, 0)),
                  pl.BlockSpec((d, f), lambda i: (0, 0)),
                  pl.BlockSpec((f, d), lambda i: (0, 0)),
                  pl.BlockSpec(memory_space=pl.ANY)],
        out_specs=pl.BlockSpec((tm, d), lambda i: (i, 0)),
        out_shape=jax.ShapeDtypeStruct((m, d), F32),
        scratch_shapes=[buf, buf, pltpu.SemaphoreType.DMA(()), pltpu.SemaphoreType.DMA(())],
        compiler_params=_params("arbitrary"),
        name="moe_combine",
    )(dest, dest, x, hp, wt, gate, ws1, ws3, ws2, ys)


def _moe(streams, layer, nw, rw, rb, w1, w3, w2, ws1, ws3, ws2):
    d = streams[0][0].shape[1]
    rwt = rw.T
    counts = jnp.zeros((N_EXPERTS, LANES), F32)
    routed = []
    for x, sh, sc, _, rpg in streams:
        hp, idx, wt, rank, counts = _router(x, nw, sh, sc, rwt, rb, counts, rpg)
        routed.append((hp, idx, wt, rank))
    n_assign = sum(s[0].shape[0] for s in streams) * TOP_K
    n_blocks = (n_assign + N_EXPERTS * (MOE_BLOCK - 1) + MOE_BLOCK - 1) // MOE_BLOCK
    cnt = counts[:, 0].astype(I32)
    padded = (cnt + MOE_BLOCK - 1) // MOE_BLOCK * MOE_BLOCK
    pad_end = jnp.cumsum(padded)
    pad_start = pad_end - padded
    bstart = jnp.arange(n_blocks, dtype=I32) * MOE_BLOCK
    block_e = jnp.minimum(jnp.sum((bstart[:, None] >= pad_end[None, :]).astype(I32), axis=1), N_EXPERTS - 1)
    block_nv = jnp.clip(cnt[block_e] - (bstart - pad_start[block_e]), 0, MOE_BLOCK).astype(I32)
    e_ar = jnp.arange(N_EXPERTS, dtype=I32)
    tr = d // 2 // LANES
    dests = [((jnp.sum(jnp.where(idx[:, :, None] == e_ar, pad_start, 0), axis=-1) + rank) * tr).T.reshape(-1)
             for _, idx, _, rank in routed]
    dest_all = jnp.concatenate(dests, axis=0) if len(dests) > 1 else dests[0]
    xs = _dispatch(block_nv, dest_all, [r[0] for r in routed], n_blocks * MOE_BLOCK, tr)
    ys = _grouped_ffn(block_e, block_nv, xs, w1, w3, w2, layer, tr)
    ws1b, ws3b, ws2b = ws1.astype(BF16), ws3.astype(BF16), ws2.astype(BF16)
    return [_combine(dest, x, hp, wt.T, gate, ws1b, ws3b, ws2b, ys, rpg, tr)
            for (x, _, _, gate, rpg), (hp, _, wt, _), dest in zip(streams, routed, dests)]


def _mixer_ab(xs, cs, mx, mc, b, t, n_c, layer, nw, w_in, w_out, q_norm, k_norm, lam_vec, subln, conv_w,
              a_log, dt_bias, out_norm):
    n_main = w_in.shape[1] - 4 * DN_HEADS
    assert n_main % LANES == 0
    w_main = w_in[:, :n_main].astype(BF16)
    w_small = jnp.zeros((w_in.shape[0], LANES), F32).at[:, :4 * DN_HEADS].set(w_in[:, n_main:])
    nqk = 2 * DIFF_HEADS * DIFF_DK
    assert n_main % nqk == 0
    tabs = _rope_tables(t, DIFF_DK)
    hw = jnp.stack([q_norm * (DIFF_DK ** -0.5 * LOG2E), k_norm])
    groups = nqk // LANES
    blocks = ((0, 1, (0,) * groups), (1, 2, (1,) * groups))
    z_x, zs_x = _norm_mod_matmul(xs, nw, mx[0], mx[1], w_main, t, w_small, tn=nqk,
                                 heads=dict(hd=DIFF_DK, weights=hw, blocks=blocks, tables=tabs, t=t))
    z_c, zs_c = _norm_mod_matmul(cs, nw, mc[0], mc[1], w_main, b * n_c, w_small, tn=nqk,
                                 heads=dict(hd=DIFF_DK, weights=hw, blocks=blocks, tables=None, t=t))
    lam_init = 0.8 - 0.6 * math.exp(-0.3 * layer)
    od_x = _diff_attention(lam_vec, z_x, z_c, z_c, subln, lam_init, b, n_c, kx=z_x, zx=z_x, t=t)
    od_c = _diff_attention(lam_vec, z_c, z_c, z_c, subln, lam_init, b, n_c)
    dn_col0 = 2 * nqk + DIFF_HEADS * DIFF_DV
    dn_x = _dn_prep(z_x, dn_col0, conv_w, t)
    dn_c = _dn_prep(z_c, dn_col0, conv_w, n_c)
    zst_x = zs_x[:, :4 * DN_HEADS].T
    zst_c = zs_c[:, :4 * DN_HEADS].T
    prm = _deltanet_params(a_log, dt_bias)
    s0 = jnp.zeros((b, DN_HEADS, DN_DK, DN_DV), F32)
    o_cf, o_cb, s_cf, s_cb = _deltanet(dn_c, zs_c, zst_c, prm, s0, s0, b, n_c)
    o_xf, o_xb, _, _ = _deltanet(dn_x, zs_x, zst_x, prm, s_cf, s_cb, b, t)
    gate_col0 = dn_col0 + conv_w.shape[1]
    w_out_b = w_out.astype(BF16)
    x1 = _outproj_ab(od_x, o_xf, o_xb, z_x, gate_col0, out_norm, w_out_b, xs, mx[2], t)
    c1 = _outproj_ab(od_c, o_cf, o_cb, z_c, gate_col0, out_norm, w_out_b, cs, mc[2], b * n_c)
    return x1, c1


def _mixer_swa(xs, cs, mx, mc, b, t, n_c, nw, w_in, w_out, q_norm, k_norm, sink):
    nq = SWA_HEADS * SWA_DH
    nkv = SWA_KV_HEADS * SWA_DH
    w_b = w_in.astype(BF16)
    tabs = _rope_tables(t, SWA_DH)
    rep = LANES // SWA_DH
    hw = jnp.stack([jnp.tile(q_norm, rep) * (SWA_DH ** -0.5 * LOG2E), jnp.tile(k_norm, rep)])
    q_rows = (0,) * (nq // LANES)
    kv_rows = (1,) * (nkv // LANES) + (None,) * (nkv // LANES)
    z_x = _norm_mod_matmul(xs, nw, mx[0], mx[1], w_b, t, tn=w_b.shape[1],
                           heads=dict(hd=SWA_DH, weights=hw, tables=tabs, t=t, blocks=((0, 1, q_rows + kv_rows),)))
    z_c = _norm_mod_matmul(cs, nw, mc[0], mc[1], w_b[:, nq:], b * n_c, tn=2 * nkv,
                           heads=dict(hd=SWA_DH, weights=hw, tables=None, t=t, blocks=((0, 1, kv_rows),)))
    att = _swa_attention(sink.reshape(SWA_KV_HEADS, SWA_GROUP), z_x, _dup_heads(z_x, nq), _dup_heads(z_x, nq + nkv),
                         _dup_heads(z_c, 0), _dup_heads(z_c, nkv), b, t, n_c)
    return _outproj(att, w_out.astype(BF16), xs, mx[2], t)


def kernel(x, c, ctx, c_ctx, mod_w, mod_b, norm_mix, norm_ffn, ab_w_in, ab_w_out, diff_q_norm, diff_k_norm, diff_lambda, diff_subln, dn_conv, dn_a_log, dn_dt_bias, dn_out_norm, swa_w_in, swa_w_out, swa_q_norm, swa_k_norm, swa_sink, router_w, router_bias, exp_w1, exp_w3, exp_w2, shared_w1, shared_w3, shared_w2):
    b, t, d = x.shape
    n_c = ctx.shape[1]
    depth = mod_w.shape[0]
    assert depth == 2 and t % Q_BLOCK == 0 and t % DN_CHUNK == 0 and n_c % DN_CHUNK == 0
    xs = x.reshape(b * t, d)
    cs = ctx.reshape(b * n_c, d)
    n_mod = -(-(b + 1) // SUBLANES) * SUBLANES
    a_mod = jnp.zeros((n_mod, d), F32).at[0].set(c_ctx).at[1:1 + b].set(c)
    for layer in range(depth):
        with_ctx = layer < depth - 1
        p = layer // 2
        mod = _modulation(a_mod, mod_w, layer, mod_b[layer])
        mc = [mod[0:1, j * d:(j + 1) * d].reshape(1, 1, d) for j in range(6)]
        mx = [mod[1:1 + b, j * d:(j + 1) * d].reshape(b, 1, d) for j in range(6)]
        if layer % 2 == 0:
            xs, c_new = _mixer_ab(xs, cs, mx, mc, b, t, n_c, layer, norm_mix[layer], ab_w_in[p], ab_w_out[p],
                                  diff_q_norm[p], diff_k_norm[p], diff_lambda[p], diff_subln[p], dn_conv[p],
                                  dn_a_log[p], dn_dt_bias[p], dn_out_norm[p])
        else:
            assert not with_ctx
            xs = _mixer_swa(xs, cs, mx, mc, b, t, n_c, norm_mix[layer], swa_w_in[p], swa_w_out[p],
                            swa_q_norm[p], swa_k_norm[p], swa_sink[p])
            c_new = None
        moe_w = (layer, norm_ffn[layer], router_w[layer], router_bias[layer], exp_w1, exp_w3, exp_w2,
                 shared_w1[layer], shared_w3[layer], shared_w2[layer])
        if with_ctx:
            cs, xs = _moe([(c_new, mc[3], mc[4], mc[5], b * n_c), (xs, mx[3], mx[4], mx[5], t)], *moe_w)
        else:
            (xs,) = _moe([(xs, mx[3], mx[4], mx[5], t)], *moe_w)
    return xs.reshape(b, t, d)
```

```python
import functools
import math

import jax
import jax.numpy as jnp
import numpy as np
from jax import lax
from jax.experimental import pallas as pl
from jax.experimental.pallas import tpu as pltpu

F32 = jnp.float32
BF16 = jnp.bfloat16
I32 = jnp.int32
U32 = jnp.uint32

EPS = 1e-6
GRID_W = 64
ROPE_BASE = 10000.0
DIFF_HEADS = 4
DIFF_DK = 128
DIFF_DV = 256
DN_HEADS = 8
DN_DK = 128
DN_DV = 128
DN_CONV = 5
DN_CHUNK = 128
SWA_HEADS = 32
SWA_KV_HEADS = 4
SWA_GROUP = SWA_HEADS // SWA_KV_HEADS
SWA_DH = 64
WINDOW = 128
Q_BLOCK = 128
N_EXPERTS = 64
TOP_K = 8
N_GROUPS = 8
GROUP_SIZE = N_EXPERTS // N_GROUPS
TOPK_GROUPS = 4
ROUTED_SCALE = 2.5
MOE_BLOCK = 512

LANES = 128
SUBLANES = 8
VMEM_LIMIT_BYTES = 56 * 1024 * 1024

ROWS_STREAMED = 1024
COLS_STREAMED = 1024
ROWS_RESIDENT = 256
ROWS_OUTPROJ = 1024
MOD_COLS = 768
DIFF_Q_ROWS = 1024
CHUNK_ROWS = 256
ROUTER_TOKENS = 512
DISPATCH_TOKENS = 512
COMBINE_TOKENS = 128

NT_DIMS = (((1,), (1,)), ((), ()))
LOG2E = math.log2(math.e)


def _params(*semantics):
    return pltpu.CompilerParams(dimension_semantics=semantics, vmem_limit_bytes=VMEM_LIMIT_BYTES)


def _tile(n, pref, mult):
    if n <= pref:
        return n
    t = pref - pref % mult
    while t > mult and n % t:
        t -= mult
    assert n % t == 0, (n, pref, mult)
    return t


def _mm(a, b):
    return jnp.dot(a.astype(BF16), b.astype(BF16), preferred_element_type=F32)


def _split2(x):
    hi = x.astype(BF16)
    lo = (x - hi.astype(F32)).astype(BF16)
    return hi, lo


def _dot3(a, b, dims=None):
    if dims is None:
        dims = (((a.ndim - 1,), (0,)), ((), ()))
    ah, al = _split2(a)
    bh, bl = _split2(b)
    d = functools.partial(lax.dot_general, dimension_numbers=dims, preferred_element_type=F32)
    return d(ah, bh) + d(ah, bl) + d(al, bh)


def _silu(x):
    return x * jax.nn.sigmoid(x)


def _softplus(x):
    return jnp.maximum(x, 0.0) + jnp.log(1.0 + jnp.exp(-jnp.abs(x)))


def _pack_pairs(lo, hi):
    ulo = lax.bitcast_convert_type(lo.astype(BF16).astype(F32), U32) >> 16
    uhi = lax.bitcast_convert_type(hi.astype(BF16).astype(F32), U32) & jnp.uint32(0xFFFF0000)
    return ulo | uhi


def _unpack_pairs(u):
    lo = lax.bitcast_convert_type(u << 16, F32)
    hi = lax.bitcast_convert_type(u & jnp.uint32(0xFFFF0000), F32)
    return lo, hi


def _store_token_tiles(ref, base, packed):
    n, width = packed.shape
    tr = width // LANES
    for j in range(tr):
        ref[pl.ds(base + j, n, stride=tr), :] = packed[:, j * LANES:(j + 1) * LANES]


def _load_token_tiles(ref, base, n, tr):
    return jnp.concatenate([ref[pl.ds(base + j, n, stride=tr), :] for j in range(tr)], axis=1)


def _tile_copy(src, src_row, dst, dst_row, tr, sem):
    return pltpu.make_async_copy(src.at[pl.ds(pl.multiple_of(src_row, tr), tr)],
                                 dst.at[pl.ds(pl.multiple_of(dst_row, tr), tr)], sem)


def _mod_kernel(a_ref, w_ref, b_ref, o_ref):
    o_ref[...] = _dot3(_silu(a_ref[...]), w_ref[0]) + b_ref[...]


def _modulation(a, w_all, layer, b):
    r, d = a.shape
    n = w_all.shape[2]
    tn = _tile(n, MOD_COLS, LANES)
    return pl.pallas_call(
        _mod_kernel,
        grid=(n // tn,),
        in_specs=[pl.BlockSpec((r, d), lambda j: (0, 0)),
                  pl.BlockSpec((1, d, tn), lambda j: (layer, 0, j)),
                  pl.BlockSpec((1, tn), lambda j: (0, j))],
        out_specs=pl.BlockSpec((r, tn), lambda j: (0, j)),
        out_shape=jax.ShapeDtypeStruct((r, n), F32),
        compiler_params=_params("parallel"),
        name="modulation",
    )(a, w_all, b.reshape(1, n))


def _norm_mod(x, nw, sh, sc):
    y = x * lax.rsqrt(jnp.mean(x * x, axis=-1, keepdims=True) + EPS) * nw
    return y * (1.0 + sc) + sh


def _store_head_groups(o_ref, ys, rows, hw_ref, seg_ref, swap_ref, cos, sin, hd):
    normed = [g for g in range(len(ys)) if rows[g] is not None]
    if normed:
        seg = seg_ref[...]
        sums = {g: jnp.dot((ys[g] * ys[g]).astype(BF16), seg, preferred_element_type=F32) for g in normed}
        for g in normed:
            ys[g] = ys[g] * lax.rsqrt(sums[g] * (1.0 / hd) + EPS) * hw_ref[rows[g]]
        if cos is not None:
            swap = swap_ref[...]
            partners = {g: jnp.dot(ys[g].astype(BF16), swap, preferred_element_type=F32) for g in normed}
            for g in normed:
                ys[g] = ys[g] * cos + partners[g] * sin
    for g, y in enumerate(ys):
        o_ref[:, g * LANES:(g + 1) * LANES] = y.astype(o_ref.dtype)


def _head_matrices(hd):
    lane = np.arange(LANES)
    seg = (lane[:, None] // hd) == (lane[None, :] // hd)
    q = hd // 4
    partner = np.where((lane % (2 * q)) < q, lane + q, lane - q)
    swap = lane[:, None] == partner[None, :]
    return jnp.asarray(seg, BF16), jnp.asarray(swap, BF16)


def _nmm_kernel(x_ref, nw_ref, sh_ref, sc_ref, w_ref, *rest, has_small, head_blocks, hd, rope, single):
    rest = list(rest)
    ws_ref = rest.pop(0) if has_small else None
    hw_ref, seg_ref, swap_ref = (rest.pop(0), rest.pop(0), rest.pop(0)) if head_blocks else (None, None, None)
    cos_ref, sin_ref = (rest.pop(0), rest.pop(0)) if rope else (None, None)
    o_ref = rest.pop(0)
    os_ref = rest.pop(0) if has_small else None
    (h_ref,) = rest
    if single:
        h = _norm_mod(x_ref[...], nw_ref[...], sh_ref[0], sc_ref[0])
        hb = h.astype(BF16)
        if has_small:
            os_ref[...] = _dot3(h, ws_ref[...])
        rows = head_blocks[0][2] if head_blocks else (None,) * (o_ref.shape[1] // LANES)
        cos = cos_ref[...] if rope else None
        sin = sin_ref[...] if rope else None
        cw = 2 * LANES
        n_out = o_ref.shape[1]
        parts = [jnp.dot(hb, w_ref[:, c0:c0 + cw], preferred_element_type=F32) for c0 in range(0, n_out, cw)]
        ys = [parts[g // 2][:, (g % 2) * LANES:(g % 2 + 1) * LANES] for g in range(n_out // LANES)]
        _store_head_groups(o_ref, ys, rows, hw_ref, seg_ref, swap_ref, cos, sin, hd)
        return
    j = pl.program_id(1)

    @pl.when(j == 0)
    def _():
        rows = x_ref.shape[0]
        step = _tile(rows, CHUNK_ROWS, SUBLANES)
        for r0 in range(0, rows, step):
            h = _norm_mod(x_ref[r0:r0 + step, :], nw_ref[...], sh_ref[0], sc_ref[0])
            h_ref[r0:r0 + step, :] = h.astype(BF16)
            if has_small:
                os_ref[r0:r0 + step, :] = _dot3(h, ws_ref[...])

    r = jnp.dot(h_ref[...], w_ref[...], preferred_element_type=F32)
    if not head_blocks:
        o_ref[...] = r.astype(o_ref.dtype)
        return
    tm, tn = r.shape
    plain = j >= 0
    for lo, hi, rows in head_blocks:
        inside = (j >= lo) & (j < hi)
        plain = plain & jnp.logical_not(inside)

        @pl.when(inside)
        def _(rows=rows):
            cos = cos_ref[...] if rope else None
            sin = sin_ref[...] if rope else None
            ys = [r[:, g * LANES:(g + 1) * LANES] for g in range(tn // LANES)]
            _store_head_groups(o_ref, ys, rows, hw_ref, seg_ref, swap_ref, cos, sin, hd)

    @pl.when(plain)
    def _():
        o_ref[...] = r.astype(o_ref.dtype)


def _norm_mod_matmul(x, nw, sh, sc, w, rows_per_group, w_small=None, tn=COLS_STREAMED, heads=None):
    m, d = x.shape
    n = w.shape[1]
    tn = _tile(n, tn, LANES)
    single = tn == n and n % (2 * LANES) == 0
    tm = math.gcd(_tile(m, ROWS_RESIDENT if single else ROWS_STREAMED, SUBLANES), rows_per_group)
    has_small = w_small is not None
    grp = lambda i, j: ((i * tm) // rows_per_group, 0, 0)
    in_specs = [pl.BlockSpec((tm, d), lambda i, j: (i, 0)),
                pl.BlockSpec((1, d), lambda i, j: (0, 0)),
                pl.BlockSpec((1, 1, d), grp),
                pl.BlockSpec((1, 1, d), grp),
                pl.BlockSpec((d, tn), lambda i, j: (0, j))]
    args = [x, nw.reshape(1, d), sh, sc, w]
    out_specs = [pl.BlockSpec((tm, tn), lambda i, j: (i, j))]
    out_shape = [jax.ShapeDtypeStruct((m, n), BF16)]
    if has_small:
        in_specs.append(pl.BlockSpec((d, LANES), lambda i, j: (0, 0)))
        args.append(w_small)
        out_specs.append(pl.BlockSpec((tm, LANES), lambda i, j: (i, 0)))
        out_shape.append(jax.ShapeDtypeStruct((m, LANES), F32))
    head_blocks, hd, rope = (), LANES, False
    if heads is not None:
        head_blocks, hd = tuple(heads["blocks"]), heads["hd"]
        hw = heads["weights"]
        in_specs += [pl.BlockSpec((hw.shape[0], 1, LANES), lambda i, j: (0, 0, 0)),
                     pl.BlockSpec((LANES, LANES), lambda i, j: (0, 0)),
                     pl.BlockSpec((LANES, LANES), lambda i, j: (0, 0))]
        args += [hw.reshape(hw.shape[0], 1, LANES), *_head_matrices(hd)]
        if heads["tables"] is not None:
            rope = True
            nt = heads["t"] // tm
            assert heads["t"] % tm == 0
            in_specs += [pl.BlockSpec((tm, LANES), lambda i, j: (i % nt, 0))] * 2
            args += list(heads["tables"])
    outs = pl.pallas_call(
        functools.partial(_nmm_kernel, has_small=has_small, head_blocks=head_blocks, hd=hd, rope=rope,
                          single=single),
        grid=(m // tm, n // tn),
        in_specs=in_specs,
        out_specs=out_specs,
        out_shape=out_shape,
        scratch_shapes=[pltpu.VMEM((tm, d), BF16)],
        compiler_params=_params("parallel", "arbitrary"),
        name="norm_mod_matmul",
    )(*args)
    return outs if has_small else outs[0]


def _rope_tables(t, head_dim):
    q = head_dim // 4
    pos = jnp.arange(t, dtype=I32)
    row = (pos // GRID_W).astype(F32)
    col = (pos % GRID_W).astype(F32)
    axis_dim = head_dim // 2
    inv_freq = ROPE_BASE ** (-jnp.arange(0, axis_dim, 2, dtype=F32) / axis_dim)
    lane = jnp.arange(LANES) % head_dim
    freq = inv_freq[lane % q]
    p = jnp.where((lane < head_dim // 2)[None, :], row[:, None], col[:, None])
    ang = p * freq[None, :]
    sign = jnp.where((lane % (2 * q)) < q, -1.0, 1.0)
    return jnp.cos(ang), jnp.sin(ang) * sign[None, :]


def _diff_attn_kernel(lv_ref, q_ref, *rest, lam_init, has_lat):
    if has_lat:
        kx_ref, vx_ref, kc_ref, vc_ref, w_ref, o_ref = rest
    else:
        kc_ref, vc_ref, w_ref, o_ref = rest
    lv = lv_ref[...]
    lam = (jnp.exp(jnp.sum(lv[0:1] * lv[1:2], keepdims=True))
           - jnp.exp(jnp.sum(lv[2:3] * lv[3:4], keepdims=True)) + lam_init)
    q = q_ref[...]

    def probs(c):
        sl = slice(c * DIFF_DK, (c + 1) * DIFF_DK)
        qc = q[:, sl]
        s_c = lax.dot_general(qc, kc_ref[:, sl], NT_DIMS, preferred_element_type=F32)
        m = jnp.max(s_c, axis=-1, keepdims=True)
        p_x = None
        if has_lat:
            s_x = lax.dot_general(qc, kx_ref[:, sl], NT_DIMS, preferred_element_type=F32)
            m = jnp.maximum(m, jnp.max(s_x, axis=-1, keepdims=True))
            p_x = jnp.exp2(s_x - m)
        p_c = jnp.exp2(s_c - m)
        l = jnp.sum(p_c, axis=-1, keepdims=True)
        if has_lat:
            l = l + jnp.sum(p_x, axis=-1, keepdims=True)
        return p_x, p_c, l

    p1x, p1c, l1 = probs(0)
    p2x, p2c, l2 = probs(1)
    ratio = lam * l1 * (1.0 / l2)
    o = jnp.dot((p1c - p2c * ratio).astype(BF16), vc_ref[...], preferred_element_type=F32)
    if has_lat:
        o = o + jnp.dot((p1x - p2x * ratio).astype(BF16), vx_ref[...], preferred_element_type=F32)
    o = o * (1.0 / l1)
    o = o * lax.rsqrt(jnp.mean(o * o, axis=-1, keepdims=True) + EPS)
    o_ref[...] = (o * w_ref[...] * (1.0 - lam_init)).astype(o_ref.dtype)


def _diff_attention(lam_vec, q, kc, zc, subln, lam_init, b, n_c, kx=None, zx=None, t=None):
    has_lat = kx is not None
    hw = 2 * DIFF_DK
    kblk0 = DIFF_HEADS
    vblk0 = (2 * DIFF_HEADS * hw) // DIFF_DV
    tq_all = t if has_lat else n_c
    tq = _tile(tq_all, DIFF_Q_ROWS, SUBLANES)
    nq = tq_all // tq
    in_specs = [pl.BlockSpec((4, DIFF_DK), lambda bi, h, qi: (0, 0)),
                pl.BlockSpec((tq, hw), lambda bi, h, qi: (bi * nq + qi, h))]
    args = [lam_vec, q]
    if has_lat:
        in_specs += [pl.BlockSpec((t, hw), lambda bi, h, qi: (bi, kblk0 + h)),
                     pl.BlockSpec((t, DIFF_DV), lambda bi, h, qi: (bi, vblk0 + h))]
        args += [kx, zx]
    in_specs += [pl.BlockSpec((n_c, hw), lambda bi, h, qi: (bi, kblk0 + h)),
                 pl.BlockSpec((n_c, DIFF_DV), lambda bi, h, qi: (bi, vblk0 + h)),
                 pl.BlockSpec((1, DIFF_DV), lambda bi, h, qi: (0, 0))]
    args += [kc, zc, subln.reshape(1, DIFF_DV)]
    return pl.pallas_call(
        functools.partial(_diff_attn_kernel, lam_init=lam_init, has_lat=has_lat),
        grid=(b, DIFF_HEADS, nq),
        in_specs=in_specs,
        out_specs=pl.BlockSpec((tq, DIFF_DV), lambda bi, h, qi: (bi * nq + qi, h)),
        out_shape=jax.ShapeDtypeStruct((b * tq_all, DIFF_HEADS * DIFF_DV), BF16),
        compiler_params=_params("parallel", "parallel", "arbitrary"),
        name="diff_attention",
    )(*args)


def _dn_prep_kernel(z_ref, cw_ref, o_ref, pad_ref, *, seg):
    halo = SUBLANES
    pad_ref[0:halo, :] = jnp.zeros((halo, LANES), F32)
    pad_ref[halo + seg:2 * halo + seg, :] = jnp.zeros((halo, LANES), F32)
    pad_ref[halo:halo + seg, :] = z_ref[...].astype(F32)
    kind = pl.program_id(1) // DN_HEADS
    qk_scale = jnp.where(kind == 0, DN_DK ** -0.5, 1.0).astype(F32)
    rows = _tile(seg, CHUNK_ROWS, SUBLANES)
    for r0 in range(0, seg, rows):
        acc = jnp.zeros((rows, LANES), F32)
        for j in range(DN_CONV):
            acc = acc + cw_ref[j:j + 1, :] * pad_ref[pl.ds(halo + r0 + j - DN_CONV // 2, rows), :]
        y = _silu(acc)
        nrm = y * lax.rsqrt(jnp.sum(y * y, axis=-1, keepdims=True) + EPS) * qk_scale
        o_ref[r0:r0 + rows, :] = jnp.where(kind < 2, nrm, y).astype(o_ref.dtype)


def _dn_prep(z, col0, conv_w, seg):
    m = z.shape[0]
    ncols = conv_w.shape[1]
    cblk0 = col0 // LANES
    cw = jnp.zeros((SUBLANES, ncols), F32).at[:DN_CONV].set(conv_w)
    return pl.pallas_call(
        functools.partial(_dn_prep_kernel, seg=seg),
        grid=(m // seg, ncols // LANES),
        in_specs=[pl.BlockSpec((seg, LANES), lambda s, g: (s, cblk0 + g)),
                  pl.BlockSpec((SUBLANES, LANES), lambda s, g: (0, g))],
        out_specs=pl.BlockSpec((seg, LANES), lambda s, g: (s, g)),
        out_shape=jax.ShapeDtypeStruct((m, ncols), BF16),
        scratch_shapes=[pltpu.VMEM((seg + 2 * SUBLANES, LANES), F32)],
        compiler_params=_params("parallel", "parallel"),
        name="dn_prep",
    )(z, cw)


def _dn_heads(reverse, q_ref, k_ref, v_ref, zs_ref, zst_ref, pr_ref, pca_ref, pcd_ref, o_ref, s_ref, ri, ci):
    c = DN_CHUNK
    if reverse:
        later, strict, later_t = ri <= ci, ri < ci, ri >= ci
    else:
        later, strict, later_t = ri >= ci, ri > ci, ri <= ci
    eye = (ri == ci).astype(F32)
    tri = later.astype(BF16)
    tri_t = later_t.astype(BF16)

    zs = zs_ref[...]
    beta_cols = jax.nn.sigmoid(zs)
    g_cols = -jnp.exp(pr_ref[0:1, :]) * _softplus(zs + pr_ref[1:2, :])
    g_hi = g_cols.astype(BF16)
    g_r1 = g_cols - g_hi.astype(F32)
    g_mid = g_r1.astype(BF16)
    g_lo = (g_r1 - g_mid.astype(F32)).astype(BF16)
    d = functools.partial(jnp.dot, preferred_element_type=F32)
    gc_cols = d(tri, g_hi) + d(tri, g_mid) + d(tri, g_lo)
    g_rows = -jnp.exp(pca_ref[...]) * _softplus(zst_ref[...] + pcd_ref[...])
    h_hi = g_rows.astype(BF16)
    h_r1 = g_rows - h_hi.astype(F32)
    h_mid = h_r1.astype(BF16)
    h_lo = (h_r1 - h_mid.astype(F32)).astype(BF16)
    gc_rows = d(h_hi, tri_t) + d(h_mid, tri_t) + d(h_lo, tri_t)

    dir_off = DN_HEADS if reverse else 0
    last = 0 if reverse else c - 1
    neg_inf = jnp.float32(-jnp.inf)
    heads = []
    for h in range(DN_HEADS):
        sl = slice(h * DN_DK, (h + 1) * DN_DK)
        cb = dir_off + h
        cg = 2 * DN_HEADS + dir_off + h
        beta = beta_cols[:, cb:cb + 1]
        gcol = gc_cols[:, cg:cg + 1]
        grow = gc_rows[cg:cg + 1, :]
        glast = grow[:, last:last + 1]
        q = q_ref[:, sl]
        k = k_ref[:, sl]
        kf = k.astype(F32)
        decay = jnp.exp(jnp.where(later, gcol - grow, neg_inf))
        kb = kf * beta
        both = lax.dot_general(jnp.concatenate([kb.astype(BF16), q], axis=0), k, NT_DIMS,
                               preferred_element_type=F32)
        lmat = jnp.where(strict, both[:c] * decay, 0.0)
        eg = jnp.exp(gcol)
        heads.append(dict(
            sl=sl, h=h, o_ref=o_ref, s_ref=s_ref, glast=glast, lmat=lmat, amat=both[c:] * decay,
            inv=eye - jnp.where((ri >> 1) == (ci >> 1), lmat, 0.0),
            rhs=jnp.concatenate([v_ref[:, sl].astype(F32) * beta, kb * eg], axis=1).astype(BF16),
            qe=(q.astype(F32) * eg).astype(BF16),
            kdec_t=(kf * jnp.exp(glast - gcol)).T.astype(BF16)))
    return heads


def _deltanet_kernel(qf_ref, kf_ref, vf_ref, zsf_ref, zstf_ref, qb_ref, kb_ref, vb_ref, zsb_ref, zstb_ref,
                     pr_ref, pca_ref, pcd_ref, s0f_ref, s0b_ref,
                     of_ref, ob_ref, soutf_ref, soutb_ref, sf_ref, sb_ref, *, n_chunks):
    step = pl.program_id(1)
    c = DN_CHUNK

    @pl.when(step == 0)
    def _():
        sf_ref[...] = s0f_ref[0]
        sb_ref[...] = s0b_ref[0]

    ri = lax.broadcasted_iota(I32, (c, c), 0)
    ci = lax.broadcasted_iota(I32, (c, c), 1)
    prm = (pr_ref, pca_ref, pcd_ref)
    heads = (_dn_heads(False, qf_ref, kf_ref, vf_ref, zsf_ref, zstf_ref, *prm, of_ref, sf_ref, ri, ci)
             + _dn_heads(True, qb_ref, kb_ref, vb_ref, zsb_ref, zstb_ref, *prm, ob_ref, sb_ref, ri, ci))
    lev = 1
    while (1 << lev) < c:
        blk = ((ri >> (lev + 1)) == (ci >> (lev + 1))) & ((ri >> lev) != (ci >> lev))
        half = [_mm(hd["inv"], jnp.where(blk, hd["lmat"], 0.0)) for hd in heads]
        for hd, t in zip(heads, half):
            hd["inv"] = hd["inv"] - _mm(t, hd["inv"])
        lev += 1
    uws = [_mm(hd["inv"], hd["rhs"]) for hd in heads]
    states = [hd["s_ref"][hd["h"]] for hd in heads]
    new_states = []
    for hd, uw, s in zip(heads, uws, states):
        ws_qs = _mm(jnp.concatenate([uw[:, DN_DV:].astype(BF16), hd["qe"]], axis=0), s)
        v_new = uw[:, :DN_DV] - ws_qs[:c]
        hd["o_ref"][:, hd["sl"]] = (ws_qs[c:] + _mm(hd["amat"], v_new)).astype(hd["o_ref"].dtype)
        new_states.append(s * jnp.exp(hd["glast"]) + _mm(hd["kdec_t"], v_new))
    for hd, s in zip(heads, new_states):
        hd["s_ref"][hd["h"]] = s

    @pl.when(step == n_chunks - 1)
    def _():
        soutf_ref[0] = sf_ref[...]
        soutb_ref[0] = sb_ref[...]


def _deltanet(dn, zs, zst, prm, s0f, s0b, b, seg):
    pr, pca, pcd = prm
    n = seg // DN_CHUNK
    hw = DN_HEADS * DN_DK
    fw = lambda bi, s: bi * n + s
    bw = lambda bi, s: bi * n + (n - 1 - s)
    state = pl.BlockSpec((1, DN_HEADS, DN_DK, DN_DV), lambda bi, s: (bi, 0, 0, 0))

    def chunk_specs(rb):
        return [pl.BlockSpec((DN_CHUNK, hw), lambda bi, s: (rb(bi, s), 0)),
                pl.BlockSpec((DN_CHUNK, hw), lambda bi, s: (rb(bi, s), 1)),
                pl.BlockSpec((DN_CHUNK, hw), lambda bi, s: (rb(bi, s), 2)),
                pl.BlockSpec((DN_CHUNK, LANES), lambda bi, s: (rb(bi, s), 0)),
                pl.BlockSpec((4 * DN_HEADS, DN_CHUNK), lambda bi, s: (0, rb(bi, s)))]

    out = lambda rb: pl.BlockSpec((DN_CHUNK, hw), lambda bi, s: (rb(bi, s), 0))
    o_shape = jax.ShapeDtypeStruct((b * seg, hw), BF16)
    s_shape = jax.ShapeDtypeStruct((b, DN_HEADS, DN_DK, DN_DV), F32)
    s_scratch = pltpu.VMEM((DN_HEADS, DN_DK, DN_DV), F32)
    return pl.pallas_call(
        functools.partial(_deltanet_kernel, n_chunks=n),
        grid=(b, n),
        in_specs=chunk_specs(fw) + chunk_specs(bw) + [
            pl.BlockSpec((SUBLANES, LANES), lambda bi, s: (0, 0)),
            pl.BlockSpec((4 * DN_HEADS, LANES), lambda bi, s: (0, 0)),
            pl.BlockSpec((4 * DN_HEADS, LANES), lambda bi, s: (0, 0)),
            state, state],
        out_specs=[out(fw), out(bw), state, state],
        out_shape=[o_shape, o_shape, s_shape, s_shape],
        scratch_shapes=[s_scratch, s_scratch],
        compiler_params=_params("parallel", "arbitrary"),
        name="deltanet",
    )(dn, dn, dn, zs, zst, dn, dn, dn, zs, zst, pr, pca, pcd, s0f, s0b)


def _deltanet_params(a_log, dt_bias):
    nh = 2 * DN_HEADS
    a = a_log.reshape(nh).astype(F32)
    dtb = dt_bias.reshape(nh).astype(F32)
    pr = jnp.zeros((SUBLANES, LANES), F32).at[0, nh:2 * nh].set(a).at[1, nh:2 * nh].set(dtb)
    pca = jnp.zeros((2 * nh, LANES), F32).at[nh:].set(jnp.broadcast_to(a[:, None], (nh, LANES)))
    pcd = jnp.zeros((2 * nh, LANES), F32).at[nh:].set(jnp.broadcast_to(dtb[:, None], (nh, LANES)))
    return pr, pca, pcd


def _outproj_ab_kernel(od_ref, of_ref, ob_ref, gate_ref, nw_ref, w_ref, x_ref, g_ref, o_ref):
    nd = od_ref.shape[1]
    y = jnp.dot(od_ref[...], w_ref[:nd, :], preferred_element_type=F32)
    gated = []
    for h in range(DN_HEADS):
        sl = slice(h * DN_DV, (h + 1) * DN_DV)
        o = of_ref[:, sl].astype(F32) + ob_ref[:, sl].astype(F32)
        o = o * lax.rsqrt(jnp.mean(o * o, axis=-1, keepdims=True) + EPS) * nw_ref[...]
        gated.append((o * _silu(gate_ref[:, sl].astype(F32))).astype(BF16))
    y = y + jnp.dot(jnp.concatenate(gated, axis=1), w_ref[nd:, :], preferred_element_type=F32)
    o_ref[...] = x_ref[...] + g_ref[0] * y


def _outproj_ab(od, o_f, o_b, z, gate_col0, out_norm, w, x, gate, rows_per_group):
    m, d = x.shape
    nd, nn = od.shape[1], o_f.shape[1]
    tm = math.gcd(_tile(m, 2 * ROWS_RESIDENT, SUBLANES), rows_per_group)
    grp = lambda i: ((i * tm) // rows_per_group, 0, 0)
    gblk = gate_col0 // nn
    return pl.pallas_call(
        _outproj_ab_kernel,
        grid=(m // tm,),
        in_specs=[pl.BlockSpec((tm, nd), lambda i: (i, 0)),
                  pl.BlockSpec((tm, nn), lambda i: (i, 0)),
                  pl.BlockSpec((tm, nn), lambda i: (i, 0)),
                  pl.BlockSpec((tm, nn), lambda i: (i, gblk)),
                  pl.BlockSpec((1, DN_DV), lambda i: (0, 0)),
                  pl.BlockSpec((nd + nn, d), lambda i: (0, 0)),
                  pl.BlockSpec((tm, d), lambda i: (i, 0)),
                  pl.BlockSpec((1, 1, d), grp)],
        out_specs=pl.BlockSpec((tm, d), lambda i: (i, 0)),
        out_shape=jax.ShapeDtypeStruct((m, d), F32),
        compiler_params=_params("parallel"),
        name="outproj_ab",
    )(od, o_f, o_b, z, out_norm.reshape(1, DN_DV), w, x, gate)


def _outproj_kernel(a_ref, w_ref, x_ref, g_ref, o_ref):
    o_ref[...] = x_ref[...] + g_ref[0] * jnp.dot(a_ref[...], w_ref[...], preferred_element_type=F32)


def _outproj(a, w, x, gate, rows_per_group):
    m, d = x.shape
    kdim = a.shape[1]
    tm = math.gcd(_tile(m, ROWS_OUTPROJ, SUBLANES), rows_per_group)
    tn = d
    grp = lambda i, j: ((i * tm) // rows_per_group, 0, j)
    return pl.pallas_call(
        _outproj_kernel,
        grid=(m // tm, d // tn),
        in_specs=[pl.BlockSpec((tm, kdim), lambda i, j: (i, 0)),
                  pl.BlockSpec((kdim, tn), lambda i, j: (0, j)),
                  pl.BlockSpec((tm, tn), lambda i, j: (i, j)),
                  pl.BlockSpec((1, 1, tn), grp)],
        out_specs=pl.BlockSpec((tm, tn), lambda i, j: (i, j)),
        out_shape=jax.ShapeDtypeStruct((m, d), F32),
        compiler_params=_params("parallel", "arbitrary"),
        name="outproj",
    )(a, w, x, gate)


def _swa_kernel(sink_ref, bias_ref, q_ref, k0_ref, k1_ref, k2_ref, v0_ref, v1_ref, v2_ref, kc_ref, vc_ref,
                o_ref):
    kvh = pl.program_id(1)
    qb = Q_BLOCK
    npair = SWA_GROUP // 2
    lane = lax.broadcasted_iota(I32, (qb, LANES), 1)
    q = q_ref[...]
    parts = []
    for p in range(npair):
        qp = q[:, p * LANES:(p + 1) * LANES]
        parts.append(jnp.where(lane < SWA_DH, qp, jnp.zeros_like(qp)))
        parts.append(jnp.where(lane >= SWA_DH, qp, jnp.zeros_like(qp)))
    qq = jnp.concatenate(parts, axis=0)
    k_lat = jnp.concatenate([k0_ref[...], k1_ref[...], k2_ref[...]], axis=0)
    v_lat = jnp.concatenate([v0_ref[...], v1_ref[...], v2_ref[...]], axis=0)
    s_lat = lax.dot_general(qq, k_lat, NT_DIMS, preferred_element_type=F32)
    s_ctx = lax.dot_general(qq, kc_ref[...], NT_DIMS, preferred_element_type=F32)
    bias = bias_ref[0]
    vc = vc_ref[...]
    outs = []
    for g in range(SWA_GROUP):
        rs = slice(g * qb, (g + 1) * qb)
        sl = s_lat[rs] + bias
        sc = s_ctx[rs]
        sink = sink_ref[kvh, g] * LOG2E
        m = jnp.maximum(jnp.maximum(jnp.max(sl, axis=-1, keepdims=True),
                                    jnp.max(sc, axis=-1, keepdims=True)), sink)
        el = jnp.exp2(sl - m)
        ec = jnp.exp2(sc - m)
        l = jnp.sum(el, axis=-1, keepdims=True) + jnp.sum(ec, axis=-1, keepdims=True) + jnp.exp2(sink - m)
        o = (jnp.dot(el.astype(BF16), v_lat, preferred_element_type=F32)
             + jnp.dot(ec.astype(BF16), vc, preferred_element_type=F32))
        outs.append(o * (1.0 / l))
    for p in range(npair):
        o_ref[:, p * LANES:(p + 1) * LANES] = jnp.where(lane < SWA_DH, outs[2 * p], outs[2 * p + 1]).astype(o_ref.dtype)


def _swa_attention(sink, q, kx, vx, kc, vc, b, t, n_c):
    qb = Q_BLOCK
    nb = t // qb
    gw = SWA_GROUP * SWA_DH
    lat = lambda off: pl.BlockSpec(
        (qb, LANES), lambda bi, h, i: (bi * nb + jnp.clip(i + off, 0, nb - 1), h))
    ctx = pl.BlockSpec((n_c, LANES), lambda bi, h, i: (bi, h))
    r_io = np.arange(qb)[:, None]
    c_io = np.arange(3 * qb)[None, :]
    inside = np.abs(r_io + qb - c_io) <= WINDOW
    variants = [inside & ((c_io >= qb) | (v & 1 == 0)) & ((c_io < 2 * qb) | (v & 2 == 0)) for v in range(4)]
    bias = jnp.asarray(np.where(np.stack(variants), 0.0, -np.inf), F32)
    return pl.pallas_call(
        _swa_kernel,
        grid=(b, SWA_KV_HEADS, nb),
        in_specs=[pl.BlockSpec(memory_space=pltpu.SMEM),
                  pl.BlockSpec((1, qb, 3 * qb),
                               lambda bi, h, i: ((i == 0).astype(I32) + 2 * (i == nb - 1).astype(I32), 0, 0)),
                  pl.BlockSpec((qb, gw), lambda bi, h, i: (bi * nb + i, h)),
                  lat(-1), lat(0), lat(1), lat(-1), lat(0), lat(1), ctx, ctx],
        out_specs=pl.BlockSpec((qb, gw), lambda bi, h, i: (bi * nb + i, h)),
        out_shape=jax.ShapeDtypeStruct((b * t, SWA_HEADS * SWA_DH), BF16),
        compiler_params=_params("parallel", "parallel", "arbitrary"),
        name="swa_attention",
    )(sink, bias, q, kx, kx, kx, vx, vx, vx, kc, vc)


def _dup_heads(a, col0):
    m = a.shape[0]
    h = a[:, col0:col0 + SWA_KV_HEADS * SWA_DH].reshape(m, SWA_KV_HEADS, 1, SWA_DH)
    return jnp.broadcast_to(h, (m, SWA_KV_HEADS, LANES // SWA_DH, SWA_DH)).reshape(m, SWA_KV_HEADS * LANES)


def _first_max(vals, iota, size, axis):
    m = jnp.max(vals, axis=axis, keepdims=True)
    first = jnp.min(jnp.where(vals == m, iota, size), axis=axis, keepdims=True)
    return m, first


def _router_kernel(x_ref, nw_ref, sh_ref, sc_ref, rwt_ref, rb_ref, c0_ref,
                   hp_ref, idx_ref, wt_ref, rank_ref, cnt_ref, carry_ref):
    @pl.when(pl.program_id(0) == 0)
    def _():
        carry_ref[...] = c0_ref[...]

    h = _norm_mod(x_ref[...], nw_ref[...], sh_ref[0], sc_ref[0])
    tm, d = h.shape
    _store_token_tiles(hp_ref, 0, _pack_pairs(h[:, :d // 2], h[:, d // 2:]))
    scores = jax.nn.sigmoid(_dot3(rwt_ref[...], h, NT_DIMS))
    sel = scores + rb_ref[...]
    neg = jnp.float32(-jnp.inf)

    g_io = lax.broadcasted_iota(I32, (GROUP_SIZE, tm), 0)
    gs_rows = []
    for g in range(N_GROUPS):
        sg = sel[g * GROUP_SIZE:(g + 1) * GROUP_SIZE]
        m1, f1 = _first_max(sg, g_io, GROUP_SIZE, 0)
        m2 = jnp.max(jnp.where(g_io == f1, neg, sg), axis=0, keepdims=True)
        gs_rows.append(m1 + m2)
    cur = jnp.concatenate(gs_rows, axis=0)
    n_io = lax.broadcasted_iota(I32, (N_GROUPS, tm), 0)
    gmask = jnp.zeros((N_GROUPS, tm), I32)
    for _ in range(TOPK_GROUPS):
        _, f = _first_max(cur, n_io, N_GROUPS, 0)
        hit = n_io == f
        gmask = jnp.where(hit, 1, gmask)
        cur = jnp.where(hit, neg, cur)
    cur = jnp.concatenate(
        [jnp.where(gmask[g:g + 1] > 0, sel[g * GROUP_SIZE:(g + 1) * GROUP_SIZE], neg) for g in range(N_GROUPS)],
        axis=0)

    e_io = lax.broadcasted_iota(I32, (N_EXPERTS, tm), 0)
    chosen = jnp.zeros((N_EXPERTS, tm), F32)
    idx_rows, w_rows = [], []
    for _ in range(TOP_K):
        _, f = _first_max(cur, e_io, N_EXPERTS, 0)
        hit = e_io == f
        idx_rows.append(f)
        w_rows.append(jnp.sum(jnp.where(hit, scores, 0.0), axis=0, keepdims=True))
        chosen = jnp.where(hit, 1.0, chosen)
        cur = jnp.where(hit, neg, cur)
    idx = jnp.concatenate(idx_rows, axis=0)
    w = jnp.concatenate(w_rows, axis=0)
    idx_ref[...] = idx
    wt_ref[...] = w * (1.0 / jnp.sum(w, axis=0, keepdims=True)) * ROUTED_SCALE

    onehot = chosen.astype(BF16)
    before = (lax.broadcasted_iota(I32, (tm, tm), 0) < lax.broadcasted_iota(I32, (tm, tm), 1)).astype(BF16)
    base = carry_ref[:, 0:1] + jnp.dot(onehot, before, preferred_element_type=F32)
    rank_ref[...] = jnp.concatenate(
        [jnp.sum(jnp.where(e_io == idx_rows[k], base, 0.0), axis=0, keepdims=True) for k in range(TOP_K)],
        axis=0).astype(I32)
    carry_ref[...] = carry_ref[...] + jnp.sum(chosen, axis=1, keepdims=True)
    cnt_ref[...] = carry_ref[...]


def _router(x, nw, sh, sc, rwt, rb, counts0, rows_per_group):
    m, d = x.shape
    tm = math.gcd(_tile(m, ROUTER_TOKENS, LANES), rows_per_group)
    assert (d // 2) % LANES == 0
    tr = d // 2 // LANES
    grp = lambda i: ((i * tm) // rows_per_group, 0, 0)
    tok = lambda rows: pl.BlockSpec((rows, tm), lambda i: (0, i))
    return pl.pallas_call(
        _router_kernel,
        grid=(m // tm,),
        in_specs=[pl.BlockSpec((tm, d), lambda i: (i, 0)),
                  pl.BlockSpec((1, d), lambda i: (0, 0)),
                  pl.BlockSpec((1, 1, d), grp),
                  pl.BlockSpec((1, 1, d), grp),
                  pl.BlockSpec((N_EXPERTS, d), lambda i: (0, 0)),
                  pl.BlockSpec((N_EXPERTS, 1), lambda i: (0, 0)),
                  pl.BlockSpec((N_EXPERTS, LANES), lambda i: (0, 0))],
        out_specs=[pl.BlockSpec((tm * tr, LANES), lambda i: (i, 0)),
                   tok(TOP_K), tok(TOP_K), tok(TOP_K),
                   pl.BlockSpec((N_EXPERTS, LANES), lambda i: (0, 0))],
        out_shape=[jax.ShapeDtypeStruct((m * tr, LANES), U32),
                   jax.ShapeDtypeStruct((TOP_K, m), I32),
                   jax.ShapeDtypeStruct((TOP_K, m), F32),
                   jax.ShapeDtypeStruct((TOP_K, m), I32),
                   jax.ShapeDtypeStruct((N_EXPERTS, LANES), F32)],
        scratch_shapes=[pltpu.VMEM((N_EXPERTS, LANES), F32)],
        compiler_params=_params("arbitrary"),
        name="moe_router",
    )(x, nw.reshape(1, d), sh, sc, rwt, rb.reshape(N_EXPERTS, 1), counts0)


def _dispatch_kernel(nv_ref, dest_ref, *rest, tr, tiles):
    hp_refs = rest[:len(tiles)]
    xs_ref, zero_ref, sem = rest[len(tiles):]
    i = pl.program_id(0)
    tm = hp_refs[0].shape[0] // tr
    blk_rows = MOE_BLOCK * tr

    @pl.when(i == 0)
    def _():
        zero_ref[...] = jnp.zeros(zero_ref.shape, U32)

        def fill(blk, carry):
            @pl.when(nv_ref[blk] < MOE_BLOCK)
            def _():
                cp = pltpu.make_async_copy(
                    zero_ref, xs_ref.at[pl.ds(pl.multiple_of(blk * blk_rows, blk_rows), blk_rows)], sem)
                cp.start()
                cp.wait()
            return carry

        lax.fori_loop(0, nv_ref.shape[0], fill, 0)

    def scatter(hp_ref):
        def copy(t, k):
            return _tile_copy(hp_ref, t * tr, xs_ref, dest_ref[t * TOP_K + k], tr, sem)

        def start(t, carry):
            for k in range(TOP_K):
                copy(t, k).start(priority=k % 2)
            return carry

        def wait(t, carry):
            for k in range(TOP_K):
                copy(t, k).wait()
            return carry

        lax.fori_loop(0, tm, start, 0)
        lax.fori_loop(0, tm, wait, 0)

    first = 0
    for hp_ref, n in zip(hp_refs, tiles):
        if len(tiles) == 1:
            scatter(hp_ref)
        else:
            pl.when((i >= first) & (i < first + n))(functools.partial(scatter, hp_ref))
        first += n


def _dispatch(block_nv, dest, hps, n_rows, tr):
    tm = functools.reduce(math.gcd, [hp.shape[0] // tr for hp in hps] + [DISPATCH_TOKENS])
    tiles = tuple(hp.shape[0] // tr // tm for hp in hps)
    firsts = [sum(tiles[:s]) for s in range(len(tiles))]
    hp_specs = [pl.BlockSpec((tm * tr, LANES), lambda i, nv, f=f, n=n: (jnp.clip(i - f, 0, n - 1), 0))
                for f, n in zip(firsts, tiles)]
    return pl.pallas_call(
        functools.partial(_dispatch_kernel, tr=tr, tiles=tiles),
        grid_spec=pltpu.PrefetchScalarGridSpec(
            num_scalar_prefetch=1,
            grid=(sum(tiles),),
            in_specs=[pl.BlockSpec((tm * TOP_K,), lambda i, nv: (i,), memory_space=pltpu.SMEM)] + hp_specs,
            out_specs=pl.BlockSpec(memory_space=pl.ANY),
            scratch_shapes=[pltpu.VMEM((MOE_BLOCK * tr, LANES), U32), pltpu.SemaphoreType.DMA(())]),
        out_shape=jax.ShapeDtypeStruct((n_rows * tr, LANES), U32),
        compiler_params=_params("arbitrary"),
        name="moe_dispatch",
    )(block_nv, dest, *hps)


def _gffn_kernel(be_ref, nv_ref, first_ref, slot_ref, next_ref, xs_ref, w1_hbm, w3_hbm, w2_hbm, ys_ref,
                 w1f_ref, w3f_ref, w2f_ref, w1b_ref, w3b_ref, w2b_ref, sem, *, tr, layer):
    i = pl.program_id(0)
    nv = nv_ref[i]

    def weight_copies(expert, slot):
        return [pltpu.make_async_copy(src.at[layer, expert], dst.at[slot], sem.at[slot])
                for src, dst in ((w1_hbm, w1f_ref), (w3_hbm, w3f_ref), (w2_hbm, w2f_ref))]

    @pl.when(first_ref[i] == 1)
    def _():
        slot = slot_ref[i]

        @pl.when(i == 0)
        def _():
            for cp in weight_copies(be_ref[i], slot):
                cp.start()

        for cp in weight_copies(be_ref[i], slot):
            cp.wait()
        w1b_ref[...] = w1f_ref[slot].astype(BF16)
        w3b_ref[...] = w3f_ref[slot].astype(BF16)
        w2b_ref[...] = w2f_ref[slot].astype(BF16)

        @pl.when(next_ref[i] >= 0)
        def _():
            for cp in weight_copies(next_ref[i], 1 - slot):
                cp.start()

    @pl.when(nv > 0)
    def _():
        lo, hi = _unpack_pairs(_load_token_tiles(xs_ref, 0, MOE_BLOCK, tr))
        dh = lo.shape[1]
        lo = lo.astype(BF16)
        hi = hi.astype(BF16)
        d = functools.partial(jnp.dot, preferred_element_type=F32)
        a = d(lo, w1b_ref[:dh, :]) + d(hi, w1b_ref[dh:, :])
        g = d(lo, w3b_ref[:dh, :]) + d(hi, w3b_ref[dh:, :])
        y = d((_silu(a) * g).astype(BF16), w2b_ref[...])
        _store_token_tiles(ys_ref, 0, _pack_pairs(y[:, :dh], y[:, dh:]))

    @pl.when(nv == 0)
    def _():
        ys_ref[...] = jnp.zeros(ys_ref.shape, U32)


def _grouped_ffn(block_e, block_nv, present, xs, w1, w3, w2, layer, tr):
    _, _, d, f = w1.shape
    blk_rows = MOE_BLOCK * tr
    nb = xs.shape[0] // blk_rows
    prev_e = jnp.concatenate([jnp.full((1,), -1, I32), block_e[:-1]])
    first = ((block_nv > 0) & (block_e != prev_e)).astype(I32)
    slot = ((jnp.cumsum(first) - 1) % 2).astype(I32)
    e_ar = jnp.arange(N_EXPERTS, dtype=I32)
    cand = jnp.where(present, e_ar, N_EXPERTS)
    later = jnp.flip(lax.cummin(jnp.flip(cand)))
    after = jnp.concatenate([later[1:], jnp.full((1,), N_EXPERTS, I32)])
    nxt = jnp.where(after[block_e] < N_EXPERTS, after[block_e], -1).astype(I32)
    hbm = pl.BlockSpec(memory_space=pl.ANY)
    return pl.pallas_call(
        functools.partial(_gffn_kernel, tr=tr, layer=layer),
        grid_spec=pltpu.PrefetchScalarGridSpec(
            num_scalar_prefetch=5,
            grid=(nb,),
            in_specs=[pl.BlockSpec((blk_rows, LANES), lambda i, *_: (i, 0)), hbm, hbm, hbm],
            out_specs=pl.BlockSpec((blk_rows, LANES), lambda i, *_: (i, 0)),
            scratch_shapes=[pltpu.VMEM((2, d, f), F32), pltpu.VMEM((2, d, f), F32), pltpu.VMEM((2, f, d), F32),
                            pltpu.VMEM((d, f), BF16), pltpu.VMEM((d, f), BF16), pltpu.VMEM((f, d), BF16),
                            pltpu.SemaphoreType.DMA((2,))]),
        out_shape=jax.ShapeDtypeStruct(xs.shape, U32),
        compiler_params=_params("arbitrary"),
        name="moe_grouped_ffn",
    )(block_e, block_nv, first, slot, nxt, xs, w1, w3, w2)


def _combine_kernel(dest_ref, dest_next_ref, x_ref, hp_ref, wt_ref, g_ref, ws1_ref, ws3_ref, ws2_ref, ys_ref,
                    o_ref, buf_a, buf_b, sem_a, sem_b, *, tr):
    i = pl.program_id(0)
    tm = x_ref.shape[0]
    dh = tr * LANES

    def copy(dref, t, k, buf, sem):
        return _tile_copy(ys_ref, dref[t * TOP_K + k], buf, (k * tm + t) * tr, tr, sem)

    def wait_all(buf, sem):
        def wait(t, carry):
            for k in range(TOP_K):
                copy(dest_ref, t, k, buf, sem).wait()
            return carry
        lax.fori_loop(0, tm, wait, 0)

    @pl.when(i == 0)
    def _():
        def start(t, carry):
            for k in range(TOP_K):
                copy(dest_ref, t, k, buf_a, sem_a).start(priority=k % 2)
            return carry
        lax.fori_loop(0, tm, start, 0)

    def step(buf, sem, buf_next, sem_next):
        wait_all(buf, sem)
        for t in range(tm):
            for k in range(TOP_K):
                copy(dest_next_ref, t, k, buf_next, sem_next).start(priority=k % 2)
        lo, hi = _unpack_pairs(_load_token_tiles(hp_ref, 0, tm, tr))
        lo = lo.astype(BF16)
        hi = hi.astype(BF16)
        d = functools.partial(jnp.dot, preferred_element_type=F32)
        a = d(lo, ws1_ref[:dh, :]) + d(hi, ws1_ref[dh:, :])
        g = d(lo, ws3_ref[:dh, :]) + d(hi, ws3_ref[dh:, :])
        shared = d((_silu(a) * g).astype(BF16), ws2_ref[...])
        acc_lo = jnp.zeros((tm, dh), F32)
        acc_hi = jnp.zeros((tm, dh), F32)
        for k in range(TOP_K):
            ylo, yhi = _unpack_pairs(_load_token_tiles(buf, k * tm * tr, tm, tr))
            wk = wt_ref[:, k:k + 1]
            acc_lo = acc_lo + wk * ylo
            acc_hi = acc_hi + wk * yhi
        o_ref[:, :dh] = x_ref[:, :dh] + g_ref[0][:, :dh] * (acc_lo + shared[:, :dh])
        o_ref[:, dh:] = x_ref[:, dh:] + g_ref[0][:, dh:] * (acc_hi + shared[:, dh:])

        @pl.when(i == pl.num_programs(0) - 1)
        def _():
            wait_all(buf_next, sem_next)

    @pl.when(i % 2 == 0)
    def _():
        step(buf_a, sem_a, buf_b, sem_b)

    @pl.when(i % 2 == 1)
    def _():
        step(buf_b, sem_b, buf_a, sem_a)


def _combine(dest, x, hp, wt, gate, ws1, ws3, ws2, ys, rows_per_group, tr):
    m, d = x.shape
    f = ws1.shape[1]
    tm = math.gcd(_tile(m, COMBINE_TOKENS, LANES), rows_per_group)
    nt = m // tm
    grp = lambda i: ((i * tm) // rows_per_group, 0, 0)
    buf = pltpu.VMEM((TOP_K * tm * tr, LANES), U32)
    return pl.pallas_call(
        functools.partial(_combine_kernel, tr=tr),
        grid=(nt,),
        in_specs=[pl.BlockSpec((tm * TOP_K,), lambda i: (i,), memory_space=pltpu.SMEM),
                  pl.BlockSpec((tm * TOP_K,), lambda i: (jnp.minimum(i + 1, nt - 1),), memory_space=pltpu.SMEM),
                  pl.BlockSpec((tm, d), lambda i: (i, 0)),
                  pl.BlockSpec((tm * tr, LANES), lambda i: (i, 0)),
                  pl.BlockSpec((tm, TOP_K), lambda i: (i, 0)),
                  pl.BlockSpec((1, 1, d), grp),
                  pl.BlockSpec((d, f), lambda i: (0, 0)),
                  pl.BlockSpec((d, f), lambda i: (0, 0)),
                  pl.BlockSpec((f, d), lambda i: (0, 0)),
                  pl.BlockSpec(memory_space=pl.ANY)],
        out_specs=pl.BlockSpec((tm, d), lambda i: (i, 0)),
        out_shape=jax.ShapeDtypeStruct((m, d), F32),
        scratch_shapes=[buf, buf, pltpu.SemaphoreType.DMA(()), pltpu.SemaphoreType.DMA(())],
        compiler_params=_params("arbitrary"),
        name="moe_combine",
    )(dest, dest, x, hp, wt, gate, ws1, ws3, ws2, ys)


def _moe(streams, layer, nw, rw, rb, w1, w3, w2, ws1, ws3, ws2):
    d = streams[0][0].shape[1]
    rwt = rw.T
    counts = jnp.zeros((N_EXPERTS, LANES), F32)
    routed = []
    for x, sh, sc, _, rpg in streams:
        hp, idx, wt, rank, counts = _router(x, nw, sh, sc, rwt, rb, counts, rpg)
        routed.append((hp, idx, wt, rank))
    n_assign = sum(s[0].shape[0] for s in streams) * TOP_K
    n_blocks = (n_assign + N_EXPERTS * (MOE_BLOCK - 1) + MOE_BLOCK - 1) // MOE_BLOCK
    cnt = counts[:, 0].astype(I32)
    padded = (cnt + MOE_BLOCK - 1) // MOE_BLOCK * MOE_BLOCK
    pad_end = jnp.cumsum(padded)
    pad_start = pad_end - padded
    bstart = jnp.arange(n_blocks, dtype=I32) * MOE_BLOCK
    block_e = jnp.minimum(jnp.sum((bstart[:, None] >= pad_end[None, :]).astype(I32), axis=1), N_EXPERTS - 1)
    block_nv = jnp.clip(cnt[block_e] - (bstart - pad_start[block_e]), 0, MOE_BLOCK).astype(I32)
    e_ar = jnp.arange(N_EXPERTS, dtype=I32)
    tr = d // 2 // LANES
    dests = [((jnp.sum(jnp.where(idx[:, :, None] == e_ar, pad_start, 0), axis=-1) + rank) * tr).T.reshape(-1)
             for _, idx, _, rank in routed]
    dest_all = jnp.concatenate(dests, axis=0) if len(dests) > 1 else dests[0]
    xs = _dispatch(block_nv, dest_all, [r[0] for r in routed], n_blocks * MOE_BLOCK, tr)
    ys = _grouped_ffn(block_e, block_nv, cnt > 0, xs, w1, w3, w2, layer, tr)
    ws1b, ws3b, ws2b = ws1.astype(BF16), ws3.astype(BF16), ws2.astype(BF16)
    return [_combine(dest, x, hp, wt.T, gate, ws1b, ws3b, ws2b, ys, rpg, tr)
            for (x, _, _, gate, rpg), (hp, _, wt, _), dest in zip(streams, routed, dests)]


def _mixer_ab(xs, cs, mx, mc, b, t, n_c, layer, nw, w_in, w_out, q_norm, k_norm, lam_vec, subln, conv_w,
              a_log, dt_bias, out_norm):
    n_main = w_in.shape[1] - 4 * DN_HEADS
    assert n_main % LANES == 0
    w_main = w_in[:, :n_main].astype(BF16)
    w_small = jnp.zeros((w_in.shape[0], LANES), F32).at[:, :4 * DN_HEADS].set(w_in[:, n_main:])
    nqk = 2 * DIFF_HEADS * DIFF_DK
    assert n_main % nqk == 0
    tabs = _rope_tables(t, DIFF_DK)
    hw = jnp.stack([q_norm * (DIFF_DK ** -0.5 * LOG2E), k_norm])
    groups = nqk // LANES
    blocks = ((0, 1, (0,) * groups), (1, 2, (1,) * groups))
    z_x, zs_x = _norm_mod_matmul(xs, nw, mx[0], mx[1], w_main, t, w_small, tn=nqk,
                                 heads=dict(hd=DIFF_DK, weights=hw, blocks=blocks, tables=tabs, t=t))
    z_c, zs_c = _norm_mod_matmul(cs, nw, mc[0], mc[1], w_main, b * n_c, w_small, tn=nqk,
                                 heads=dict(hd=DIFF_DK, weights=hw, blocks=blocks, tables=None, t=t))
    lam_init = 0.8 - 0.6 * math.exp(-0.3 * layer)
    od_x = _diff_attention(lam_vec, z_x, z_c, z_c, subln, lam_init, b, n_c, kx=z_x, zx=z_x, t=t)
    od_c = _diff_attention(lam_vec, z_c, z_c, z_c, subln, lam_init, b, n_c)
    dn_col0 = 2 * nqk + DIFF_HEADS * DIFF_DV
    dn_x = _dn_prep(z_x, dn_col0, conv_w, t)
    dn_c = _dn_prep(z_c, dn_col0, conv_w, n_c)
    zst_x = zs_x[:, :4 * DN_HEADS].T
    zst_c = zs_c[:, :4 * DN_HEADS].T
    prm = _deltanet_params(a_log, dt_bias)
    s0 = jnp.zeros((b, DN_HEADS, DN_DK, DN_DV), F32)
    o_cf, o_cb, s_cf, s_cb = _deltanet(dn_c, zs_c, zst_c, prm, s0, s0, b, n_c)
    o_xf, o_xb, _, _ = _deltanet(dn_x, zs_x, zst_x, prm, s_cf, s_cb, b, t)
    gate_col0 = dn_col0 + conv_w.shape[1]
    w_out_b = w_out.astype(BF16)
    x1 = _outproj_ab(od_x, o_xf, o_xb, z_x, gate_col0, out_norm, w_out_b, xs, mx[2], t)
    c1 = _outproj_ab(od_c, o_cf, o_cb, z_c, gate_col0, out_norm, w_out_b, cs, mc[2], b * n_c)
    return x1, c1


def _mixer_swa(xs, cs, mx, mc, b, t, n_c, nw, w_in, w_out, q_norm, k_norm, sink):
    nq = SWA_HEADS * SWA_DH
    nkv = SWA_KV_HEADS * SWA_DH
    w_b = w_in.astype(BF16)
    tabs = _rope_tables(t, SWA_DH)
    rep = LANES // SWA_DH
    hw = jnp.stack([jnp.tile(q_norm, rep) * (SWA_DH ** -0.5 * LOG2E), jnp.tile(k_norm, rep)])
    q_rows = (0,) * (nq // LANES)
    kv_rows = (1,) * (nkv // LANES) + (None,) * (nkv // LANES)
    z_x = _norm_mod_matmul(xs, nw, mx[0], mx[1], w_b, t, tn=w_b.shape[1],
                           heads=dict(hd=SWA_DH, weights=hw, tables=tabs, t=t, blocks=((0, 1, q_rows + kv_rows),)))
    z_c = _norm_mod_matmul(cs, nw, mc[0], mc[1], w_b[:, nq:], b * n_c, tn=2 * nkv,
                           heads=dict(hd=SWA_DH, weights=hw, tables=None, t=t, blocks=((0, 1, kv_rows),)))
    att = _swa_attention(sink.reshape(SWA_KV_HEADS, SWA_GROUP), z_x, _dup_heads(z_x, nq), _dup_heads(z_x, nq + nkv),
                         _dup_heads(z_c, 0), _dup_heads(z_c, nkv), b, t, n_c)
    return _outproj(att, w_out.astype(BF16), xs, mx[2], t)


def kernel(x, c, ctx, c_ctx, mod_w, mod_b, norm_mix, norm_ffn, ab_w_in, ab_w_out, diff_q_norm, diff_k_norm, diff_lambda, diff_subln, dn_conv, dn_a_log, dn_dt_bias, dn_out_norm, swa_w_in, swa_w_out, swa_q_norm, swa_k_norm, swa_sink, router_w, router_bias, exp_w1, exp_w3, exp_w2, shared_w1, shared_w3, shared_w2):
    b, t, d = x.shape
    n_c = ctx.shape[1]
    depth = mod_w.shape[0]
    assert depth == 2 and t % Q_BLOCK == 0 and t % DN_CHUNK == 0 and n_c % DN_CHUNK == 0
    xs = x.reshape(b * t, d)
    cs = ctx.reshape(b * n_c, d)
    n_mod = -(-(b + 1) // SUBLANES) * SUBLANES
    a_mod = jnp.zeros((n_mod, d), F32).at[0].set(c_ctx).at[1:1 + b].set(c)
    for layer in range(depth):
        with_ctx = layer < depth - 1
        p = layer // 2
        mod = _modulation(a_mod, mod_w, layer, mod_b[layer])
        mc = [mod[0:1, j * d:(j + 1) * d].reshape(1, 1, d) for j in range(6)]
        mx = [mod[1:1 + b, j * d:(j + 1) * d].reshape(b, 1, d) for j in range(6)]
        if layer % 2 == 0:
            xs, c_new = _mixer_ab(xs, cs, mx, mc, b, t, n_c, layer, norm_mix[layer], ab_w_in[p], ab_w_out[p],
                                  diff_q_norm[p], diff_k_norm[p], diff_lambda[p], diff_subln[p], dn_conv[p],
                                  dn_a_log[p], dn_dt_bias[p], dn_out_norm[p])
        else:
            assert not with_ctx
            xs = _mixer_swa(xs, cs, mx, mc, b, t, n_c, norm_mix[layer], swa_w_in[p], swa_w_out[p],
                            swa_q_norm[p], swa_k_norm[p], swa_sink[p])
            c_new = None
        moe_w = (layer, norm_ffn[layer], router_w[layer], router_bias[layer], exp_w1, exp_w3, exp_w2,
                 shared_w1[layer], shared_w3[layer], shared_w2[layer])
        if with_ctx:
            cs, xs = _moe([(c_new, mc[3], mc[4], mc[5], b * n_c), (xs, mx[3], mx[4], mx[5], t)], *moe_w)
        else:
            (xs,) = _moe([(xs, mx[3], mx[4], mx[5], t)], *moe_w)
    return xs.reshape(b, t, d)
```
